```python
import jax, jax.numpy as jnp
from jax import lax
import numpy as np

D_MODEL = 1024
BATCH = 8
SEQ = 2048
DEPTH = 1

GRID_W = 64
CTX_LEN = 256
CONV_DIM = 512
CONV_K = 31
GLA_HEADS = 4
GLA_DK = 64
GLA_DV = 128
DECAY_RANK = 16
GATE_NORM = 16.0
CHUNK = 64
D_FF = 4 * D_MODEL
N_MOD = 6
EPS = 1e-6

COL_GLU = 0
COL_Q = COL_GLU + 2 * CONV_DIM
COL_K = COL_Q + GLA_HEADS * GLA_DK
COL_V = COL_K + GLA_HEADS * GLA_DK
COL_R = COL_V + GLA_HEADS * GLA_DV
COL_DEC = COL_R + GLA_HEADS * GLA_DV
COL_GATE = COL_DEC + 2 * DECAY_RANK
COL_END = COL_GATE + 2 * D_MODEL

kernel_name = "hybrid_conformer_gla_dit_block"


def rmsnorm(x, g):
    x32 = x.astype(jnp.float32)
    y = x32 * lax.rsqrt(jnp.mean(x32 * x32, axis=-1, keepdims=True) + EPS)
    return (y * g).astype(x.dtype)


def layernorm(x, g, b):
    x32 = x.astype(jnp.float32)
    mu = jnp.mean(x32, axis=-1, keepdims=True)
    var = jnp.mean(jnp.square(x32 - mu), axis=-1, keepdims=True)
    y = (x32 - mu) * lax.rsqrt(var + EPS)
    return (y * g + b).astype(x.dtype)


def adaln(cvec, w_mod, b_mod, n):
    m = jax.nn.silu(cvec) @ w_mod[:, :n * D_MODEL] + b_mod[:n * D_MODEL]
    return jnp.split(m[:, None, :], n, axis=-1)


def modulate(xn, shift, scale):
    return xn * (1.0 + scale) + shift


def heads(a, d):
    B_, T, _ = a.shape
    return a.reshape(B_, T, -1, d).transpose(0, 2, 1, 3)


def flip_t(a):
    return jnp.flip(a, axis=2)


def dwconv1d(x, w, b):
    y = lax.conv_general_dilated(
        x, w[:, None, :], window_strides=(1,),
        padding=[(CONV_K // 2, CONV_K // 2)],
        dimension_numbers=("NWC", "WIO", "NWC"),
        feature_group_count=x.shape[-1])
    return y + b


def conv_grid(a, w, b, rows):
    B_, T, C = a.shape
    half = C // 2
    g = a.reshape(B_, rows, GRID_W, C)
    ah = g[..., :half].reshape(B_ * rows, GRID_W, half)
    yh = dwconv1d(ah, w[:, :half], b[:half]).reshape(B_, rows, GRID_W, half)
    av = g[..., half:].transpose(0, 2, 1, 3).reshape(B_ * GRID_W, rows, half)
    yv = dwconv1d(av, w[:, half:], b[half:]).reshape(B_, GRID_W, rows, half).transpose(0, 2, 1, 3)
    return jnp.concatenate([yh, yv], axis=-1).reshape(B_, T, C)


def decay_logs(z, w_decay, b_decay):
    B_, T, _ = z.shape
    z = z.reshape(B_, T, 2, DECAY_RANK)
    logits = jnp.einsum("btdr,drk->btdk", z, w_decay) + b_decay
    la = jax.nn.log_sigmoid(logits.astype(jnp.float32)) / GATE_NORM
    return heads(la[:, :, 0], GLA_DK), heads(la[:, :, 1], GLA_DK)


def gla_scan(q, k, v, la, S0):
    B_, H, T, _ = q.shape
    n = T // CHUNK

    def chunks(a):
        return a.reshape(B_, H, n, CHUNK, a.shape[-1]).transpose(2, 0, 1, 3, 4)

    lower = jnp.tril(jnp.ones((CHUNK, CHUNK), dtype=bool))[:, :, None]

    def step(S, inp):
        qc, kc, vc, lac = inp
        G = jnp.cumsum(lac, axis=2)
        diff = G[:, :, :, None, :] - G[:, :, None, :, :]
        dec = jnp.exp(jnp.where(lower, diff, -jnp.inf))
        A = jnp.einsum("bhid,bhjd,bhijd->bhij", qc, kc, dec)
        o = (jnp.einsum("bhij,bhjv->bhiv", A, vc)
             + jnp.einsum("bhid,bhdv->bhiv", qc * jnp.exp(G), S))
        G_last = G[:, :, -1:, :]
        S_new = (jnp.exp(G_last[:, :, 0, :, None]) * S
                 + jnp.einsum("bhjd,bhjv->bhdv", kc * jnp.exp(G_last - G), vc))
        return S_new, o

    S_fin, o = lax.scan(step, S0, (chunks(q), chunks(k), chunks(v), chunks(la)))
    o = o.transpose(1, 2, 0, 3, 4).reshape(B_, H, T, v.shape[-1])
    return o, S_fin


def gla_final_state(k, v, la):
    G = jnp.cumsum(la, axis=2)
    return jnp.einsum("bhtd,bhtv->bhdv", k * jnp.exp(G[:, :, -1:] - G), v)


def context_states(uc, w_in, w_decay, b_decay):
    kv = uc @ w_in[:, COL_K:COL_R]
    k_c = heads(kv[..., :GLA_HEADS * GLA_DK], GLA_DK)
    v_c = heads(kv[..., GLA_HEADS * GLA_DK:], GLA_DV)
    la_f, la_b = decay_logs(uc @ w_in[:, COL_DEC:COL_GATE], w_decay, b_decay)
    S_f = gla_final_state(k_c, v_c, la_f)
    S_b = gla_final_state(flip_t(k_c), flip_t(v_c), flip_t(la_b))
    return S_f, S_b


def mixer(u, rows, S_f0, S_b0, w_in, conv_w, conv_b, conv_ln_g, conv_ln_b,
          w_conv_out, w_decay, b_decay, gla_norm_g, w_gla_out, w_out):
    B_, T, _ = u.shape
    proj = u @ w_in
    glu_in = proj[..., COL_GLU:COL_Q]
    a = glu_in[..., :CONV_DIM] * jax.nn.sigmoid(glu_in[..., CONV_DIM:])
    a = conv_grid(a, conv_w, conv_b, rows) if rows is not None else dwconv1d(a, conv_w, conv_b)
    y_conv = jax.nn.silu(layernorm(a, conv_ln_g, conv_ln_b)) @ w_conv_out
    q = heads(proj[..., COL_Q:COL_K], GLA_DK) * (GLA_DK ** -0.5)
    k = heads(proj[..., COL_K:COL_V], GLA_DK)
    v = heads(proj[..., COL_V:COL_R], GLA_DV)
    r = proj[..., COL_R:COL_DEC]
    la_f, la_b = decay_logs(proj[..., COL_DEC:COL_GATE], w_decay, b_decay)
    o_f, _ = gla_scan(q, k, v, la_f, S_f0)
    o_b, _ = gla_scan(flip_t(q), flip_t(k), flip_t(v), flip_t(la_b), S_b0)
    o = rmsnorm(o_f + flip_t(o_b), gla_norm_g)
    o = o.transpose(0, 2, 1, 3).reshape(B_, T, GLA_HEADS * GLA_DV).astype(u.dtype)
    y_gla = (o * jax.nn.silu(r)) @ w_gla_out
    gates = jax.nn.sigmoid(proj[..., COL_GATE:COL_END])
    merged = gates[..., :D_MODEL] * y_conv + gates[..., D_MODEL:] * y_gla
    return merged @ w_out


def sq_relu_mlp(u, w_ff1, w_ff2):
    return jnp.square(jax.nn.relu(u @ w_ff1)) @ w_ff2


def setup_inputs(seed: int = 0) -> dict:
    key = jax.random.key(seed)
    ks = jax.random.split(key, 24)
    D = D_MODEL
    f32 = jnp.float32

    def nrm(k, shape, scale):
        return jax.random.normal(k, shape, f32) * scale

    return {
        "x": nrm(ks[0], (BATCH, SEQ, D), 1.0),
        "c": nrm(ks[1], (BATCH, D), 1.0),
        "ctx": nrm(ks[2], (BATCH, CTX_LEN, D), 1.0),
        "c_ctx": nrm(ks[3], (D,), 1.0),
        "w_mod": nrm(ks[4], (DEPTH, D, N_MOD * D), 0.5 * D ** -0.5),
        "b_mod": nrm(ks[5], (DEPTH, N_MOD * D), 0.01),
        "g_pre1": 1.0 + nrm(ks[6], (DEPTH, D), 0.05),
        "g_post1": 1.0 + nrm(ks[7], (DEPTH, D), 0.05),
        "g_pre2": 1.0 + nrm(ks[8], (DEPTH, D), 0.05),
        "g_post2": 1.0 + nrm(ks[9], (DEPTH, D), 0.05),
        "w_in": nrm(ks[10], (DEPTH, D, COL_END), D ** -0.5),
        "conv_w": nrm(ks[11], (DEPTH, CONV_K, CONV_DIM), CONV_K ** -0.5),
        "conv_b": nrm(ks[12], (DEPTH, CONV_DIM), 0.01),
        "conv_ln_g": 1.0 + nrm(ks[13], (DEPTH, CONV_DIM), 0.05),
        "conv_ln_b": nrm(ks[14], (DEPTH, CONV_DIM), 0.01),
        "w_conv_out": nrm(ks[15], (DEPTH, CONV_DIM, D), CONV_DIM ** -0.5),
        "w_decay": nrm(ks[16], (DEPTH, 2, DECAY_RANK, GLA_HEADS * GLA_DK), DECAY_RANK ** -0.5),
        "b_decay": nrm(ks[17], (DEPTH, 2, GLA_HEADS * GLA_DK), 0.5),
        "gla_norm_g": 1.0 + nrm(ks[18], (DEPTH, GLA_DV), 0.05),
        "w_gla_out": nrm(ks[19], (DEPTH, GLA_HEADS * GLA_DV, D), (GLA_HEADS * GLA_DV) ** -0.5),
        "w_out": nrm(ks[20], (DEPTH, D, D), D ** -0.5),
        "w_ff1": nrm(ks[21], (DEPTH, D, D_FF), D ** -0.5),
        "w_ff2": nrm(ks[22], (DEPTH, D_FF, D), D_FF ** -0.5),
    }


def reference(x, c, ctx, c_ctx, w_mod, b_mod, g_pre1, g_post1, g_pre2, g_post2,
              w_in, conv_w, conv_b, conv_ln_g, conv_ln_b, w_conv_out, w_decay,
              b_decay, gla_norm_g, w_gla_out, w_out, w_ff1, w_ff2):
    rows = x.shape[1] // GRID_W
    h = x
    hc = ctx
    cc = c_ctx[None, :]
    for l in range(DEPTH):
        mp = (w_in[l], conv_w[l], conv_b[l], conv_ln_g[l], conv_ln_b[l], w_conv_out[l],
              w_decay[l], b_decay[l], gla_norm_g[l], w_gla_out[l], w_out[l])
        sh1, sc1, gt1, sh2, sc2, gt2 = adaln(c, w_mod[l], b_mod[l], N_MOD)
        csh1, csc1 = adaln(cc, w_mod[l], b_mod[l], 2)
        uc = modulate(rmsnorm(hc, g_pre1[l]), csh1, csc1)
        S_f, S_b = context_states(uc, w_in[l], w_decay[l], b_decay[l])
        u = modulate(rmsnorm(h, g_pre1[l]), sh1, sc1)
        y = mixer(u, rows, S_f, S_b, *mp)
        h_mid = h + gt1 * rmsnorm(y, g_post1[l])
        u2 = modulate(rmsnorm(h_mid, g_pre2[l]), sh2, sc2)
        h_new = h_mid + gt2 * rmsnorm(sq_relu_mlp(u2, w_ff1[l], w_ff2[l]), g_post2[l])
        if l < DEPTH - 1:
            _, _, cgt1, csh2, csc2, cgt2 = adaln(cc, w_mod[l], b_mod[l], N_MOD)
            zeros = jnp.zeros((hc.shape[0], GLA_HEADS, GLA_DK, GLA_DV), jnp.float32)
            yc = mixer(uc, None, zeros, zeros, *mp)
            hc_mid = hc + cgt1 * rmsnorm(yc, g_post1[l])
            uc2 = modulate(rmsnorm(hc_mid, g_pre2[l]), csh2, csc2)
            hc = hc_mid + cgt2 * rmsnorm(sq_relu_mlp(uc2, w_ff1[l], w_ff2[l]), g_post2[l])
        h = h_new
    return h
```

```python
import functools

import jax
import jax.numpy as jnp
from jax import lax
from jax.experimental import pallas as pl
from jax.experimental.pallas import tpu as pltpu

D_MODEL = 1024
GRID_W = 64
CONV_DIM = 512
CONV_K = 31
GLA_HEADS = 4
GLA_DK = 64
GLA_DV = 128
DECAY_RANK = 16
GATE_NORM = 16.0
CHUNK = 64
N_MOD = 6
EPS = 1e-6

COL_Q = 2 * CONV_DIM
COL_K = COL_Q + GLA_HEADS * GLA_DK
COL_V = COL_K + GLA_HEADS * GLA_DK
COL_R = COL_V + GLA_HEADS * GLA_DV
COL_DEC = COL_R + GLA_HEADS * GLA_DV
COL_GATE = COL_DEC + 2 * DECAY_RANK
COL_END = COL_GATE + 2 * D_MODEL

LANES = 128
HK = GLA_HEADS * GLA_DK
HV = GLA_HEADS * GLA_DV
CONV_HALF = CONV_DIM // 2
CONV_PAD = CONV_K // 2
H_PITCH = GRID_W + 32
H_OFF = 16

VMEM_LIMIT = 56 * 1024 * 1024

F32 = jnp.float32
BF16 = jnp.bfloat16

NT = (((1,), (1,)), ((), ()))
TN = (((0,), (0,)), ((), ()))


def _dot(a, b):
    return jnp.dot(a, b, preferred_element_type=F32)


def _rms(x, g):
    ms = jnp.mean(x * x, axis=-1, keepdims=True)
    return x * lax.rsqrt(ms + EPS) * g


def _sigmoid(x):
    return jax.nn.sigmoid(x)


def _split_bf16(x):
    hi = x.astype(BF16)
    lo = (x - hi.astype(F32)).astype(BF16)
    return hi, lo


def _adaln_kernel(c_ref, w_ref, b_ref, o_ref):
    c = c_ref[...]
    s = c * _sigmoid(c)
    o_ref[...] = jnp.dot(s, w_ref[...], precision=lax.Precision.HIGHEST,
                         preferred_element_type=F32) + b_ref[...]


def _adaln(c_all, w_mod, b_mod):
    n_rows, d = c_all.shape
    n_out = w_mod.shape[1]
    tn = 1536
    return pl.pallas_call(
        _adaln_kernel,
        grid=(n_out // tn,),
        in_specs=[
            pl.BlockSpec((n_rows, d), lambda j: (0, 0)),
            pl.BlockSpec((d, tn), lambda j: (0, j)),
            pl.BlockSpec((1, tn), lambda j: (0, j)),
        ],
        out_specs=pl.BlockSpec((n_rows, tn), lambda j: (0, j)),
        out_shape=jax.ShapeDtypeStruct((n_rows, n_out), F32),
        compiler_params=pltpu.CompilerParams(
            dimension_semantics=("arbitrary",), vmem_limit_bytes=VMEM_LIMIT),
        name="adaln",
    )(c_all, w_mod, b_mod)


def _prenorm_mod(x, g, mod_ref):
    sh = mod_ref[0:1, :]
    sc = mod_ref[1:2, :]
    return (_rms(x, g) * (1.0 + sc) + sh).astype(BF16)


def _decay_logs(ub, wdin_ref, wdec_ref, bdec_ref):
    z = _dot(ub, wdin_ref[...])
    z_hi, z_lo = _split_bf16(z)
    logits = _dot(z_hi, wdec_ref[...]) + _dot(z_lo, wdec_ref[...]) + bdec_ref[...]
    ls = jnp.minimum(logits, 0.0) - jnp.log1p(jnp.exp(-jnp.abs(logits)))
    return ls * (1.0 / GATE_NORM)


def _proj_kernel(x_ref, mod_ref, g_ref, wm_ref, wdin_ref, wdec_ref, bdec_ref, wg_ref,
                 a_ref, q_ref, k_ref, v_ref, rs_ref, laf_ref, lab_ref, gt_ref):
    ub = _prenorm_mod(x_ref[...], g_ref[...], mod_ref)
    glu = _dot(ub, wm_ref[:, 0:COL_Q])
    a_ref[...] = (glu[:, :CONV_DIM] * _sigmoid(glu[:, CONV_DIM:])).astype(BF16)
    q_ref[...] = (_dot(ub, wm_ref[:, COL_Q:COL_K]) * (GLA_DK ** -0.5)).astype(BF16)
    k_ref[...] = _dot(ub, wm_ref[:, COL_K:COL_V]).astype(BF16)
    v_ref[...] = _dot(ub, wm_ref[:, COL_V:COL_R]).astype(BF16)
    r = _dot(ub, wm_ref[:, COL_R:COL_DEC])
    rs_ref[...] = (r * _sigmoid(r)).astype(BF16)
    la = _decay_logs(ub, wdin_ref, wdec_ref, bdec_ref)
    laf_ref[...] = la[:, :HK]
    lab_ref[...] = la[:, HK:]
    gt_ref[...] = _sigmoid(_dot(ub, wg_ref[...])).astype(BF16)


def _ctx_proj_kernel(x_ref, mod_ref, g_ref, wkv_ref, wdin_ref, wdec_ref, bdec_ref,
                     k_ref, v_ref, laf_ref, lab_ref):
    ub = _prenorm_mod(x_ref[...], g_ref[...], mod_ref)
    k_ref[...] = _dot(ub, wkv_ref[:, :HK]).astype(BF16)
    v_ref[...] = _dot(ub, wkv_ref[:, HK:]).astype(BF16)
    la = _decay_logs(ub, wdin_ref, wdec_ref, bdec_ref)
    laf_ref[...] = la[:, :HK]
    lab_ref[...] = la[:, HK:]


def _const_spec(arr):
    nd = arr.ndim
    return pl.BlockSpec(arr.shape, lambda *_: (0,) * nd)


def _tok_spec(tm, width):
    return pl.BlockSpec((None, tm, width), lambda b, i: (b, i, 0))


def _proj(x, mod, g, wm, wdin, wdec, bdec, wg, tm):
    bsz, t, d = x.shape
    widths = (CONV_DIM, HK, HK, HV, HV, HK, HK, 2 * D_MODEL)
    dtypes = (BF16, BF16, BF16, BF16, BF16, F32, F32, BF16)
    return pl.pallas_call(
        _proj_kernel,
        grid=(bsz, t // tm),
        in_specs=[
            _tok_spec(tm, d),
            pl.BlockSpec((None,) + mod.shape[1:], lambda b, i: (b, 0, 0)),
            _const_spec(g), _const_spec(wm), _const_spec(wdin), _const_spec(wdec),
            _const_spec(bdec), _const_spec(wg),
        ],
        out_specs=[_tok_spec(tm, w) for w in widths],
        out_shape=[jax.ShapeDtypeStruct((bsz, t, w), dt) for w, dt in zip(widths, dtypes)],
        compiler_params=pltpu.CompilerParams(
            dimension_semantics=("parallel", "parallel"), vmem_limit_bytes=VMEM_LIMIT),
        name="proj",
    )(x, mod, g, wm, wdin, wdec, bdec, wg)


def _ctx_proj(x, mod, g, wkv, wdin, wdec, bdec, tm):
    bsz, t, d = x.shape
    widths = (HK, HV, HK, HK)
    dtypes = (BF16, BF16, F32, F32)
    return pl.pallas_call(
        _ctx_proj_kernel,
        grid=(bsz, t // tm),
        in_specs=[
            _tok_spec(tm, d),
            pl.BlockSpec((None,) + mod.shape[1:], lambda b, i: (0, 0, 0)),
            _const_spec(g), _const_spec(wkv), _const_spec(wdin), _const_spec(wdec),
            _const_spec(bdec),
        ],
        out_specs=[_tok_spec(tm, w) for w in widths],
        out_shape=[jax.ShapeDtypeStruct((bsz, t, w), dt) for w, dt in zip(widths, dtypes)],
        compiler_params=pltpu.CompilerParams(
            dimension_semantics=("parallel", "parallel"), vmem_limit_bytes=VMEM_LIMIT),
        name="ctx_proj",
    )(x, mod, g, wkv, wdin, wdec, bdec)


def _gla_chunk(qc, kc, vc, lac, tri_b, tri_mask, mid, last, st_ref, head_lane, want_out):
    la_hi, la_lo = _split_bf16(lac)
    g = _dot(tri_b, la_hi) + _dot(tri_b, la_lo)
    g_mid = g[mid:mid + 1, :]
    g_last = g[last:last + 1, :]
    e_last = jnp.exp(g_last)
    kt = kc.astype(F32) * jnp.exp(g_mid - g)
    kend = kt * jnp.exp(g_last - g_mid)
    if want_out:
        qt = qc.astype(F32) * jnp.exp(g - g_mid)
        qg = (qt * jnp.exp(g_mid)).astype(BF16)
        qt = qt.astype(BF16)
    outs = []
    for h in range(GLA_HEADS):
        pair = h // 2
        lanes = slice(LANES * pair, LANES * (pair + 1))
        hm = head_lane[h % 2]
        vh = vc[:, GLA_DV * h:GLA_DV * (h + 1)]
        kend_h = (kend[:, lanes] * hm).astype(BF16)
        st = st_ref[h]
        if want_out:
            kt_h = (kt[:, lanes] * hm).astype(BF16)
            a = lax.dot_general(qt[:, lanes], kt_h, NT, preferred_element_type=F32)
            a = jnp.where(tri_mask, a, 0.0).astype(BF16)
            o = _dot(a, vh) + lax.dot_general(qg[:, lanes], st.astype(BF16), NT,
                                              preferred_element_type=F32)
            outs.append(o)
        st_ref[h] = st * e_last[:, lanes] + lax.dot_general(
            vh, kend_h, TN, preferred_element_type=F32)
    return outs


def _gla_kernel(q_ref, k_ref, v_ref, laf_ref, lab_ref, kc_ref, vc_ref, lafc_ref, labc_ref,
                rs_ref, g_ref, o_ref, of_ref, ob_ref, sf_ref, sb_ref):
    t = q_ref.shape[0]
    t_ctx = kc_ref.shape[0]
    n = t // CHUNK
    n_ctx = t_ctx // CHUNK

    row = lax.broadcasted_iota(jnp.int32, (CHUNK, CHUNK), 0)
    col = lax.broadcasted_iota(jnp.int32, (CHUNK, CHUNK), 1)
    low_mask = col <= row
    up_mask = col >= row
    low_b = jnp.where(low_mask, 1.0, 0.0).astype(BF16)
    up_b = jnp.where(up_mask, 1.0, 0.0).astype(BF16)
    lane = lax.broadcasted_iota(jnp.int32, (1, LANES), 1)
    head_lane = (jnp.where(lane < GLA_DK, 1.0, 0.0), jnp.where(lane >= GLA_DK, 1.0, 0.0))

    fwd = dict(tri_b=low_b, tri_mask=low_mask, mid=CHUNK // 2 - 1, last=CHUNK - 1,
               st_ref=sf_ref, head_lane=head_lane)
    bwd = dict(tri_b=up_b, tri_mask=up_mask, mid=CHUNK // 2, last=0,
               st_ref=sb_ref, head_lane=head_lane)

    sf_ref[...] = jnp.zeros_like(sf_ref)
    sb_ref[...] = jnp.zeros_like(sb_ref)

    def ctx_body(i, carry):
        t0 = pl.multiple_of(i * CHUNK, CHUNK)
        t1 = pl.multiple_of((n_ctx - 1 - i) * CHUNK, CHUNK)
        _gla_chunk(None, kc_ref[pl.ds(t0, CHUNK), :], vc_ref[pl.ds(t0, CHUNK), :],
                   lafc_ref[pl.ds(t0, CHUNK), :], want_out=False, **fwd)
        _gla_chunk(None, kc_ref[pl.ds(t1, CHUNK), :], vc_ref[pl.ds(t1, CHUNK), :],
                   labc_ref[pl.ds(t1, CHUNK), :], want_out=False, **bwd)
        return carry

    lax.fori_loop(0, n_ctx, ctx_body, 0)

    def body(i, carry):
        t0 = pl.multiple_of(i * CHUNK, CHUNK)
        t1 = pl.multiple_of((n - 1 - i) * CHUNK, CHUNK)
        of = _gla_chunk(q_ref[pl.ds(t0, CHUNK), :], k_ref[pl.ds(t0, CHUNK), :],
                        v_ref[pl.ds(t0, CHUNK), :], laf_ref[pl.ds(t0, CHUNK), :],
                        want_out=True, **fwd)
        ob = _gla_chunk(q_ref[pl.ds(t1, CHUNK), :], k_ref[pl.ds(t1, CHUNK), :],
                        v_ref[pl.ds(t1, CHUNK), :], lab_ref[pl.ds(t1, CHUNK), :],
                        want_out=True, **bwd)
        for h in range(GLA_HEADS):
            of_ref[pl.ds(t0, CHUNK), GLA_DV * h:GLA_DV * (h + 1)] = of[h]
            ob_ref[pl.ds(t1, CHUNK), GLA_DV * h:GLA_DV * (h + 1)] = ob[h]
        return carry

    lax.fori_loop(0, n, body, 0)

    tile = 256
    gn = g_ref[...]

    def fin(j, carry):
        r0 = pl.multiple_of(j * tile, tile)
        o = of_ref[pl.ds(r0, tile), :] + ob_ref[pl.ds(r0, tile), :]
        rs = rs_ref[pl.ds(r0, tile), :].astype(F32)
        for h in range(GLA_HEADS):
            cols = slice(GLA_DV * h, GLA_DV * (h + 1))
            o_ref[pl.ds(r0, tile), cols] = (_rms(o[:, cols], gn) * rs[:, cols]).astype(BF16)
        return carry

    lax.fori_loop(0, t // tile, fin, 0)


def _gla(q, k, v, laf, lab, kc, vc, lafc, labc, rs, gnorm):
    bsz, t, _ = q.shape

    def bspec(arr):
        return pl.BlockSpec((None,) + arr.shape[1:], lambda b: (b, 0, 0))

    args = (q, k, v, laf, lab, kc, vc, lafc, labc, rs)
    return pl.pallas_call(
        _gla_kernel,
        grid=(bsz,),
        in_specs=[bspec(a) for a in args] + [_const_spec(gnorm)],
        out_specs=pl.BlockSpec((None, t, HV), lambda b: (b, 0, 0)),
        out_shape=jax.ShapeDtypeStruct((bsz, t, HV), BF16),
        scratch_shapes=[
            pltpu.VMEM((t, HV), F32), pltpu.VMEM((t, HV), F32),
            pltpu.VMEM((GLA_HEADS, GLA_DV, LANES), F32),
            pltpu.VMEM((GLA_HEADS, GLA_DV, LANES), F32),
        ],
        compiler_params=pltpu.CompilerParams(
            dimension_semantics=("parallel",), vmem_limit_bytes=VMEM_LIMIT),
        name="gla",
    )(*args, gnorm)


def _conv_kernel(a_ref, w_ref, b_ref, lg_ref, lb_ref, o_ref, ph_ref, pv_ref):
    t = a_ref.shape[0]
    rows = t // GRID_W
    v_pad = CONV_PAD * GRID_W

    ph_ref[...] = jnp.zeros_like(ph_ref)
    pv_ref[pl.ds(0, v_pad), :] = jnp.zeros((v_pad, CONV_HALF), F32)
    pv_ref[pl.ds(v_pad + t, v_pad), :] = jnp.zeros((v_pad, CONV_HALF), F32)

    def fill(r, carry):
        t0 = pl.multiple_of(r * GRID_W, GRID_W)
        blk = a_ref[pl.ds(t0, GRID_W), :].astype(F32)
        ph_ref[pl.ds(pl.multiple_of(r * H_PITCH + H_OFF, 8), GRID_W), :] = blk[:, :CONV_HALF]
        pv_ref[pl.ds(pl.multiple_of(v_pad + t0, GRID_W), GRID_W), :] = blk[:, CONV_HALF:]
        return carry

    lax.fori_loop(0, rows, fill, 0)

    bias = b_ref[...]
    ln_g = lg_ref[...]
    ln_b = lb_ref[...]

    def row_body(r, carry):
        t0 = pl.multiple_of(r * GRID_W, GRID_W)
        xrow = ph_ref[pl.ds(pl.multiple_of(r * H_PITCH, 8), H_PITCH), :]
        acc_h = jnp.zeros((GRID_W, CONV_HALF), F32)
        acc_v = jnp.zeros((GRID_W, CONV_HALF), F32)
        for kk in range(CONV_K):
            start = H_OFF - CONV_PAD + kk
            win = pltpu.roll(xrow, H_PITCH - start, axis=0)[:GRID_W]
            acc_h = acc_h + w_ref[kk:kk + 1, :CONV_HALF] * win
            tap = pv_ref[pl.ds(pl.multiple_of(t0 + kk * GRID_W, GRID_W), GRID_W), :]
            acc_v = acc_v + w_ref[kk:kk + 1, CONV_HALF:] * tap
        y = jnp.concatenate([acc_h, acc_v], axis=-1) + bias
        mu = jnp.mean(y, axis=-1, keepdims=True)
        yc = y - mu
        var = jnp.mean(yc * yc, axis=-1, keepdims=True)
        yn = yc * lax.rsqrt(var + EPS) * ln_g + ln_b
        o_ref[pl.ds(t0, GRID_W), :] = (yn * _sigmoid(yn)).astype(BF16)
        return carry

    lax.fori_loop(0, rows, row_body, 0)


def _conv(a, w, b, ln_g, ln_b):
    bsz, t, c = a.shape
    rows = t // GRID_W
    return pl.pallas_call(
        _conv_kernel,
        grid=(bsz,),
        in_specs=[pl.BlockSpec((None, t, c), lambda i: (i, 0, 0)),
                  _const_spec(w), _const_spec(b), _const_spec(ln_g), _const_spec(ln_b)],
        out_specs=pl.BlockSpec((None, t, c), lambda i: (i, 0, 0)),
        out_shape=jax.ShapeDtypeStruct((bsz, t, c), BF16),
        scratch_shapes=[
            pltpu.VMEM((rows * H_PITCH, CONV_HALF), F32),
            pltpu.VMEM((t + 2 * CONV_PAD * GRID_W, CONV_HALF), F32),
        ],
        compiler_params=pltpu.CompilerParams(
            dimension_semantics=("parallel",), vmem_limit_bytes=VMEM_LIMIT),
        name="conv",
    )(a, w, b, ln_g, ln_b)


def _post_kernel(x_ref, ac_ref, og_ref, gt_ref, mod_ref, gpost1_ref, gpre2_ref, gpost2_ref,
                 wco_ref, wgo_ref, wout_ref, wff1_ref, wff2_ref, o_ref):
    d = x_ref.shape[-1]
    d_ff = wff1_ref.shape[1]
    ff_tile = 1024
    y_conv = _dot(ac_ref[...], wco_ref[...])
    y_gla = _dot(og_ref[...], wgo_ref[...])
    gates = gt_ref[...].astype(F32)
    merged = gates[:, :d] * y_conv + gates[:, d:] * y_gla
    y = _dot(merged.astype(BF16), wout_ref[...])
    gt1 = mod_ref[2:3, :]
    sh2 = mod_ref[3:4, :]
    sc2 = mod_ref[4:5, :]
    gt2 = mod_ref[5:6, :]
    h_mid = x_ref[...] + gt1 * _rms(y, gpost1_ref[...])
    u2 = (_rms(h_mid, gpre2_ref[...]) * (1.0 + sc2) + sh2).astype(BF16)
    acc = jnp.zeros(h_mid.shape, F32)
    for j in range(d_ff // ff_tile):
        f = _dot(u2, wff1_ref[:, j * ff_tile:(j + 1) * ff_tile])
        f = jnp.square(jnp.maximum(f, 0.0)).astype(BF16)
        acc = acc + _dot(f, wff2_ref[j * ff_tile:(j + 1) * ff_tile, :])
    o_ref[...] = h_mid + gt2 * _rms(acc, gpost2_ref[...])


def _post(x, ac, og, gates, mod, gpost1, gpre2, gpost2, wco, wgo, wout, wff1, wff2, tm):
    bsz, t, d = x.shape

    def wspec(arr):
        nd = arr.ndim
        return pl.BlockSpec(arr.shape, lambda *_: (0,) * nd, pipeline_mode=pl.Buffered(1))

    return pl.pallas_call(
        _post_kernel,
        grid=(bsz, t // tm),
        in_specs=[
            _tok_spec(tm, d), _tok_spec(tm, CONV_DIM), _tok_spec(tm, HV),
            _tok_spec(tm, 2 * d),
            pl.BlockSpec((None,) + mod.shape[1:], lambda b, i: (b, 0, 0)),
            _const_spec(gpost1), _const_spec(gpre2), _const_spec(gpost2),
            wspec(wco), wspec(wgo), wspec(wout), wspec(wff1), wspec(wff2),
        ],
        out_specs=_tok_spec(tm, d),
        out_shape=jax.ShapeDtypeStruct((bsz, t, d), F32),
        compiler_params=pltpu.CompilerParams(
            dimension_semantics=("parallel", "parallel"), vmem_limit_bytes=VMEM_LIMIT),
        name="post",
    )(x, ac, og, gates, mod, gpost1, gpre2, gpost2, wco, wgo, wout, wff1, wff2)


def kernel(x, c, ctx, c_ctx, w_mod, b_mod, g_pre1, g_post1, g_pre2, g_post2, w_in, conv_w,
           conv_b, conv_ln_g, conv_ln_b, w_conv_out, w_decay, b_decay, gla_norm_g, w_gla_out,
           w_out, w_ff1, w_ff2):
    bsz, t, d = x.shape
    depth = w_in.shape[0]
    assert depth == 1 and d == D_MODEL and t % (GRID_W * 8) == 0

    n_rows = -(-(bsz + 1) // 8) * 8
    c_all = jnp.zeros((n_rows, d), F32).at[:bsz].set(c).at[bsz].set(c_ctx)
    m = _adaln(c_all, w_mod[0], b_mod[0][None, :])
    mod = m[:bsz].reshape(bsz, N_MOD, d)
    mod_ctx = m[bsz, :2 * d].reshape(1, 2, d)

    w_in0 = w_in[0]
    wm = w_in0[:, :COL_DEC].astype(BF16)
    wdin = jnp.pad(w_in0[:, COL_DEC:COL_GATE], ((0, 0), (0, LANES - 2 * DECAY_RANK))).astype(BF16)
    wg = w_in0[:, COL_GATE:].astype(BF16)
    wkv = w_in0[:, COL_K:COL_R].astype(BF16)
    wdec = jnp.zeros((LANES, 2 * HK), F32)
    wdec = wdec.at[:DECAY_RANK, :HK].set(w_decay[0, 0])
    wdec = wdec.at[DECAY_RANK:2 * DECAY_RANK, HK:].set(w_decay[0, 1]).astype(BF16)
    bdec = b_decay[0].reshape(1, 2 * HK)
    row = lambda v: v.reshape(1, -1)

    a, q, k, v, rs, laf, lab, gates = _proj(
        x, mod, row(g_pre1[0]), wm, wdin, wdec, bdec, wg, tm=512)
    kc, vc, lafc, labc = _ctx_proj(
        ctx, mod_ctx, row(g_pre1[0]), wkv, wdin, wdec, bdec, tm=ctx.shape[1])

    og = _gla(q, k, v, laf, lab, kc, vc, lafc, labc, rs, row(gla_norm_g[0]))
    ac = _conv(a, conv_w[0], row(conv_b[0]), row(conv_ln_g[0]), row(conv_ln_b[0]))

    return _post(x, ac, og, gates, mod, row(g_post1[0]), row(g_pre2[0]), row(g_post2[0]),
                 w_conv_out[0].astype(BF16), w_gla_out[0].astype(BF16), w_out[0].astype(BF16),
                 w_ff1[0].astype(BF16), w_ff2[0].astype(BF16), tm=256)
```

```python
import jax
import jax.numpy as jnp
from jax import lax
from jax.experimental import pallas as pl
from jax.experimental.pallas import tpu as pltpu

D_MODEL = 1024
GRID_W = 64
CONV_DIM = 512
CONV_K = 31
GLA_HEADS = 4
GLA_DK = 64
GLA_DV = 128
DECAY_RANK = 16
GATE_NORM = 16.0
CHUNK = 64
N_MOD = 6
EPS = 1e-6

COL_Q = 2 * CONV_DIM
COL_K = COL_Q + GLA_HEADS * GLA_DK
COL_V = COL_K + GLA_HEADS * GLA_DK
COL_R = COL_V + GLA_HEADS * GLA_DV
COL_DEC = COL_R + GLA_HEADS * GLA_DV
COL_GATE = COL_DEC + 2 * DECAY_RANK
COL_END = COL_GATE + 2 * D_MODEL

LANES = 128
HK = GLA_HEADS * GLA_DK
HV = GLA_HEADS * GLA_DV
CONV_HALF = CONV_DIM // 2
CONV_PAD = CONV_K // 2
H_PITCH = GRID_W + 32
H_OFF = 16
SUPER = 2 * CHUNK
STEP = 2 * SUPER

VMEM_LIMIT = 56 * 1024 * 1024

F32 = jnp.float32
BF16 = jnp.bfloat16

NT = (((1,), (1,)), ((), ()))
TN = (((0,), (0,)), ((), ()))


def _dot(a, b):
    return jnp.dot(a, b, preferred_element_type=F32)


def _rms(x, g):
    ms = jnp.mean(x * x, axis=-1, keepdims=True)
    return x * lax.rsqrt(ms + EPS) * g


def _sigmoid(x):
    return jax.nn.sigmoid(x)


def _split_bf16(x):
    hi = x.astype(BF16)
    lo = (x - hi.astype(F32)).astype(BF16)
    return hi, lo


def _adaln_kernel(c_ref, w_ref, b_ref, o_ref):
    c = c_ref[...]
    s = c * _sigmoid(c)
    o_ref[...] = jnp.dot(s, w_ref[...], precision=lax.Precision.HIGHEST,
                         preferred_element_type=F32) + b_ref[...]


def _adaln(c_all, w_mod, b_mod):
    n_rows, d = c_all.shape
    n_out = w_mod.shape[1]
    tn = 1536
    return pl.pallas_call(
        _adaln_kernel,
        grid=(n_out // tn,),
        in_specs=[
            pl.BlockSpec((n_rows, d), lambda j: (0, 0)),
            pl.BlockSpec((d, tn), lambda j: (0, j)),
            pl.BlockSpec((1, tn), lambda j: (0, j)),
        ],
        out_specs=pl.BlockSpec((n_rows, tn), lambda j: (0, j)),
        out_shape=jax.ShapeDtypeStruct((n_rows, n_out), F32),
        compiler_params=pltpu.CompilerParams(
            dimension_semantics=("arbitrary",), vmem_limit_bytes=VMEM_LIMIT),
        name="adaln",
    )(c_all, w_mod, b_mod)


def _prenorm_mod(x, g, mod_ref):
    sh = mod_ref[0:1, :]
    sc = mod_ref[1:2, :]
    return (_rms(x, g) * (1.0 + sc) + sh).astype(BF16)


def _decay_logs(ub, wdin_ref, wdec_ref, bdec_ref):
    z = _dot(ub, wdin_ref[...])
    z_hi, z_lo = _split_bf16(z)
    logits = _dot(z_hi, wdec_ref[...]) + _dot(z_lo, wdec_ref[...]) + bdec_ref[...]
    ls = jnp.minimum(logits, 0.0) - jnp.log1p(jnp.exp(-jnp.abs(logits)))
    return ls * (1.0 / GATE_NORM)


def _proj_kernel(x_ref, mod_ref, g_ref, wm_ref, wdin_ref, wdec_ref, bdec_ref, wg_ref,
                 a_ref, q_ref, k_ref, v_ref, rs_ref, laf_ref, lab_ref, gt_ref):
    ub = _prenorm_mod(x_ref[...], g_ref[...], mod_ref)
    glu = _dot(ub, wm_ref[:, 0:COL_Q])
    a_ref[...] = (glu[:, :CONV_DIM] * _sigmoid(glu[:, CONV_DIM:])).astype(BF16)
    q_ref[...] = (_dot(ub, wm_ref[:, COL_Q:COL_K]) * (GLA_DK ** -0.5)).astype(BF16)
    k_ref[...] = _dot(ub, wm_ref[:, COL_K:COL_V]).astype(BF16)
    v_ref[...] = _dot(ub, wm_ref[:, COL_V:COL_R]).astype(BF16)
    r = _dot(ub, wm_ref[:, COL_R:COL_DEC])
    rs_ref[...] = (r * _sigmoid(r)).astype(BF16)
    la = _decay_logs(ub, wdin_ref, wdec_ref, bdec_ref)
    laf_ref[...] = la[:, :HK]
    lab_ref[...] = la[:, HK:]
    gt_ref[...] = _sigmoid(_dot(ub, wg_ref[...])).astype(BF16)


def _ctx_proj_kernel(x_ref, mod_ref, g_ref, wkv_ref, wdin_ref, wdec_ref, bdec_ref,
                     k_ref, v_ref, laf_ref, lab_ref):
    ub = _prenorm_mod(x_ref[...], g_ref[...], mod_ref)
    k_ref[...] = _dot(ub, wkv_ref[:, :HK]).astype(BF16)
    v_ref[...] = _dot(ub, wkv_ref[:, HK:]).astype(BF16)
    la = _decay_logs(ub, wdin_ref, wdec_ref, bdec_ref)
    laf_ref[...] = la[:, :HK]
    lab_ref[...] = la[:, HK:]


def _const_spec(arr):
    nd = arr.ndim
    return pl.BlockSpec(arr.shape, lambda *_: (0,) * nd)


def _tok_spec(tm, width):
    return pl.BlockSpec((None, tm, width), lambda b, i: (b, i, 0))


def _proj(x, mod, g, wm, wdin, wdec, bdec, wg, tm):
    bsz, t, d = x.shape
    widths = (CONV_DIM, HK, HK, HV, HV, HK, HK, 2 * D_MODEL)
    dtypes = (BF16, BF16, BF16, BF16, BF16, F32, F32, BF16)
    return pl.pallas_call(
        _proj_kernel,
        grid=(bsz, t // tm),
        in_specs=[
            _tok_spec(tm, d),
            pl.BlockSpec((None,) + mod.shape[1:], lambda b, i: (b, 0, 0)),
            _const_spec(g), _const_spec(wm), _const_spec(wdin), _const_spec(wdec),
            _const_spec(bdec), _const_spec(wg),
        ],
        out_specs=[_tok_spec(tm, w) for w in widths],
        out_shape=[jax.ShapeDtypeStruct((bsz, t, w), dt) for w, dt in zip(widths, dtypes)],
        compiler_params=pltpu.CompilerParams(
            dimension_semantics=("parallel", "parallel"), vmem_limit_bytes=VMEM_LIMIT),
        name="proj",
    )(x, mod, g, wm, wdin, wdec, bdec, wg)


def _ctx_proj(x, mod, g, wkv, wdin, wdec, bdec, tm):
    bsz, t, d = x.shape
    widths = (HK, HV, HK, HK)
    dtypes = (BF16, BF16, F32, F32)
    return pl.pallas_call(
        _ctx_proj_kernel,
        grid=(bsz, t // tm),
        in_specs=[
            _tok_spec(tm, d),
            pl.BlockSpec((None,) + mod.shape[1:], lambda b, i: (0, 0, 0)),
            _const_spec(g), _const_spec(wkv), _const_spec(wdin), _const_spec(wdec),
            _const_spec(bdec),
        ],
        out_specs=[_tok_spec(tm, w) for w in widths],
        out_shape=[jax.ShapeDtypeStruct((bsz, t, w), dt) for w, dt in zip(widths, dtypes)],
        compiler_params=pltpu.CompilerParams(
            dimension_semantics=("parallel", "parallel"), vmem_limit_bytes=VMEM_LIMIT),
        name="ctx_proj",
    )(x, mod, g, wkv, wdin, wdec, bdec)


def _gla_scale(qs, ks, las, dirn, want_out):
    tri_b, mid, last = dirn["tri_b"], dirn["mid"], dirn["last"]
    la_hi, la_lo = _split_bf16(las)
    g = _dot(tri_b, la_hi) + _dot(tri_b, la_lo)
    u = dict(kt=[], kend=[], qt=[], qg=[], e=[])
    for c in range(SUPER // CHUNK):
        rows = slice(c * CHUNK, (c + 1) * CHUNK)
        gc = g[rows]
        g_mid = gc[mid:mid + 1]
        g_last = gc[last:last + 1]
        u["e"].append(jnp.exp(g_last))
        kt = ks[rows].astype(F32) * jnp.exp(g_mid - gc)
        u["kt"].append(kt)
        u["kend"].append(kt * jnp.exp(g_last - g_mid))
        if want_out:
            qt = qs[rows].astype(F32) * jnp.exp(gc - g_mid)
            u["qt"].append(qt.astype(BF16))
            u["qg"].append((qt * jnp.exp(g_mid)).astype(BF16))
    return u


def _pair_lanes(pair):
    return slice(LANES * pair, LANES * (pair + 1))


def _gla_scores(u, pair, head_lane):
    lanes = _pair_lanes(pair)
    kt = jnp.concatenate([c[:, lanes] for c in u["kt"]], axis=0)
    qt = jnp.concatenate([c[:, lanes] for c in u["qt"]], axis=0)
    kt2 = jnp.concatenate([(kt * hm).astype(BF16) for hm in head_lane], axis=0)
    return lax.dot_general(qt, kt2, NT, preferred_element_type=F32)


def _gla_update(u, vs, h, head_lane):
    lanes = _pair_lanes(h // 2)
    hm = head_lane[h % 2]
    k0 = (u["kend"][0][:, lanes] * hm).astype(BF16)
    k1 = (u["kend"][1][:, lanes] * hm).astype(BF16)
    z = jnp.zeros_like(k0)
    kbd = jnp.concatenate([jnp.concatenate([k0, z], axis=1),
                           jnp.concatenate([z, k1], axis=1)], axis=0)
    return lax.dot_general(vs[:, GLA_DV * h:GLA_DV * (h + 1)], kbd, TN,
                           preferred_element_type=F32)


def _gla_intra(scores, vs, pair, mask2):
    a = jnp.where(mask2, scores, 0.0).astype(BF16)
    v0 = vs[:, GLA_DV * 2 * pair:GLA_DV * (2 * pair + 1)]
    v1 = vs[:, GLA_DV * (2 * pair + 1):GLA_DV * (2 * pair + 2)]
    z = jnp.zeros_like(v0)
    vbd = jnp.concatenate([jnp.concatenate([v0, z], axis=1),
                           jnp.concatenate([z, v1], axis=1)], axis=0)
    return _dot(a, vbd)


def _gla_kernel(q_ref, k_ref, v_ref, laf_ref, lab_ref, kc_ref, vc_ref, lafc_ref, labc_ref,
                rs_ref, g_ref, o_ref, of_ref, ob_ref, sf_ref, sb_ref):
    t = q_ref.shape[0]
    t_ctx = kc_ref.shape[0]
    half = t // 2
    n_steps = t // STEP
    supers = STEP // SUPER
    n_chunks = SUPER // CHUNK
    pairs = GLA_HEADS // 2

    row = lax.broadcasted_iota(jnp.int32, (SUPER, SUPER), 0)
    col = lax.broadcasted_iota(jnp.int32, (SUPER, SUPER), 1)
    same_chunk = (row // CHUNK) == (col // CHUNK)
    low_mask = jnp.logical_and(same_chunk, col <= row)
    up_mask = jnp.logical_and(same_chunk, col >= row)
    lane = lax.broadcasted_iota(jnp.int32, (1, LANES), 1)
    head_lane = (jnp.where(lane < GLA_DK, 1.0, 0.0), jnp.where(lane >= GLA_DK, 1.0, 0.0))

    fwd = dict(tri_b=jnp.where(low_mask, 1.0, 0.0).astype(BF16),
               mask2=jnp.concatenate([low_mask, low_mask], axis=1),
               mid=CHUNK // 2 - 1, last=CHUNK - 1, order=(0, 1), st_ref=sf_ref)
    bwd = dict(tri_b=jnp.where(up_mask, 1.0, 0.0).astype(BF16),
               mask2=jnp.concatenate([up_mask, up_mask], axis=1),
               mid=CHUNK // 2, last=0, order=(1, 0), st_ref=sb_ref)

    sf_ref[...] = jnp.zeros_like(sf_ref)
    sb_ref[...] = jnp.zeros_like(sb_ref)
    gn = g_ref[...]

    def run(units, want_out, finalize):
        vals = []
        for dirn, refs, t0 in units:
            q_r, k_r, v_r, la_r = refs
            rows = pl.ds(t0, SUPER)
            qs = q_r[rows, :] if want_out else None
            vals.append((_gla_scale(qs, k_r[rows, :], la_r[rows, :], dirn, want_out),
                         v_r[rows, :]))
        scores = []
        if want_out:
            scores = [[_gla_scores(u, p, head_lane) for p in range(pairs)] for u, _ in vals]
        upd = [[_gla_update(u, vs, h, head_lane) for h in range(GLA_HEADS)] for u, vs in vals]
        intra = []
        if want_out:
            intra = [[_gla_intra(scores[n][p], vals[n][1], p, units[n][0]["mask2"])
                      for p in range(pairs)] for n in range(len(units))]
        states = {}
        for n, (dirn, _, t0) in enumerate(units):
            key = id(dirn["st_ref"])
            if key not in states:
                states[key] = (dirn["st_ref"], [dirn["st_ref"][h] for h in range(GLA_HEADS)])
            st = states[key][1]
            u = vals[n][0]
            inter = [[None] * n_chunks for _ in range(pairs)]
            for c in dirn["order"]:
                if want_out:
                    for p in range(pairs):
                        st2 = jnp.concatenate([st[2 * p].astype(BF16),
                                               st[2 * p + 1].astype(BF16)], axis=0)
                        inter[p][c] = lax.dot_general(u["qg"][c][:, _pair_lanes(p)], st2, NT,
                                                      preferred_element_type=F32)
                for h in range(GLA_HEADS):
                    e = u["e"][c][:, _pair_lanes(h // 2)]
                    st[h] = st[h] * e + upd[n][h][:, LANES * c:LANES * (c + 1)]
            if not want_out:
                continue
            fwd_unit = dirn is fwd
            for p in range(pairs):
                o2 = intra[n][p] + jnp.concatenate(inter[p], axis=0)
                for j in range(2):
                    cols = slice(GLA_DV * (2 * p + j), GLA_DV * (2 * p + j + 1))
                    o = o2[:, GLA_DV * j:GLA_DV * (j + 1)]
                    if finalize:
                        other = (ob_ref[pl.ds(t0 - half, SUPER), cols] if fwd_unit
                                 else of_ref[pl.ds(t0, SUPER), cols])
                        rs = rs_ref[pl.ds(t0, SUPER), cols].astype(F32)
                        o_ref[pl.ds(t0, SUPER), cols] = (_rms(o + other, gn) * rs).astype(BF16)
                    elif fwd_unit:
                        of_ref[pl.ds(t0, SUPER), cols] = o
                    else:
                        ob_ref[pl.ds(t0 - half, SUPER), cols] = o
        for st_ref, st in states.values():
            for h in range(GLA_HEADS):
                st_ref[h] = st[h]

    ctx_units = []
    for s in range(t_ctx // SUPER):
        ctx_units.append((fwd, (None, kc_ref, vc_ref, lafc_ref), s * SUPER))
        ctx_units.append((bwd, (None, kc_ref, vc_ref, labc_ref), t_ctx - (s + 1) * SUPER))
    run(ctx_units, want_out=False, finalize=False)

    def step(i, finalize):
        units = []
        for s in range(supers):
            t0 = pl.multiple_of(i * STEP + s * SUPER, SUPER)
            t1 = pl.multiple_of(t - (i + 1) * STEP + (supers - 1 - s) * SUPER, SUPER)
            units.append((fwd, (q_ref, k_ref, v_ref, laf_ref), t0))
            units.append((bwd, (q_ref, k_ref, v_ref, lab_ref), t1))
        run(units, want_out=True, finalize=finalize)

    def park_body(i, carry):
        step(i, False)
        return carry

    def finish_body(i, carry):
        step(i, True)
        return carry

    lax.fori_loop(0, n_steps // 2, park_body, 0)
    lax.fori_loop(n_steps // 2, n_steps, finish_body, 0)


def _gla(q, k, v, laf, lab, kc, vc, lafc, labc, rs, gnorm):
    bsz, t, _ = q.shape
    assert t % (2 * STEP) == 0 and kc.shape[1] % SUPER == 0

    def bspec(arr):
        return pl.BlockSpec((None,) + arr.shape[1:], lambda b: (b, 0, 0))

    args = (q, k, v, laf, lab, kc, vc, lafc, labc, rs)
    return pl.pallas_call(
        _gla_kernel,
        grid=(bsz,),
        in_specs=[bspec(a) for a in args] + [_const_spec(gnorm)],
        out_specs=pl.BlockSpec((None, t, HV), lambda b: (b, 0, 0)),
        out_shape=jax.ShapeDtypeStruct((bsz, t, HV), BF16),
        scratch_shapes=[
            pltpu.VMEM((t // 2, HV), F32), pltpu.VMEM((t // 2, HV), F32),
            pltpu.VMEM((GLA_HEADS, GLA_DV, LANES), F32),
            pltpu.VMEM((GLA_HEADS, GLA_DV, LANES), F32),
        ],
        compiler_params=pltpu.CompilerParams(
            dimension_semantics=("parallel",), vmem_limit_bytes=VMEM_LIMIT),
        name="gla",
    )(*args, gnorm)


def _conv_kernel(a_ref, w_ref, b_ref, lg_ref, lb_ref, o_ref, ph_ref, pv_ref):
    t = a_ref.shape[0]
    rows = t // GRID_W
    v_pad = CONV_PAD * GRID_W

    ph_ref[...] = jnp.zeros_like(ph_ref)
    pv_ref[pl.ds(0, v_pad), :] = jnp.zeros((v_pad, CONV_HALF), F32)
    pv_ref[pl.ds(v_pad + t, v_pad), :] = jnp.zeros((v_pad, CONV_HALF), F32)

    def fill(r, carry):
        t0 = pl.multiple_of(r * GRID_W, GRID_W)
        blk = a_ref[pl.ds(t0, GRID_W), :].astype(F32)
        ph_ref[pl.ds(pl.multiple_of(r * H_PITCH + H_OFF, 8), GRID_W), :] = blk[:, :CONV_HALF]
        pv_ref[pl.ds(pl.multiple_of(v_pad + t0, GRID_W), GRID_W), :] = blk[:, CONV_HALF:]
        return carry

    lax.fori_loop(0, rows, fill, 0)

    bias = b_ref[...]
    ln_g = lg_ref[...]
    ln_b = lb_ref[...]

    def row_body(r, carry):
        t0 = pl.multiple_of(r * GRID_W, GRID_W)
        xrow = ph_ref[pl.ds(pl.multiple_of(r * H_PITCH, 8), H_PITCH), :]
        acc_h = jnp.zeros((GRID_W, CONV_HALF), F32)
        acc_v = jnp.zeros((GRID_W, CONV_HALF), F32)
        for kk in range(CONV_K):
            start = H_OFF - CONV_PAD + kk
            win = pltpu.roll(xrow, H_PITCH - start, axis=0)[:GRID_W]
            acc_h = acc_h + w_ref[kk:kk + 1, :CONV_HALF] * win
            tap = pv_ref[pl.ds(pl.multiple_of(t0 + kk * GRID_W, GRID_W), GRID_W), :]
            acc_v = acc_v + w_ref[kk:kk + 1, CONV_HALF:] * tap
        y = jnp.concatenate([acc_h, acc_v], axis=-1) + bias
        mu = jnp.mean(y, axis=-1, keepdims=True)
        yc = y - mu
        var = jnp.mean(yc * yc, axis=-1, keepdims=True)
        yn = yc * lax.rsqrt(var + EPS) * ln_g + ln_b
        o_ref[pl.ds(t0, GRID_W), :] = (yn * _sigmoid(yn)).astype(BF16)
        return carry

    lax.fori_loop(0, rows, row_body, 0)


def _conv(a, w, b, ln_g, ln_b):
    bsz, t, c = a.shape
    rows = t // GRID_W
    return pl.pallas_call(
        _conv_kernel,
        grid=(bsz,),
        in_specs=[pl.BlockSpec((None, t, c), lambda i: (i, 0, 0)),
                  _const_spec(w), _const_spec(b), _const_spec(ln_g), _const_spec(ln_b)],
        out_specs=pl.BlockSpec((None, t, c), lambda i: (i, 0, 0)),
        out_shape=jax.ShapeDtypeStruct((bsz, t, c), BF16),
        scratch_shapes=[
            pltpu.VMEM((rows * H_PITCH, CONV_HALF), F32),
            pltpu.VMEM((t + 2 * CONV_PAD * GRID_W, CONV_HALF), F32),
        ],
        compiler_params=pltpu.CompilerParams(
            dimension_semantics=("parallel",), vmem_limit_bytes=VMEM_LIMIT),
        name="conv",
    )(a, w, b, ln_g, ln_b)


def _post_kernel(x_ref, ac_ref, og_ref, gt_ref, mod_ref, gpost1_ref, gpre2_ref, gpost2_ref,
                 wco_ref, wgo_ref, wout_ref, wff1_ref, wff2_ref, o_ref):
    d = x_ref.shape[-1]
    d_ff = wff1_ref.shape[1]
    ff_tile = 1024
    y_conv = _dot(ac_ref[...], wco_ref[...])
    y_gla = _dot(og_ref[...], wgo_ref[...])
    gates = gt_ref[...].astype(F32)
    merged = gates[:, :d] * y_conv + gates[:, d:] * y_gla
    y = _dot(merged.astype(BF16), wout_ref[...])
    gt1 = mod_ref[2:3, :]
    sh2 = mod_ref[3:4, :]
    sc2 = mod_ref[4:5, :]
    gt2 = mod_ref[5:6, :]
    h_mid = x_ref[...] + gt1 * _rms(y, gpost1_ref[...])
    u2 = (_rms(h_mid, gpre2_ref[...]) * (1.0 + sc2) + sh2).astype(BF16)
    acc = jnp.zeros(h_mid.shape, F32)
    for j in range(d_ff // ff_tile):
        f = _dot(u2, wff1_ref[:, j * ff_tile:(j + 1) * ff_tile])
        f = jnp.square(jnp.maximum(f, 0.0)).astype(BF16)
        acc = acc + _dot(f, wff2_ref[j * ff_tile:(j + 1) * ff_tile, :])
    o_ref[...] = h_mid + gt2 * _rms(acc, gpost2_ref[...])


def _post(x, ac, og, gates, mod, gpost1, gpre2, gpost2, wco, wgo, wout, wff1, wff2, tm):
    bsz, t, d = x.shape

    def wspec(arr):
        nd = arr.ndim
        return pl.BlockSpec(arr.shape, lambda *_: (0,) * nd, pipeline_mode=pl.Buffered(1))

    return pl.pallas_call(
        _post_kernel,
        grid=(bsz, t // tm),
        in_specs=[
            _tok_spec(tm, d), _tok_spec(tm, CONV_DIM), _tok_spec(tm, HV),
            _tok_spec(tm, 2 * d),
            pl.BlockSpec((None,) + mod.shape[1:], lambda b, i: (b, 0, 0)),
            _const_spec(gpost1), _const_spec(gpre2), _const_spec(gpost2),
            wspec(wco), wspec(wgo), wspec(wout), wspec(wff1), wspec(wff2),
        ],
        out_specs=_tok_spec(tm, d),
        out_shape=jax.ShapeDtypeStruct((bsz, t, d), F32),
        compiler_params=pltpu.CompilerParams(
            dimension_semantics=("parallel", "parallel"), vmem_limit_bytes=VMEM_LIMIT),
        name="post",
    )(x, ac, og, gates, mod, gpost1, gpre2, gpost2, wco, wgo, wout, wff1, wff2)


def kernel(x, c, ctx, c_ctx, w_mod, b_mod, g_pre1, g_post1, g_pre2, g_post2, w_in, conv_w,
           conv_b, conv_ln_g, conv_ln_b, w_conv_out, w_decay, b_decay, gla_norm_g, w_gla_out,
           w_out, w_ff1, w_ff2):
    bsz, t, d = x.shape
    depth = w_in.shape[0]
    assert depth == 1 and d == D_MODEL and t % (GRID_W * 8) == 0

    n_rows = -(-(bsz + 1) // 8) * 8
    c_all = jnp.zeros((n_rows, d), F32).at[:bsz].set(c).at[bsz].set(c_ctx)
    m = _adaln(c_all, w_mod[0], b_mod[0][None, :])
    mod = m[:bsz].reshape(bsz, N_MOD, d)
    mod_ctx = m[bsz, :2 * d].reshape(1, 2, d)

    w_in0 = w_in[0]
    wm = w_in0[:, :COL_DEC].astype(BF16)
    wdin = jnp.pad(w_in0[:, COL_DEC:COL_GATE], ((0, 0), (0, LANES - 2 * DECAY_RANK))).astype(BF16)
    wg = w_in0[:, COL_GATE:].astype(BF16)
    wkv = w_in0[:, COL_K:COL_R].astype(BF16)
    wdec = jnp.zeros((LANES, 2 * HK), F32)
    wdec = wdec.at[:DECAY_RANK, :HK].set(w_decay[0, 0])
    wdec = wdec.at[DECAY_RANK:2 * DECAY_RANK, HK:].set(w_decay[0, 1]).astype(BF16)
    bdec = b_decay[0].reshape(1, 2 * HK)
    row = lambda v: v.reshape(1, -1)

    a, q, k, v, rs, laf, lab, gates = _proj(
        x, mod, row(g_pre1[0]), wm, wdin, wdec, bdec, wg, tm=512)
    kc, vc, lafc, labc = _ctx_proj(
        ctx, mod_ctx, row(g_pre1[0]), wkv, wdin, wdec, bdec, tm=ctx.shape[1])

    og = _gla(q, k, v, laf, lab, kc, vc, lafc, labc, rs, row(gla_norm_g[0]))
    ac = _conv(a, conv_w[0], row(conv_b[0]), row(conv_ln_g[0]), row(conv_ln_b[0]))

    return _post(x, ac, og, gates, mod, row(g_post1[0]), row(g_pre2[0]), row(g_post2[0]),
                 w_conv_out[0].astype(BF16), w_gla_out[0].astype(BF16), w_out[0].astype(BF16),
                 w_ff1[0].astype(BF16), w_ff2[0].astype(BF16), tm=256)
```

```python
import jax
import jax.numpy as jnp
from jax import lax
from jax.experimental import pallas as pl
from jax.experimental.pallas import tpu as pltpu

D_MODEL = 1024
GRID_W = 64
CONV_DIM = 512
CONV_K = 31
GLA_HEADS = 4
GLA_DK = 64
GLA_DV = 128
DECAY_RANK = 16
GATE_NORM = 16.0
CHUNK = 64
N_MOD = 6
EPS = 1e-6

COL_Q = 2 * CONV_DIM
COL_K = COL_Q + GLA_HEADS * GLA_DK
COL_V = COL_K + GLA_HEADS * GLA_DK
COL_R = COL_V + GLA_HEADS * GLA_DV
COL_DEC = COL_R + GLA_HEADS * GLA_DV
COL_GATE = COL_DEC + 2 * DECAY_RANK
COL_END = COL_GATE + 2 * D_MODEL

LANES = 128
HK = GLA_HEADS * GLA_DK
HV = GLA_HEADS * GLA_DV
CONV_HALF = CONV_DIM // 2
CONV_PAD = CONV_K // 2
SUPER = 2 * CHUNK
STEP = 2 * SUPER

VMEM_LIMIT = 56 * 1024 * 1024

F32 = jnp.float32
BF16 = jnp.bfloat16

NT = (((1,), (1,)), ((), ()))
TN = (((0,), (0,)), ((), ()))


def _dot(a, b):
    return jnp.dot(a, b, preferred_element_type=F32)


def _rms(x, g):
    ms = jnp.mean(x * x, axis=-1, keepdims=True)
    return x * lax.rsqrt(ms + EPS) * g


def _sigmoid(x):
    return jax.nn.sigmoid(x)


def _split_bf16(x):
    hi = x.astype(BF16)
    lo = (x - hi.astype(F32)).astype(BF16)
    return hi, lo


def _adaln_kernel(c_ref, w_ref, b_ref, o_ref):
    c = c_ref[...]
    s = c * _sigmoid(c)
    o_ref[...] = jnp.dot(s, w_ref[...], precision=lax.Precision.HIGHEST,
                         preferred_element_type=F32) + b_ref[...]


def _adaln(c_all, w_mod, b_mod):
    n_rows, d = c_all.shape
    n_out = w_mod.shape[1]
    tn = 1536
    return pl.pallas_call(
        _adaln_kernel,
        grid=(n_out // tn,),
        in_specs=[
            pl.BlockSpec((n_rows, d), lambda j: (0, 0)),
            pl.BlockSpec((d, tn), lambda j: (0, j)),
            pl.BlockSpec((1, tn), lambda j: (0, j)),
        ],
        out_specs=pl.BlockSpec((n_rows, tn), lambda j: (0, j)),
        out_shape=jax.ShapeDtypeStruct((n_rows, n_out), F32),
        compiler_params=pltpu.CompilerParams(
            dimension_semantics=("arbitrary",), vmem_limit_bytes=VMEM_LIMIT),
        name="adaln",
    )(c_all, w_mod, b_mod)


def _prenorm_mod(x, g, mod_ref):
    sh = mod_ref[0:1, :]
    sc = mod_ref[1:2, :]
    return (_rms(x, g) * (1.0 + sc) + sh).astype(BF16)


def _decay_logs(ub, wdin_ref, wdec_ref, bdec_ref):
    z = _dot(ub, wdin_ref[...])
    z_hi, z_lo = _split_bf16(z)
    logits = _dot(z_hi, wdec_ref[...]) + _dot(z_lo, wdec_ref[...]) + bdec_ref[...]
    ls = jnp.minimum(logits, 0.0) - jnp.log1p(jnp.exp(-jnp.abs(logits)))
    return ls * (1.0 / GATE_NORM)


def _proj_kernel(x_ref, mod_ref, g_ref, wm_ref, wdin_ref, wdec_ref, bdec_ref, wg_ref,
                 ah_ref, av_ref, q_ref, k_ref, v_ref, rs_ref, laf_ref, lab_ref, gt_ref):
    ub = _prenorm_mod(x_ref[...], g_ref[...], mod_ref)
    glu = _dot(ub, wm_ref[:, 0:COL_Q])
    a = (glu[:, :CONV_DIM] * _sigmoid(glu[:, CONV_DIM:])).astype(BF16)
    av_ref[...] = a[:, CONV_HALF:]
    for r in range(a.shape[0] // GRID_W):
        ah_ref[:, CONV_HALF * r:CONV_HALF * (r + 1)] = a[GRID_W * r:GRID_W * (r + 1), :CONV_HALF]
    q_ref[...] = (_dot(ub, wm_ref[:, COL_Q:COL_K]) * (GLA_DK ** -0.5)).astype(BF16)
    k_ref[...] = _dot(ub, wm_ref[:, COL_K:COL_V]).astype(BF16)
    v_ref[...] = _dot(ub, wm_ref[:, COL_V:COL_R]).astype(BF16)
    r = _dot(ub, wm_ref[:, COL_R:COL_DEC])
    rs_ref[...] = (r * _sigmoid(r)).astype(BF16)
    la = _decay_logs(ub, wdin_ref, wdec_ref, bdec_ref)
    laf_ref[...] = la[:, :HK]
    lab_ref[...] = la[:, HK:]
    gt_ref[...] = _sigmoid(_dot(ub, wg_ref[...])).astype(BF16)


def _ctx_proj_kernel(x_ref, mod_ref, g_ref, wkv_ref, wdin_ref, wdec_ref, bdec_ref,
                     k_ref, v_ref, laf_ref, lab_ref):
    ub = _prenorm_mod(x_ref[...], g_ref[...], mod_ref)
    k_ref[...] = _dot(ub, wkv_ref[:, :HK]).astype(BF16)
    v_ref[...] = _dot(ub, wkv_ref[:, HK:]).astype(BF16)
    la = _decay_logs(ub, wdin_ref, wdec_ref, bdec_ref)
    laf_ref[...] = la[:, :HK]
    lab_ref[...] = la[:, HK:]


def _const_spec(arr):
    nd = arr.ndim
    return pl.BlockSpec(arr.shape, lambda *_: (0,) * nd)


def _tok_spec(tm, width):
    return pl.BlockSpec((None, tm, width), lambda b, i: (b, i, 0))


def _proj(x, mod, g, wm, wdin, wdec, bdec, wg, tm):
    bsz, t, d = x.shape
    widths = (CONV_HALF, HK, HK, HV, HV, HK, HK, 2 * D_MODEL)
    dtypes = (BF16, BF16, BF16, BF16, BF16, F32, F32, BF16)
    assert tm % GRID_W == 0
    ah_width = (tm // GRID_W) * CONV_HALF
    ah_spec = pl.BlockSpec((None, GRID_W, ah_width), lambda b, i: (b, 0, i))
    ah_shape = jax.ShapeDtypeStruct((bsz, GRID_W, (t // GRID_W) * CONV_HALF), BF16)
    return pl.pallas_call(
        _proj_kernel,
        grid=(bsz, t // tm),
        in_specs=[
            _tok_spec(tm, d),
            pl.BlockSpec((None,) + mod.shape[1:], lambda b, i: (b, 0, 0)),
            _const_spec(g), _const_spec(wm), _const_spec(wdin), _const_spec(wdec),
            _const_spec(bdec), _const_spec(wg),
        ],
        out_specs=[ah_spec] + [_tok_spec(tm, w) for w in widths],
        out_shape=[ah_shape] + [jax.ShapeDtypeStruct((bsz, t, w), dt)
                                for w, dt in zip(widths, dtypes)],
        compiler_params=pltpu.CompilerParams(
            dimension_semantics=("parallel", "parallel"), vmem_limit_bytes=VMEM_LIMIT),
        name="proj",
    )(x, mod, g, wm, wdin, wdec, bdec, wg)


def _ctx_proj(x, mod, g, wkv, wdin, wdec, bdec, tm):
    bsz, t, d = x.shape
    widths = (HK, HV, HK, HK)
    dtypes = (BF16, BF16, F32, F32)
    return pl.pallas_call(
        _ctx_proj_kernel,
        grid=(bsz, t // tm),
        in_specs=[
            _tok_spec(tm, d),
            pl.BlockSpec((None,) + mod.shape[1:], lambda b, i: (0, 0, 0)),
            _const_spec(g), _const_spec(wkv), _const_spec(wdin), _const_spec(wdec),
            _const_spec(bdec),
        ],
        out_specs=[_tok_spec(tm, w) for w in widths],
        out_shape=[jax.ShapeDtypeStruct((bsz, t, w), dt) for w, dt in zip(widths, dtypes)],
        compiler_params=pltpu.CompilerParams(
            dimension_semantics=("parallel", "parallel"), vmem_limit_bytes=VMEM_LIMIT),
        name="ctx_proj",
    )(x, mod, g, wkv, wdin, wdec, bdec)


def _gla_scale(qs, ks, las, dirn, want_out):
    tri_b, mid, last = dirn["tri_b"], dirn["mid"], dirn["last"]
    la_hi, la_lo = _split_bf16(las)
    g = _dot(tri_b, la_hi) + _dot(tri_b, la_lo)
    u = dict(kt=[], kend=[], qt=[], qg=[], e=[])
    for c in range(SUPER // CHUNK):
        rows = slice(c * CHUNK, (c + 1) * CHUNK)
        gc = g[rows]
        g_mid = gc[mid:mid + 1]
        g_last = gc[last:last + 1]
        u["e"].append(jnp.exp(g_last))
        kt = ks[rows].astype(F32) * jnp.exp(g_mid - gc)
        u["kt"].append(kt)
        u["kend"].append(kt * jnp.exp(g_last - g_mid))
        if want_out:
            qt = qs[rows].astype(F32) * jnp.exp(gc - g_mid)
            u["qt"].append(qt.astype(BF16))
            u["qg"].append((qt * jnp.exp(g_mid)).astype(BF16))
    return u


def _pair_lanes(pair):
    return slice(LANES * pair, LANES * (pair + 1))


def _gla_scores(u, pair, head_lane):
    lanes = _pair_lanes(pair)
    kt = jnp.concatenate([c[:, lanes] for c in u["kt"]], axis=0)
    qt = jnp.concatenate([c[:, lanes] for c in u["qt"]], axis=0)
    kt2 = jnp.concatenate([(kt * hm).astype(BF16) for hm in head_lane], axis=0)
    return lax.dot_general(qt, kt2, NT, preferred_element_type=F32)


def _gla_update(u, vs, h, head_lane):
    lanes = _pair_lanes(h // 2)
    hm = head_lane[h % 2]
    k0 = (u["kend"][0][:, lanes] * hm).astype(BF16)
    k1 = (u["kend"][1][:, lanes] * hm).astype(BF16)
    z = jnp.zeros_like(k0)
    kbd = jnp.concatenate([jnp.concatenate([k0, z], axis=1),
                           jnp.concatenate([z, k1], axis=1)], axis=0)
    return lax.dot_general(vs[:, GLA_DV * h:GLA_DV * (h + 1)], kbd, TN,
                           preferred_element_type=F32)


def _gla_intra(scores, vs, pair, mask2):
    a = jnp.where(mask2, scores, 0.0).astype(BF16)
    v0 = vs[:, GLA_DV * 2 * pair:GLA_DV * (2 * pair + 1)]
    v1 = vs[:, GLA_DV * (2 * pair + 1):GLA_DV * (2 * pair + 2)]
    z = jnp.zeros_like(v0)
    vbd = jnp.concatenate([jnp.concatenate([v0, z], axis=1),
                           jnp.concatenate([z, v1], axis=1)], axis=0)
    return _dot(a, vbd)


def _gla_kernel(q_ref, k_ref, v_ref, laf_ref, lab_ref, kc_ref, vc_ref, lafc_ref, labc_ref,
                rs_ref, g_ref, o_ref, of_ref, ob_ref, sf_ref, sb_ref):
    t = q_ref.shape[0]
    t_ctx = kc_ref.shape[0]
    half = t // 2
    n_steps = t // STEP
    supers = STEP // SUPER
    n_chunks = SUPER // CHUNK
    pairs = GLA_HEADS // 2

    row = lax.broadcasted_iota(jnp.int32, (SUPER, SUPER), 0)
    col = lax.broadcasted_iota(jnp.int32, (SUPER, SUPER), 1)
    same_chunk = (row // CHUNK) == (col // CHUNK)
    low_mask = jnp.logical_and(same_chunk, col <= row)
    up_mask = jnp.logical_and(same_chunk, col >= row)
    lane = lax.broadcasted_iota(jnp.int32, (1, LANES), 1)
    head_lane = (jnp.where(lane < GLA_DK, 1.0, 0.0), jnp.where(lane >= GLA_DK, 1.0, 0.0))

    fwd = dict(tri_b=jnp.where(low_mask, 1.0, 0.0).astype(BF16),
               mask2=jnp.concatenate([low_mask, low_mask], axis=1),
               mid=CHUNK // 2 - 1, last=CHUNK - 1, order=(0, 1), st_ref=sf_ref)
    bwd = dict(tri_b=jnp.where(up_mask, 1.0, 0.0).astype(BF16),
               mask2=jnp.concatenate([up_mask, up_mask], axis=1),
               mid=CHUNK // 2, last=0, order=(1, 0), st_ref=sb_ref)

    sf_ref[...] = jnp.zeros_like(sf_ref)
    sb_ref[...] = jnp.zeros_like(sb_ref)
    gn = g_ref[...]

    def run(units, want_out, finalize):
        vals = []
        for dirn, refs, t0 in units:
            q_r, k_r, v_r, la_r = refs
            rows = pl.ds(t0, SUPER)
            qs = q_r[rows, :] if want_out else None
            vals.append((_gla_scale(qs, k_r[rows, :], la_r[rows, :], dirn, want_out),
                         v_r[rows, :]))
        scores = []
        if want_out:
            scores = [[_gla_scores(u, p, head_lane) for p in range(pairs)] for u, _ in vals]
        upd = [[_gla_update(u, vs, h, head_lane) for h in range(GLA_HEADS)] for u, vs in vals]
        intra = []
        if want_out:
            intra = [[_gla_intra(scores[n][p], vals[n][1], p, units[n][0]["mask2"])
                      for p in range(pairs)] for n in range(len(units))]
        states = {}
        for n, (dirn, _, t0) in enumerate(units):
            key = id(dirn["st_ref"])
            if key not in states:
                states[key] = (dirn["st_ref"], [dirn["st_ref"][h] for h in range(GLA_HEADS)])
            st = states[key][1]
            u = vals[n][0]
            inter = [[None] * n_chunks for _ in range(pairs)]
            for c in dirn["order"]:
                if want_out:
                    for p in range(pairs):
                        st2 = jnp.concatenate([st[2 * p].astype(BF16),
                                               st[2 * p + 1].astype(BF16)], axis=0)
                        inter[p][c] = lax.dot_general(u["qg"][c][:, _pair_lanes(p)], st2, NT,
                                                      preferred_element_type=F32)
                for h in range(GLA_HEADS):
                    e = u["e"][c][:, _pair_lanes(h // 2)]
                    st[h] = st[h] * e + upd[n][h][:, LANES * c:LANES * (c + 1)]
            if not want_out:
                continue
            fwd_unit = dirn is fwd
            for p in range(pairs):
                o2 = intra[n][p] + jnp.concatenate(inter[p], axis=0)
                for j in range(2):
                    cols = slice(GLA_DV * (2 * p + j), GLA_DV * (2 * p + j + 1))
                    o = o2[:, GLA_DV * j:GLA_DV * (j + 1)]
                    if finalize:
                        other = (ob_ref[pl.ds(t0 - half, SUPER), cols] if fwd_unit
                                 else of_ref[pl.ds(t0, SUPER), cols])
                        rs = rs_ref[pl.ds(t0, SUPER), cols].astype(F32)
                        o_ref[pl.ds(t0, SUPER), cols] = (_rms(o + other, gn) * rs).astype(BF16)
                    elif fwd_unit:
                        of_ref[pl.ds(t0, SUPER), cols] = o
                    else:
                        ob_ref[pl.ds(t0 - half, SUPER), cols] = o
        for st_ref, st in states.values():
            for h in range(GLA_HEADS):
                st_ref[h] = st[h]

    ctx_units = []
    for s in range(t_ctx // SUPER):
        ctx_units.append((fwd, (None, kc_ref, vc_ref, lafc_ref), s * SUPER))
        ctx_units.append((bwd, (None, kc_ref, vc_ref, labc_ref), t_ctx - (s + 1) * SUPER))
    run(ctx_units, want_out=False, finalize=False)

    def step(i, finalize):
        units = []
        for s in range(supers):
            t0 = pl.multiple_of(i * STEP + s * SUPER, SUPER)
            t1 = pl.multiple_of(t - (i + 1) * STEP + (supers - 1 - s) * SUPER, SUPER)
            units.append((fwd, (q_ref, k_ref, v_ref, laf_ref), t0))
            units.append((bwd, (q_ref, k_ref, v_ref, lab_ref), t1))
        run(units, want_out=True, finalize=finalize)

    def park_body(i, carry):
        step(i, False)
        return carry

    def finish_body(i, carry):
        step(i, True)
        return carry

    lax.fori_loop(0, n_steps // 2, park_body, 0)
    lax.fori_loop(n_steps // 2, n_steps, finish_body, 0)


def _gla(q, k, v, laf, lab, kc, vc, lafc, labc, rs, gnorm):
    bsz, t, _ = q.shape
    assert t % (2 * STEP) == 0 and kc.shape[1] % SUPER == 0

    def bspec(arr):
        return pl.BlockSpec((None,) + arr.shape[1:], lambda b: (b, 0, 0))

    args = (q, k, v, laf, lab, kc, vc, lafc, labc, rs)
    return pl.pallas_call(
        _gla_kernel,
        grid=(bsz,),
        in_specs=[bspec(a) for a in args] + [_const_spec(gnorm)],
        out_specs=pl.BlockSpec((None, t, HV), lambda b: (b, 0, 0)),
        out_shape=jax.ShapeDtypeStruct((bsz, t, HV), BF16),
        scratch_shapes=[
            pltpu.VMEM((t // 2, HV), F32), pltpu.VMEM((t // 2, HV), F32),
            pltpu.VMEM((GLA_HEADS, GLA_DV, LANES), F32),
            pltpu.VMEM((GLA_HEADS, GLA_DV, LANES), F32),
        ],
        compiler_params=pltpu.CompilerParams(
            dimension_semantics=("parallel",), vmem_limit_bytes=VMEM_LIMIT),
        name="gla",
    )(*args, gnorm)


def _conv_kernel(ah_ref, av_ref, w_ref, b_ref, yh_ref, yv_ref, ph_ref, pv_ref):
    t = ah_ref.shape[0]
    rows = t // GRID_W
    tile = 64

    def conv(src_ref, pad_ref, w_cols, line, dst_ref):
        pad = CONV_PAD * line
        pad_ref[pl.ds(0, pad), :] = jnp.zeros((pad, CONV_HALF), F32)
        pad_ref[pl.ds(pad + t, pad), :] = jnp.zeros((pad, CONV_HALF), F32)

        def fill(i, carry):
            t0 = pl.multiple_of(i * tile, tile)
            pad_ref[pl.ds(pl.multiple_of(pad + t0, 8), tile), :] = (
                src_ref[pl.ds(t0, tile), :].astype(F32))
            return carry

        lax.fori_loop(0, t // tile, fill, 0)
        bias = b_ref[:, w_cols]

        def body(i, carry):
            t0 = pl.multiple_of(i * tile, tile)
            acc = jnp.broadcast_to(bias, (tile, CONV_HALF))
            for kk in range(CONV_K):
                win = pad_ref[pl.ds(pl.multiple_of(t0 + kk * line, 8), tile), :]
                acc = acc + w_ref[kk:kk + 1, w_cols] * win
            dst_ref[pl.ds(t0, tile), :] = acc
            return carry

        lax.fori_loop(0, t // tile, body, 0)

    conv(ah_ref, ph_ref, slice(0, CONV_HALF), rows, yh_ref)
    conv(av_ref, pv_ref, slice(CONV_HALF, CONV_DIM), GRID_W, yv_ref)


def _conv(ah, av, w, b):
    bsz, t, c = ah.shape
    rows = t // GRID_W
    tok = pl.BlockSpec((None, t, c), lambda i: (i, 0, 0))
    return pl.pallas_call(
        _conv_kernel,
        grid=(bsz,),
        in_specs=[tok, tok, _const_spec(w), _const_spec(b)],
        out_specs=[tok, tok],
        out_shape=[jax.ShapeDtypeStruct((bsz, t, c), F32)] * 2,
        scratch_shapes=[
            pltpu.VMEM((t + 2 * CONV_PAD * rows, c), F32),
            pltpu.VMEM((t + 2 * CONV_PAD * GRID_W, c), F32),
        ],
        compiler_params=pltpu.CompilerParams(
            dimension_semantics=("parallel",), vmem_limit_bytes=VMEM_LIMIT),
        name="conv",
    )(ah, av, w, b)


def _post_kernel(x_ref, yh_ref, yv_ref, og_ref, gt_ref, mod_ref, lng_ref, lnb_ref, gpost1_ref,
                 gpre2_ref, gpost2_ref, wco_ref, wgo_ref, wout_ref, wff1_ref, wff2_ref, o_ref):
    d = x_ref.shape[-1]
    d_ff = wff1_ref.shape[1]
    ff_tile = 1024
    yh = jnp.concatenate([yh_ref[:, CONV_HALF * r:CONV_HALF * (r + 1)]
                          for r in range(x_ref.shape[0] // GRID_W)], axis=0)
    yc = jnp.concatenate([yh, yv_ref[...]], axis=-1)
    yc = yc - jnp.mean(yc, axis=-1, keepdims=True)
    var = jnp.mean(yc * yc, axis=-1, keepdims=True)
    yn = yc * lax.rsqrt(var + EPS) * lng_ref[...] + lnb_ref[...]
    y_conv = _dot((yn * _sigmoid(yn)).astype(BF16), wco_ref[...])
    y_gla = _dot(og_ref[...], wgo_ref[...])
    gates = gt_ref[...].astype(F32)
    merged = gates[:, :d] * y_conv + gates[:, d:] * y_gla
    y = _dot(merged.astype(BF16), wout_ref[...])
    gt1 = mod_ref[2:3, :]
    sh2 = mod_ref[3:4, :]
    sc2 = mod_ref[4:5, :]
    gt2 = mod_ref[5:6, :]
    h_mid = x_ref[...] + gt1 * _rms(y, gpost1_ref[...])
    u2 = (_rms(h_mid, gpre2_ref[...]) * (1.0 + sc2) + sh2).astype(BF16)
    acc = jnp.zeros(h_mid.shape, F32)
    for j in range(d_ff // ff_tile):
        f = _dot(u2, wff1_ref[:, j * ff_tile:(j + 1) * ff_tile])
        f = jnp.square(jnp.maximum(f, 0.0)).astype(BF16)
        acc = acc + _dot(f, wff2_ref[j * ff_tile:(j + 1) * ff_tile, :])
    o_ref[...] = h_mid + gt2 * _rms(acc, gpost2_ref[...])


def _post(x, yh, yv, og, gates, mod, ln_g, ln_b, gpost1, gpre2, gpost2, wco, wgo, wout, wff1,
          wff2, tm):
    bsz, t, d = x.shape
    assert tm % GRID_W == 0
    yh = yh.reshape(bsz, GRID_W, (t // GRID_W) * CONV_HALF)
    yh_spec = pl.BlockSpec((None, GRID_W, (tm // GRID_W) * CONV_HALF), lambda b, i: (b, 0, i))

    def wspec(arr):
        nd = arr.ndim
        return pl.BlockSpec(arr.shape, lambda *_: (0,) * nd, pipeline_mode=pl.Buffered(1))

    return pl.pallas_call(
        _post_kernel,
        grid=(bsz, t // tm),
        in_specs=[
            _tok_spec(tm, d), yh_spec, _tok_spec(tm, CONV_HALF), _tok_spec(tm, HV),
            _tok_spec(tm, 2 * d),
            pl.BlockSpec((None,) + mod.shape[1:], lambda b, i: (b, 0, 0)),
            _const_spec(ln_g), _const_spec(ln_b),
            _const_spec(gpost1), _const_spec(gpre2), _const_spec(gpost2),
            wspec(wco), wspec(wgo), wspec(wout), wspec(wff1), wspec(wff2),
        ],
        out_specs=_tok_spec(tm, d),
        out_shape=jax.ShapeDtypeStruct((bsz, t, d), F32),
        compiler_params=pltpu.CompilerParams(
            dimension_semantics=("parallel", "parallel"), vmem_limit_bytes=VMEM_LIMIT),
        name="post",
    )(x, yh, yv, og, gates, mod, ln_g, ln_b, gpost1, gpre2, gpost2, wco, wgo, wout, wff1, wff2)


def kernel(x, c, ctx, c_ctx, w_mod, b_mod, g_pre1, g_post1, g_pre2, g_post2, w_in, conv_w,
           conv_b, conv_ln_g, conv_ln_b, w_conv_out, w_decay, b_decay, gla_norm_g, w_gla_out,
           w_out, w_ff1, w_ff2):
    bsz, t, d = x.shape
    depth = w_in.shape[0]
    assert depth == 1 and d == D_MODEL and t % (GRID_W * 8) == 0

    n_rows = -(-(bsz + 1) // 8) * 8
    c_all = jnp.zeros((n_rows, d), F32).at[:bsz].set(c).at[bsz].set(c_ctx)
    m = _adaln(c_all, w_mod[0], b_mod[0][None, :])
    mod = m[:bsz].reshape(bsz, N_MOD, d)
    mod_ctx = m[bsz, :2 * d].reshape(1, 2, d)

    w_in0 = w_in[0]
    wm = w_in0[:, :COL_DEC].astype(BF16)
    wdin = jnp.pad(w_in0[:, COL_DEC:COL_GATE], ((0, 0), (0, LANES - 2 * DECAY_RANK))).astype(BF16)
    wg = w_in0[:, COL_GATE:].astype(BF16)
    wkv = w_in0[:, COL_K:COL_R].astype(BF16)
    wdec = jnp.zeros((LANES, 2 * HK), F32)
    wdec = wdec.at[:DECAY_RANK, :HK].set(w_decay[0, 0])
    wdec = wdec.at[DECAY_RANK:2 * DECAY_RANK, HK:].set(w_decay[0, 1]).astype(BF16)
    bdec = b_decay[0].reshape(1, 2 * HK)
    row = lambda v: v.reshape(1, -1)

    ah, av, q, k, v, rs, laf, lab, gates = _proj(
        x, mod, row(g_pre1[0]), wm, wdin, wdec, bdec, wg, tm=512)
    kc, vc, lafc, labc = _ctx_proj(
        ctx, mod_ctx, row(g_pre1[0]), wkv, wdin, wdec, bdec, tm=ctx.shape[1])

    og = _gla(q, k, v, laf, lab, kc, vc, lafc, labc, rs, row(gla_norm_g[0]))
    yh, yv = _conv(ah.reshape(bsz, t, CONV_HALF), av, conv_w[0], row(conv_b[0]))

    return _post(x, yh, yv, og, gates, mod, row(conv_ln_g[0]), row(conv_ln_b[0]),
                 row(g_post1[0]), row(g_pre2[0]), row(g_post2[0]),
                 w_conv_out[0].astype(BF16), w_gla_out[0].astype(BF16), w_out[0].astype(BF16),
                 w_ff1[0].astype(BF16), w_ff2[0].astype(BF16), tm=256)
```

```python
import jax
import jax.numpy as jnp
import numpy as np
from jax import lax
from jax.experimental import pallas as pl
from jax.experimental.pallas import tpu as pltpu

D_MODEL = 1024
GRID_W = 64
CONV_DIM = 512
CONV_K = 31
GLA_HEADS = 4
GLA_DK = 64
GLA_DV = 128
DECAY_RANK = 16
GATE_NORM = 16.0
CHUNK = 64
N_MOD = 6
EPS = 1e-6

COL_Q = 2 * CONV_DIM
COL_K = COL_Q + GLA_HEADS * GLA_DK
COL_V = COL_K + GLA_HEADS * GLA_DK
COL_R = COL_V + GLA_HEADS * GLA_DV
COL_DEC = COL_R + GLA_HEADS * GLA_DV
COL_GATE = COL_DEC + 2 * DECAY_RANK
COL_END = COL_GATE + 2 * D_MODEL

LANES = 128
HK = GLA_HEADS * GLA_DK
HV = GLA_HEADS * GLA_DV
CONV_HALF = CONV_DIM // 2
CONV_PAD = CONV_K // 2
PERM_ROWS = 8
PERM_TOKENS = PERM_ROWS * GRID_W
SUPER = 2 * CHUNK
STEP = 2 * SUPER

VMEM_LIMIT = 56 * 1024 * 1024

F32 = jnp.float32
BF16 = jnp.bfloat16

NT = (((1,), (1,)), ((), ()))
TN = (((0,), (0,)), ((), ()))


def _dot(a, b):
    return jnp.dot(a, b, preferred_element_type=F32)


def _rms(x, g):
    ms = jnp.mean(x * x, axis=-1, keepdims=True)
    return x * lax.rsqrt(ms + EPS) * g


def _sigmoid(x):
    return jax.nn.sigmoid(x)


def _split_bf16(x):
    hi = x.astype(BF16)
    lo = (x - hi.astype(F32)).astype(BF16)
    return hi, lo


def _to_colmajor_perm():
    out = np.arange(PERM_TOKENS)
    src = (out % PERM_ROWS) * GRID_W + out // PERM_ROWS
    p = np.zeros((PERM_TOKENS, PERM_TOKENS), np.float32)
    p[out, src] = 1.0
    return p


def _adaln_kernel(c_ref, w_ref, b_ref, o_ref):
    c = c_ref[...]
    s = c * _sigmoid(c)
    o_ref[...] = jnp.dot(s, w_ref[...], precision=lax.Precision.HIGHEST,
                         preferred_element_type=F32) + b_ref[...]


def _adaln(c_all, w_mod, b_mod):
    n_rows, d = c_all.shape
    n_out = w_mod.shape[1]
    tn = 1536
    return pl.pallas_call(
        _adaln_kernel,
        grid=(n_out // tn,),
        in_specs=[
            pl.BlockSpec((n_rows, d), lambda j: (0, 0)),
            pl.BlockSpec((d, tn), lambda j: (0, j)),
            pl.BlockSpec((1, tn), lambda j: (0, j)),
        ],
        out_specs=pl.BlockSpec((n_rows, tn), lambda j: (0, j)),
        out_shape=jax.ShapeDtypeStruct((n_rows, n_out), F32),
        compiler_params=pltpu.CompilerParams(
            dimension_semantics=("arbitrary",), vmem_limit_bytes=VMEM_LIMIT),
        name="adaln",
    )(c_all, w_mod, b_mod)


def _prenorm_mod(x, g, mod_ref):
    sh = mod_ref[0:1, :]
    sc = mod_ref[1:2, :]
    return (_rms(x, g) * (1.0 + sc) + sh).astype(BF16)


def _decay_logs(ub, wdin_ref, wdec_ref, bdec_ref):
    z = _dot(ub, wdin_ref[...])
    z_hi, z_lo = _split_bf16(z)
    logits = _dot(z_hi, wdec_ref[...]) + _dot(z_lo, wdec_ref[...]) + bdec_ref[...]
    ls = jnp.minimum(logits, 0.0) - jnp.log1p(jnp.exp(-jnp.abs(logits)))
    return ls * (1.0 / GATE_NORM)


def _proj_kernel(x_ref, mod_ref, g_ref, perm_ref, wm_ref, wdin_ref, wdec_ref, bdec_ref, wg_ref,
                 ah_ref, av_ref, q_ref, k_ref, v_ref, rs_ref, laf_ref, lab_ref, gt_ref):
    subs = [slice(s, s + PERM_TOKENS) for s in range(0, x_ref.shape[0], PERM_TOKENS)]
    g = g_ref[...]
    ub = [_prenorm_mod(x_ref[s, :], g, mod_ref) for s in subs]
    for n, s in enumerate(subs):
        glu = _dot(ub[n], wm_ref[:, 0:COL_Q])
        a = (glu[:, :CONV_DIM] * _sigmoid(glu[:, CONV_DIM:])).astype(BF16)
        av_ref[s, :] = a[:, CONV_HALF:]
        a_cm = _dot(perm_ref[...], a[:, :CONV_HALF])
        ah_ref[:, PERM_ROWS * n:PERM_ROWS * (n + 1), :] = a_cm.reshape(
            GRID_W, PERM_ROWS, CONV_HALF)
    for n, s in enumerate(subs):
        q_ref[s, :] = (_dot(ub[n], wm_ref[:, COL_Q:COL_K]) * (GLA_DK ** -0.5)).astype(BF16)
    for n, s in enumerate(subs):
        k_ref[s, :] = _dot(ub[n], wm_ref[:, COL_K:COL_V]).astype(BF16)
    for n, s in enumerate(subs):
        v_ref[s, :] = _dot(ub[n], wm_ref[:, COL_V:COL_R]).astype(BF16)
    for n, s in enumerate(subs):
        r = _dot(ub[n], wm_ref[:, COL_R:COL_DEC])
        rs_ref[s, :] = (r * _sigmoid(r)).astype(BF16)
    for n, s in enumerate(subs):
        la = _decay_logs(ub[n], wdin_ref, wdec_ref, bdec_ref)
        laf_ref[s, :] = la[:, :HK]
        lab_ref[s, :] = la[:, HK:]
    for n, s in enumerate(subs):
        gt_ref[s, :] = _sigmoid(_dot(ub[n], wg_ref[...])).astype(BF16)


def _ctx_proj_kernel(x_ref, mod_ref, g_ref, wkv_ref, wdin_ref, wdec_ref, bdec_ref,
                     k_ref, v_ref, laf_ref, lab_ref):
    ub = _prenorm_mod(x_ref[...], g_ref[...], mod_ref)
    k_ref[...] = _dot(ub, wkv_ref[:, :HK]).astype(BF16)
    v_ref[...] = _dot(ub, wkv_ref[:, HK:]).astype(BF16)
    la = _decay_logs(ub, wdin_ref, wdec_ref, bdec_ref)
    laf_ref[...] = la[:, :HK]
    lab_ref[...] = la[:, HK:]


def _const_spec(arr):
    nd = arr.ndim
    return pl.BlockSpec(arr.shape, lambda *_: (0,) * nd)


def _tok_spec(tm, width):
    return pl.BlockSpec((None, tm, width), lambda b, i: (b, i, 0))


def _colmajor_spec(tm):
    return pl.BlockSpec((None, GRID_W, tm // GRID_W, CONV_HALF), lambda b, i: (b, 0, i, 0))


def _single_spec(arr):
    nd = arr.ndim
    return pl.BlockSpec(arr.shape, lambda *_: (0,) * nd, pipeline_mode=pl.Buffered(1))


def _proj(x, mod, g, perm, wm, wdin, wdec, bdec, wg, tm):
    bsz, t, d = x.shape
    widths = (CONV_HALF, HK, HK, HV, HV, HK, HK, 2 * D_MODEL)
    dtypes = (BF16, BF16, BF16, BF16, BF16, F32, F32, BF16)
    assert tm % PERM_TOKENS == 0
    ah_shape = jax.ShapeDtypeStruct((bsz, GRID_W, t // GRID_W, CONV_HALF), F32)
    return pl.pallas_call(
        _proj_kernel,
        grid=(bsz, t // tm),
        in_specs=[
            _tok_spec(tm, d),
            pl.BlockSpec((None,) + mod.shape[1:], lambda b, i: (b, 0, 0)),
            _const_spec(g), _single_spec(perm), _single_spec(wm), _single_spec(wdin),
            _single_spec(wdec), _const_spec(bdec), _single_spec(wg),
        ],
        out_specs=[_colmajor_spec(tm)] + [_tok_spec(tm, w) for w in widths],
        out_shape=[ah_shape] + [jax.ShapeDtypeStruct((bsz, t, w), dt)
                                for w, dt in zip(widths, dtypes)],
        compiler_params=pltpu.CompilerParams(
            dimension_semantics=("parallel", "parallel"), vmem_limit_bytes=VMEM_LIMIT),
        name="proj",
    )(x, mod, g, perm, wm, wdin, wdec, bdec, wg)


def _ctx_proj(x, mod, g, wkv, wdin, wdec, bdec, tm):
    bsz, t, d = x.shape
    widths = (HK, HV, HK, HK)
    dtypes = (BF16, BF16, F32, F32)
    return pl.pallas_call(
        _ctx_proj_kernel,
        grid=(bsz, t // tm),
        in_specs=[
            _tok_spec(tm, d),
            pl.BlockSpec((None,) + mod.shape[1:], lambda b, i: (0, 0, 0)),
            _const_spec(g), _const_spec(wkv), _const_spec(wdin), _const_spec(wdec),
            _const_spec(bdec),
        ],
        out_specs=[_tok_spec(tm, w) for w in widths],
        out_shape=[jax.ShapeDtypeStruct((bsz, t, w), dt) for w, dt in zip(widths, dtypes)],
        compiler_params=pltpu.CompilerParams(
            dimension_semantics=("parallel", "parallel"), vmem_limit_bytes=VMEM_LIMIT),
        name="ctx_proj",
    )(x, mod, g, wkv, wdin, wdec, bdec)


def _gla_scale(qs, ks, las, dirn, want_out):
    tri_b, mid, last = dirn["tri_b"], dirn["mid"], dirn["last"]
    la_hi, la_lo = _split_bf16(las)
    g = _dot(tri_b, la_hi) + _dot(tri_b, la_lo)
    u = dict(kt=[], kend=[], qt=[], qg=[], e=[])
    for c in range(SUPER // CHUNK):
        rows = slice(c * CHUNK, (c + 1) * CHUNK)
        gc = g[rows]
        g_mid = gc[mid:mid + 1]
        g_last = gc[last:last + 1]
        u["e"].append(jnp.exp(g_last))
        kt = ks[rows].astype(F32) * jnp.exp(g_mid - gc)
        u["kt"].append(kt)
        u["kend"].append(kt * jnp.exp(g_last - g_mid))
        if want_out:
            qt = qs[rows].astype(F32) * jnp.exp(gc - g_mid)
            u["qt"].append(qt.astype(BF16))
            u["qg"].append((qt * jnp.exp(g_mid)).astype(BF16))
    return u


def _pair_lanes(pair):
    return slice(LANES * pair, LANES * (pair + 1))


def _gla_scores(u, pair, head_lane):
    lanes = _pair_lanes(pair)
    kt = jnp.concatenate([c[:, lanes] for c in u["kt"]], axis=0)
    qt = jnp.concatenate([c[:, lanes] for c in u["qt"]], axis=0)
    kt2 = jnp.concatenate([(kt * hm).astype(BF16) for hm in head_lane], axis=0)
    return lax.dot_general(qt, kt2, NT, preferred_element_type=F32)


def _gla_update(u, vs, h, head_lane):
    lanes = _pair_lanes(h // 2)
    hm = head_lane[h % 2]
    k0 = (u["kend"][0][:, lanes] * hm).astype(BF16)
    k1 = (u["kend"][1][:, lanes] * hm).astype(BF16)
    z = jnp.zeros_like(k0)
    kbd = jnp.concatenate([jnp.concatenate([k0, z], axis=1),
                           jnp.concatenate([z, k1], axis=1)], axis=0)
    return lax.dot_general(vs[:, GLA_DV * h:GLA_DV * (h + 1)], kbd, TN,
                           preferred_element_type=F32)


def _gla_intra(scores, vs, pair, mask2):
    a = jnp.where(mask2, scores, 0.0).astype(BF16)
    v0 = vs[:, GLA_DV * 2 * pair:GLA_DV * (2 * pair + 1)]
    v1 = vs[:, GLA_DV * (2 * pair + 1):GLA_DV * (2 * pair + 2)]
    z = jnp.zeros_like(v0)
    vbd = jnp.concatenate([jnp.concatenate([v0, z], axis=1),
                           jnp.concatenate([z, v1], axis=1)], axis=0)
    return _dot(a, vbd)


def _gla_kernel(q_ref, k_ref, v_ref, laf_ref, lab_ref, kc_ref, vc_ref, lafc_ref, labc_ref,
                rs_ref, g_ref, o_ref, of_ref, ob_ref, sf_ref, sb_ref):
    t = q_ref.shape[0]
    t_ctx = kc_ref.shape[0]
    half = t // 2
    n_steps = t // STEP
    supers = STEP // SUPER
    n_chunks = SUPER // CHUNK
    pairs = GLA_HEADS // 2

    row = lax.broadcasted_iota(jnp.int32, (SUPER, SUPER), 0)
    col = lax.broadcasted_iota(jnp.int32, (SUPER, SUPER), 1)
    same_chunk = (row // CHUNK) == (col // CHUNK)
    low_mask = jnp.logical_and(same_chunk, col <= row)
    up_mask = jnp.logical_and(same_chunk, col >= row)
    lane = lax.broadcasted_iota(jnp.int32, (1, LANES), 1)
    head_lane = (jnp.where(lane < GLA_DK, 1.0, 0.0), jnp.where(lane >= GLA_DK, 1.0, 0.0))

    fwd = dict(tri_b=jnp.where(low_mask, 1.0, 0.0).astype(BF16),
               mask2=jnp.concatenate([low_mask, low_mask], axis=1),
               mid=CHUNK // 2 - 1, last=CHUNK - 1, order=(0, 1), st_ref=sf_ref)
    bwd = dict(tri_b=jnp.where(up_mask, 1.0, 0.0).astype(BF16),
               mask2=jnp.concatenate([up_mask, up_mask], axis=1),
               mid=CHUNK // 2, last=0, order=(1, 0), st_ref=sb_ref)

    sf_ref[...] = jnp.zeros_like(sf_ref)
    sb_ref[...] = jnp.zeros_like(sb_ref)
    gn = g_ref[...]

    def run(units, want_out, finalize):
        vals = []
        for dirn, refs, t0 in units:
            q_r, k_r, v_r, la_r = refs
            rows = pl.ds(t0, SUPER)
            qs = q_r[rows, :] if want_out else None
            vals.append((_gla_scale(qs, k_r[rows, :], la_r[rows, :], dirn, want_out),
                         v_r[rows, :]))
        scores = []
        if want_out:
            scores = [[_gla_scores(u, p, head_lane) for p in range(pairs)] for u, _ in vals]
        upd = [[_gla_update(u, vs, h, head_lane) for h in range(GLA_HEADS)] for u, vs in vals]
        intra = []
        if want_out:
            intra = [[_gla_intra(scores[n][p], vals[n][1], p, units[n][0]["mask2"])
                      for p in range(pairs)] for n in range(len(units))]
        states = {}
        for n, (dirn, _, t0) in enumerate(units):
            key = id(dirn["st_ref"])
            if key not in states:
                states[key] = (dirn["st_ref"], [dirn["st_ref"][h] for h in range(GLA_HEADS)])
            st = states[key][1]
            u = vals[n][0]
            inter = [[None] * n_chunks for _ in range(pairs)]
            for c in dirn["order"]:
                if want_out:
                    for p in range(pairs):
                        st2 = jnp.concatenate([st[2 * p].astype(BF16),
                                               st[2 * p + 1].astype(BF16)], axis=0)
                        inter[p][c] = lax.dot_general(u["qg"][c][:, _pair_lanes(p)], st2, NT,
                                                      preferred_element_type=F32)
                for h in range(GLA_HEADS):
                    e = u["e"][c][:, _pair_lanes(h // 2)]
                    st[h] = st[h] * e + upd[n][h][:, LANES * c:LANES * (c + 1)]
            if not want_out:
                continue
            fwd_unit = dirn is fwd
            for p in range(pairs):
                o2 = intra[n][p] + jnp.concatenate(inter[p], axis=0)
                for j in range(2):
                    cols = slice(GLA_DV * (2 * p + j), GLA_DV * (2 * p + j + 1))
                    o = o2[:, GLA_DV * j:GLA_DV * (j + 1)]
                    if finalize:
                        other = (ob_ref[pl.ds(t0 - half, SUPER), cols] if fwd_unit
                                 else of_ref[pl.ds(t0, SUPER), cols])
                        rs = rs_ref[pl.ds(t0, SUPER), cols].astype(F32)
                        o_ref[pl.ds(t0, SUPER), cols] = (_rms(o + other, gn) * rs).astype(BF16)
                    elif fwd_unit:
                        of_ref[pl.ds(t0, SUPER), cols] = o
                    else:
                        ob_ref[pl.ds(t0 - half, SUPER), cols] = o
        for st_ref, st in states.values():
            for h in range(GLA_HEADS):
                st_ref[h] = st[h]

    ctx_units = []
    for s in range(t_ctx // SUPER):
        ctx_units.append((fwd, (None, kc_ref, vc_ref, lafc_ref), s * SUPER))
        ctx_units.append((bwd, (None, kc_ref, vc_ref, labc_ref), t_ctx - (s + 1) * SUPER))
    run(ctx_units, want_out=False, finalize=False)

    def step(i, finalize):
        units = []
        for s in range(supers):
            t0 = pl.multiple_of(i * STEP + s * SUPER, SUPER)
            t1 = pl.multiple_of(t - (i + 1) * STEP + (supers - 1 - s) * SUPER, SUPER)
            units.append((fwd, (q_ref, k_ref, v_ref, laf_ref), t0))
            units.append((bwd, (q_ref, k_ref, v_ref, lab_ref), t1))
        run(units, want_out=True, finalize=finalize)

    def park_body(i, carry):
        step(i, False)
        return carry

    def finish_body(i, carry):
        step(i, True)
        return carry

    lax.fori_loop(0, n_steps // 2, park_body, 0)
    lax.fori_loop(n_steps // 2, n_steps, finish_body, 0)


def _gla(q, k, v, laf, lab, kc, vc, lafc, labc, rs, gnorm):
    bsz, t, _ = q.shape
    assert t % (2 * STEP) == 0 and kc.shape[1] % SUPER == 0

    def bspec(arr):
        return pl.BlockSpec((None,) + arr.shape[1:], lambda b: (b, 0, 0))

    args = (q, k, v, laf, lab, kc, vc, lafc, labc, rs)
    return pl.pallas_call(
        _gla_kernel,
        grid=(bsz,),
        in_specs=[bspec(a) for a in args] + [_const_spec(gnorm)],
        out_specs=pl.BlockSpec((None, t, HV), lambda b: (b, 0, 0)),
        out_shape=jax.ShapeDtypeStruct((bsz, t, HV), BF16),
        scratch_shapes=[
            pltpu.VMEM((t // 2, HV), F32), pltpu.VMEM((t // 2, HV), F32),
            pltpu.VMEM((GLA_HEADS, GLA_DV, LANES), F32),
            pltpu.VMEM((GLA_HEADS, GLA_DV, LANES), F32),
        ],
        compiler_params=pltpu.CompilerParams(
            dimension_semantics=("parallel",), vmem_limit_bytes=VMEM_LIMIT),
        name="gla",
    )(*args, gnorm)


def _conv_kernel(ah_ref, av_ref, w_ref, b_ref, yh_ref, yv_ref, ph_ref, pv_ref):
    t = ah_ref.shape[0]
    rows = t // GRID_W
    tile = 64

    def conv(src_ref, pad_ref, w_cols, line, dst_ref):
        pad = CONV_PAD * line
        pad_ref[pl.ds(0, pad), :] = jnp.zeros((pad, CONV_HALF), F32)
        pad_ref[pl.ds(pad + t, pad), :] = jnp.zeros((pad, CONV_HALF), F32)

        def fill(i, carry):
            t0 = pl.multiple_of(i * tile, tile)
            pad_ref[pl.ds(pl.multiple_of(pad + t0, 8), tile), :] = (
                src_ref[pl.ds(t0, tile), :].astype(F32))
            return carry

        lax.fori_loop(0, t // tile, fill, 0)
        bias = b_ref[:, w_cols]

        def body(i, carry):
            t0 = pl.multiple_of(i * tile, tile)
            acc = jnp.broadcast_to(bias, (tile, CONV_HALF))
            for kk in range(CONV_K):
                win = pad_ref[pl.ds(pl.multiple_of(t0 + kk * line, 8), tile), :]
                acc = acc + w_ref[kk:kk + 1, w_cols] * win
            dst_ref[pl.ds(t0, tile), :] = acc
            return carry

        lax.fori_loop(0, t // tile, body, 0)

    conv(ah_ref, ph_ref, slice(0, CONV_HALF), rows, yh_ref)
    conv(av_ref, pv_ref, slice(CONV_HALF, CONV_DIM), GRID_W, yv_ref)


def _conv(ah, av, w, b):
    bsz, t, c = ah.shape
    rows = t // GRID_W
    tok = pl.BlockSpec((None, t, c), lambda i: (i, 0, 0))
    return pl.pallas_call(
        _conv_kernel,
        grid=(bsz,),
        in_specs=[tok, tok, _const_spec(w), _const_spec(b)],
        out_specs=[tok, tok],
        out_shape=[jax.ShapeDtypeStruct((bsz, t, c), F32)] * 2,
        scratch_shapes=[
            pltpu.VMEM((t + 2 * CONV_PAD * rows, c), F32),
            pltpu.VMEM((t + 2 * CONV_PAD * GRID_W, c), F32),
        ],
        compiler_params=pltpu.CompilerParams(
            dimension_semantics=("parallel",), vmem_limit_bytes=VMEM_LIMIT),
        name="conv",
    )(ah, av, w, b)


def _post_kernel(x_ref, yh_ref, yv_ref, og_ref, gt_ref, mod_ref, perm_ref, lng_ref, lnb_ref,
                 gpost1_ref, gpre2_ref, gpost2_ref, wco_ref, wgo_ref, wout_ref, wff1_ref,
                 wff2_ref, o_ref):
    d = x_ref.shape[-1]
    d_ff = wff1_ref.shape[1]
    ff_tile = 1024
    sub = 256
    subs = [slice(s, s + sub) for s in range(0, x_ref.shape[0], sub)]
    gt1 = mod_ref[2:3, :]
    sh2 = mod_ref[3:4, :]
    sc2 = mod_ref[4:5, :]
    gt2 = mod_ref[5:6, :]

    yh = []
    for n in range(x_ref.shape[0] // PERM_TOKENS):
        blk = yh_ref[:, PERM_ROWS * n:PERM_ROWS * (n + 1), :].reshape(PERM_TOKENS, CONV_HALF)
        hi, lo = _split_bf16(blk)
        yh.append(_dot(perm_ref[...], hi) + _dot(perm_ref[...], lo))
    yh = jnp.concatenate(yh, axis=0)

    yn = []
    for s in subs:
        yc = jnp.concatenate([yh[s], yv_ref[s, :]], axis=-1)
        yc = yc - jnp.mean(yc, axis=-1, keepdims=True)
        var = jnp.mean(yc * yc, axis=-1, keepdims=True)
        z = yc * lax.rsqrt(var + EPS) * lng_ref[...] + lnb_ref[...]
        yn.append((z * _sigmoid(z)).astype(BF16))
    y_gla = [_dot(og_ref[s, :], wgo_ref[...]) for s in subs]
    y_conv = [_dot(v, wco_ref[...]) for v in yn]
    merged = []
    for n, s in enumerate(subs):
        gates = gt_ref[s, :].astype(F32)
        merged.append((gates[:, :d] * y_conv[n] + gates[:, d:] * y_gla[n]).astype(BF16))
    y = [_dot(v, wout_ref[...]) for v in merged]
    h_mid = [x_ref[s, :] + gt1 * _rms(y[n], gpost1_ref[...]) for n, s in enumerate(subs)]
    u2 = [(_rms(h, gpre2_ref[...]) * (1.0 + sc2) + sh2).astype(BF16) for h in h_mid]
    acc = [jnp.zeros(h.shape, F32) for h in h_mid]
    for j in range(d_ff // ff_tile):
        cols = slice(j * ff_tile, (j + 1) * ff_tile)
        f = [_dot(v, wff1_ref[:, cols]) for v in u2]
        f = [jnp.square(jnp.maximum(v, 0.0)).astype(BF16) for v in f]
        acc = [acc[n] + _dot(f[n], wff2_ref[cols, :]) for n in range(len(subs))]
    for n, s in enumerate(subs):
        o_ref[s, :] = h_mid[n] + gt2 * _rms(acc[n], gpost2_ref[...])


def _post(x, yh, yv, og, gates, mod, perm, ln_g, ln_b, gpost1, gpre2, gpost2, wco, wgo, wout,
          wff1, wff2, tm):
    bsz, t, d = x.shape
    assert tm % PERM_TOKENS == 0
    return pl.pallas_call(
        _post_kernel,
        grid=(bsz, t // tm),
        in_specs=[
            _tok_spec(tm, d), _colmajor_spec(tm), _tok_spec(tm, CONV_HALF), _tok_spec(tm, HV),
            _tok_spec(tm, 2 * d),
            pl.BlockSpec((None,) + mod.shape[1:], lambda b, i: (b, 0, 0)),
            _single_spec(perm), _const_spec(ln_g), _const_spec(ln_b),
            _const_spec(gpost1), _const_spec(gpre2), _const_spec(gpost2),
            _single_spec(wco), _single_spec(wgo), _single_spec(wout), _single_spec(wff1),
            _single_spec(wff2),
        ],
        out_specs=_tok_spec(tm, d),
        out_shape=jax.ShapeDtypeStruct((bsz, t, d), F32),
        compiler_params=pltpu.CompilerParams(
            dimension_semantics=("parallel", "parallel"), vmem_limit_bytes=VMEM_LIMIT),
        name="post",
    )(x, yh, yv, og, gates, mod, perm, ln_g, ln_b, gpost1, gpre2, gpost2, wco, wgo, wout, wff1,
      wff2)


def kernel(x, c, ctx, c_ctx, w_mod, b_mod, g_pre1, g_post1, g_pre2, g_post2, w_in, conv_w,
           conv_b, conv_ln_g, conv_ln_b, w_conv_out, w_decay, b_decay, gla_norm_g, w_gla_out,
           w_out, w_ff1, w_ff2):
    bsz, t, d = x.shape
    depth = w_in.shape[0]
    assert depth == 1 and d == D_MODEL and t % (GRID_W * 8) == 0

    n_rows = -(-(bsz + 1) // 8) * 8
    c_all = jnp.zeros((n_rows, d), F32).at[:bsz].set(c).at[bsz].set(c_ctx)
    m = _adaln(c_all, w_mod[0], b_mod[0][None, :])
    mod = m[:bsz].reshape(bsz, N_MOD, d)
    mod_ctx = m[bsz, :2 * d].reshape(1, 2, d)

    w_in0 = w_in[0]
    wm = w_in0[:, :COL_DEC].astype(BF16)
    wdin = jnp.pad(w_in0[:, COL_DEC:COL_GATE], ((0, 0), (0, LANES - 2 * DECAY_RANK))).astype(BF16)
    wg = w_in0[:, COL_GATE:].astype(BF16)
    wkv = w_in0[:, COL_K:COL_R].astype(BF16)
    wdec = jnp.zeros((LANES, 2 * HK), F32)
    wdec = wdec.at[:DECAY_RANK, :HK].set(w_decay[0, 0])
    wdec = wdec.at[DECAY_RANK:2 * DECAY_RANK, HK:].set(w_decay[0, 1]).astype(BF16)
    bdec = b_decay[0].reshape(1, 2 * HK)
    row = lambda v: v.reshape(1, -1)

    perm = _to_colmajor_perm()
    to_colmajor = jnp.asarray(perm, BF16)
    to_rowmajor = jnp.asarray(perm.T, BF16)

    ah, av, q, k, v, rs, laf, lab, gates = _proj(
        x, mod, row(g_pre1[0]), to_colmajor, wm, wdin, wdec, bdec, wg, tm=1024)
    kc, vc, lafc, labc = _ctx_proj(
        ctx, mod_ctx, row(g_pre1[0]), wkv, wdin, wdec, bdec, tm=ctx.shape[1])

    og = _gla(q, k, v, laf, lab, kc, vc, lafc, labc, rs, row(gla_norm_g[0]))
    yh, yv = _conv(ah.reshape(bsz, t, CONV_HALF), av, conv_w[0], row(conv_b[0]))
    yh = yh.reshape(bsz, GRID_W, t // GRID_W, CONV_HALF)

    return _post(x, yh, yv, og, gates, mod, to_rowmajor, row(conv_ln_g[0]), row(conv_ln_b[0]),
                 row(g_post1[0]), row(g_pre2[0]), row(g_post2[0]),
                 w_conv_out[0].astype(BF16), w_gla_out[0].astype(BF16), w_out[0].astype(BF16),
                 w_ff1[0].astype(BF16), w_ff2[0].astype(BF16), tm=512)
```

```python
import jax
import jax.numpy as jnp
import numpy as np
from jax import lax
from jax.experimental import pallas as pl
from jax.experimental.pallas import tpu as pltpu

D_MODEL = 1024
GRID_W = 64
CONV_DIM = 512
CONV_K = 31
GLA_HEADS = 4
GLA_DK = 64
GLA_DV = 128
DECAY_RANK = 16
GATE_NORM = 16.0
CHUNK = 64
N_MOD = 6
EPS = 1e-6

COL_Q = 2 * CONV_DIM
COL_K = COL_Q + GLA_HEADS * GLA_DK
COL_V = COL_K + GLA_HEADS * GLA_DK
COL_R = COL_V + GLA_HEADS * GLA_DV
COL_DEC = COL_R + GLA_HEADS * GLA_DV
COL_GATE = COL_DEC + 2 * DECAY_RANK
COL_END = COL_GATE + 2 * D_MODEL

LANES = 128
HK = GLA_HEADS * GLA_DK
HV = GLA_HEADS * GLA_DV
CONV_HALF = CONV_DIM // 2
CONV_PAD = CONV_K // 2
PERM_ROWS = 8
PERM_TOKENS = PERM_ROWS * GRID_W
PROJ_SUB = 256
SUPER = 2 * CHUNK
STEP = 2 * SUPER

VMEM_LIMIT = 56 * 1024 * 1024

F32 = jnp.float32
BF16 = jnp.bfloat16

NT = (((1,), (1,)), ((), ()))
TN = (((0,), (0,)), ((), ()))


def _dot(a, b):
    return jnp.dot(a, b, preferred_element_type=F32)


def _rms(x, g):
    ms = jnp.mean(x * x, axis=-1, keepdims=True)
    return x * lax.rsqrt(ms + EPS) * g


def _sigmoid(x):
    return jax.nn.sigmoid(x)


def _split_bf16(x):
    hi = x.astype(BF16)
    lo = (x - hi.astype(F32)).astype(BF16)
    return hi, lo


def _to_colmajor_perm():
    out = np.arange(PERM_TOKENS)
    src = (out % PERM_ROWS) * GRID_W + out // PERM_ROWS
    p = np.zeros((PERM_TOKENS, PERM_TOKENS), np.float32)
    p[out, src] = 1.0
    return p


def _adaln_kernel(c_ref, w_ref, b_ref, o_ref):
    c = c_ref[...]
    s_hi, s_lo = _split_bf16(c * _sigmoid(c))
    w_hi, w_lo = _split_bf16(w_ref[...])
    o_ref[...] = _dot(s_hi, w_hi) + _dot(s_lo, w_hi) + _dot(s_hi, w_lo) + b_ref[...]


def _adaln(c_all, w_mod, b_mod):
    n_rows, d = c_all.shape
    n_out = w_mod.shape[-1]
    tn = 1536
    return pl.pallas_call(
        _adaln_kernel,
        grid=(n_out // tn,),
        in_specs=[
            pl.BlockSpec((n_rows, d), lambda j: (0, 0)),
            pl.BlockSpec((None, d, tn), lambda j: (0, 0, j)),
            pl.BlockSpec((1, tn), lambda j: (0, j)),
        ],
        out_specs=pl.BlockSpec((n_rows, tn), lambda j: (0, j)),
        out_shape=jax.ShapeDtypeStruct((n_rows, n_out), F32),
        compiler_params=pltpu.CompilerParams(
            dimension_semantics=("arbitrary",), vmem_limit_bytes=VMEM_LIMIT),
        name="adaln",
    )(c_all, w_mod, b_mod)


def _prenorm_mod(x, g, mod_ref):
    sh = mod_ref[0:1, :]
    sc = mod_ref[1:2, :]
    return (_rms(x, g) * (1.0 + sc) + sh).astype(BF16)


def _decay_logs(z, wdec_ref, bdec_ref):
    z_hi, z_lo = z
    logits = _dot(z_hi, wdec_ref[...]) + _dot(z_lo, wdec_ref[...]) + bdec_ref[...]
    ls = jnp.minimum(logits, 0.0) - jnp.log(1.0 + jnp.exp(-jnp.abs(logits)))
    return ls * (1.0 / GATE_NORM)


def _proj_kernel(x_ref, mod_ref, g_ref, perm_ref, wm_ref, wdin_ref, wdec_ref, bdec_ref, wg_ref,
                 ah_ref, av_ref, q_ref, k_ref, v_ref, rs_ref, laf_ref, lab_ref, gt_ref):
    subs = [slice(s, s + PROJ_SUB) for s in range(0, x_ref.shape[0], PROJ_SUB)]
    g = g_ref[...]
    ub = [_prenorm_mod(x_ref[s, :], g, mod_ref) for s in subs]
    z = [_split_bf16(_dot(u, wdin_ref[...])) for u in ub]
    a_h = []
    for n, s in enumerate(subs):
        glu = _dot(ub[n], wm_ref[:, 0:COL_Q])
        a = (glu[:, :CONV_DIM] * _sigmoid(glu[:, CONV_DIM:])).astype(BF16)
        av_ref[s, :] = a[:, CONV_HALF:]
        a_h.append(a[:, :CONV_HALF])
    for n, s in enumerate(subs):
        q_ref[s, :] = (_dot(ub[n], wm_ref[:, COL_Q:COL_K]) * (GLA_DK ** -0.5)).astype(BF16)
    for n, s in enumerate(subs):
        k_ref[s, :] = _dot(ub[n], wm_ref[:, COL_K:COL_V]).astype(BF16)
    for n, s in enumerate(subs):
        v_ref[s, :] = _dot(ub[n], wm_ref[:, COL_V:COL_R]).astype(BF16)
    for n, s in enumerate(subs):
        la = _decay_logs(z[n], wdec_ref, bdec_ref)
        laf_ref[s, :] = la[:, :HK]
        lab_ref[s, :] = la[:, HK:]
    for n, s in enumerate(subs):
        r = _dot(ub[n], wm_ref[:, COL_R:COL_DEC])
        rs_ref[s, :] = (r * _sigmoid(r)).astype(BF16)
    per_group = PERM_TOKENS // PROJ_SUB
    for n in range(x_ref.shape[0] // PERM_TOKENS):
        a_rm = jnp.concatenate(a_h[per_group * n:per_group * (n + 1)], axis=0)
        a_cm = _dot(perm_ref[...], a_rm)
        ah_ref[:, PERM_ROWS * n:PERM_ROWS * (n + 1), :] = a_cm.reshape(
            GRID_W, PERM_ROWS, CONV_HALF)
    for n, s in enumerate(subs):
        gt_ref[s, :] = _sigmoid(_dot(ub[n], wg_ref[...])).astype(BF16)


def _ctx_proj_kernel(x_ref, mod_ref, g_ref, wkv_ref, wdin_ref, wdec_ref, bdec_ref,
                     k_ref, v_ref, laf_ref, lab_ref):
    ub = _prenorm_mod(x_ref[...], g_ref[...], mod_ref)
    z = _split_bf16(_dot(ub, wdin_ref[...]))
    k_ref[...] = _dot(ub, wkv_ref[:, :HK]).astype(BF16)
    v_ref[...] = _dot(ub, wkv_ref[:, HK:]).astype(BF16)
    la = _decay_logs(z, wdec_ref, bdec_ref)
    laf_ref[...] = la[:, :HK]
    lab_ref[...] = la[:, HK:]


def _const_spec(arr):
    nd = arr.ndim
    return pl.BlockSpec(arr.shape, lambda *_: (0,) * nd)


def _tok_spec(tm, width):
    return pl.BlockSpec((None, tm, width), lambda b, i: (b, i, 0))


def _colmajor_spec(tm):
    return pl.BlockSpec((None, GRID_W, tm // GRID_W, CONV_HALF), lambda b, i: (b, 0, i, 0))


def _single_spec(arr):
    nd = arr.ndim
    return pl.BlockSpec(arr.shape, lambda *_: (0,) * nd, pipeline_mode=pl.Buffered(1))


def _proj(x, mod, g, perm, wm, wdin, wdec, bdec, wg, tm):
    bsz, t, d = x.shape
    widths = (CONV_HALF, HK, HK, HV, HV, HK, HK, 2 * D_MODEL)
    dtypes = (BF16, BF16, BF16, BF16, BF16, F32, F32, BF16)
    assert tm % PERM_TOKENS == 0
    ah_shape = jax.ShapeDtypeStruct((bsz, GRID_W, t // GRID_W, CONV_HALF), F32)
    return pl.pallas_call(
        _proj_kernel,
        grid=(bsz, t // tm),
        in_specs=[
            _tok_spec(tm, d),
            pl.BlockSpec((None,) + mod.shape[1:], lambda b, i: (b, 0, 0)),
            _const_spec(g), _single_spec(perm), _single_spec(wm), _single_spec(wdin),
            _single_spec(wdec), _const_spec(bdec), _single_spec(wg),
        ],
        out_specs=[_colmajor_spec(tm)] + [_tok_spec(tm, w) for w in widths],
        out_shape=[ah_shape] + [jax.ShapeDtypeStruct((bsz, t, w), dt)
                                for w, dt in zip(widths, dtypes)],
        compiler_params=pltpu.CompilerParams(
            dimension_semantics=("parallel", "parallel"), vmem_limit_bytes=VMEM_LIMIT),
        name="proj",
    )(x, mod, g, perm, wm, wdin, wdec, bdec, wg)


def _ctx_proj(x, mod, g, wkv, wdin, wdec, bdec, tm):
    bsz, t, d = x.shape
    widths = (HK, HV, HK, HK)
    dtypes = (BF16, BF16, F32, F32)
    return pl.pallas_call(
        _ctx_proj_kernel,
        grid=(bsz, t // tm),
        in_specs=[
            _tok_spec(tm, d),
            pl.BlockSpec((None,) + mod.shape[1:], lambda b, i: (0, 0, 0)),
            _const_spec(g), _const_spec(wkv), _const_spec(wdin), _const_spec(wdec),
            _const_spec(bdec),
        ],
        out_specs=[_tok_spec(tm, w) for w in widths],
        out_shape=[jax.ShapeDtypeStruct((bsz, t, w), dt) for w, dt in zip(widths, dtypes)],
        compiler_params=pltpu.CompilerParams(
            dimension_semantics=("parallel", "parallel"), vmem_limit_bytes=VMEM_LIMIT),
        name="ctx_proj",
    )(x, mod, g, wkv, wdin, wdec, bdec)


def _gla_scale(qs, ks, las, dirn, want_out):
    tri_b, mid, last = dirn["tri_b"], dirn["mid"], dirn["last"]
    la_hi, la_lo = _split_bf16(las)
    g = _dot(tri_b, la_hi) + _dot(tri_b, la_lo)
    u = dict(kt=[], kend=[], qt=[], qg=[], e=[])
    for c in range(SUPER // CHUNK):
        rows = slice(c * CHUNK, (c + 1) * CHUNK)
        gc = g[rows]
        g_mid = gc[mid:mid + 1]
        g_last = gc[last:last + 1]
        u["e"].append(jnp.exp(g_last))
        kt = ks[rows].astype(F32) * jnp.exp(g_mid - gc)
        u["kt"].append(kt)
        u["kend"].append(kt * jnp.exp(g_last - g_mid))
        if want_out:
            qt = qs[rows].astype(F32) * jnp.exp(gc - g_mid)
            u["qt"].append(qt.astype(BF16))
            u["qg"].append((qt * jnp.exp(g_mid)).astype(BF16))
    return u


def _pair_lanes(pair):
    return slice(LANES * pair, LANES * (pair + 1))


def _gla_scores(u, pair, head_lane):
    lanes = _pair_lanes(pair)
    kt = jnp.concatenate([c[:, lanes] for c in u["kt"]], axis=0)
    qt = jnp.concatenate([c[:, lanes] for c in u["qt"]], axis=0)
    kt2 = jnp.concatenate([(kt * hm).astype(BF16) for hm in head_lane], axis=0)
    return lax.dot_general(qt, kt2, NT, preferred_element_type=F32)


def _gla_update(u, vs, h, head_lane):
    lanes = _pair_lanes(h // 2)
    hm = head_lane[h % 2]
    k0 = (u["kend"][0][:, lanes] * hm).astype(BF16)
    k1 = (u["kend"][1][:, lanes] * hm).astype(BF16)
    z = jnp.zeros_like(k0)
    kbd = jnp.concatenate([jnp.concatenate([k0, z], axis=1),
                           jnp.concatenate([z, k1], axis=1)], axis=0)
    return lax.dot_general(vs[:, GLA_DV * h:GLA_DV * (h + 1)], kbd, TN,
                           preferred_element_type=F32)


def _gla_intra(scores, vs, pair, mask2):
    a = jnp.where(mask2, scores, 0.0).astype(BF16)
    v0 = vs[:, GLA_DV * 2 * pair:GLA_DV * (2 * pair + 1)]
    v1 = vs[:, GLA_DV * (2 * pair + 1):GLA_DV * (2 * pair + 2)]
    z = jnp.zeros_like(v0)
    vbd = jnp.concatenate([jnp.concatenate([v0, z], axis=1),
                           jnp.concatenate([z, v1], axis=1)], axis=0)
    return _dot(a, vbd)


def _gla_kernel(q_ref, k_ref, v_ref, laf_ref, lab_ref, kc_ref, vc_ref, lafc_ref, labc_ref,
                rs_ref, g_ref, o_ref, of_ref, ob_ref, sf_ref, sb_ref):
    t = q_ref.shape[0]
    t_ctx = kc_ref.shape[0]
    half = t // 2
    n_steps = t // STEP
    supers = STEP // SUPER
    n_chunks = SUPER // CHUNK
    pairs = GLA_HEADS // 2

    row = lax.broadcasted_iota(jnp.int32, (SUPER, SUPER), 0)
    col = lax.broadcasted_iota(jnp.int32, (SUPER, SUPER), 1)
    same_chunk = (row // CHUNK) == (col // CHUNK)
    low_mask = jnp.logical_and(same_chunk, col <= row)
    up_mask = jnp.logical_and(same_chunk, col >= row)
    lane = lax.broadcasted_iota(jnp.int32, (1, LANES), 1)
    head_lane = (jnp.where(lane < GLA_DK, 1.0, 0.0), jnp.where(lane >= GLA_DK, 1.0, 0.0))

    fwd = dict(tri_b=jnp.where(low_mask, 1.0, 0.0).astype(BF16),
               mask2=jnp.concatenate([low_mask, low_mask], axis=1),
               mid=CHUNK // 2 - 1, last=CHUNK - 1, order=(0, 1), st_ref=sf_ref)
    bwd = dict(tri_b=jnp.where(up_mask, 1.0, 0.0).astype(BF16),
               mask2=jnp.concatenate([up_mask, up_mask], axis=1),
               mid=CHUNK // 2, last=0, order=(1, 0), st_ref=sb_ref)

    sf_ref[...] = jnp.zeros_like(sf_ref)
    sb_ref[...] = jnp.zeros_like(sb_ref)
    gn = g_ref[...]

    def run(units, want_out, finalize):
        vals = []
        for dirn, refs, t0 in units:
            q_r, k_r, v_r, la_r = refs
            rows = pl.ds(t0, SUPER)
            qs = q_r[rows, :] if want_out else None
            vals.append((_gla_scale(qs, k_r[rows, :], la_r[rows, :], dirn, want_out),
                         v_r[rows, :]))
        scores = []
        if want_out:
            scores = [[_gla_scores(u, p, head_lane) for p in range(pairs)] for u, _ in vals]
        upd = [[_gla_update(u, vs, h, head_lane) for h in range(GLA_HEADS)] for u, vs in vals]
        intra = []
        if want_out:
            intra = [[_gla_intra(scores[n][p], vals[n][1], p, units[n][0]["mask2"])
                      for p in range(pairs)] for n in range(len(units))]
        states = {}
        for n, (dirn, _, t0) in enumerate(units):
            key = id(dirn["st_ref"])
            if key not in states:
                states[key] = (dirn["st_ref"], [dirn["st_ref"][h] for h in range(GLA_HEADS)])
            st = states[key][1]
            u = vals[n][0]
            inter = [[None] * n_chunks for _ in range(pairs)]
            for c in dirn["order"]:
                if want_out:
                    for p in range(pairs):
                        st2 = jnp.concatenate([st[2 * p].astype(BF16),
                                               st[2 * p + 1].astype(BF16)], axis=0)
                        inter[p][c] = lax.dot_general(u["qg"][c][:, _pair_lanes(p)], st2, NT,
                                                      preferred_element_type=F32)
                for h in range(GLA_HEADS):
                    e = u["e"][c][:, _pair_lanes(h // 2)]
                    st[h] = st[h] * e + upd[n][h][:, LANES * c:LANES * (c + 1)]
            if not want_out:
                continue
            fwd_unit = dirn is fwd
            for p in range(pairs):
                o2 = intra[n][p] + jnp.concatenate(inter[p], axis=0)
                for j in range(2):
                    cols = slice(GLA_DV * (2 * p + j), GLA_DV * (2 * p + j + 1))
                    o = o2[:, GLA_DV * j:GLA_DV * (j + 1)]
                    if finalize:
                        other = (ob_ref[pl.ds(t0 - half, SUPER), cols] if fwd_unit
                                 else of_ref[pl.ds(t0, SUPER), cols])
                        rs = rs_ref[pl.ds(t0, SUPER), cols].astype(F32)
                        o_ref[pl.ds(t0, SUPER), cols] = (_rms(o + other, gn) * rs).astype(BF16)
                    elif fwd_unit:
                        of_ref[pl.ds(t0, SUPER), cols] = o
                    else:
                        ob_ref[pl.ds(t0 - half, SUPER), cols] = o
        for st_ref, st in states.values():
            for h in range(GLA_HEADS):
                st_ref[h] = st[h]

    ctx_units = []
    for s in range(t_ctx // SUPER):
        ctx_units.append((fwd, (None, kc_ref, vc_ref, lafc_ref), s * SUPER))
        ctx_units.append((bwd, (None, kc_ref, vc_ref, labc_ref), t_ctx - (s + 1) * SUPER))
    run(ctx_units, want_out=False, finalize=False)

    def step(i, finalize):
        units = []
        for s in range(supers):
            t0 = pl.multiple_of(i * STEP + s * SUPER, SUPER)
            t1 = pl.multiple_of(t - (i + 1) * STEP + (supers - 1 - s) * SUPER, SUPER)
            units.append((fwd, (q_ref, k_ref, v_ref, laf_ref), t0))
            units.append((bwd, (q_ref, k_ref, v_ref, lab_ref), t1))
        run(units, want_out=True, finalize=finalize)

    def park_body(i, carry):
        step(i, False)
        return carry

    def finish_body(i, carry):
        step(i, True)
        return carry

    lax.fori_loop(0, n_steps // 2, park_body, 0)
    lax.fori_loop(n_steps // 2, n_steps, finish_body, 0)


def _gla(q, k, v, laf, lab, kc, vc, lafc, labc, rs, gnorm):
    bsz, t, _ = q.shape
    assert t % (2 * STEP) == 0 and kc.shape[1] % SUPER == 0

    def bspec(arr):
        return pl.BlockSpec((None,) + arr.shape[1:], lambda b: (b, 0, 0))

    args = (q, k, v, laf, lab, kc, vc, lafc, labc, rs)
    return pl.pallas_call(
        _gla_kernel,
        grid=(bsz,),
        in_specs=[bspec(a) for a in args] + [_const_spec(gnorm)],
        out_specs=pl.BlockSpec((None, t, HV), lambda b: (b, 0, 0)),
        out_shape=jax.ShapeDtypeStruct((bsz, t, HV), BF16),
        scratch_shapes=[
            pltpu.VMEM((t // 2, HV), F32), pltpu.VMEM((t // 2, HV), F32),
            pltpu.VMEM((GLA_HEADS, GLA_DV, LANES), F32),
            pltpu.VMEM((GLA_HEADS, GLA_DV, LANES), F32),
        ],
        compiler_params=pltpu.CompilerParams(
            dimension_semantics=("parallel",), vmem_limit_bytes=VMEM_LIMIT),
        name="gla",
    )(*args, gnorm)


def _conv_kernel(ah_ref, av_ref, w_ref, b_ref, yh_ref, yv_ref, ph_ref, pv_ref):
    t = ah_ref.shape[0]
    rows = t // GRID_W
    tile = 64

    def conv(src_ref, pad_ref, w_cols, line, dst_ref):
        pad = CONV_PAD * line
        pad_ref[pl.ds(0, pad), :] = jnp.zeros((pad, CONV_HALF), F32)
        pad_ref[pl.ds(pad + t, pad), :] = jnp.zeros((pad, CONV_HALF), F32)

        def fill(i, carry):
            t0 = pl.multiple_of(i * tile, tile)
            pad_ref[pl.ds(pl.multiple_of(pad + t0, 8), tile), :] = (
                src_ref[pl.ds(t0, tile), :].astype(F32))
            return carry

        lax.fori_loop(0, t // tile, fill, 0)
        bias = b_ref[:, w_cols]

        def body(i, carry):
            t0 = pl.multiple_of(i * tile, tile)
            acc = jnp.broadcast_to(bias, (tile, CONV_HALF))
            for kk in range(CONV_K):
                win = pad_ref[pl.ds(pl.multiple_of(t0 + kk * line, 8), tile), :]
                acc = acc + w_ref[kk:kk + 1, w_cols] * win
            dst_ref[pl.ds(t0, tile), :] = acc
            return carry

        lax.fori_loop(0, t // tile, body, 0)

    conv(ah_ref, ph_ref, slice(0, CONV_HALF), rows, yh_ref)
    conv(av_ref, pv_ref, slice(CONV_HALF, CONV_DIM), GRID_W, yv_ref)


def _conv(ah, av, w, b):
    bsz, t, c = ah.shape
    rows = t // GRID_W
    tok = pl.BlockSpec((None, t, c), lambda i: (i, 0, 0))
    return pl.pallas_call(
        _conv_kernel,
        grid=(bsz,),
        in_specs=[tok, tok, _const_spec(w), _const_spec(b)],
        out_specs=[tok, tok],
        out_shape=[jax.ShapeDtypeStruct((bsz, t, c), F32)] * 2,
        scratch_shapes=[
            pltpu.VMEM((t + 2 * CONV_PAD * rows, c), F32),
            pltpu.VMEM((t + 2 * CONV_PAD * GRID_W, c), F32),
        ],
        compiler_params=pltpu.CompilerParams(
            dimension_semantics=("parallel",), vmem_limit_bytes=VMEM_LIMIT),
        name="conv",
    )(ah, av, w, b)


def _post_kernel(x_ref, yh_ref, yv_ref, og_ref, gt_ref, mod_ref, perm_ref, lng_ref, lnb_ref,
                 gpost1_ref, gpre2_ref, gpost2_ref, wco_ref, wgo_ref, wout_ref, wff1_ref,
                 wff2_ref, o_ref):
    d = x_ref.shape[-1]
    d_ff = wff1_ref.shape[1]
    ff_tile = 1024
    sub = 256
    subs = [slice(s, s + sub) for s in range(0, x_ref.shape[0], sub)]
    gt1 = mod_ref[2:3, :]
    sh2 = mod_ref[3:4, :]
    sc2 = mod_ref[4:5, :]
    gt2 = mod_ref[5:6, :]

    yh = []
    for n in range(x_ref.shape[0] // PERM_TOKENS):
        blk = yh_ref[:, PERM_ROWS * n:PERM_ROWS * (n + 1), :].reshape(PERM_TOKENS, CONV_HALF)
        hi, lo = _split_bf16(blk)
        yh.append(_dot(perm_ref[...], hi) + _dot(perm_ref[...], lo))
    yh = jnp.concatenate(yh, axis=0)

    yn = []
    for s in subs:
        yc = jnp.concatenate([yh[s], yv_ref[s, :]], axis=-1)
        yc = yc - jnp.mean(yc, axis=-1, keepdims=True)
        var = jnp.mean(yc * yc, axis=-1, keepdims=True)
        z = yc * lax.rsqrt(var + EPS) * lng_ref[...] + lnb_ref[...]
        yn.append((z * _sigmoid(z)).astype(BF16))
    y_gla = [_dot(og_ref[s, :], wgo_ref[...]) for s in subs]
    y_conv = [_dot(v, wco_ref[...]) for v in yn]
    merged = []
    for n, s in enumerate(subs):
        gates = gt_ref[s, :].astype(F32)
        merged.append((gates[:, :d] * y_conv[n] + gates[:, d:] * y_gla[n]).astype(BF16))
    y = [_dot(v, wout_ref[...]) for v in merged]
    h_mid = [x_ref[s, :] + gt1 * _rms(y[n], gpost1_ref[...]) for n, s in enumerate(subs)]
    u2 = [(_rms(h, gpre2_ref[...]) * (1.0 + sc2) + sh2).astype(BF16) for h in h_mid]
    acc = [jnp.zeros(h.shape, F32) for h in h_mid]
    for j in range(d_ff // ff_tile):
        cols = slice(j * ff_tile, (j + 1) * ff_tile)
        f = [_dot(v, wff1_ref[:, cols]) for v in u2]
        f = [jnp.square(jnp.maximum(v, 0.0)).astype(BF16) for v in f]
        acc = [acc[n] + _dot(f[n], wff2_ref[cols, :]) for n in range(len(subs))]
    for n, s in enumerate(subs):
        o_ref[s, :] = h_mid[n] + gt2 * _rms(acc[n], gpost2_ref[...])


def _post(x, yh, yv, og, gates, mod, perm, ln_g, ln_b, gpost1, gpre2, gpost2, wco, wgo, wout,
          wff1, wff2, tm):
    bsz, t, d = x.shape
    assert tm % PERM_TOKENS == 0
    return pl.pallas_call(
        _post_kernel,
        grid=(bsz, t // tm),
        in_specs=[
            _tok_spec(tm, d), _colmajor_spec(tm), _tok_spec(tm, CONV_HALF), _tok_spec(tm, HV),
            _tok_spec(tm, 2 * d),
            pl.BlockSpec((None,) + mod.shape[1:], lambda b, i: (b, 0, 0)),
            _single_spec(perm), _const_spec(ln_g), _const_spec(ln_b),
            _const_spec(gpost1), _const_spec(gpre2), _const_spec(gpost2),
            _single_spec(wco), _single_spec(wgo), _single_spec(wout), _single_spec(wff1),
            _single_spec(wff2),
        ],
        out_specs=_tok_spec(tm, d),
        out_shape=jax.ShapeDtypeStruct((bsz, t, d), F32),
        compiler_params=pltpu.CompilerParams(
            dimension_semantics=("parallel", "parallel"), vmem_limit_bytes=VMEM_LIMIT),
        name="post",
    )(x, yh, yv, og, gates, mod, perm, ln_g, ln_b, gpost1, gpre2, gpost2, wco, wgo, wout, wff1,
      wff2)


def kernel(x, c, ctx, c_ctx, w_mod, b_mod, g_pre1, g_post1, g_pre2, g_post2, w_in, conv_w,
           conv_b, conv_ln_g, conv_ln_b, w_conv_out, w_decay, b_decay, gla_norm_g, w_gla_out,
           w_out, w_ff1, w_ff2):
    bsz, t, d = x.shape
    depth = w_in.shape[0]
    assert depth == 1 and d == D_MODEL and t % (GRID_W * 8) == 0

    n_rows = -(-(bsz + 1) // 8) * 8
    c_all = jnp.zeros((n_rows, d), F32).at[:bsz].set(c).at[bsz].set(c_ctx)
    m = _adaln(c_all, w_mod, b_mod)
    mod = m[:bsz].reshape(bsz, N_MOD, d)
    mod_ctx = m[bsz, :2 * d].reshape(1, 2, d)

    w_in0 = w_in[0]
    wm = w_in0[:, :COL_DEC].astype(BF16)
    wdin = jnp.pad(w_in0[:, COL_DEC:COL_GATE], ((0, 0), (0, LANES - 2 * DECAY_RANK))).astype(BF16)
    wg = w_in0[:, COL_GATE:].astype(BF16)
    wkv = w_in0[:, COL_K:COL_R].astype(BF16)
    wdec = jnp.zeros((LANES, 2 * HK), F32)
    wdec = wdec.at[:DECAY_RANK, :HK].set(w_decay[0, 0])
    wdec = wdec.at[DECAY_RANK:2 * DECAY_RANK, HK:].set(w_decay[0, 1]).astype(BF16)
    bdec = b_decay[0].reshape(1, 2 * HK)
    row = lambda v: v.reshape(1, -1)

    perm = _to_colmajor_perm()
    to_colmajor = jnp.asarray(perm, BF16)
    to_rowmajor = jnp.asarray(perm.T, BF16)

    ah, av, q, k, v, rs, laf, lab, gates = _proj(
        x, mod, row(g_pre1[0]), to_colmajor, wm, wdin, wdec, bdec, wg, tm=1024)
    kc, vc, lafc, labc = _ctx_proj(
        ctx, mod_ctx, row(g_pre1[0]), wkv, wdin, wdec, bdec, tm=ctx.shape[1])

    og = _gla(q, k, v, laf, lab, kc, vc, lafc, labc, rs, row(gla_norm_g[0]))
    yh, yv = _conv(ah.reshape(bsz, t, CONV_HALF), av, conv_w[0], row(conv_b[0]))
    yh = yh.reshape(bsz, GRID_W, t // GRID_W, CONV_HALF)

    return _post(x, yh, yv, og, gates, mod, to_rowmajor, row(conv_ln_g[0]), row(conv_ln_b[0]),
                 row(g_post1[0]), row(g_pre2[0]), row(g_post2[0]),
                 w_conv_out[0].astype(BF16), w_gla_out[0].astype(BF16), w_out[0].astype(BF16),
                 w_ff1[0].astype(BF16), w_ff2[0].astype(BF16), tm=512)
```

```python
import jax
import jax.numpy as jnp
import numpy as np
from jax import lax
from jax.experimental import pallas as pl
from jax.experimental.pallas import tpu as pltpu

D_MODEL = 1024
GRID_W = 64
CONV_DIM = 512
CONV_K = 31
GLA_HEADS = 4
GLA_DK = 64
GLA_DV = 128
DECAY_RANK = 16
GATE_NORM = 16.0
CHUNK = 64
N_MOD = 6
EPS = 1e-6

COL_Q = 2 * CONV_DIM
COL_K = COL_Q + GLA_HEADS * GLA_DK
COL_V = COL_K + GLA_HEADS * GLA_DK
COL_R = COL_V + GLA_HEADS * GLA_DV
COL_DEC = COL_R + GLA_HEADS * GLA_DV
COL_GATE = COL_DEC + 2 * DECAY_RANK
COL_END = COL_GATE + 2 * D_MODEL

LANES = 128
HK = GLA_HEADS * GLA_DK
HV = GLA_HEADS * GLA_DV
CONV_HALF = CONV_DIM // 2
CONV_PAD = CONV_K // 2
PERM_ROWS = 8
PERM_TOKENS = PERM_ROWS * GRID_W
PROJ_SUB = 256
SUPER = 2 * CHUNK
STEP = 2 * SUPER

VMEM_LIMIT = 56 * 1024 * 1024

F32 = jnp.float32
BF16 = jnp.bfloat16

NT = (((1,), (1,)), ((), ()))
TN = (((0,), (0,)), ((), ()))


def _dot(a, b):
    return jnp.dot(a, b, preferred_element_type=F32)


def _rms(x, g):
    ms = jnp.mean(x * x, axis=-1, keepdims=True)
    return x * lax.rsqrt(ms + EPS) * g


def _sigmoid(x):
    return jax.nn.sigmoid(x)


def _split_bf16(x):
    hi = x.astype(BF16)
    lo = (x - hi.astype(F32)).astype(BF16)
    return hi, lo


def _to_colmajor_perm():
    out = np.arange(PERM_TOKENS)
    src = (out % PERM_ROWS) * GRID_W + out // PERM_ROWS
    p = np.zeros((PERM_TOKENS, PERM_TOKENS), np.float32)
    p[out, src] = 1.0
    return p


def _adaln_kernel(c_ref, w_ref, b_ref, o_ref):
    c = c_ref[...]
    s_hi, s_lo = _split_bf16(c * _sigmoid(c))
    w_hi, w_lo = _split_bf16(w_ref[...])
    o_ref[...] = _dot(s_hi, w_hi) + _dot(s_lo, w_hi) + _dot(s_hi, w_lo) + b_ref[...]


def _adaln(c_all, w_mod, b_mod):
    n_rows, d = c_all.shape
    n_out = w_mod.shape[-1]
    tn = 1536
    return pl.pallas_call(
        _adaln_kernel,
        grid=(n_out // tn,),
        in_specs=[
            pl.BlockSpec((n_rows, d), lambda j: (0, 0)),
            pl.BlockSpec((None, d, tn), lambda j: (0, 0, j)),
            pl.BlockSpec((1, tn), lambda j: (0, j)),
        ],
        out_specs=pl.BlockSpec((n_rows, tn), lambda j: (0, j)),
        out_shape=jax.ShapeDtypeStruct((n_rows, n_out), F32),
        compiler_params=pltpu.CompilerParams(
            dimension_semantics=("arbitrary",), vmem_limit_bytes=VMEM_LIMIT),
        name="adaln",
    )(c_all, w_mod, b_mod)


def _split_w_in_kernel(w_ref, wm_ref, wdin_ref, wg_ref):
    w = w_ref[...]
    wm_ref[...] = w[:, :COL_DEC].astype(BF16)
    wdin_ref[...] = w[:, COL_DEC:COL_DEC + LANES].astype(BF16)
    wg_ref[...] = w[:, COL_GATE:].astype(BF16)


def _split_w_in(w_in):
    _, d, n = w_in.shape
    tr = 256
    widths = (COL_DEC, LANES, n - COL_GATE)
    return pl.pallas_call(
        _split_w_in_kernel,
        grid=(d // tr,),
        in_specs=[pl.BlockSpec((None, tr, n), lambda i: (0, i, 0))],
        out_specs=[pl.BlockSpec((tr, w), lambda i: (i, 0)) for w in widths],
        out_shape=[jax.ShapeDtypeStruct((d, w), BF16) for w in widths],
        compiler_params=pltpu.CompilerParams(
            dimension_semantics=("parallel",), vmem_limit_bytes=VMEM_LIMIT),
        name="split_w_in",
    )(w_in)


def _prenorm_mod(x, g, mod_ref):
    sh = mod_ref[0:1, :]
    sc = mod_ref[1:2, :]
    return (_rms(x, g) * (1.0 + sc) + sh).astype(BF16)


def _decay_logs(z, wdec_ref, bdec_ref):
    z_hi, z_lo = z
    logits = _dot(z_hi, wdec_ref[...]) + _dot(z_lo, wdec_ref[...]) + bdec_ref[...]
    ls = jnp.minimum(logits, 0.0) - jnp.log(1.0 + jnp.exp(-jnp.abs(logits)))
    return ls * (1.0 / GATE_NORM)


def _proj_kernel(x_ref, mod_ref, g_ref, perm_ref, wm_ref, wdin_ref, wdec_ref, bdec_ref, wg_ref,
                 ah_ref, av_ref, q_ref, k_ref, v_ref, rs_ref, laf_ref, lab_ref, gt_ref):
    subs = [slice(s, s + PROJ_SUB) for s in range(0, x_ref.shape[0], PROJ_SUB)]
    g = g_ref[...]
    ub = [_prenorm_mod(x_ref[s, :], g, mod_ref) for s in subs]
    z = [_split_bf16(_dot(u, wdin_ref[...])) for u in ub]
    a_h = []
    for n, s in enumerate(subs):
        glu = _dot(ub[n], wm_ref[:, 0:COL_Q])
        a = (glu[:, :CONV_DIM] * _sigmoid(glu[:, CONV_DIM:])).astype(BF16)
        av_ref[s, :] = a[:, CONV_HALF:]
        a_h.append(a[:, :CONV_HALF])
    for n, s in enumerate(subs):
        q_ref[s, :] = (_dot(ub[n], wm_ref[:, COL_Q:COL_K]) * (GLA_DK ** -0.5)).astype(BF16)
    for n, s in enumerate(subs):
        k_ref[s, :] = _dot(ub[n], wm_ref[:, COL_K:COL_V]).astype(BF16)
    for n, s in enumerate(subs):
        v_ref[s, :] = _dot(ub[n], wm_ref[:, COL_V:COL_R]).astype(BF16)
    for n, s in enumerate(subs):
        la = _decay_logs(z[n], wdec_ref, bdec_ref)
        laf_ref[s, :] = la[:, :HK]
        lab_ref[s, :] = la[:, HK:]
    for n, s in enumerate(subs):
        r = _dot(ub[n], wm_ref[:, COL_R:COL_DEC])
        rs_ref[s, :] = (r * _sigmoid(r)).astype(BF16)
    per_group = PERM_TOKENS // PROJ_SUB
    for n in range(x_ref.shape[0] // PERM_TOKENS):
        a_rm = jnp.concatenate(a_h[per_group * n:per_group * (n + 1)], axis=0)
        a_cm = _dot(perm_ref[...], a_rm)
        ah_ref[:, PERM_ROWS * n:PERM_ROWS * (n + 1), :] = a_cm.reshape(
            GRID_W, PERM_ROWS, CONV_HALF)
    for n, s in enumerate(subs):
        gt_ref[s, :] = _sigmoid(_dot(ub[n], wg_ref[...])).astype(BF16)


def _ctx_proj_kernel(x_ref, mod_ref, g_ref, wm_ref, wdin_ref, wdec_ref, bdec_ref,
                     k_ref, v_ref, laf_ref, lab_ref):
    ub = _prenorm_mod(x_ref[...], g_ref[...], mod_ref)
    z = _split_bf16(_dot(ub, wdin_ref[...]))
    k_ref[...] = _dot(ub, wm_ref[:, COL_K:COL_V]).astype(BF16)
    v_ref[...] = _dot(ub, wm_ref[:, COL_V:COL_R]).astype(BF16)
    la = _decay_logs(z, wdec_ref, bdec_ref)
    laf_ref[...] = la[:, :HK]
    lab_ref[...] = la[:, HK:]


def _const_spec(arr):
    nd = arr.ndim
    return pl.BlockSpec(arr.shape, lambda *_: (0,) * nd)


def _tok_spec(tm, width):
    return pl.BlockSpec((None, tm, width), lambda b, i: (b, i, 0))


def _colmajor_spec(tm):
    return pl.BlockSpec((None, GRID_W, tm // GRID_W, CONV_HALF), lambda b, i: (b, 0, i, 0))


def _single_spec(arr):
    nd = arr.ndim
    return pl.BlockSpec(arr.shape, lambda *_: (0,) * nd, pipeline_mode=pl.Buffered(1))


def _proj(x, mod, g, perm, wm, wdin, wdec, bdec, wg, tm):
    bsz, t, d = x.shape
    widths = (CONV_HALF, HK, HK, HV, HV, HK, HK, 2 * D_MODEL)
    dtypes = (BF16, BF16, BF16, BF16, BF16, F32, F32, BF16)
    assert tm % PERM_TOKENS == 0
    ah_shape = jax.ShapeDtypeStruct((bsz, GRID_W, t // GRID_W, CONV_HALF), F32)
    return pl.pallas_call(
        _proj_kernel,
        grid=(bsz, t // tm),
        in_specs=[
            _tok_spec(tm, d),
            pl.BlockSpec((None,) + mod.shape[1:], lambda b, i: (b, 0, 0)),
            _const_spec(g), _single_spec(perm), _single_spec(wm), _single_spec(wdin),
            _single_spec(wdec), _const_spec(bdec), _single_spec(wg),
        ],
        out_specs=[_colmajor_spec(tm)] + [_tok_spec(tm, w) for w in widths],
        out_shape=[ah_shape] + [jax.ShapeDtypeStruct((bsz, t, w), dt)
                                for w, dt in zip(widths, dtypes)],
        compiler_params=pltpu.CompilerParams(
            dimension_semantics=("parallel", "parallel"), vmem_limit_bytes=VMEM_LIMIT),
        name="proj",
    )(x, mod, g, perm, wm, wdin, wdec, bdec, wg)


def _ctx_proj(x, mod, g, wm, wdin, wdec, bdec, tm):
    bsz, t, d = x.shape
    widths = (HK, HV, HK, HK)
    dtypes = (BF16, BF16, F32, F32)
    return pl.pallas_call(
        _ctx_proj_kernel,
        grid=(bsz, t // tm),
        in_specs=[
            _tok_spec(tm, d),
            pl.BlockSpec((None,) + mod.shape[1:], lambda b, i: (0, 0, 0)),
            _const_spec(g), _const_spec(wm), _const_spec(wdin), _const_spec(wdec),
            _const_spec(bdec),
        ],
        out_specs=[_tok_spec(tm, w) for w in widths],
        out_shape=[jax.ShapeDtypeStruct((bsz, t, w), dt) for w, dt in zip(widths, dtypes)],
        compiler_params=pltpu.CompilerParams(
            dimension_semantics=("parallel", "parallel"), vmem_limit_bytes=VMEM_LIMIT),
        name="ctx_proj",
    )(x, mod, g, wm, wdin, wdec, bdec)


def _gla_scale(qs, ks, las, dirn, want_out):
    tri_b, mid, last = dirn["tri_b"], dirn["mid"], dirn["last"]
    la_hi, la_lo = _split_bf16(las)
    g = _dot(tri_b, la_hi) + _dot(tri_b, la_lo)
    u = dict(kt=[], kend=[], qt=[], qg=[], e=[])
    for c in range(SUPER // CHUNK):
        rows = slice(c * CHUNK, (c + 1) * CHUNK)
        gc = g[rows]
        g_mid = gc[mid:mid + 1]
        g_last = gc[last:last + 1]
        u["e"].append(jnp.exp(g_last))
        kt = ks[rows].astype(F32) * jnp.exp(g_mid - gc)
        u["kt"].append(kt)
        u["kend"].append(kt * jnp.exp(g_last - g_mid))
        if want_out:
            qt = qs[rows].astype(F32) * jnp.exp(gc - g_mid)
            u["qt"].append(qt.astype(BF16))
            u["qg"].append((qt * jnp.exp(g_mid)).astype(BF16))
    return u


def _pair_lanes(pair):
    return slice(LANES * pair, LANES * (pair + 1))


def _gla_scores(u, pair, head_lane):
    lanes = _pair_lanes(pair)
    kt = jnp.concatenate([c[:, lanes] for c in u["kt"]], axis=0)
    qt = jnp.concatenate([c[:, lanes] for c in u["qt"]], axis=0)
    kt2 = jnp.concatenate([(kt * hm).astype(BF16) for hm in head_lane], axis=0)
    return lax.dot_general(qt, kt2, NT, preferred_element_type=F32)


def _gla_update(u, vs, h, head_lane):
    lanes = _pair_lanes(h // 2)
    hm = head_lane[h % 2]
    k0 = (u["kend"][0][:, lanes] * hm).astype(BF16)
    k1 = (u["kend"][1][:, lanes] * hm).astype(BF16)
    z = jnp.zeros_like(k0)
    kbd = jnp.concatenate([jnp.concatenate([k0, z], axis=1),
                           jnp.concatenate([z, k1], axis=1)], axis=0)
    return lax.dot_general(vs[:, GLA_DV * h:GLA_DV * (h + 1)], kbd, TN,
                           preferred_element_type=F32)


def _gla_intra(scores, vs, pair, mask2):
    a = jnp.where(mask2, scores, 0.0).astype(BF16)
    v0 = vs[:, GLA_DV * 2 * pair:GLA_DV * (2 * pair + 1)]
    v1 = vs[:, GLA_DV * (2 * pair + 1):GLA_DV * (2 * pair + 2)]
    z = jnp.zeros_like(v0)
    vbd = jnp.concatenate([jnp.concatenate([v0, z], axis=1),
                           jnp.concatenate([z, v1], axis=1)], axis=0)
    return _dot(a, vbd)


def _gla_kernel(q_ref, k_ref, v_ref, laf_ref, lab_ref, kc_ref, vc_ref, lafc_ref, labc_ref,
                rs_ref, g_ref, o_ref, of_ref, ob_ref, sf_ref, sb_ref):
    t = q_ref.shape[0]
    t_ctx = kc_ref.shape[0]
    half = t // 2
    n_steps = t // STEP
    supers = STEP // SUPER
    n_chunks = SUPER // CHUNK
    pairs = GLA_HEADS // 2

    row = lax.broadcasted_iota(jnp.int32, (SUPER, SUPER), 0)
    col = lax.broadcasted_iota(jnp.int32, (SUPER, SUPER), 1)
    same_chunk = (row // CHUNK) == (col // CHUNK)
    low_mask = jnp.logical_and(same_chunk, col <= row)
    up_mask = jnp.logical_and(same_chunk, col >= row)
    lane = lax.broadcasted_iota(jnp.int32, (1, LANES), 1)
    head_lane = (jnp.where(lane < GLA_DK, 1.0, 0.0), jnp.where(lane >= GLA_DK, 1.0, 0.0))

    fwd = dict(tri_b=jnp.where(low_mask, 1.0, 0.0).astype(BF16),
               mask2=jnp.concatenate([low_mask, low_mask], axis=1),
               mid=CHUNK // 2 - 1, last=CHUNK - 1, order=(0, 1), st_ref=sf_ref)
    bwd = dict(tri_b=jnp.where(up_mask, 1.0, 0.0).astype(BF16),
               mask2=jnp.concatenate([up_mask, up_mask], axis=1),
               mid=CHUNK // 2, last=0, order=(1, 0), st_ref=sb_ref)

    sf_ref[...] = jnp.zeros_like(sf_ref)
    sb_ref[...] = jnp.zeros_like(sb_ref)
    gn = g_ref[...]

    def run(units, want_out, finalize):
        vals = []
        for dirn, refs, t0 in units:
            q_r, k_r, v_r, la_r = refs
            rows = pl.ds(t0, SUPER)
            qs = q_r[rows, :] if want_out else None
            vals.append((_gla_scale(qs, k_r[rows, :], la_r[rows, :], dirn, want_out),
                         v_r[rows, :]))
        scores = []
        if want_out:
            scores = [[_gla_scores(u, p, head_lane) for p in range(pairs)] for u, _ in vals]
        upd = [[_gla_update(u, vs, h, head_lane) for h in range(GLA_HEADS)] for u, vs in vals]
        intra = []
        if want_out:
            intra = [[_gla_intra(scores[n][p], vals[n][1], p, units[n][0]["mask2"])
                      for p in range(pairs)] for n in range(len(units))]
        states = {}
        for n, (dirn, _, t0) in enumerate(units):
            key = id(dirn["st_ref"])
            if key not in states:
                states[key] = (dirn["st_ref"], [dirn["st_ref"][h] for h in range(GLA_HEADS)])
            st = states[key][1]
            u = vals[n][0]
            inter = [[None] * n_chunks for _ in range(pairs)]
            for c in dirn["order"]:
                if want_out:
                    for p in range(pairs):
                        st2 = jnp.concatenate([st[2 * p].astype(BF16),
                                               st[2 * p + 1].astype(BF16)], axis=0)
                        inter[p][c] = lax.dot_general(u["qg"][c][:, _pair_lanes(p)], st2, NT,
                                                      preferred_element_type=F32)
                for h in range(GLA_HEADS):
                    e = u["e"][c][:, _pair_lanes(h // 2)]
                    st[h] = st[h] * e + upd[n][h][:, LANES * c:LANES * (c + 1)]
            if not want_out:
                continue
            fwd_unit = dirn is fwd
            for p in range(pairs):
                o2 = intra[n][p] + jnp.concatenate(inter[p], axis=0)
                for j in range(2):
                    cols = slice(GLA_DV * (2 * p + j), GLA_DV * (2 * p + j + 1))
                    o = o2[:, GLA_DV * j:GLA_DV * (j + 1)]
                    if finalize:
                        other = (ob_ref[pl.ds(t0 - half, SUPER), cols] if fwd_unit
                                 else of_ref[pl.ds(t0, SUPER), cols])
                        rs = rs_ref[pl.ds(t0, SUPER), cols].astype(F32)
                        o_ref[pl.ds(t0, SUPER), cols] = (_rms(o + other, gn) * rs).astype(BF16)
                    elif fwd_unit:
                        of_ref[pl.ds(t0, SUPER), cols] = o
                    else:
                        ob_ref[pl.ds(t0 - half, SUPER), cols] = o
        for st_ref, st in states.values():
            for h in range(GLA_HEADS):
                st_ref[h] = st[h]

    ctx_units = []
    for s in range(t_ctx // SUPER):
        ctx_units.append((fwd, (None, kc_ref, vc_ref, lafc_ref), s * SUPER))
        ctx_units.append((bwd, (None, kc_ref, vc_ref, labc_ref), t_ctx - (s + 1) * SUPER))
    run(ctx_units, want_out=False, finalize=False)

    def step(i, finalize):
        units = []
        for s in range(supers):
            t0 = pl.multiple_of(i * STEP + s * SUPER, SUPER)
            t1 = pl.multiple_of(t - (i + 1) * STEP + (supers - 1 - s) * SUPER, SUPER)
            units.append((fwd, (q_ref, k_ref, v_ref, laf_ref), t0))
            units.append((bwd, (q_ref, k_ref, v_ref, lab_ref), t1))
        run(units, want_out=True, finalize=finalize)

    def park_body(i, carry):
        step(i, False)
        return carry

    def finish_body(i, carry):
        step(i, True)
        return carry

    lax.fori_loop(0, n_steps // 2, park_body, 0)
    lax.fori_loop(n_steps // 2, n_steps, finish_body, 0)


def _gla(q, k, v, laf, lab, kc, vc, lafc, labc, rs, gnorm):
    bsz, t, _ = q.shape
    assert t % (2 * STEP) == 0 and kc.shape[1] % SUPER == 0

    def bspec(arr):
        return pl.BlockSpec((None,) + arr.shape[1:], lambda b: (b, 0, 0))

    args = (q, k, v, laf, lab, kc, vc, lafc, labc, rs)
    return pl.pallas_call(
        _gla_kernel,
        grid=(bsz,),
        in_specs=[bspec(a) for a in args] + [_const_spec(gnorm)],
        out_specs=pl.BlockSpec((None, t, HV), lambda b: (b, 0, 0)),
        out_shape=jax.ShapeDtypeStruct((bsz, t, HV), BF16),
        scratch_shapes=[
            pltpu.VMEM((t // 2, HV), F32), pltpu.VMEM((t // 2, HV), F32),
            pltpu.VMEM((GLA_HEADS, GLA_DV, LANES), F32),
            pltpu.VMEM((GLA_HEADS, GLA_DV, LANES), F32),
        ],
        compiler_params=pltpu.CompilerParams(
            dimension_semantics=("parallel",), vmem_limit_bytes=VMEM_LIMIT),
        name="gla",
    )(*args, gnorm)


def _conv_kernel(ah_ref, av_ref, w_ref, b_ref, yh_ref, yv_ref, ph_ref, pv_ref):
    t = ah_ref.shape[0]
    rows = t // GRID_W
    tile = 64

    def conv(src_ref, pad_ref, w_cols, line, dst_ref):
        pad = CONV_PAD * line
        pad_ref[pl.ds(0, pad), :] = jnp.zeros((pad, CONV_HALF), F32)
        pad_ref[pl.ds(pad + t, pad), :] = jnp.zeros((pad, CONV_HALF), F32)

        def fill(i, carry):
            t0 = pl.multiple_of(i * tile, tile)
            pad_ref[pl.ds(pl.multiple_of(pad + t0, 8), tile), :] = (
                src_ref[pl.ds(t0, tile), :].astype(F32))
            return carry

        lax.fori_loop(0, t // tile, fill, 0)
        bias = b_ref[:, w_cols]

        def body(i, carry):
            t0 = pl.multiple_of(i * tile, tile)
            acc = jnp.broadcast_to(bias, (tile, CONV_HALF))
            for kk in range(CONV_K):
                win = pad_ref[pl.ds(pl.multiple_of(t0 + kk * line, 8), tile), :]
                acc = acc + w_ref[kk:kk + 1, w_cols] * win
            dst_ref[pl.ds(t0, tile), :] = acc
            return carry

        lax.fori_loop(0, t // tile, body, 0)

    conv(ah_ref, ph_ref, slice(0, CONV_HALF), rows, yh_ref)
    conv(av_ref, pv_ref, slice(CONV_HALF, CONV_DIM), GRID_W, yv_ref)


def _conv(ah, av, w, b):
    bsz, t, c = ah.shape
    rows = t // GRID_W
    tok = pl.BlockSpec((None, t, c), lambda i: (i, 0, 0))
    return pl.pallas_call(
        _conv_kernel,
        grid=(bsz,),
        in_specs=[tok, tok, _const_spec(w), _const_spec(b)],
        out_specs=[tok, tok],
        out_shape=[jax.ShapeDtypeStruct((bsz, t, c), F32)] * 2,
        scratch_shapes=[
            pltpu.VMEM((t + 2 * CONV_PAD * rows, c), F32),
            pltpu.VMEM((t + 2 * CONV_PAD * GRID_W, c), F32),
        ],
        compiler_params=pltpu.CompilerParams(
            dimension_semantics=("parallel",), vmem_limit_bytes=VMEM_LIMIT),
        name="conv",
    )(ah, av, w, b)


def _post_kernel(x_ref, yh_ref, yv_ref, og_ref, gt_ref, mod_ref, perm_ref, lng_ref, lnb_ref,
                 gpost1_ref, gpre2_ref, gpost2_ref, wco_ref, wgo_ref, wout_ref, wff1_ref,
                 wff2_ref, o_ref):
    d = x_ref.shape[-1]
    d_ff = wff1_ref.shape[1]
    ff_tile = 1024
    sub = 256
    subs = [slice(s, s + sub) for s in range(0, x_ref.shape[0], sub)]
    gt1 = mod_ref[2:3, :]
    sh2 = mod_ref[3:4, :]
    sc2 = mod_ref[4:5, :]
    gt2 = mod_ref[5:6, :]

    yh = []
    for n in range(x_ref.shape[0] // PERM_TOKENS):
        blk = yh_ref[:, PERM_ROWS * n:PERM_ROWS * (n + 1), :].reshape(PERM_TOKENS, CONV_HALF)
        hi, lo = _split_bf16(blk)
        yh.append(_dot(perm_ref[...], hi) + _dot(perm_ref[...], lo))
    yh = jnp.concatenate(yh, axis=0)

    yn = []
    for s in subs:
        yc = jnp.concatenate([yh[s], yv_ref[s, :]], axis=-1)
        yc = yc - jnp.mean(yc, axis=-1, keepdims=True)
        var = jnp.mean(yc * yc, axis=-1, keepdims=True)
        z = yc * lax.rsqrt(var + EPS) * lng_ref[...] + lnb_ref[...]
        yn.append((z * _sigmoid(z)).astype(BF16))
    y_gla = [_dot(og_ref[s, :], wgo_ref[...]) for s in subs]
    y_conv = [_dot(v, wco_ref[...]) for v in yn]
    merged = []
    for n, s in enumerate(subs):
        gates = gt_ref[s, :].astype(F32)
        merged.append((gates[:, :d] * y_conv[n] + gates[:, d:] * y_gla[n]).astype(BF16))
    y = [_dot(v, wout_ref[...]) for v in merged]
    h_mid = [x_ref[s, :] + gt1 * _rms(y[n], gpost1_ref[...]) for n, s in enumerate(subs)]
    u2 = [(_rms(h, gpre2_ref[...]) * (1.0 + sc2) + sh2).astype(BF16) for h in h_mid]
    acc = [jnp.zeros(h.shape, F32) for h in h_mid]
    for j in range(d_ff // ff_tile):
        cols = slice(j * ff_tile, (j + 1) * ff_tile)
        f = [_dot(v, wff1_ref[:, cols]) for v in u2]
        f = [jnp.square(jnp.maximum(v, 0.0)).astype(BF16) for v in f]
        acc = [acc[n] + _dot(f[n], wff2_ref[cols, :]) for n in range(len(subs))]
    for n, s in enumerate(subs):
        o_ref[s, :] = h_mid[n] + gt2 * _rms(acc[n], gpost2_ref[...])


def _post(x, yh, yv, og, gates, mod, perm, ln_g, ln_b, gpost1, gpre2, gpost2, wco, wgo, wout,
          wff1, wff2, tm):
    bsz, t, d = x.shape
    assert tm % PERM_TOKENS == 0
    return pl.pallas_call(
        _post_kernel,
        grid=(bsz, t // tm),
        in_specs=[
            _tok_spec(tm, d), _colmajor_spec(tm), _tok_spec(tm, CONV_HALF), _tok_spec(tm, HV),
            _tok_spec(tm, 2 * d),
            pl.BlockSpec((None,) + mod.shape[1:], lambda b, i: (b, 0, 0)),
            _single_spec(perm), _const_spec(ln_g), _const_spec(ln_b),
            _const_spec(gpost1), _const_spec(gpre2), _const_spec(gpost2),
            _single_spec(wco), _single_spec(wgo), _single_spec(wout), _single_spec(wff1),
            _single_spec(wff2),
        ],
        out_specs=_tok_spec(tm, d),
        out_shape=jax.ShapeDtypeStruct((bsz, t, d), F32),
        compiler_params=pltpu.CompilerParams(
            dimension_semantics=("parallel", "parallel"), vmem_limit_bytes=VMEM_LIMIT),
        name="post",
    )(x, yh, yv, og, gates, mod, perm, ln_g, ln_b, gpost1, gpre2, gpost2, wco, wgo, wout, wff1,
      wff2)


def kernel(x, c, ctx, c_ctx, w_mod, b_mod, g_pre1, g_post1, g_pre2, g_post2, w_in, conv_w,
           conv_b, conv_ln_g, conv_ln_b, w_conv_out, w_decay, b_decay, gla_norm_g, w_gla_out,
           w_out, w_ff1, w_ff2):
    bsz, t, d = x.shape
    depth = w_in.shape[0]
    assert depth == 1 and d == D_MODEL and t % (GRID_W * 8) == 0

    n_rows = -(-(bsz + 1) // 8) * 8
    c_all = jnp.zeros((n_rows, d), F32).at[:bsz].set(c).at[bsz].set(c_ctx)
    m = _adaln(c_all, w_mod, b_mod)
    mod = m[:bsz].reshape(bsz, N_MOD, d)
    mod_ctx = m[bsz, :2 * d].reshape(1, 2, d)

    wm, wdin, wg = _split_w_in(w_in)
    wdec = jnp.zeros((LANES, 2 * HK), F32)
    wdec = wdec.at[:DECAY_RANK, :HK].set(w_decay[0, 0])
    wdec = wdec.at[DECAY_RANK:2 * DECAY_RANK, HK:].set(w_decay[0, 1]).astype(BF16)
    bdec = b_decay[0].reshape(1, 2 * HK)
    row = lambda v: v.reshape(1, -1)

    perm = _to_colmajor_perm()
    to_colmajor = jnp.asarray(perm, BF16)
    to_rowmajor = jnp.asarray(perm.T, BF16)

    ah, av, q, k, v, rs, laf, lab, gates = _proj(
        x, mod, row(g_pre1[0]), to_colmajor, wm, wdin, wdec, bdec, wg, tm=1024)
    kc, vc, lafc, labc = _ctx_proj(
        ctx, mod_ctx, row(g_pre1[0]), wm, wdin, wdec, bdec, tm=ctx.shape[1])

    og = _gla(q, k, v, laf, lab, kc, vc, lafc, labc, rs, row(gla_norm_g[0]))
    yh, yv = _conv(ah.reshape(bsz, t, CONV_HALF), av, conv_w[0], row(conv_b[0]))
    yh = yh.reshape(bsz, GRID_W, t // GRID_W, CONV_HALF)

    return _post(x, yh, yv, og, gates, mod, to_rowmajor, row(conv_ln_g[0]), row(conv_ln_b[0]),
                 row(g_post1[0]), row(g_pre2[0]), row(g_post2[0]),
                 w_conv_out[0].astype(BF16), w_gla_out[0].astype(BF16), w_out[0].astype(BF16),
                 w_ff1[0].astype(BF16), w_ff2[0].astype(BF16), tm=512)
```

```python
import functools

import jax
import jax.numpy as jnp
import numpy as np
from jax import lax
from jax.experimental import pallas as pl
from jax.experimental.pallas import tpu as pltpu

D_MODEL = 1024
GRID_W = 64
CONV_DIM = 512
CONV_K = 31
GLA_HEADS = 4
GLA_DK = 64
GLA_DV = 128
DECAY_RANK = 16
GATE_NORM = 16.0
CHUNK = 64
N_MOD = 6
EPS = 1e-6

COL_Q = 2 * CONV_DIM
COL_K = COL_Q + GLA_HEADS * GLA_DK
COL_V = COL_K + GLA_HEADS * GLA_DK
COL_R = COL_V + GLA_HEADS * GLA_DV
COL_DEC = COL_R + GLA_HEADS * GLA_DV
COL_GATE = COL_DEC + 2 * DECAY_RANK
COL_END = COL_GATE + 2 * D_MODEL

LANES = 128
HK = GLA_HEADS * GLA_DK
HV = GLA_HEADS * GLA_DV
CONV_HALF = CONV_DIM // 2
CONV_PAD = CONV_K // 2
PERM_ROWS = 8
PERM_TOKENS = PERM_ROWS * GRID_W
PROJ_SUB = 256
SUPER = 2 * CHUNK
STEP = 2 * SUPER

VMEM_LIMIT = 56 * 1024 * 1024

F32 = jnp.float32
BF16 = jnp.bfloat16

NT = (((1,), (1,)), ((), ()))
TN = (((0,), (0,)), ((), ()))


def _dot(a, b):
    return jnp.dot(a, b, preferred_element_type=F32)


def _rms(x, g):
    ms = jnp.mean(x * x, axis=-1, keepdims=True)
    return x * lax.rsqrt(ms + EPS) * g


def _sigmoid(x):
    return jax.nn.sigmoid(x)


def _split_bf16(x):
    hi = x.astype(BF16)
    lo = (x - hi.astype(F32)).astype(BF16)
    return hi, lo


def _to_colmajor_perm():
    out = np.arange(PERM_TOKENS)
    src = (out % PERM_ROWS) * GRID_W + out // PERM_ROWS
    p = np.zeros((PERM_TOKENS, PERM_TOKENS), np.float32)
    p[out, src] = 1.0
    return p


def _adaln_kernel(c_ref, w_ref, b_ref, o_ref):
    c = c_ref[...]
    s_hi, s_lo = _split_bf16(c * _sigmoid(c))
    w_hi, w_lo = _split_bf16(w_ref[...])
    o_ref[...] = _dot(s_hi, w_hi) + _dot(s_lo, w_hi) + _dot(s_hi, w_lo) + b_ref[...]


def _adaln(c_all, w_mod, b_mod):
    n_rows, d = c_all.shape
    n_out = w_mod.shape[-1]
    tn = 1536
    return pl.pallas_call(
        _adaln_kernel,
        grid=(n_out // tn,),
        in_specs=[
            pl.BlockSpec((n_rows, d), lambda j: (0, 0)),
            pl.BlockSpec((None, d, tn), lambda j: (0, 0, j)),
            pl.BlockSpec((1, tn), lambda j: (0, j)),
        ],
        out_specs=pl.BlockSpec((n_rows, tn), lambda j: (0, j)),
        out_shape=jax.ShapeDtypeStruct((n_rows, n_out), F32),
        compiler_params=pltpu.CompilerParams(
            dimension_semantics=("arbitrary",), vmem_limit_bytes=VMEM_LIMIT),
        name="adaln",
    )(c_all, w_mod, b_mod)


def _split_w_in_kernel(wm_t_ref, wd_t_ref, wg_t_ref, wm_ref, wdin_ref, wg_ref, *, n_gate):
    j = pl.program_id(0)
    wm_ref[...] = wm_t_ref[...].astype(BF16).T

    @pl.when(j == 0)
    def _():
        wdin_ref[...] = wd_t_ref[...].astype(BF16).T

    @pl.when(j < n_gate)
    def _():
        wg_ref[...] = wg_t_ref[...].astype(BF16).T


def _split_w_in(w_in_t):
    n, d = w_in_t.shape
    tr = 256
    n_main = COL_DEC // tr
    n_gate = (n - COL_GATE) // tr
    assert COL_DEC % tr == 0 and (n - COL_GATE) % tr == 0 and n_gate <= n_main
    gate_blk = lambda j: jnp.minimum(j, n_gate - 1)
    return pl.pallas_call(
        functools.partial(_split_w_in_kernel, n_gate=n_gate),
        grid=(n_main,),
        in_specs=[
            pl.BlockSpec((tr, d), lambda j: (j, 0)),
            pl.BlockSpec((LANES, d), lambda j: (COL_DEC // LANES, 0)),
            pl.BlockSpec((pl.Element(tr), pl.Element(d)),
                         lambda j: (pl.multiple_of(COL_GATE + tr * gate_blk(j), 32), 0)),
        ],
        out_specs=[
            pl.BlockSpec((d, tr), lambda j: (0, j)),
            pl.BlockSpec((d, LANES), lambda j: (0, 0)),
            pl.BlockSpec((d, tr), lambda j: (0, gate_blk(j))),
        ],
        out_shape=[jax.ShapeDtypeStruct((d, w), BF16) for w in (COL_DEC, LANES, n - COL_GATE)],
        compiler_params=pltpu.CompilerParams(
            dimension_semantics=("arbitrary",), vmem_limit_bytes=VMEM_LIMIT),
        name="split_w_in",
    )(w_in_t, w_in_t, w_in_t)


def _prenorm_mod(x, g, mod_ref):
    sh = mod_ref[0:1, :]
    sc = mod_ref[1:2, :]
    return (_rms(x, g) * (1.0 + sc) + sh).astype(BF16)


def _decay_logs(z, wdec_ref, bdec_ref):
    z_hi, z_lo = z
    logits = _dot(z_hi, wdec_ref[...]) + _dot(z_lo, wdec_ref[...]) + bdec_ref[...]
    ls = jnp.minimum(logits, 0.0) - jnp.log(1.0 + jnp.exp(-jnp.abs(logits)))
    return ls * (1.0 / GATE_NORM)


def _proj_kernel(x_ref, mod_ref, g_ref, perm_ref, wm_ref, wdin_ref, wdec_ref, bdec_ref, wg_ref,
                 ah_ref, av_ref, q_ref, k_ref, v_ref, rs_ref, laf_ref, lab_ref, gt_ref):
    subs = [slice(s, s + PROJ_SUB) for s in range(0, x_ref.shape[0], PROJ_SUB)]
    g = g_ref[...]
    ub = [_prenorm_mod(x_ref[s, :], g, mod_ref) for s in subs]
    z = [_split_bf16(_dot(u, wdin_ref[...])) for u in ub]
    a_h = []
    for n, s in enumerate(subs):
        glu = _dot(ub[n], wm_ref[:, 0:COL_Q])
        a = (glu[:, :CONV_DIM] * _sigmoid(glu[:, CONV_DIM:])).astype(BF16)
        av_ref[s, :] = a[:, CONV_HALF:]
        a_h.append(a[:, :CONV_HALF])
    for n, s in enumerate(subs):
        q_ref[s, :] = (_dot(ub[n], wm_ref[:, COL_Q:COL_K]) * (GLA_DK ** -0.5)).astype(BF16)
    for n, s in enumerate(subs):
        k_ref[s, :] = _dot(ub[n], wm_ref[:, COL_K:COL_V]).astype(BF16)
    for n, s in enumerate(subs):
        v_ref[s, :] = _dot(ub[n], wm_ref[:, COL_V:COL_R]).astype(BF16)
    for n, s in enumerate(subs):
        la = _decay_logs(z[n], wdec_ref, bdec_ref)
        laf_ref[s, :] = la[:, :HK]
        lab_ref[s, :] = la[:, HK:]
    for n, s in enumerate(subs):
        r = _dot(ub[n], wm_ref[:, COL_R:COL_DEC])
        rs_ref[s, :] = (r * _sigmoid(r)).astype(BF16)
    per_group = PERM_TOKENS // PROJ_SUB
    for n in range(x_ref.shape[0] // PERM_TOKENS):
        a_rm = jnp.concatenate(a_h[per_group * n:per_group * (n + 1)], axis=0)
        a_cm = _dot(perm_ref[...], a_rm)
        ah_ref[:, PERM_ROWS * n:PERM_ROWS * (n + 1), :] = a_cm.reshape(
            GRID_W, PERM_ROWS, CONV_HALF)
    for n, s in enumerate(subs):
        gt_ref[s, :] = _sigmoid(_dot(ub[n], wg_ref[...])).astype(BF16)


def _ctx_proj_kernel(x_ref, mod_ref, g_ref, wm_ref, wdin_ref, wdec_ref, bdec_ref,
                     k_ref, v_ref, laf_ref, lab_ref):
    ub = _prenorm_mod(x_ref[...], g_ref[...], mod_ref)
    z = _split_bf16(_dot(ub, wdin_ref[...]))
    k_ref[...] = _dot(ub, wm_ref[:, COL_K:COL_V]).astype(BF16)
    v_ref[...] = _dot(ub, wm_ref[:, COL_V:COL_R]).astype(BF16)
    la = _decay_logs(z, wdec_ref, bdec_ref)
    laf_ref[...] = la[:, :HK]
    lab_ref[...] = la[:, HK:]


def _const_spec(arr):
    nd = arr.ndim
    return pl.BlockSpec(arr.shape, lambda *_: (0,) * nd)


def _tok_spec(tm, width):
    return pl.BlockSpec((None, tm, width), lambda b, i: (b, i, 0))


def _colmajor_spec(tm):
    return pl.BlockSpec((None, GRID_W, tm // GRID_W, CONV_HALF), lambda b, i: (b, 0, i, 0))


def _single_spec(arr):
    nd = arr.ndim
    return pl.BlockSpec(arr.shape, lambda *_: (0,) * nd, pipeline_mode=pl.Buffered(1))


def _proj(x, mod, g, perm, wm, wdin, wdec, bdec, wg, tm):
    bsz, t, d = x.shape
    widths = (CONV_HALF, HK, HK, HV, HV, HK, HK, 2 * D_MODEL)
    dtypes = (BF16, BF16, BF16, BF16, BF16, F32, F32, BF16)
    assert tm % PERM_TOKENS == 0
    ah_shape = jax.ShapeDtypeStruct((bsz, GRID_W, t // GRID_W, CONV_HALF), F32)
    return pl.pallas_call(
        _proj_kernel,
        grid=(bsz, t // tm),
        in_specs=[
            _tok_spec(tm, d),
            pl.BlockSpec((None,) + mod.shape[1:], lambda b, i: (b, 0, 0)),
            _const_spec(g), _single_spec(perm), _single_spec(wm), _single_spec(wdin),
            _single_spec(wdec), _const_spec(bdec), _single_spec(wg),
        ],
        out_specs=[_colmajor_spec(tm)] + [_tok_spec(tm, w) for w in widths],
        out_shape=[ah_shape] + [jax.ShapeDtypeStruct((bsz, t, w), dt)
                                for w, dt in zip(widths, dtypes)],
        compiler_params=pltpu.CompilerParams(
            dimension_semantics=("parallel", "parallel"), vmem_limit_bytes=VMEM_LIMIT),
        name="proj",
    )(x, mod, g, perm, wm, wdin, wdec, bdec, wg)


def _ctx_proj(x, mod, g, wm, wdin, wdec, bdec, tm):
    bsz, t, d = x.shape
    widths = (HK, HV, HK, HK)
    dtypes = (BF16, BF16, F32, F32)
    return pl.pallas_call(
        _ctx_proj_kernel,
        grid=(bsz, t // tm),
        in_specs=[
            _tok_spec(tm, d),
            pl.BlockSpec((None,) + mod.shape[1:], lambda b, i: (0, 0, 0)),
            _const_spec(g), _const_spec(wm), _const_spec(wdin), _const_spec(wdec),
            _const_spec(bdec),
        ],
        out_specs=[_tok_spec(tm, w) for w in widths],
        out_shape=[jax.ShapeDtypeStruct((bsz, t, w), dt) for w, dt in zip(widths, dtypes)],
        compiler_params=pltpu.CompilerParams(
            dimension_semantics=("parallel", "parallel"), vmem_limit_bytes=VMEM_LIMIT),
        name="ctx_proj",
    )(x, mod, g, wm, wdin, wdec, bdec)


def _gla_scale(qs, ks, las, dirn, want_out):
    tri_b, mid, last = dirn["tri_b"], dirn["mid"], dirn["last"]
    la_hi, la_lo = _split_bf16(las)
    g = _dot(tri_b, la_hi) + _dot(tri_b, la_lo)
    u = dict(kt=[], kend=[], qt=[], qg=[], e=[])
    for c in range(SUPER // CHUNK):
        rows = slice(c * CHUNK, (c + 1) * CHUNK)
        gc = g[rows]
        g_mid = gc[mid:mid + 1]
        g_last = gc[last:last + 1]
        u["e"].append(jnp.exp(g_last))
        kt = ks[rows].astype(F32) * jnp.exp(g_mid - gc)
        u["kt"].append(kt)
        u["kend"].append(kt * jnp.exp(g_last - g_mid))
        if want_out:
            qt = qs[rows].astype(F32) * jnp.exp(gc - g_mid)
            u["qt"].append(qt.astype(BF16))
            u["qg"].append((qt * jnp.exp(g_mid)).astype(BF16))
    return u


def _pair_lanes(pair):
    return slice(LANES * pair, LANES * (pair + 1))


def _gla_scores(u, pair, head_lane):
    lanes = _pair_lanes(pair)
    kt = jnp.concatenate([c[:, lanes] for c in u["kt"]], axis=0)
    qt = jnp.concatenate([c[:, lanes] for c in u["qt"]], axis=0)
    kt2 = jnp.concatenate([(kt * hm).astype(BF16) for hm in head_lane], axis=0)
    return lax.dot_general(qt, kt2, NT, preferred_element_type=F32)


def _gla_update(u, vs, h, head_lane):
    lanes = _pair_lanes(h // 2)
    hm = head_lane[h % 2]
    k0 = (u["kend"][0][:, lanes] * hm).astype(BF16)
    k1 = (u["kend"][1][:, lanes] * hm).astype(BF16)
    z = jnp.zeros_like(k0)
    kbd = jnp.concatenate([jnp.concatenate([k0, z], axis=1),
                           jnp.concatenate([z, k1], axis=1)], axis=0)
    return lax.dot_general(vs[:, GLA_DV * h:GLA_DV * (h + 1)], kbd, TN,
                           preferred_element_type=F32)


def _gla_intra(scores, vs, pair, mask2):
    a = jnp.where(mask2, scores, 0.0).astype(BF16)
    v0 = vs[:, GLA_DV * 2 * pair:GLA_DV * (2 * pair + 1)]
    v1 = vs[:, GLA_DV * (2 * pair + 1):GLA_DV * (2 * pair + 2)]
    z = jnp.zeros_like(v0)
    vbd = jnp.concatenate([jnp.concatenate([v0, z], axis=1),
                           jnp.concatenate([z, v1], axis=1)], axis=0)
    return _dot(a, vbd)


def _gla_kernel(q_ref, k_ref, v_ref, laf_ref, lab_ref, kc_ref, vc_ref, lafc_ref, labc_ref,
                rs_ref, g_ref, o_ref, of_ref, ob_ref, sf_ref, sb_ref):
    t = q_ref.shape[0]
    t_ctx = kc_ref.shape[0]
    half = t // 2
    n_steps = t // STEP
    supers = STEP // SUPER
    n_chunks = SUPER // CHUNK
    pairs = GLA_HEADS // 2

    row = lax.broadcasted_iota(jnp.int32, (SUPER, SUPER), 0)
    col = lax.broadcasted_iota(jnp.int32, (SUPER, SUPER), 1)
    same_chunk = (row // CHUNK) == (col // CHUNK)
    low_mask = jnp.logical_and(same_chunk, col <= row)
    up_mask = jnp.logical_and(same_chunk, col >= row)
    lane = lax.broadcasted_iota(jnp.int32, (1, LANES), 1)
    head_lane = (jnp.where(lane < GLA_DK, 1.0, 0.0), jnp.where(lane >= GLA_DK, 1.0, 0.0))

    fwd = dict(tri_b=jnp.where(low_mask, 1.0, 0.0).astype(BF16),
               mask2=jnp.concatenate([low_mask, low_mask], axis=1),
               mid=CHUNK // 2 - 1, last=CHUNK - 1, order=(0, 1), st_ref=sf_ref)
    bwd = dict(tri_b=jnp.where(up_mask, 1.0, 0.0).astype(BF16),
               mask2=jnp.concatenate([up_mask, up_mask], axis=1),
               mid=CHUNK // 2, last=0, order=(1, 0), st_ref=sb_ref)

    sf_ref[...] = jnp.zeros_like(sf_ref)
    sb_ref[...] = jnp.zeros_like(sb_ref)
    gn = g_ref[...]

    def run(units, want_out, finalize):
        vals = []
        for dirn, refs, t0 in units:
            q_r, k_r, v_r, la_r = refs
            rows = pl.ds(t0, SUPER)
            qs = q_r[rows, :] if want_out else None
            vals.append((_gla_scale(qs, k_r[rows, :], la_r[rows, :], dirn, want_out),
                         v_r[rows, :]))
        scores = []
        if want_out:
            scores = [[_gla_scores(u, p, head_lane) for p in range(pairs)] for u, _ in vals]
        upd = [[_gla_update(u, vs, h, head_lane) for h in range(GLA_HEADS)] for u, vs in vals]
        intra = []
        if want_out:
            intra = [[_gla_intra(scores[n][p], vals[n][1], p, units[n][0]["mask2"])
                      for p in range(pairs)] for n in range(len(units))]
        states = {}
        for n, (dirn, _, t0) in enumerate(units):
            key = id(dirn["st_ref"])
            if key not in states:
                states[key] = (dirn["st_ref"], [dirn["st_ref"][h] for h in range(GLA_HEADS)])
            st = states[key][1]
            u = vals[n][0]
            inter = [[None] * n_chunks for _ in range(pairs)]
            for c in dirn["order"]:
                if want_out:
                    for p in range(pairs):
                        st2 = jnp.concatenate([st[2 * p].astype(BF16),
                                               st[2 * p + 1].astype(BF16)], axis=0)
                        inter[p][c] = lax.dot_general(u["qg"][c][:, _pair_lanes(p)], st2, NT,
                                                      preferred_element_type=F32)
                for h in range(GLA_HEADS):
                    e = u["e"][c][:, _pair_lanes(h // 2)]
                    st[h] = st[h] * e + upd[n][h][:, LANES * c:LANES * (c + 1)]
            if not want_out:
                continue
            fwd_unit = dirn is fwd
            for p in range(pairs):
                o2 = intra[n][p] + jnp.concatenate(inter[p], axis=0)
                for j in range(2):
                    cols = slice(GLA_DV * (2 * p + j), GLA_DV * (2 * p + j + 1))
                    o = o2[:, GLA_DV * j:GLA_DV * (j + 1)]
                    if finalize:
                        other = (ob_ref[pl.ds(t0 - half, SUPER), cols] if fwd_unit
                                 else of_ref[pl.ds(t0, SUPER), cols])
                        rs = rs_ref[pl.ds(t0, SUPER), cols].astype(F32)
                        o_ref[pl.ds(t0, SUPER), cols] = (_rms(o + other, gn) * rs).astype(BF16)
                    elif fwd_unit:
                        of_ref[pl.ds(t0, SUPER), cols] = o
                    else:
                        ob_ref[pl.ds(t0 - half, SUPER), cols] = o
        for st_ref, st in states.values():
            for h in range(GLA_HEADS):
                st_ref[h] = st[h]

    ctx_units = []
    for s in range(t_ctx // SUPER):
        ctx_units.append((fwd, (None, kc_ref, vc_ref, lafc_ref), s * SUPER))
        ctx_units.append((bwd, (None, kc_ref, vc_ref, labc_ref), t_ctx - (s + 1) * SUPER))
    run(ctx_units, want_out=False, finalize=False)

    def step(i, finalize):
        units = []
        for s in range(supers):
            t0 = pl.multiple_of(i * STEP + s * SUPER, SUPER)
            t1 = pl.multiple_of(t - (i + 1) * STEP + (supers - 1 - s) * SUPER, SUPER)
            units.append((fwd, (q_ref, k_ref, v_ref, laf_ref), t0))
            units.append((bwd, (q_ref, k_ref, v_ref, lab_ref), t1))
        run(units, want_out=True, finalize=finalize)

    def park_body(i, carry):
        step(i, False)
        return carry

    def finish_body(i, carry):
        step(i, True)
        return carry

    lax.fori_loop(0, n_steps // 2, park_body, 0)
    lax.fori_loop(n_steps // 2, n_steps, finish_body, 0)


def _gla(q, k, v, laf, lab, kc, vc, lafc, labc, rs, gnorm):
    bsz, t, _ = q.shape
    assert t % (2 * STEP) == 0 and kc.shape[1] % SUPER == 0

    def bspec(arr):
        return pl.BlockSpec((None,) + arr.shape[1:], lambda b: (b, 0, 0))

    args = (q, k, v, laf, lab, kc, vc, lafc, labc, rs)
    return pl.pallas_call(
        _gla_kernel,
        grid=(bsz,),
        in_specs=[bspec(a) for a in args] + [_const_spec(gnorm)],
        out_specs=pl.BlockSpec((None, t, HV), lambda b: (b, 0, 0)),
        out_shape=jax.ShapeDtypeStruct((bsz, t, HV), BF16),
        scratch_shapes=[
            pltpu.VMEM((t // 2, HV), F32), pltpu.VMEM((t // 2, HV), F32),
            pltpu.VMEM((GLA_HEADS, GLA_DV, LANES), F32),
            pltpu.VMEM((GLA_HEADS, GLA_DV, LANES), F32),
        ],
        compiler_params=pltpu.CompilerParams(
            dimension_semantics=("parallel",), vmem_limit_bytes=VMEM_LIMIT),
        name="gla",
    )(*args, gnorm)


def _conv_kernel(ah_ref, av_ref, w_ref, b_ref, yh_ref, yv_ref, ph_ref, pv_ref):
    t = ah_ref.shape[0]
    rows = t // GRID_W
    tile = 64

    def conv(src_ref, pad_ref, w_cols, line, dst_ref):
        pad = CONV_PAD * line
        pad_ref[pl.ds(0, pad), :] = jnp.zeros((pad, CONV_HALF), F32)
        pad_ref[pl.ds(pad + t, pad), :] = jnp.zeros((pad, CONV_HALF), F32)

        def fill(i, carry):
            t0 = pl.multiple_of(i * tile, tile)
            pad_ref[pl.ds(pl.multiple_of(pad + t0, 8), tile), :] = (
                src_ref[pl.ds(t0, tile), :].astype(F32))
            return carry

        lax.fori_loop(0, t // tile, fill, 0)
        bias = b_ref[:, w_cols]

        def body(i, carry):
            t0 = pl.multiple_of(i * tile, tile)
            acc = jnp.broadcast_to(bias, (tile, CONV_HALF))
            for kk in range(CONV_K):
                win = pad_ref[pl.ds(pl.multiple_of(t0 + kk * line, 8), tile), :]
                acc = acc + w_ref[kk:kk + 1, w_cols] * win
            dst_ref[pl.ds(t0, tile), :] = acc
            return carry

        lax.fori_loop(0, t // tile, body, 0)

    conv(ah_ref, ph_ref, slice(0, CONV_HALF), rows, yh_ref)
    conv(av_ref, pv_ref, slice(CONV_HALF, CONV_DIM), GRID_W, yv_ref)


def _conv(ah, av, w, b):
    bsz, t, c = ah.shape
    rows = t // GRID_W
    tok = pl.BlockSpec((None, t, c), lambda i: (i, 0, 0))
    return pl.pallas_call(
        _conv_kernel,
        grid=(bsz,),
        in_specs=[tok, tok, _const_spec(w), _const_spec(b)],
        out_specs=[tok, tok],
        out_shape=[jax.ShapeDtypeStruct((bsz, t, c), F32)] * 2,
        scratch_shapes=[
            pltpu.VMEM((t + 2 * CONV_PAD * rows, c), F32),
            pltpu.VMEM((t + 2 * CONV_PAD * GRID_W, c), F32),
        ],
        compiler_params=pltpu.CompilerParams(
            dimension_semantics=("parallel",), vmem_limit_bytes=VMEM_LIMIT),
        name="conv",
    )(ah, av, w, b)


def _post_kernel(x_ref, yh_ref, yv_ref, og_ref, gt_ref, mod_ref, perm_ref, lng_ref, lnb_ref,
                 gpost1_ref, gpre2_ref, gpost2_ref, wco_ref, wgo_ref, wout_ref, wff1_ref,
                 wff2_ref, o_ref):
    d = x_ref.shape[-1]
    d_ff = wff1_ref.shape[1]
    ff_tile = 1024
    sub = 256
    subs = [slice(s, s + sub) for s in range(0, x_ref.shape[0], sub)]
    gt1 = mod_ref[2:3, :]
    sh2 = mod_ref[3:4, :]
    sc2 = mod_ref[4:5, :]
    gt2 = mod_ref[5:6, :]

    yh = []
    for n in range(x_ref.shape[0] // PERM_TOKENS):
        blk = yh_ref[:, PERM_ROWS * n:PERM_ROWS * (n + 1), :].reshape(PERM_TOKENS, CONV_HALF)
        hi, lo = _split_bf16(blk)
        yh.append(_dot(perm_ref[...], hi) + _dot(perm_ref[...], lo))
    yh = jnp.concatenate(yh, axis=0)

    yn = []
    for s in subs:
        yc = jnp.concatenate([yh[s], yv_ref[s, :]], axis=-1)
        yc = yc - jnp.mean(yc, axis=-1, keepdims=True)
        var = jnp.mean(yc * yc, axis=-1, keepdims=True)
        z = yc * lax.rsqrt(var + EPS) * lng_ref[...] + lnb_ref[...]
        yn.append((z * _sigmoid(z)).astype(BF16))
    y_gla = [_dot(og_ref[s, :], wgo_ref[...]) for s in subs]
    y_conv = [_dot(v, wco_ref[...]) for v in yn]
    merged = []
    for n, s in enumerate(subs):
        gates = gt_ref[s, :].astype(F32)
        merged.append((gates[:, :d] * y_conv[n] + gates[:, d:] * y_gla[n]).astype(BF16))
    y = [_dot(v, wout_ref[...]) for v in merged]
    h_mid = [x_ref[s, :] + gt1 * _rms(y[n], gpost1_ref[...]) for n, s in enumerate(subs)]
    u2 = [(_rms(h, gpre2_ref[...]) * (1.0 + sc2) + sh2).astype(BF16) for h in h_mid]
    acc = [jnp.zeros(h.shape, F32) for h in h_mid]
    for j in range(d_ff // ff_tile):
        cols = slice(j * ff_tile, (j + 1) * ff_tile)
        f = [_dot(v, wff1_ref[:, cols]) for v in u2]
        f = [jnp.square(jnp.maximum(v, 0.0)).astype(BF16) for v in f]
        acc = [acc[n] + _dot(f[n], wff2_ref[cols, :]) for n in range(len(subs))]
    for n, s in enumerate(subs):
        o_ref[s, :] = h_mid[n] + gt2 * _rms(acc[n], gpost2_ref[...])


def _post(x, yh, yv, og, gates, mod, perm, ln_g, ln_b, gpost1, gpre2, gpost2, wco, wgo, wout,
          wff1, wff2, tm):
    bsz, t, d = x.shape
    assert tm % PERM_TOKENS == 0
    return pl.pallas_call(
        _post_kernel,
        grid=(bsz, t // tm),
        in_specs=[
            _tok_spec(tm, d), _colmajor_spec(tm), _tok_spec(tm, CONV_HALF), _tok_spec(tm, HV),
            _tok_spec(tm, 2 * d),
            pl.BlockSpec((None,) + mod.shape[1:], lambda b, i: (b, 0, 0)),
            _single_spec(perm), _const_spec(ln_g), _const_spec(ln_b),
            _const_spec(gpost1), _const_spec(gpre2), _const_spec(gpost2),
            _single_spec(wco), _single_spec(wgo), _single_spec(wout), _single_spec(wff1),
            _single_spec(wff2),
        ],
        out_specs=_tok_spec(tm, d),
        out_shape=jax.ShapeDtypeStruct((bsz, t, d), F32),
        compiler_params=pltpu.CompilerParams(
            dimension_semantics=("parallel", "parallel"), vmem_limit_bytes=VMEM_LIMIT),
        name="post",
    )(x, yh, yv, og, gates, mod, perm, ln_g, ln_b, gpost1, gpre2, gpost2, wco, wgo, wout, wff1,
      wff2)


def kernel(x, c, ctx, c_ctx, w_mod, b_mod, g_pre1, g_post1, g_pre2, g_post2, w_in, conv_w,
           conv_b, conv_ln_g, conv_ln_b, w_conv_out, w_decay, b_decay, gla_norm_g, w_gla_out,
           w_out, w_ff1, w_ff2):
    bsz, t, d = x.shape
    depth = w_in.shape[0]
    assert depth == 1 and d == D_MODEL and t % (GRID_W * 8) == 0

    n_rows = -(-(bsz + 1) // 8) * 8
    c_all = jnp.zeros((n_rows, d), F32).at[:bsz].set(c).at[bsz].set(c_ctx)
    m = _adaln(c_all, w_mod, b_mod)
    mod = m[:bsz].reshape(bsz, N_MOD, d)
    mod_ctx = m[bsz, :2 * d].reshape(1, 2, d)

    wm, wdin, wg = _split_w_in(jnp.swapaxes(w_in[0], 0, 1))
    wdec = jnp.zeros((LANES, 2 * HK), F32)
    wdec = wdec.at[:DECAY_RANK, :HK].set(w_decay[0, 0])
    wdec = wdec.at[DECAY_RANK:2 * DECAY_RANK, HK:].set(w_decay[0, 1]).astype(BF16)
    bdec = b_decay[0].reshape(1, 2 * HK)
    row = lambda v: v.reshape(1, -1)

    perm = _to_colmajor_perm()
    to_colmajor = jnp.asarray(perm, BF16)
    to_rowmajor = jnp.asarray(perm.T, BF16)

    ah, av, q, k, v, rs, laf, lab, gates = _proj(
        x, mod, row(g_pre1[0]), to_colmajor, wm, wdin, wdec, bdec, wg, tm=1024)
    kc, vc, lafc, labc = _ctx_proj(
        ctx, mod_ctx, row(g_pre1[0]), wm, wdin, wdec, bdec, tm=ctx.shape[1])

    og = _gla(q, k, v, laf, lab, kc, vc, lafc, labc, rs, row(gla_norm_g[0]))
    yh, yv = _conv(ah.reshape(bsz, t, CONV_HALF), av, conv_w[0], row(conv_b[0]))
    yh = yh.reshape(bsz, GRID_W, t // GRID_W, CONV_HALF)

    return _post(x, yh, yv, og, gates, mod, to_rowmajor, row(conv_ln_g[0]), row(conv_ln_b[0]),
                 row(g_post1[0]), row(g_pre2[0]), row(g_post2[0]),
                 w_conv_out[0].astype(BF16), w_gla_out[0].astype(BF16), w_out[0].astype(BF16),
                 w_ff1[0].astype(BF16), w_ff2[0].astype(BF16), tm=512)
```

```python
import functools

import jax
import jax.numpy as jnp
import numpy as np
from jax import lax
from jax.experimental import pallas as pl
from jax.experimental.pallas import tpu as pltpu

D_MODEL = 1024
GRID_W = 64
CONV_DIM = 512
CONV_K = 31
GLA_HEADS = 4
GLA_DK = 64
GLA_DV = 128
DECAY_RANK = 16
GATE_NORM = 16.0
CHUNK = 64
N_MOD = 6
EPS = 1e-6

COL_Q = 2 * CONV_DIM
COL_K = COL_Q + GLA_HEADS * GLA_DK
COL_V = COL_K + GLA_HEADS * GLA_DK
COL_R = COL_V + GLA_HEADS * GLA_DV
COL_DEC = COL_R + GLA_HEADS * GLA_DV
COL_GATE = COL_DEC + 2 * DECAY_RANK
COL_END = COL_GATE + 2 * D_MODEL

LANES = 128
HK = GLA_HEADS * GLA_DK
HV = GLA_HEADS * GLA_DV
CONV_HALF = CONV_DIM // 2
CONV_PAD = CONV_K // 2
PERM_ROWS = 8
PERM_TOKENS = PERM_ROWS * GRID_W
PROJ_SUB = 256
SUPER = 2 * CHUNK
STEP = 2 * SUPER

VMEM_LIMIT = 56 * 1024 * 1024

F32 = jnp.float32
BF16 = jnp.bfloat16

NT = (((1,), (1,)), ((), ()))
TN = (((0,), (0,)), ((), ()))


def _dot(a, b):
    return jnp.dot(a, b, preferred_element_type=F32)


def _rms(x, g):
    ms = jnp.mean(x * x, axis=-1, keepdims=True)
    return x * lax.rsqrt(ms + EPS) * g


def _sigmoid(x):
    return jax.nn.sigmoid(x)


def _split_bf16(x):
    hi = x.astype(BF16)
    lo = (x - hi.astype(F32)).astype(BF16)
    return hi, lo


def _to_colmajor_perm():
    out = np.arange(PERM_TOKENS)
    src = (out % PERM_ROWS) * GRID_W + out // PERM_ROWS
    p = np.zeros((PERM_TOKENS, PERM_TOKENS), np.float32)
    p[out, src] = 1.0
    return p


def _adaln_kernel(c_ref, w_ref, b_ref, o_ref):
    c = c_ref[...]
    s_hi, s_lo = _split_bf16(c * _sigmoid(c))
    w_hi, w_lo = _split_bf16(w_ref[...])
    o_ref[...] = _dot(s_hi, w_hi) + _dot(s_lo, w_hi) + _dot(s_hi, w_lo) + b_ref[...]


def _adaln(c_all, w_mod, b_mod):
    n_rows, d = c_all.shape
    n_out = w_mod.shape[-1]
    tn = 1536
    return pl.pallas_call(
        _adaln_kernel,
        grid=(n_out // tn,),
        in_specs=[
            pl.BlockSpec((n_rows, d), lambda j: (0, 0)),
            pl.BlockSpec((None, d, tn), lambda j: (0, 0, j)),
            pl.BlockSpec((1, tn), lambda j: (0, j)),
        ],
        out_specs=pl.BlockSpec((n_rows, tn), lambda j: (0, j)),
        out_shape=jax.ShapeDtypeStruct((n_rows, n_out), F32),
        compiler_params=pltpu.CompilerParams(
            dimension_semantics=("arbitrary",), vmem_limit_bytes=VMEM_LIMIT),
        name="adaln",
    )(c_all, w_mod, b_mod)


def _split_w_in_kernel(wm_t_ref, wd_t_ref, wg_t_ref, wm_ref, wdin_ref, wg_ref, *, n_gate):
    j = pl.program_id(0)
    wm_ref[...] = wm_t_ref[...].astype(BF16).T

    @pl.when(j == 0)
    def _():
        wdin_ref[...] = wd_t_ref[...].astype(BF16).T

    @pl.when(j < n_gate)
    def _():
        wg_ref[...] = wg_t_ref[...].astype(BF16).T


def _split_w_in(w_in_t):
    n, d = w_in_t.shape
    tr = 256
    n_main = COL_DEC // tr
    n_gate = (n - COL_GATE) // tr
    assert COL_DEC % tr == 0 and (n - COL_GATE) % tr == 0 and n_gate <= n_main
    gate_blk = lambda j: jnp.minimum(j, n_gate - 1)
    return pl.pallas_call(
        functools.partial(_split_w_in_kernel, n_gate=n_gate),
        grid=(n_main,),
        in_specs=[
            pl.BlockSpec((tr, d), lambda j: (j, 0)),
            pl.BlockSpec((LANES, d), lambda j: (COL_DEC // LANES, 0)),
            pl.BlockSpec((pl.Element(tr), pl.Element(d)),
                         lambda j: (pl.multiple_of(COL_GATE + tr * gate_blk(j), 32), 0)),
        ],
        out_specs=[
            pl.BlockSpec((d, tr), lambda j: (0, j)),
            pl.BlockSpec((d, LANES), lambda j: (0, 0)),
            pl.BlockSpec((d, tr), lambda j: (0, gate_blk(j))),
        ],
        out_shape=[jax.ShapeDtypeStruct((d, w), BF16) for w in (COL_DEC, LANES, n - COL_GATE)],
        compiler_params=pltpu.CompilerParams(
            dimension_semantics=("arbitrary",), vmem_limit_bytes=VMEM_LIMIT),
        name="split_w_in",
    )(w_in_t, w_in_t, w_in_t)


def _prenorm_mod(x, g, mod_ref):
    sh = mod_ref[0:1, :]
    sc = mod_ref[1:2, :]
    return (_rms(x, g) * (1.0 + sc) + sh).astype(BF16)


def _decay_logs(z, wdec_ref, bdec_ref):
    z_hi, z_lo = z
    logits = _dot(z_hi, wdec_ref[...]) + _dot(z_lo, wdec_ref[...]) + bdec_ref[...]
    ls = jnp.minimum(logits, 0.0) - jnp.log(1.0 + jnp.exp(-jnp.abs(logits)))
    return ls * (1.0 / GATE_NORM)


def _proj_kernel(x_ref, mod_ref, g_ref, wm_ref, wdin_ref, wdec_ref, bdec_ref, wg_ref,
                 ah_ref, av_ref, q_ref, k_ref, v_ref, rs_ref, laf_ref, lab_ref, gt_ref):
    subs = [slice(s, s + PROJ_SUB) for s in range(0, x_ref.shape[0], PROJ_SUB)]
    g = g_ref[...]
    ub = [_prenorm_mod(x_ref[s, :], g, mod_ref) for s in subs]
    z = [_split_bf16(_dot(u, wdin_ref[...])) for u in ub]
    for n, s in enumerate(subs):
        glu = _dot(ub[n], wm_ref[:, 0:COL_Q])
        a = (glu[:, :CONV_DIM] * _sigmoid(glu[:, CONV_DIM:])).astype(BF16)
        ah_ref[s, :] = a[:, :CONV_HALF]
        av_ref[s, :] = a[:, CONV_HALF:]
    for n, s in enumerate(subs):
        q_ref[s, :] = (_dot(ub[n], wm_ref[:, COL_Q:COL_K]) * (GLA_DK ** -0.5)).astype(BF16)
    for n, s in enumerate(subs):
        k_ref[s, :] = _dot(ub[n], wm_ref[:, COL_K:COL_V]).astype(BF16)
    for n, s in enumerate(subs):
        v_ref[s, :] = _dot(ub[n], wm_ref[:, COL_V:COL_R]).astype(BF16)
    for n, s in enumerate(subs):
        la = _decay_logs(z[n], wdec_ref, bdec_ref)
        laf_ref[s, :] = la[:, :HK]
        lab_ref[s, :] = la[:, HK:]
    for n, s in enumerate(subs):
        r = _dot(ub[n], wm_ref[:, COL_R:COL_DEC])
        rs_ref[s, :] = (r * _sigmoid(r)).astype(BF16)
    for n, s in enumerate(subs):
        gt_ref[s, :] = _sigmoid(_dot(ub[n], wg_ref[...])).astype(BF16)


def _ctx_proj_kernel(x_ref, mod_ref, g_ref, wm_ref, wdin_ref, wdec_ref, bdec_ref,
                     k_ref, v_ref, laf_ref, lab_ref):
    ub = _prenorm_mod(x_ref[...], g_ref[...], mod_ref)
    z = _split_bf16(_dot(ub, wdin_ref[...]))
    k_ref[...] = _dot(ub, wm_ref[:, COL_K:COL_V]).astype(BF16)
    v_ref[...] = _dot(ub, wm_ref[:, COL_V:COL_R]).astype(BF16)
    la = _decay_logs(z, wdec_ref, bdec_ref)
    laf_ref[...] = la[:, :HK]
    lab_ref[...] = la[:, HK:]


def _const_spec(arr):
    nd = arr.ndim
    return pl.BlockSpec(arr.shape, lambda *_: (0,) * nd)


def _tok_spec(tm, width):
    return pl.BlockSpec((None, tm, width), lambda b, i: (b, i, 0))


def _single_spec(arr):
    nd = arr.ndim
    return pl.BlockSpec(arr.shape, lambda *_: (0,) * nd, pipeline_mode=pl.Buffered(1))


def _proj(x, mod, g, wm, wdin, wdec, bdec, wg, tm):
    bsz, t, d = x.shape
    widths = (CONV_HALF, CONV_HALF, HK, HK, HV, HV, HK, HK, 2 * D_MODEL)
    dtypes = (BF16, BF16, BF16, BF16, BF16, BF16, F32, F32, BF16)
    assert tm % PROJ_SUB == 0
    return pl.pallas_call(
        _proj_kernel,
        grid=(bsz, t // tm),
        in_specs=[
            _tok_spec(tm, d),
            pl.BlockSpec((None,) + mod.shape[1:], lambda b, i: (b, 0, 0)),
            _const_spec(g), _single_spec(wm), _single_spec(wdin),
            _single_spec(wdec), _const_spec(bdec), _single_spec(wg),
        ],
        out_specs=[_tok_spec(tm, w) for w in widths],
        out_shape=[jax.ShapeDtypeStruct((bsz, t, w), dt) for w, dt in zip(widths, dtypes)],
        compiler_params=pltpu.CompilerParams(
            dimension_semantics=("parallel", "parallel"), vmem_limit_bytes=VMEM_LIMIT),
        name="proj",
    )(x, mod, g, wm, wdin, wdec, bdec, wg)


def _ctx_proj(x, mod, g, wm, wdin, wdec, bdec, tm):
    bsz, t, d = x.shape
    widths = (HK, HV, HK, HK)
    dtypes = (BF16, BF16, F32, F32)
    return pl.pallas_call(
        _ctx_proj_kernel,
        grid=(bsz, t // tm),
        in_specs=[
            _tok_spec(tm, d),
            pl.BlockSpec((None,) + mod.shape[1:], lambda b, i: (0, 0, 0)),
            _const_spec(g), _const_spec(wm), _const_spec(wdin), _const_spec(wdec),
            _const_spec(bdec),
        ],
        out_specs=[_tok_spec(tm, w) for w in widths],
        out_shape=[jax.ShapeDtypeStruct((bsz, t, w), dt) for w, dt in zip(widths, dtypes)],
        compiler_params=pltpu.CompilerParams(
            dimension_semantics=("parallel", "parallel"), vmem_limit_bytes=VMEM_LIMIT),
        name="ctx_proj",
    )(x, mod, g, wm, wdin, wdec, bdec)


def _gla_scale(qs, ks, las, dirn, want_out):
    tri_b, mid, last = dirn["tri_b"], dirn["mid"], dirn["last"]
    la_hi, la_lo = _split_bf16(las)
    g = _dot(tri_b, la_hi) + _dot(tri_b, la_lo)
    u = dict(kt=[], kend=[], qt=[], qg=[], e=[])
    for c in range(SUPER // CHUNK):
        rows = slice(c * CHUNK, (c + 1) * CHUNK)
        gc = g[rows]
        g_mid = gc[mid:mid + 1]
        g_last = gc[last:last + 1]
        u["e"].append(jnp.exp(g_last))
        kt = ks[rows].astype(F32) * jnp.exp(g_mid - gc)
        u["kt"].append(kt)
        u["kend"].append(kt * jnp.exp(g_last - g_mid))
        if want_out:
            qt = qs[rows].astype(F32) * jnp.exp(gc - g_mid)
            u["qt"].append(qt.astype(BF16))
            u["qg"].append((qt * jnp.exp(g_mid)).astype(BF16))
    return u


def _pair_lanes(pair):
    return slice(LANES * pair, LANES * (pair + 1))


def _gla_scores(u, pair, head_lane):
    lanes = _pair_lanes(pair)
    kt = jnp.concatenate([c[:, lanes] for c in u["kt"]], axis=0)
    qt = jnp.concatenate([c[:, lanes] for c in u["qt"]], axis=0)
    kt2 = jnp.concatenate([(kt * hm).astype(BF16) for hm in head_lane], axis=0)
    return lax.dot_general(qt, kt2, NT, preferred_element_type=F32)


def _gla_update(u, vs, h, head_lane):
    lanes = _pair_lanes(h // 2)
    hm = head_lane[h % 2]
    k0 = (u["kend"][0][:, lanes] * hm).astype(BF16)
    k1 = (u["kend"][1][:, lanes] * hm).astype(BF16)
    z = jnp.zeros_like(k0)
    kbd = jnp.concatenate([jnp.concatenate([k0, z], axis=1),
                           jnp.concatenate([z, k1], axis=1)], axis=0)
    return lax.dot_general(vs[:, GLA_DV * h:GLA_DV * (h + 1)], kbd, TN,
                           preferred_element_type=F32)


def _gla_intra(scores, vs, pair, mask2):
    a = jnp.where(mask2, scores, 0.0).astype(BF16)
    v0 = vs[:, GLA_DV * 2 * pair:GLA_DV * (2 * pair + 1)]
    v1 = vs[:, GLA_DV * (2 * pair + 1):GLA_DV * (2 * pair + 2)]
    z = jnp.zeros_like(v0)
    vbd = jnp.concatenate([jnp.concatenate([v0, z], axis=1),
                           jnp.concatenate([z, v1], axis=1)], axis=0)
    return _dot(a, vbd)


def _gla_kernel(q_ref, k_ref, v_ref, laf_ref, lab_ref, kc_ref, vc_ref, lafc_ref, labc_ref,
                rs_ref, g_ref, o_ref, of_ref, ob_ref, sf_ref, sb_ref):
    t = q_ref.shape[0]
    t_ctx = kc_ref.shape[0]
    half = t // 2
    n_steps = t // STEP
    supers = STEP // SUPER
    n_chunks = SUPER // CHUNK
    pairs = GLA_HEADS // 2

    row = lax.broadcasted_iota(jnp.int32, (SUPER, SUPER), 0)
    col = lax.broadcasted_iota(jnp.int32, (SUPER, SUPER), 1)
    same_chunk = (row // CHUNK) == (col // CHUNK)
    low_mask = jnp.logical_and(same_chunk, col <= row)
    up_mask = jnp.logical_and(same_chunk, col >= row)
    lane = lax.broadcasted_iota(jnp.int32, (1, LANES), 1)
    head_lane = (jnp.where(lane < GLA_DK, 1.0, 0.0), jnp.where(lane >= GLA_DK, 1.0, 0.0))

    fwd = dict(tri_b=jnp.where(low_mask, 1.0, 0.0).astype(BF16),
               mask2=jnp.concatenate([low_mask, low_mask], axis=1),
               mid=CHUNK // 2 - 1, last=CHUNK - 1, order=(0, 1), st_ref=sf_ref)
    bwd = dict(tri_b=jnp.where(up_mask, 1.0, 0.0).astype(BF16),
               mask2=jnp.concatenate([up_mask, up_mask], axis=1),
               mid=CHUNK // 2, last=0, order=(1, 0), st_ref=sb_ref)

    sf_ref[...] = jnp.zeros_like(sf_ref)
    sb_ref[...] = jnp.zeros_like(sb_ref)
    gn = g_ref[...]

    def run(units, want_out, finalize):
        vals = []
        for dirn, refs, t0 in units:
            q_r, k_r, v_r, la_r = refs
            rows = pl.ds(t0, SUPER)
            qs = q_r[rows, :] if want_out else None
            vals.append((_gla_scale(qs, k_r[rows, :], la_r[rows, :], dirn, want_out),
                         v_r[rows, :]))
        scores = []
        if want_out:
            scores = [[_gla_scores(u, p, head_lane) for p in range(pairs)] for u, _ in vals]
        upd = [[_gla_update(u, vs, h, head_lane) for h in range(GLA_HEADS)] for u, vs in vals]
        intra = []
        if want_out:
            intra = [[_gla_intra(scores[n][p], vals[n][1], p, units[n][0]["mask2"])
                      for p in range(pairs)] for n in range(len(units))]
        states = {}
        for n, (dirn, _, t0) in enumerate(units):
            key = id(dirn["st_ref"])
            if key not in states:
                states[key] = (dirn["st_ref"], [dirn["st_ref"][h] for h in range(GLA_HEADS)])
            st = states[key][1]
            u = vals[n][0]
            inter = [[None] * n_chunks for _ in range(pairs)]
            for c in dirn["order"]:
                if want_out:
                    for p in range(pairs):
                        st2 = jnp.concatenate([st[2 * p].astype(BF16),
                                               st[2 * p + 1].astype(BF16)], axis=0)
                        inter[p][c] = lax.dot_general(u["qg"][c][:, _pair_lanes(p)], st2, NT,
                                                      preferred_element_type=F32)
                for h in range(GLA_HEADS):
                    e = u["e"][c][:, _pair_lanes(h // 2)]
                    st[h] = st[h] * e + upd[n][h][:, LANES * c:LANES * (c + 1)]
            if not want_out:
                continue
            fwd_unit = dirn is fwd
            for p in range(pairs):
                o2 = intra[n][p] + jnp.concatenate(inter[p], axis=0)
                for j in range(2):
                    cols = slice(GLA_DV * (2 * p + j), GLA_DV * (2 * p + j + 1))
                    o = o2[:, GLA_DV * j:GLA_DV * (j + 1)]
                    if finalize:
                        other = (ob_ref[pl.ds(t0 - half, SUPER), cols] if fwd_unit
                                 else of_ref[pl.ds(t0, SUPER), cols])
                        rs = rs_ref[pl.ds(t0, SUPER), cols].astype(F32)
                        o_ref[pl.ds(t0, SUPER), cols] = (_rms(o + other, gn) * rs).astype(BF16)
                    elif fwd_unit:
                        of_ref[pl.ds(t0, SUPER), cols] = o
                    else:
                        ob_ref[pl.ds(t0 - half, SUPER), cols] = o
        for st_ref, st in states.values():
            for h in range(GLA_HEADS):
                st_ref[h] = st[h]

    ctx_units = []
    for s in range(t_ctx // SUPER):
        ctx_units.append((fwd, (None, kc_ref, vc_ref, lafc_ref), s * SUPER))
        ctx_units.append((bwd, (None, kc_ref, vc_ref, labc_ref), t_ctx - (s + 1) * SUPER))
    run(ctx_units, want_out=False, finalize=False)

    def step(i, finalize):
        units = []
        for s in range(supers):
            t0 = pl.multiple_of(i * STEP + s * SUPER, SUPER)
            t1 = pl.multiple_of(t - (i + 1) * STEP + (supers - 1 - s) * SUPER, SUPER)
            units.append((fwd, (q_ref, k_ref, v_ref, laf_ref), t0))
            units.append((bwd, (q_ref, k_ref, v_ref, lab_ref), t1))
        run(units, want_out=True, finalize=finalize)

    def park_body(i, carry):
        step(i, False)
        return carry

    def finish_body(i, carry):
        step(i, True)
        return carry

    lax.fori_loop(0, n_steps // 2, park_body, 0)
    lax.fori_loop(n_steps // 2, n_steps, finish_body, 0)


def _gla(q, k, v, laf, lab, kc, vc, lafc, labc, rs, gnorm):
    bsz, t, _ = q.shape
    assert t % (2 * STEP) == 0 and kc.shape[1] % SUPER == 0

    def bspec(arr):
        return pl.BlockSpec((None,) + arr.shape[1:], lambda b: (b, 0, 0))

    args = (q, k, v, laf, lab, kc, vc, lafc, labc, rs)
    return pl.pallas_call(
        _gla_kernel,
        grid=(bsz,),
        in_specs=[bspec(a) for a in args] + [_const_spec(gnorm)],
        out_specs=pl.BlockSpec((None, t, HV), lambda b: (b, 0, 0)),
        out_shape=jax.ShapeDtypeStruct((bsz, t, HV), BF16),
        scratch_shapes=[
            pltpu.VMEM((t // 2, HV), F32), pltpu.VMEM((t // 2, HV), F32),
            pltpu.VMEM((GLA_HEADS, GLA_DV, LANES), F32),
            pltpu.VMEM((GLA_HEADS, GLA_DV, LANES), F32),
        ],
        compiler_params=pltpu.CompilerParams(
            dimension_semantics=("parallel",), vmem_limit_bytes=VMEM_LIMIT),
        name="gla",
    )(*args, gnorm)


def _post_kernel(x_ref, ah_ref, av_ref, og_ref, gt_ref, mod_ref, cw_ref, cb_ref, pc_ref, pr_ref,
                 lng_ref, lnb_ref, gpost1_ref, gpre2_ref, gpost2_ref, wco_ref, wgo_ref, wout_ref,
                 wff1_ref, wff2_ref, o_ref, pv_ref, ph_ref, yc_ref, stash_ref, *, n_tiles,
                 img_tiles):
    s = pl.program_id(0)
    tm, d = x_ref.shape
    t_img = av_ref.shape[0]
    d_ff = wff1_ref.shape[1]
    ff_tile = 1024
    sub = 256
    acc_rows = 64
    v_pad = CONV_PAD * GRID_W
    h_pad = CONV_PAD * PERM_ROWS
    subs = [slice(r, r + sub) for r in range(0, tm, sub)]

    @pl.when(s == 0)
    def _init():
        stash_ref[...] = jnp.zeros_like(stash_ref)
        pv_ref[pl.ds(0, v_pad), :] = jnp.zeros((v_pad, CONV_HALF), F32)
        pv_ref[pl.ds(v_pad + t_img, v_pad), :] = jnp.zeros((v_pad, CONV_HALF), F32)
        ph_ref[pl.ds(0, h_pad), :] = jnp.zeros((h_pad, CONV_HALF), F32)
        ph_ref[pl.ds(h_pad + tm, h_pad), :] = jnp.zeros((h_pad, CONV_HALF), F32)

    @pl.when(jnp.logical_and(s % img_tiles == 0, s < n_tiles))
    def _new_image():
        def fill(i, carry):
            r0 = pl.multiple_of(i * acc_rows, acc_rows)
            pv_ref[pl.ds(pl.multiple_of(v_pad + r0, acc_rows), acc_rows), :] = (
                av_ref[pl.ds(r0, acc_rows), :].astype(F32))
            return carry
        lax.fori_loop(0, t_img // acc_rows, fill, 0)

    yn_prev = stash_ref[(s + 1) % 2]

    cs = jnp.minimum(s, n_tiles - 1)
    t0 = pl.multiple_of((cs % img_tiles) * tm, tm)
    cols_h = slice(0, CONV_HALF)
    cols_v = slice(CONV_HALF, CONV_DIM)

    pending = []

    def tap_item(src_ref, base, shift, w_cols, r):
        def run():
            a = jnp.broadcast_to(cb_ref[:, w_cols], (acc_rows, CONV_HALF))
            for kk in range(CONV_K):
                a = a + cw_ref[kk:kk + 1, w_cols] * src_ref[pl.ds(base + r + kk * shift,
                                                                  acc_rows), :]
            yc_ref[pl.ds(r, acc_rows), w_cols] = a
            tot = jnp.sum(a[:, 0:LANES] + a[:, LANES:], axis=0, keepdims=True)
            pending.append(jnp.minimum(jnp.abs(tot), 1.0) * 0.0)
        return run

    def anchor(v):
        if not pending:
            return v
        z = functools.reduce(lambda p, q: p + q, pending)
        pending.clear()
        return v + jnp.concatenate([z] * (v.shape[1] // LANES), axis=1)

    def to_rowmajor_item():
        hi, lo = _split_bf16(yc_ref[:, cols_h])
        yc_ref[:, cols_h] = _dot(pr_ref[...], hi) + _dot(pr_ref[...], lo)

    chunk_starts = range(0, tm, acc_rows)
    items = ([tap_item(pv_ref, t0, GRID_W, cols_v, r) for r in chunk_starts]
             + [tap_item(ph_ref, 0, PERM_ROWS, cols_h, r) for r in chunk_starts]
             + [to_rowmajor_item])
    items = iter(items)

    def conv_work(n):
        for _ in range(n):
            item = next(items, None)
            if item is not None:
                item()

    ph_ref[pl.ds(h_pad, tm), :] = _dot(pc_ref[...], ah_ref[...])

    gt1 = mod_ref[2:3, :]
    sh2 = mod_ref[3:4, :]
    sc2 = mod_ref[4:5, :]
    gt2 = mod_ref[5:6, :]
    y_gla = [_dot(og_ref[r, :], wgo_ref[...]) for r in subs]
    y_conv = [_dot(yn_prev[r], wco_ref[...]) for r in subs]
    merged = []
    for n, r in enumerate(subs):
        gates = gt_ref[r, :].astype(F32)
        merged.append((gates[:, :d] * y_conv[n] + gates[:, d:] * y_gla[n]).astype(BF16))
    y = [_dot(v, wout_ref[...]) for v in merged]
    h_mid = [x_ref[r, :] + gt1 * _rms(y[n], gpost1_ref[...]) for n, r in enumerate(subs)]
    u2 = [(_rms(h, gpre2_ref[...]) * (1.0 + sc2) + sh2).astype(BF16) for h in h_mid]
    acc = [jnp.zeros(h.shape, F32) for h in h_mid]
    n_ff = d_ff // ff_tile
    per_sub = -(-2 * len(chunk_starts) // (n_ff * len(subs)))
    for j in range(n_ff):
        cols = slice(j * ff_tile, (j + 1) * ff_tile)
        f = [_dot(v, wff1_ref[:, cols]) for v in u2]
        for n in range(len(subs)):
            conv_work(per_sub)
            f[n] = anchor(jnp.square(jnp.maximum(f[n], 0.0))).astype(BF16)
        acc = [acc[n] + _dot(f[n], wff2_ref[cols, :]) for n in range(len(subs))]
    assert next(items, None) is to_rowmajor_item and not pending
    to_rowmajor_item()

    yn = []
    for r in subs:
        yc = yc_ref[r, :]
        yc = yc - jnp.mean(yc, axis=-1, keepdims=True)
        var = jnp.mean(yc * yc, axis=-1, keepdims=True)
        z = yc * lax.rsqrt(var + EPS) * lng_ref[...] + lnb_ref[...]
        yn.append((z * _sigmoid(z)).astype(BF16))
    stash_ref[s % 2] = jnp.concatenate(yn, axis=0)

    for n, r in enumerate(subs):
        o_ref[r, :] = h_mid[n] + gt2 * _rms(acc[n], gpost2_ref[...])


def _post(x, ah, av, og, gates, mod, conv_w, conv_b, to_colmajor, to_rowmajor, ln_g, ln_b,
          gpost1, gpre2, gpost2, wco, wgo, wout, wff1, wff2):
    bsz, t, d = x.shape
    tm = PERM_TOKENS
    img_tiles = t // tm
    n_tiles = bsz * img_tiles

    def conv_tile(s):
        cs = jnp.minimum(s, n_tiles - 1)
        return cs // img_tiles, cs % img_tiles

    def main_tile(s):
        ms = jnp.maximum(s - 1, 0)
        return ms // img_tiles, ms % img_tiles

    def main_spec(width):
        return pl.BlockSpec((None, tm, width), lambda s: main_tile(s) + (0,))

    kern = functools.partial(_post_kernel, n_tiles=n_tiles, img_tiles=img_tiles)
    return pl.pallas_call(
        kern,
        grid=(n_tiles + 1,),
        in_specs=[
            main_spec(d),
            pl.BlockSpec((None, tm, CONV_HALF), lambda s: conv_tile(s) + (0,)),
            pl.BlockSpec((None, t, CONV_HALF), lambda s: (conv_tile(s)[0], 0, 0)),
            main_spec(HV), main_spec(2 * d),
            pl.BlockSpec((None,) + mod.shape[1:], lambda s: (main_tile(s)[0], 0, 0)),
            _const_spec(conv_w), _const_spec(conv_b),
            _single_spec(to_colmajor), _single_spec(to_rowmajor),
            _const_spec(ln_g), _const_spec(ln_b),
            _const_spec(gpost1), _const_spec(gpre2), _const_spec(gpost2),
            _single_spec(wco), _single_spec(wgo), _single_spec(wout), _single_spec(wff1),
            _single_spec(wff2),
        ],
        out_specs=main_spec(d),
        out_shape=jax.ShapeDtypeStruct((bsz, t, d), F32),
        scratch_shapes=[
            pltpu.VMEM((t + 2 * CONV_PAD * GRID_W, CONV_HALF), F32),
            pltpu.VMEM((tm + 2 * CONV_PAD * PERM_ROWS, CONV_HALF), F32),
            pltpu.VMEM((tm, CONV_DIM), F32),
            pltpu.VMEM((2, tm, CONV_DIM), BF16),
        ],
        compiler_params=pltpu.CompilerParams(
            dimension_semantics=("arbitrary",), vmem_limit_bytes=VMEM_LIMIT),
        name="post",
    )(x, ah, av, og, gates, mod, conv_w, conv_b, to_colmajor, to_rowmajor, ln_g, ln_b,
      gpost1, gpre2, gpost2, wco, wgo, wout, wff1, wff2)


def kernel(x, c, ctx, c_ctx, w_mod, b_mod, g_pre1, g_post1, g_pre2, g_post2, w_in, conv_w,
           conv_b, conv_ln_g, conv_ln_b, w_conv_out, w_decay, b_decay, gla_norm_g, w_gla_out,
           w_out, w_ff1, w_ff2):
    bsz, t, d = x.shape
    depth = w_in.shape[0]
    assert depth == 1 and d == D_MODEL and t % (GRID_W * 8) == 0

    n_rows = -(-(bsz + 1) // 8) * 8
    c_all = jnp.zeros((n_rows, d), F32).at[:bsz].set(c).at[bsz].set(c_ctx)
    m = _adaln(c_all, w_mod, b_mod)
    mod = m[:bsz].reshape(bsz, N_MOD, d)
    mod_ctx = m[bsz, :2 * d].reshape(1, 2, d)

    wm, wdin, wg = _split_w_in(jnp.swapaxes(w_in[0], 0, 1))
    wdec = jnp.zeros((LANES, 2 * HK), F32)
    wdec = wdec.at[:DECAY_RANK, :HK].set(w_decay[0, 0])
    wdec = wdec.at[DECAY_RANK:2 * DECAY_RANK, HK:].set(w_decay[0, 1]).astype(BF16)
    bdec = b_decay[0].reshape(1, 2 * HK)
    row = lambda v: v.reshape(1, -1)

    perm = _to_colmajor_perm()
    to_colmajor = jnp.asarray(perm, BF16)
    to_rowmajor = jnp.asarray(perm.T, BF16)

    ah, av, q, k, v, rs, laf, lab, gates = _proj(
        x, mod, row(g_pre1[0]), wm, wdin, wdec, bdec, wg, tm=1024)
    kc, vc, lafc, labc = _ctx_proj(
        ctx, mod_ctx, row(g_pre1[0]), wm, wdin, wdec, bdec, tm=ctx.shape[1])

    og = _gla(q, k, v, laf, lab, kc, vc, lafc, labc, rs, row(gla_norm_g[0]))

    return _post(x, ah, av, og, gates, mod, conv_w[0], row(conv_b[0]), to_colmajor, to_rowmajor,
                 row(conv_ln_g[0]), row(conv_ln_b[0]),
                 row(g_post1[0]), row(g_pre2[0]), row(g_post2[0]),
                 w_conv_out[0].astype(BF16), w_gla_out[0].astype(BF16), w_out[0].astype(BF16),
                 w_ff1[0].astype(BF16), w_ff2[0].astype(BF16))
```

```python
import functools

import jax
import jax.numpy as jnp
import numpy as np
from jax import lax
from jax.experimental import pallas as pl
from jax.experimental.pallas import tpu as pltpu

D_MODEL = 1024
GRID_W = 64
CONV_DIM = 512
CONV_K = 31
GLA_HEADS = 4
GLA_DK = 64
GLA_DV = 128
DECAY_RANK = 16
GATE_NORM = 16.0
CHUNK = 64
N_MOD = 6
EPS = 1e-6

COL_Q = 2 * CONV_DIM
COL_K = COL_Q + GLA_HEADS * GLA_DK
COL_V = COL_K + GLA_HEADS * GLA_DK
COL_R = COL_V + GLA_HEADS * GLA_DV
COL_DEC = COL_R + GLA_HEADS * GLA_DV
COL_GATE = COL_DEC + 2 * DECAY_RANK
COL_END = COL_GATE + 2 * D_MODEL

LANES = 128
HK = GLA_HEADS * GLA_DK
HV = GLA_HEADS * GLA_DV
CONV_HALF = CONV_DIM // 2
CONV_PAD = CONV_K // 2
PERM_ROWS = 8
PERM_TOKENS = PERM_ROWS * GRID_W
PROJ_SUB = 256
SUPER = 2 * CHUNK
STEP = 2 * SUPER

VMEM_LIMIT = 56 * 1024 * 1024

F32 = jnp.float32
BF16 = jnp.bfloat16

NT = (((1,), (1,)), ((), ()))
TN = (((0,), (0,)), ((), ()))


def _dot(a, b):
    return jnp.dot(a, b, preferred_element_type=F32)


def _rms(x, g):
    ms = jnp.mean(x * x, axis=-1, keepdims=True)
    return x * lax.rsqrt(ms + EPS) * g


def _sigmoid(x):
    return jax.nn.sigmoid(x)


def _split_bf16(x):
    hi = x.astype(BF16)
    lo = (x - hi.astype(F32)).astype(BF16)
    return hi, lo


def _to_colmajor_perm():
    out = np.arange(PERM_TOKENS)
    src = (out % PERM_ROWS) * GRID_W + out // PERM_ROWS
    p = np.zeros((PERM_TOKENS, PERM_TOKENS), np.float32)
    p[out, src] = 1.0
    return p


def _adaln_kernel(c_ref, w_ref, b_ref, o_ref):
    c = c_ref[...]
    s_hi, s_lo = _split_bf16(c * _sigmoid(c))
    w_hi, w_lo = _split_bf16(w_ref[...])
    o_ref[...] = _dot(s_hi, w_hi) + _dot(s_lo, w_hi) + _dot(s_hi, w_lo) + b_ref[...]


def _adaln(c_all, w_mod, b_mod):
    n_rows, d = c_all.shape
    n_out = w_mod.shape[-1]
    tn = 1536
    return pl.pallas_call(
        _adaln_kernel,
        grid=(n_out // tn,),
        in_specs=[
            pl.BlockSpec((n_rows, d), lambda j: (0, 0)),
            pl.BlockSpec((None, d, tn), lambda j: (0, 0, j)),
            pl.BlockSpec((1, tn), lambda j: (0, j)),
        ],
        out_specs=pl.BlockSpec((n_rows, tn), lambda j: (0, j)),
        out_shape=jax.ShapeDtypeStruct((n_rows, n_out), F32),
        compiler_params=pltpu.CompilerParams(
            dimension_semantics=("arbitrary",), vmem_limit_bytes=VMEM_LIMIT),
        name="adaln",
    )(c_all, w_mod, b_mod)


def _split_w_in_kernel(wm_t_ref, wd_t_ref, wg_t_ref, wm_ref, wdin_ref, wg_ref, *, n_gate):
    j = pl.program_id(0)
    wm_ref[...] = wm_t_ref[...].astype(BF16).T

    @pl.when(j == 0)
    def _():
        wdin_ref[...] = wd_t_ref[...].astype(BF16).T

    @pl.when(j < n_gate)
    def _():
        wg_ref[...] = wg_t_ref[...].astype(BF16).T


def _split_w_in(w_in_t):
    n, d = w_in_t.shape
    tr = 256
    n_main = COL_DEC // tr
    n_gate = (n - COL_GATE) // tr
    assert COL_DEC % tr == 0 and (n - COL_GATE) % tr == 0 and n_gate <= n_main
    gate_blk = lambda j: jnp.minimum(j, n_gate - 1)
    return pl.pallas_call(
        functools.partial(_split_w_in_kernel, n_gate=n_gate),
        grid=(n_main,),
        in_specs=[
            pl.BlockSpec((tr, d), lambda j: (j, 0)),
            pl.BlockSpec((LANES, d), lambda j: (COL_DEC // LANES, 0)),
            pl.BlockSpec((pl.Element(tr), pl.Element(d)),
                         lambda j: (pl.multiple_of(COL_GATE + tr * gate_blk(j), 32), 0)),
        ],
        out_specs=[
            pl.BlockSpec((d, tr), lambda j: (0, j)),
            pl.BlockSpec((d, LANES), lambda j: (0, 0)),
            pl.BlockSpec((d, tr), lambda j: (0, gate_blk(j))),
        ],
        out_shape=[jax.ShapeDtypeStruct((d, w), BF16) for w in (COL_DEC, LANES, n - COL_GATE)],
        compiler_params=pltpu.CompilerParams(
            dimension_semantics=("arbitrary",), vmem_limit_bytes=VMEM_LIMIT),
        name="split_w_in",
    )(w_in_t, w_in_t, w_in_t)


def _prenorm_mod(x, g, mod_ref):
    sh = mod_ref[0:1, :]
    sc = mod_ref[1:2, :]
    return (_rms(x, g) * (1.0 + sc) + sh).astype(BF16)


def _decay_logs(z, wdec_ref, bdec_ref):
    z_hi, z_lo = z
    logits = _dot(z_hi, wdec_ref[...]) + _dot(z_lo, wdec_ref[...]) + bdec_ref[...]
    ls = jnp.minimum(logits, 0.0) - jnp.log(1.0 + jnp.exp(-jnp.abs(logits)))
    return ls * (1.0 / GATE_NORM)


def _proj_kernel(x_ref, mod_ref, g_ref, perm_ref, wm_ref, wdin_ref, wdec_ref, bdec_ref, wg_ref,
                 ah_ref, av_ref, q_ref, k_ref, v_ref, rs_ref, laf_ref, lab_ref, gt_ref):
    subs = [slice(s, s + PROJ_SUB) for s in range(0, x_ref.shape[0], PROJ_SUB)]
    g = g_ref[...]
    ub = [_prenorm_mod(x_ref[s, :], g, mod_ref) for s in subs]
    z = [_split_bf16(_dot(u, wdin_ref[...])) for u in ub]
    a_h = []
    for n, s in enumerate(subs):
        glu = _dot(ub[n], wm_ref[:, 0:COL_Q])
        a = (glu[:, :CONV_DIM] * _sigmoid(glu[:, CONV_DIM:])).astype(BF16)
        av_ref[s, :] = a[:, CONV_HALF:]
        a_h.append(a[:, :CONV_HALF])
    for n, s in enumerate(subs):
        q_ref[s, :] = (_dot(ub[n], wm_ref[:, COL_Q:COL_K]) * (GLA_DK ** -0.5)).astype(BF16)
    for n, s in enumerate(subs):
        k_ref[s, :] = _dot(ub[n], wm_ref[:, COL_K:COL_V]).astype(BF16)
    for n, s in enumerate(subs):
        v_ref[s, :] = _dot(ub[n], wm_ref[:, COL_V:COL_R]).astype(BF16)
    for n, s in enumerate(subs):
        la = _decay_logs(z[n], wdec_ref, bdec_ref)
        laf_ref[s, :] = la[:, :HK]
        lab_ref[s, :] = la[:, HK:]
    for n, s in enumerate(subs):
        r = _dot(ub[n], wm_ref[:, COL_R:COL_DEC])
        rs_ref[s, :] = (r * _sigmoid(r)).astype(BF16)
    per_group = PERM_TOKENS // PROJ_SUB
    for n in range(x_ref.shape[0] // PERM_TOKENS):
        a_rm = jnp.concatenate(a_h[per_group * n:per_group * (n + 1)], axis=0)
        a_cm = _dot(perm_ref[...], a_rm)
        ah_ref[:, PERM_ROWS * n:PERM_ROWS * (n + 1), :] = a_cm.reshape(
            GRID_W, PERM_ROWS, CONV_HALF)
    for n, s in enumerate(subs):
        gt_ref[s, :] = _sigmoid(_dot(ub[n], wg_ref[...])).astype(BF16)


def _ctx_proj_kernel(x_ref, mod_ref, g_ref, wm_ref, wdin_ref, wdec_ref, bdec_ref,
                     k_ref, v_ref, laf_ref, lab_ref):
    nb, t, d = x_ref.shape
    ub = _prenorm_mod(x_ref[...].reshape(nb * t, d), g_ref[...], mod_ref)
    z = _split_bf16(_dot(ub, wdin_ref[...]))
    k_ref[...] = _dot(ub, wm_ref[:, COL_K:COL_V]).astype(BF16).reshape(k_ref.shape)
    v_ref[...] = _dot(ub, wm_ref[:, COL_V:COL_R]).astype(BF16).reshape(v_ref.shape)
    la = _decay_logs(z, wdec_ref, bdec_ref)
    laf_ref[...] = la[:, :HK].reshape(laf_ref.shape)
    lab_ref[...] = la[:, HK:].reshape(lab_ref.shape)


def _const_spec(arr):
    nd = arr.ndim
    return pl.BlockSpec(arr.shape, lambda *_: (0,) * nd)


def _tok_spec(tm, width):
    return pl.BlockSpec((None, tm, width), lambda b, i: (b, i, 0))


def _colmajor_spec(tm):
    return pl.BlockSpec((None, GRID_W, tm // GRID_W, CONV_HALF), lambda b, i: (b, 0, i, 0))


def _single_spec(arr):
    nd = arr.ndim
    return pl.BlockSpec(arr.shape, lambda *_: (0,) * nd, pipeline_mode=pl.Buffered(1))


def _proj(x, mod, g, perm, wm, wdin, wdec, bdec, wg, tm):
    bsz, t, d = x.shape
    widths = (CONV_HALF, HK, HK, HV, HV, HK, HK, 2 * D_MODEL)
    dtypes = (BF16, BF16, BF16, BF16, BF16, F32, F32, BF16)
    assert tm % PERM_TOKENS == 0
    ah_shape = jax.ShapeDtypeStruct((bsz, GRID_W, t // GRID_W, CONV_HALF), F32)
    return pl.pallas_call(
        _proj_kernel,
        grid=(bsz, t // tm),
        in_specs=[
            _tok_spec(tm, d),
            pl.BlockSpec((None,) + mod.shape[1:], lambda b, i: (b, 0, 0)),
            _const_spec(g), _single_spec(perm), _single_spec(wm), _single_spec(wdin),
            _single_spec(wdec), _const_spec(bdec), _single_spec(wg),
        ],
        out_specs=[_colmajor_spec(tm)] + [_tok_spec(tm, w) for w in widths],
        out_shape=[ah_shape] + [jax.ShapeDtypeStruct((bsz, t, w), dt)
                                for w, dt in zip(widths, dtypes)],
        compiler_params=pltpu.CompilerParams(
            dimension_semantics=("parallel", "parallel"), vmem_limit_bytes=VMEM_LIMIT),
        name="proj",
    )(x, mod, g, perm, wm, wdin, wdec, bdec, wg)


def _ctx_proj(x, mod, g, wm, wdin, wdec, bdec, nb):
    bsz, t, d = x.shape
    widths = (HK, HV, HK, HK)
    dtypes = (BF16, BF16, F32, F32)
    assert bsz % nb == 0
    return pl.pallas_call(
        _ctx_proj_kernel,
        grid=(bsz // nb,),
        in_specs=[
            pl.BlockSpec((nb, t, d), lambda i: (i, 0, 0)),
            pl.BlockSpec((None,) + mod.shape[1:], lambda i: (0, 0, 0)),
            _const_spec(g), _single_spec(wm), _single_spec(wdin), _single_spec(wdec),
            _const_spec(bdec),
        ],
        out_specs=[pl.BlockSpec((nb, t, w), lambda i: (i, 0, 0)) for w in widths],
        out_shape=[jax.ShapeDtypeStruct((bsz, t, w), dt) for w, dt in zip(widths, dtypes)],
        compiler_params=pltpu.CompilerParams(
            dimension_semantics=("parallel",), vmem_limit_bytes=VMEM_LIMIT),
        name="ctx_proj",
    )(x, mod, g, wm, wdin, wdec, bdec)


def _gla_scale(qs, ks, las, dirn, want_out):
    tri_b, mid, last = dirn["tri_b"], dirn["mid"], dirn["last"]
    la_hi, la_lo = _split_bf16(las)
    g = _dot(tri_b, la_hi) + _dot(tri_b, la_lo)
    u = dict(kt=[], kend=[], qt=[], qg=[], e=[])
    for c in range(SUPER // CHUNK):
        rows = slice(c * CHUNK, (c + 1) * CHUNK)
        gc = g[rows]
        g_mid = gc[mid:mid + 1]
        g_last = gc[last:last + 1]
        u["e"].append(jnp.exp(g_last))
        kt = ks[rows].astype(F32) * jnp.exp(g_mid - gc)
        u["kt"].append(kt)
        u["kend"].append(kt * jnp.exp(g_last - g_mid))
        if want_out:
            qt = qs[rows].astype(F32) * jnp.exp(gc - g_mid)
            u["qt"].append(qt.astype(BF16))
            u["qg"].append((qt * jnp.exp(g_mid)).astype(BF16))
    return u


def _pair_lanes(pair):
    return slice(LANES * pair, LANES * (pair + 1))


def _gla_scores(u, pair, head_lane):
    lanes = _pair_lanes(pair)
    kt = jnp.concatenate([c[:, lanes] for c in u["kt"]], axis=0)
    qt = jnp.concatenate([c[:, lanes] for c in u["qt"]], axis=0)
    kt2 = jnp.concatenate([(kt * hm).astype(BF16) for hm in head_lane], axis=0)
    return lax.dot_general(qt, kt2, NT, preferred_element_type=F32)


def _gla_update(u, vs, h, head_lane):
    lanes = _pair_lanes(h // 2)
    hm = head_lane[h % 2]
    k0 = (u["kend"][0][:, lanes] * hm).astype(BF16)
    k1 = (u["kend"][1][:, lanes] * hm).astype(BF16)
    z = jnp.zeros_like(k0)
    kbd = jnp.concatenate([jnp.concatenate([k0, z], axis=1),
                           jnp.concatenate([z, k1], axis=1)], axis=0)
    return lax.dot_general(vs[:, GLA_DV * h:GLA_DV * (h + 1)], kbd, TN,
                           preferred_element_type=F32)


def _gla_intra(scores, vs, pair, mask2):
    a = jnp.where(mask2, scores, 0.0).astype(BF16)
    v0 = vs[:, GLA_DV * 2 * pair:GLA_DV * (2 * pair + 1)]
    v1 = vs[:, GLA_DV * (2 * pair + 1):GLA_DV * (2 * pair + 2)]
    z = jnp.zeros_like(v0)
    vbd = jnp.concatenate([jnp.concatenate([v0, z], axis=1),
                           jnp.concatenate([z, v1], axis=1)], axis=0)
    return _dot(a, vbd)


def _gla_kernel(q_ref, k_ref, v_ref, laf_ref, lab_ref, kc_ref, vc_ref, lafc_ref, labc_ref,
                rs_ref, g_ref, o_ref, of_ref, ob_ref, sf_ref, sb_ref):
    t = q_ref.shape[0]
    t_ctx = kc_ref.shape[0]
    half = t // 2
    n_steps = t // STEP
    supers = STEP // SUPER
    n_chunks = SUPER // CHUNK
    pairs = GLA_HEADS // 2

    row = lax.broadcasted_iota(jnp.int32, (SUPER, SUPER), 0)
    col = lax.broadcasted_iota(jnp.int32, (SUPER, SUPER), 1)
    same_chunk = (row // CHUNK) == (col // CHUNK)
    low_mask = jnp.logical_and(same_chunk, col <= row)
    up_mask = jnp.logical_and(same_chunk, col >= row)
    lane = lax.broadcasted_iota(jnp.int32, (1, LANES), 1)
    head_lane = (jnp.where(lane < GLA_DK, 1.0, 0.0), jnp.where(lane >= GLA_DK, 1.0, 0.0))

    fwd = dict(tri_b=jnp.where(low_mask, 1.0, 0.0).astype(BF16),
               mask2=jnp.concatenate([low_mask, low_mask], axis=1),
               mid=CHUNK // 2 - 1, last=CHUNK - 1, order=(0, 1), st_ref=sf_ref)
    bwd = dict(tri_b=jnp.where(up_mask, 1.0, 0.0).astype(BF16),
               mask2=jnp.concatenate([up_mask, up_mask], axis=1),
               mid=CHUNK // 2, last=0, order=(1, 0), st_ref=sb_ref)

    sf_ref[...] = jnp.zeros_like(sf_ref)
    sb_ref[...] = jnp.zeros_like(sb_ref)
    gn = g_ref[...]

    def run(units, want_out, finalize):
        vals = []
        for dirn, refs, t0 in units:
            q_r, k_r, v_r, la_r = refs
            rows = pl.ds(t0, SUPER)
            qs = q_r[rows, :] if want_out else None
            vals.append((_gla_scale(qs, k_r[rows, :], la_r[rows, :], dirn, want_out),
                         v_r[rows, :]))
        scores = []
        if want_out:
            scores = [[_gla_scores(u, p, head_lane) for p in range(pairs)] for u, _ in vals]
        upd = [[_gla_update(u, vs, h, head_lane) for h in range(GLA_HEADS)] for u, vs in vals]
        intra = []
        if want_out:
            intra = [[_gla_intra(scores[n][p], vals[n][1], p, units[n][0]["mask2"])
                      for p in range(pairs)] for n in range(len(units))]
        states = {}
        for n, (dirn, _, t0) in enumerate(units):
            key = id(dirn["st_ref"])
            if key not in states:
                states[key] = (dirn["st_ref"], [dirn["st_ref"][h] for h in range(GLA_HEADS)])
            st = states[key][1]
            u = vals[n][0]
            inter = [[None] * n_chunks for _ in range(pairs)]
            for c in dirn["order"]:
                if want_out:
                    for p in range(pairs):
                        st2 = jnp.concatenate([st[2 * p].astype(BF16),
                                               st[2 * p + 1].astype(BF16)], axis=0)
                        inter[p][c] = lax.dot_general(u["qg"][c][:, _pair_lanes(p)], st2, NT,
                                                      preferred_element_type=F32)
                for h in range(GLA_HEADS):
                    e = u["e"][c][:, _pair_lanes(h // 2)]
                    st[h] = st[h] * e + upd[n][h][:, LANES * c:LANES * (c + 1)]
            if not want_out:
                continue
            fwd_unit = dirn is fwd
            for p in range(pairs):
                o2 = intra[n][p] + jnp.concatenate(inter[p], axis=0)
                for j in range(2):
                    cols = slice(GLA_DV * (2 * p + j), GLA_DV * (2 * p + j + 1))
                    o = o2[:, GLA_DV * j:GLA_DV * (j + 1)]
                    if finalize:
                        other = (ob_ref[pl.ds(t0 - half, SUPER), cols] if fwd_unit
                                 else of_ref[pl.ds(t0, SUPER), cols])
                        rs = rs_ref[pl.ds(t0, SUPER), cols].astype(F32)
                        o_ref[pl.ds(t0, SUPER), cols] = (_rms(o + other, gn) * rs).astype(BF16)
                    elif fwd_unit:
                        of_ref[pl.ds(t0, SUPER), cols] = o
                    else:
                        ob_ref[pl.ds(t0 - half, SUPER), cols] = o
        for st_ref, st in states.values():
            for h in range(GLA_HEADS):
                st_ref[h] = st[h]

    ctx_units = []
    for s in range(t_ctx // SUPER):
        ctx_units.append((fwd, (None, kc_ref, vc_ref, lafc_ref), s * SUPER))
        ctx_units.append((bwd, (None, kc_ref, vc_ref, labc_ref), t_ctx - (s + 1) * SUPER))
    run(ctx_units, want_out=False, finalize=False)

    def step(i, finalize):
        units = []
        for s in range(supers):
            t0 = pl.multiple_of(i * STEP + s * SUPER, SUPER)
            t1 = pl.multiple_of(t - (i + 1) * STEP + (supers - 1 - s) * SUPER, SUPER)
            units.append((fwd, (q_ref, k_ref, v_ref, laf_ref), t0))
            units.append((bwd, (q_ref, k_ref, v_ref, lab_ref), t1))
        run(units, want_out=True, finalize=finalize)

    def park_body(i, carry):
        step(i, False)
        return carry

    def finish_body(i, carry):
        step(i, True)
        return carry

    lax.fori_loop(0, n_steps // 2, park_body, 0)
    lax.fori_loop(n_steps // 2, n_steps, finish_body, 0)


def _gla(q, k, v, laf, lab, kc, vc, lafc, labc, rs, gnorm):
    bsz, t, _ = q.shape
    assert t % (2 * STEP) == 0 and kc.shape[1] % SUPER == 0

    def bspec(arr):
        return pl.BlockSpec((None,) + arr.shape[1:], lambda b: (b, 0, 0))

    args = (q, k, v, laf, lab, kc, vc, lafc, labc, rs)
    return pl.pallas_call(
        _gla_kernel,
        grid=(bsz,),
        in_specs=[bspec(a) for a in args] + [_const_spec(gnorm)],
        out_specs=pl.BlockSpec((None, t, HV), lambda b: (b, 0, 0)),
        out_shape=jax.ShapeDtypeStruct((bsz, t, HV), BF16),
        scratch_shapes=[
            pltpu.VMEM((t // 2, HV), F32), pltpu.VMEM((t // 2, HV), F32),
            pltpu.VMEM((GLA_HEADS, GLA_DV, LANES), F32),
            pltpu.VMEM((GLA_HEADS, GLA_DV, LANES), F32),
        ],
        compiler_params=pltpu.CompilerParams(
            dimension_semantics=("parallel",), vmem_limit_bytes=VMEM_LIMIT),
        name="gla",
    )(*args, gnorm)


def _tap_segments(n_tiles, lines_per_tile, n_lines, group):
    segs = []
    for t0 in range(0, n_tiles, group):
        t1 = min(t0 + group, n_tiles)
        lo_line = int(t0 * lines_per_tile)
        hi_line = int(-(-t1 * lines_per_tile // 1)) - 1
        k_lo = max(0, CONV_PAD - hi_line)
        k_hi = min(CONV_K, n_lines + CONV_PAD - lo_line)
        if segs and segs[-1][2:] == (k_lo, k_hi):
            segs[-1] = (segs[-1][0], t1, k_lo, k_hi)
        else:
            segs.append((t0, t1, k_lo, k_hi))
    return segs


def _conv_kernel(ah_ref, av_ref, w_ref, b_ref, yh_ref, yv_ref, ph_ref, pv_ref):
    t = ah_ref.shape[0]
    rows = t // GRID_W
    tile = 64

    def conv(src_ref, pad_ref, w_cols, line, n_lines, group, dst_ref):
        pad = CONV_PAD * line
        pad_ref[pl.ds(0, pad), :] = jnp.zeros((pad, CONV_HALF), F32)
        pad_ref[pl.ds(pad + t, pad), :] = jnp.zeros((pad, CONV_HALF), F32)

        def fill(i, carry):
            t0 = pl.multiple_of(i * tile, tile)
            pad_ref[pl.ds(pl.multiple_of(pad + t0, 8), tile), :] = (
                src_ref[pl.ds(t0, tile), :].astype(F32))
            return carry

        lax.fori_loop(0, t // tile, fill, 0)
        bias = b_ref[:, w_cols]

        for first, last, k_lo, k_hi in _tap_segments(t // tile, tile / line, n_lines, group):
            def body(i, carry, k_lo=k_lo, k_hi=k_hi):
                t0 = pl.multiple_of(i * tile, tile)
                acc = jnp.broadcast_to(bias, (tile, CONV_HALF))
                for kk in range(k_lo, k_hi):
                    win = pad_ref[pl.ds(pl.multiple_of(t0 + kk * line, 8), tile), :]
                    acc = acc + w_ref[kk:kk + 1, w_cols] * win
                dst_ref[pl.ds(t0, tile), :] = acc
                return carry

            lax.fori_loop(first, last, body, 0)

    conv(ah_ref, ph_ref, slice(0, CONV_HALF), rows, GRID_W, 2, yh_ref)
    conv(av_ref, pv_ref, slice(CONV_HALF, CONV_DIM), GRID_W, rows, 4, yv_ref)


def _conv(ah, av, w, b):
    bsz, t, c = ah.shape
    rows = t // GRID_W
    tok = pl.BlockSpec((None, t, c), lambda i: (i, 0, 0))
    return pl.pallas_call(
        _conv_kernel,
        grid=(bsz,),
        in_specs=[tok, tok, _const_spec(w), _const_spec(b)],
        out_specs=[tok, tok],
        out_shape=[jax.ShapeDtypeStruct((bsz, t, c), F32)] * 2,
        scratch_shapes=[
            pltpu.VMEM((t + 2 * CONV_PAD * rows, c), F32),
            pltpu.VMEM((t + 2 * CONV_PAD * GRID_W, c), F32),
        ],
        compiler_params=pltpu.CompilerParams(
            dimension_semantics=("parallel",), vmem_limit_bytes=VMEM_LIMIT),
        name="conv",
    )(ah, av, w, b)


def _post_kernel(x_ref, yh_ref, yv_ref, og_ref, gt_ref, mod_ref, perm_ref, lng_ref, lnb_ref,
                 gpost1_ref, gpre2_ref, gpost2_ref, wco_ref, wgo_ref, wout_ref, wff1_ref,
                 wff2_ref, o_ref):
    d = x_ref.shape[-1]
    d_ff = wff1_ref.shape[1]
    ff_tile = 1024
    sub = 256
    subs = [slice(s, s + sub) for s in range(0, x_ref.shape[0], sub)]
    gt1 = mod_ref[2:3, :]
    sh2 = mod_ref[3:4, :]
    sc2 = mod_ref[4:5, :]
    gt2 = mod_ref[5:6, :]

    yh = []
    for n in range(x_ref.shape[0] // PERM_TOKENS):
        blk = yh_ref[:, PERM_ROWS * n:PERM_ROWS * (n + 1), :].reshape(PERM_TOKENS, CONV_HALF)
        hi, lo = _split_bf16(blk)
        yh.append(_dot(perm_ref[...], hi) + _dot(perm_ref[...], lo))
    yh = jnp.concatenate(yh, axis=0)

    yn = []
    for s in subs:
        yc = jnp.concatenate([yh[s], yv_ref[s, :]], axis=-1)
        yc = yc - jnp.mean(yc, axis=-1, keepdims=True)
        var = jnp.mean(yc * yc, axis=-1, keepdims=True)
        z = yc * lax.rsqrt(var + EPS) * lng_ref[...] + lnb_ref[...]
        yn.append((z * _sigmoid(z)).astype(BF16))
    y_gla = [_dot(og_ref[s, :], wgo_ref[...]) for s in subs]
    y_conv = [_dot(v, wco_ref[...]) for v in yn]
    merged = []
    for n, s in enumerate(subs):
        gates = gt_ref[s, :].astype(F32)
        merged.append((gates[:, :d] * y_conv[n] + gates[:, d:] * y_gla[n]).astype(BF16))
    y = [_dot(v, wout_ref[...]) for v in merged]
    h_mid = [x_ref[s, :] + gt1 * _rms(y[n], gpost1_ref[...]) for n, s in enumerate(subs)]
    u2 = [(_rms(h, gpre2_ref[...]) * (1.0 + sc2) + sh2).astype(BF16) for h in h_mid]
    acc = [jnp.zeros(h.shape, F32) for h in h_mid]
    for j in range(d_ff // ff_tile):
        cols = slice(j * ff_tile, (j + 1) * ff_tile)
        f = [_dot(v, wff1_ref[:, cols]) for v in u2]
        f = [jnp.square(jnp.maximum(v, 0.0)).astype(BF16) for v in f]
        acc = [acc[n] + _dot(f[n], wff2_ref[cols, :]) for n in range(len(subs))]
    for n, s in enumerate(subs):
        o_ref[s, :] = h_mid[n] + gt2 * _rms(acc[n], gpost2_ref[...])


def _post(x, yh, yv, og, gates, mod, perm, ln_g, ln_b, gpost1, gpre2, gpost2, wco, wgo, wout,
          wff1, wff2, tm):
    bsz, t, d = x.shape
    assert tm % PERM_TOKENS == 0
    return pl.pallas_call(
        _post_kernel,
        grid=(bsz, t // tm),
        in_specs=[
            _tok_spec(tm, d), _colmajor_spec(tm), _tok_spec(tm, CONV_HALF), _tok_spec(tm, HV),
            _tok_spec(tm, 2 * d),
            pl.BlockSpec((None,) + mod.shape[1:], lambda b, i: (b, 0, 0)),
            _single_spec(perm), _const_spec(ln_g), _const_spec(ln_b),
            _const_spec(gpost1), _const_spec(gpre2), _const_spec(gpost2),
            _single_spec(wco), _single_spec(wgo), _single_spec(wout), _single_spec(wff1),
            _single_spec(wff2),
        ],
        out_specs=_tok_spec(tm, d),
        out_shape=jax.ShapeDtypeStruct((bsz, t, d), F32),
        compiler_params=pltpu.CompilerParams(
            dimension_semantics=("parallel", "parallel"), vmem_limit_bytes=VMEM_LIMIT),
        name="post",
    )(x, yh, yv, og, gates, mod, perm, ln_g, ln_b, gpost1, gpre2, gpost2, wco, wgo, wout, wff1,
      wff2)


def kernel(x, c, ctx, c_ctx, w_mod, b_mod, g_pre1, g_post1, g_pre2, g_post2, w_in, conv_w,
           conv_b, conv_ln_g, conv_ln_b, w_conv_out, w_decay, b_decay, gla_norm_g, w_gla_out,
           w_out, w_ff1, w_ff2):
    bsz, t, d = x.shape
    depth = w_in.shape[0]
    assert depth == 1 and d == D_MODEL and t % (GRID_W * 8) == 0

    n_rows = -(-(bsz + 1) // 8) * 8
    c_all = jnp.zeros((n_rows, d), F32).at[:bsz].set(c).at[bsz].set(c_ctx)
    m = _adaln(c_all, w_mod, b_mod)
    mod = m[:bsz].reshape(bsz, N_MOD, d)
    mod_ctx = m[bsz, :2 * d].reshape(1, 2, d)

    wm, wdin, wg = _split_w_in(jnp.swapaxes(w_in[0], 0, 1))
    wdec = jnp.zeros((LANES, 2 * HK), F32)
    wdec = wdec.at[:DECAY_RANK, :HK].set(w_decay[0, 0])
    wdec = wdec.at[DECAY_RANK:2 * DECAY_RANK, HK:].set(w_decay[0, 1]).astype(BF16)
    bdec = b_decay[0].reshape(1, 2 * HK)
    row = lambda v: v.reshape(1, -1)

    perm = _to_colmajor_perm()
    to_colmajor = jnp.asarray(perm, BF16)
    to_rowmajor = jnp.asarray(perm.T, BF16)

    ah, av, q, k, v, rs, laf, lab, gates = _proj(
        x, mod, row(g_pre1[0]), to_colmajor, wm, wdin, wdec, bdec, wg, tm=1024)
    kc, vc, lafc, labc = _ctx_proj(
        ctx, mod_ctx, row(g_pre1[0]), wm, wdin, wdec, bdec, nb=4)

    og = _gla(q, k, v, laf, lab, kc, vc, lafc, labc, rs, row(gla_norm_g[0]))
    yh, yv = _conv(ah.reshape(bsz, t, CONV_HALF), av, conv_w[0], row(conv_b[0]))
    yh = yh.reshape(bsz, GRID_W, t // GRID_W, CONV_HALF)

    return _post(x, yh, yv, og, gates, mod, to_rowmajor, row(conv_ln_g[0]), row(conv_ln_b[0]),
                 row(g_post1[0]), row(g_pre2[0]), row(g_post2[0]),
                 w_conv_out[0].astype(BF16), w_gla_out[0].astype(BF16), w_out[0].astype(BF16),
                 w_ff1[0].astype(BF16), w_ff2[0].astype(BF16), tm=512)
```

```python
import functools

import jax
import jax.numpy as jnp
import numpy as np
from jax import lax
from jax.experimental import pallas as pl
from jax.experimental.pallas import tpu as pltpu

D_MODEL = 1024
GRID_W = 64
CONV_DIM = 512
CONV_K = 31
GLA_HEADS = 4
GLA_DK = 64
GLA_DV = 128
DECAY_RANK = 16
GATE_NORM = 16.0
CHUNK = 64
N_MOD = 6
EPS = 1e-6

COL_Q = 2 * CONV_DIM
COL_K = COL_Q + GLA_HEADS * GLA_DK
COL_V = COL_K + GLA_HEADS * GLA_DK
COL_R = COL_V + GLA_HEADS * GLA_DV
COL_DEC = COL_R + GLA_HEADS * GLA_DV
COL_GATE = COL_DEC + 2 * DECAY_RANK
COL_END = COL_GATE + 2 * D_MODEL

LANES = 128
HK = GLA_HEADS * GLA_DK
HV = GLA_HEADS * GLA_DV
CONV_HALF = CONV_DIM // 2
CONV_PAD = CONV_K // 2
PERM_ROWS = 8
PERM_TOKENS = PERM_ROWS * GRID_W
PROJ_SUB = 256
SUPER = 2 * CHUNK
STEP = 2 * SUPER

VMEM_LIMIT = 56 * 1024 * 1024

F32 = jnp.float32
BF16 = jnp.bfloat16

NT = (((1,), (1,)), ((), ()))
TN = (((0,), (0,)), ((), ()))


def _dot(a, b):
    return jnp.dot(a, b, preferred_element_type=F32)


def _rms(x, g):
    ms = jnp.mean(x * x, axis=-1, keepdims=True)
    return x * lax.rsqrt(ms + EPS) * g


def _sigmoid(x):
    return jax.nn.sigmoid(x)


def _split_bf16(x):
    hi = x.astype(BF16)
    lo = (x - hi.astype(F32)).astype(BF16)
    return hi, lo


def _to_colmajor_perm():
    out = np.arange(PERM_TOKENS)
    src = (out % PERM_ROWS) * GRID_W + out // PERM_ROWS
    p = np.zeros((PERM_TOKENS, PERM_TOKENS), np.float32)
    p[out, src] = 1.0
    return p


def _adaln_kernel(c_ref, w_ref, b_ref, o_ref):
    c = c_ref[...]
    s_hi, s_lo = _split_bf16(c * _sigmoid(c))
    w_hi, w_lo = _split_bf16(w_ref[...])
    o_ref[...] = _dot(s_hi, w_hi) + _dot(s_lo, w_hi) + _dot(s_hi, w_lo) + b_ref[...]


def _adaln(c_all, w_mod, b_mod):
    n_rows, d = c_all.shape
    n_out = w_mod.shape[-1]
    tn = 1536
    return pl.pallas_call(
        _adaln_kernel,
        grid=(n_out // tn,),
        in_specs=[
            pl.BlockSpec((n_rows, d), lambda j: (0, 0)),
            pl.BlockSpec((None, d, tn), lambda j: (0, 0, j)),
            pl.BlockSpec((1, tn), lambda j: (0, j)),
        ],
        out_specs=pl.BlockSpec((n_rows, tn), lambda j: (0, j)),
        out_shape=jax.ShapeDtypeStruct((n_rows, n_out), F32),
        compiler_params=pltpu.CompilerParams(
            dimension_semantics=("arbitrary",), vmem_limit_bytes=VMEM_LIMIT),
        name="adaln",
    )(c_all, w_mod, b_mod)


def _split_w_in_kernel(wm_t_ref, wd_t_ref, wg_t_ref, wm_ref, wdin_ref, wg_ref, *, n_gate):
    j = pl.program_id(0)
    wm_ref[...] = wm_t_ref[...].astype(BF16).T

    @pl.when(j == 0)
    def _():
        wdin_ref[...] = wd_t_ref[...].astype(BF16).T

    @pl.when(j < n_gate)
    def _():
        wg_ref[...] = wg_t_ref[...].astype(BF16).T


def _split_w_in(w_in_t):
    n, d = w_in_t.shape
    tr = 256
    n_main = COL_DEC // tr
    n_gate = (n - COL_GATE) // tr
    assert COL_DEC % tr == 0 and (n - COL_GATE) % tr == 0 and n_gate <= n_main
    gate_blk = lambda j: jnp.minimum(j, n_gate - 1)
    return pl.pallas_call(
        functools.partial(_split_w_in_kernel, n_gate=n_gate),
        grid=(n_main,),
        in_specs=[
            pl.BlockSpec((tr, d), lambda j: (j, 0)),
            pl.BlockSpec((LANES, d), lambda j: (COL_DEC // LANES, 0)),
            pl.BlockSpec((pl.Element(tr), pl.Element(d)),
                         lambda j: (pl.multiple_of(COL_GATE + tr * gate_blk(j), 32), 0)),
        ],
        out_specs=[
            pl.BlockSpec((d, tr), lambda j: (0, j)),
            pl.BlockSpec((d, LANES), lambda j: (0, 0)),
            pl.BlockSpec((d, tr), lambda j: (0, gate_blk(j))),
        ],
        out_shape=[jax.ShapeDtypeStruct((d, w), BF16) for w in (COL_DEC, LANES, n - COL_GATE)],
        compiler_params=pltpu.CompilerParams(
            dimension_semantics=("arbitrary",), vmem_limit_bytes=VMEM_LIMIT),
        name="split_w_in",
    )(w_in_t, w_in_t, w_in_t)


def _prenorm_mod(x, g, mod_ref):
    sh = mod_ref[0:1, :]
    sc = mod_ref[1:2, :]
    return (_rms(x, g) * (1.0 + sc) + sh).astype(BF16)


def _decay_logs(z, wdec_ref, bdec_ref):
    z_hi, z_lo = z
    logits = _dot(z_hi, wdec_ref[...]) + _dot(z_lo, wdec_ref[...]) + bdec_ref[...]
    ls = jnp.minimum(logits, 0.0) - jnp.log(1.0 + jnp.exp(-jnp.abs(logits)))
    return ls * (1.0 / GATE_NORM)


N_LATE_W = 5


def _proj_kernel(x_ref, mod_ref, g_ref, perm_ref, wm_ref, wdin_ref, wdec_ref, bdec_ref, wg_ref,
                 *refs):
    late_f32 = refs[:N_LATE_W]
    (ah_ref, av_ref, q_ref, k_ref, v_ref, rs_ref, laf_ref, lab_ref,
     gt_ref) = refs[N_LATE_W:-N_LATE_W]
    late_bf16 = refs[-N_LATE_W:]
    for src, dst in zip(late_f32, late_bf16):
        dst[...] = src[...].astype(BF16)
    subs = [slice(s, s + PROJ_SUB) for s in range(0, x_ref.shape[0], PROJ_SUB)]
    g = g_ref[...]
    ub = [_prenorm_mod(x_ref[s, :], g, mod_ref) for s in subs]
    z = [_split_bf16(_dot(u, wdin_ref[...])) for u in ub]
    a_h = []
    for n, s in enumerate(subs):
        glu = _dot(ub[n], wm_ref[:, 0:COL_Q])
        a = (glu[:, :CONV_DIM] * _sigmoid(glu[:, CONV_DIM:])).astype(BF16)
        av_ref[s, :] = a[:, CONV_HALF:]
        a_h.append(a[:, :CONV_HALF])
    for n, s in enumerate(subs):
        q_ref[s, :] = (_dot(ub[n], wm_ref[:, COL_Q:COL_K]) * (GLA_DK ** -0.5)).astype(BF16)
    for n, s in enumerate(subs):
        k_ref[s, :] = _dot(ub[n], wm_ref[:, COL_K:COL_V]).astype(BF16)
    for n, s in enumerate(subs):
        v_ref[s, :] = _dot(ub[n], wm_ref[:, COL_V:COL_R]).astype(BF16)
    for n, s in enumerate(subs):
        la = _decay_logs(z[n], wdec_ref, bdec_ref)
        laf_ref[s, :] = la[:, :HK]
        lab_ref[s, :] = la[:, HK:]
    for n, s in enumerate(subs):
        r = _dot(ub[n], wm_ref[:, COL_R:COL_DEC])
        rs_ref[s, :] = (r * _sigmoid(r)).astype(BF16)
    per_group = PERM_TOKENS // PROJ_SUB
    for n in range(x_ref.shape[0] // PERM_TOKENS):
        a_rm = jnp.concatenate(a_h[per_group * n:per_group * (n + 1)], axis=0)
        a_cm = _dot(perm_ref[...], a_rm)
        ah_ref[:, PERM_ROWS * n:PERM_ROWS * (n + 1), :] = a_cm.reshape(
            GRID_W, PERM_ROWS, CONV_HALF)
    for n, s in enumerate(subs):
        gt_ref[s, :] = _sigmoid(_dot(ub[n], wg_ref[...])).astype(BF16)


def _ctx_proj_kernel(x_ref, mod_ref, g_ref, wm_ref, wdin_ref, wdec_ref, bdec_ref,
                     k_ref, v_ref, laf_ref, lab_ref):
    nb, t, d = x_ref.shape
    ub = _prenorm_mod(x_ref[...].reshape(nb * t, d), g_ref[...], mod_ref)
    z = _split_bf16(_dot(ub, wdin_ref[...]))
    k_ref[...] = _dot(ub, wm_ref[:, COL_K:COL_V]).astype(BF16).reshape(k_ref.shape)
    v_ref[...] = _dot(ub, wm_ref[:, COL_V:COL_R]).astype(BF16).reshape(v_ref.shape)
    la = _decay_logs(z, wdec_ref, bdec_ref)
    laf_ref[...] = la[:, :HK].reshape(laf_ref.shape)
    lab_ref[...] = la[:, HK:].reshape(lab_ref.shape)


def _const_spec(arr):
    nd = arr.ndim
    return pl.BlockSpec(arr.shape, lambda *_: (0,) * nd)


def _tok_spec(tm, width):
    return pl.BlockSpec((None, tm, width), lambda b, i: (b, i, 0))


def _colmajor_spec(tm):
    return pl.BlockSpec((None, GRID_W, tm // GRID_W, CONV_HALF), lambda b, i: (b, 0, i, 0))


def _single_spec(arr):
    nd = arr.ndim
    return pl.BlockSpec(arr.shape, lambda *_: (0,) * nd, pipeline_mode=pl.Buffered(1))


def _proj(x, mod, g, perm, wm, wdin, wdec, bdec, wg, late_weights, tm):
    bsz, t, d = x.shape
    widths = (CONV_HALF, HK, HK, HV, HV, HK, HK, 2 * D_MODEL)
    dtypes = (BF16, BF16, BF16, BF16, BF16, F32, F32, BF16)
    assert tm % PERM_TOKENS == 0 and len(late_weights) == N_LATE_W
    ah_shape = jax.ShapeDtypeStruct((bsz, GRID_W, t // GRID_W, CONV_HALF), F32)
    n_i = t // tm
    n_steps = bsz * n_i
    late_in, late_out, late_shape = [], [], []
    for w in late_weights:
        _, rows, cols = w.shape
        assert rows % (16 * n_steps) == 0
        blk = rows // n_steps
        late_in.append(pl.BlockSpec((None, blk, cols), lambda b, i: (0, b * n_i + i, 0)))
        late_out.append(pl.BlockSpec((blk, cols), lambda b, i: (b * n_i + i, 0)))
        late_shape.append(jax.ShapeDtypeStruct((rows, cols), BF16))
    return pl.pallas_call(
        _proj_kernel,
        grid=(bsz, n_i),
        in_specs=[
            _tok_spec(tm, d),
            pl.BlockSpec((None,) + mod.shape[1:], lambda b, i: (b, 0, 0)),
            _const_spec(g), _single_spec(perm), _single_spec(wm), _single_spec(wdin),
            _single_spec(wdec), _const_spec(bdec), _single_spec(wg),
        ] + late_in,
        out_specs=[_colmajor_spec(tm)] + [_tok_spec(tm, w) for w in widths] + late_out,
        out_shape=[ah_shape] + [jax.ShapeDtypeStruct((bsz, t, w), dt)
                                for w, dt in zip(widths, dtypes)] + late_shape,
        compiler_params=pltpu.CompilerParams(
            dimension_semantics=("parallel", "parallel"), vmem_limit_bytes=VMEM_LIMIT),
        name="proj",
    )(x, mod, g, perm, wm, wdin, wdec, bdec, wg, *late_weights)


def _ctx_proj(x, mod, g, wm, wdin, wdec, bdec, nb):
    bsz, t, d = x.shape
    widths = (HK, HV, HK, HK)
    dtypes = (BF16, BF16, F32, F32)
    assert bsz % nb == 0
    return pl.pallas_call(
        _ctx_proj_kernel,
        grid=(bsz // nb,),
        in_specs=[
            pl.BlockSpec((nb, t, d), lambda i: (i, 0, 0)),
            pl.BlockSpec((None,) + mod.shape[1:], lambda i: (0, 0, 0)),
            _const_spec(g), _single_spec(wm), _single_spec(wdin), _single_spec(wdec),
            _const_spec(bdec),
        ],
        out_specs=[pl.BlockSpec((nb, t, w), lambda i: (i, 0, 0)) for w in widths],
        out_shape=[jax.ShapeDtypeStruct((bsz, t, w), dt) for w, dt in zip(widths, dtypes)],
        compiler_params=pltpu.CompilerParams(
            dimension_semantics=("parallel",), vmem_limit_bytes=VMEM_LIMIT),
        name="ctx_proj",
    )(x, mod, g, wm, wdin, wdec, bdec)


def _gla_scale(qs, ks, las, dirn, want_out):
    tri_b, mid, last = dirn["tri_b"], dirn["mid"], dirn["last"]
    la_hi, la_lo = _split_bf16(las)
    g = _dot(tri_b, la_hi) + _dot(tri_b, la_lo)
    u = dict(kt=[], kend=[], qt=[], qg=[], e=[])
    for c in range(SUPER // CHUNK):
        rows = slice(c * CHUNK, (c + 1) * CHUNK)
        gc = g[rows]
        g_mid = gc[mid:mid + 1]
        g_last = gc[last:last + 1]
        u["e"].append(jnp.exp(g_last))
        kt = ks[rows].astype(F32) * jnp.exp(g_mid - gc)
        u["kt"].append(kt)
        u["kend"].append(kt * jnp.exp(g_last - g_mid))
        if want_out:
            qt = qs[rows].astype(F32) * jnp.exp(gc - g_mid)
            u["qt"].append(qt.astype(BF16))
            u["qg"].append((qt * jnp.exp(g_mid)).astype(BF16))
    return u


def _pair_lanes(pair):
    return slice(LANES * pair, LANES * (pair + 1))


def _gla_scores(u, pair, head_lane):
    lanes = _pair_lanes(pair)
    kt = jnp.concatenate([c[:, lanes] for c in u["kt"]], axis=0)
    qt = jnp.concatenate([c[:, lanes] for c in u["qt"]], axis=0)
    kt2 = jnp.concatenate([(kt * hm).astype(BF16) for hm in head_lane], axis=0)
    return lax.dot_general(qt, kt2, NT, preferred_element_type=F32)


def _gla_update(u, vs, h, head_lane):
    lanes = _pair_lanes(h // 2)
    hm = head_lane[h % 2]
    k0 = (u["kend"][0][:, lanes] * hm).astype(BF16)
    k1 = (u["kend"][1][:, lanes] * hm).astype(BF16)
    z = jnp.zeros_like(k0)
    kbd = jnp.concatenate([jnp.concatenate([k0, z], axis=1),
                           jnp.concatenate([z, k1], axis=1)], axis=0)
    return lax.dot_general(vs[:, GLA_DV * h:GLA_DV * (h + 1)], kbd, TN,
                           preferred_element_type=F32)


def _gla_intra(scores, vs, pair, mask2):
    a = jnp.where(mask2, scores, 0.0).astype(BF16)
    v0 = vs[:, GLA_DV * 2 * pair:GLA_DV * (2 * pair + 1)]
    v1 = vs[:, GLA_DV * (2 * pair + 1):GLA_DV * (2 * pair + 2)]
    z = jnp.zeros_like(v0)
    vbd = jnp.concatenate([jnp.concatenate([v0, z], axis=1),
                           jnp.concatenate([z, v1], axis=1)], axis=0)
    return _dot(a, vbd)


def _gla_kernel(q_ref, k_ref, v_ref, laf_ref, lab_ref, kc_ref, vc_ref, lafc_ref, labc_ref,
                rs_ref, g_ref, o_ref, of_ref, ob_ref, sf_ref, sb_ref):
    t = q_ref.shape[0]
    t_ctx = kc_ref.shape[0]
    half = t // 2
    n_steps = t // STEP
    supers = STEP // SUPER
    n_chunks = SUPER // CHUNK
    pairs = GLA_HEADS // 2

    row = lax.broadcasted_iota(jnp.int32, (SUPER, SUPER), 0)
    col = lax.broadcasted_iota(jnp.int32, (SUPER, SUPER), 1)
    same_chunk = (row // CHUNK) == (col // CHUNK)
    low_mask = jnp.logical_and(same_chunk, col <= row)
    up_mask = jnp.logical_and(same_chunk, col >= row)
    lane = lax.broadcasted_iota(jnp.int32, (1, LANES), 1)
    head_lane = (jnp.where(lane < GLA_DK, 1.0, 0.0), jnp.where(lane >= GLA_DK, 1.0, 0.0))

    fwd = dict(tri_b=jnp.where(low_mask, 1.0, 0.0).astype(BF16),
               mask2=jnp.concatenate([low_mask, low_mask], axis=1),
               mid=CHUNK // 2 - 1, last=CHUNK - 1, order=(0, 1), st_ref=sf_ref)
    bwd = dict(tri_b=jnp.where(up_mask, 1.0, 0.0).astype(BF16),
               mask2=jnp.concatenate([up_mask, up_mask], axis=1),
               mid=CHUNK // 2, last=0, order=(1, 0), st_ref=sb_ref)

    sf_ref[...] = jnp.zeros_like(sf_ref)
    sb_ref[...] = jnp.zeros_like(sb_ref)
    gn = g_ref[...]

    def run(units, want_out, finalize):
        vals = []
        for dirn, refs, t0 in units:
            q_r, k_r, v_r, la_r = refs
            rows = pl.ds(t0, SUPER)
            qs = q_r[rows, :] if want_out else None
            vals.append((_gla_scale(qs, k_r[rows, :], la_r[rows, :], dirn, want_out),
                         v_r[rows, :]))
        scores = []
        if want_out:
            scores = [[_gla_scores(u, p, head_lane) for p in range(pairs)] for u, _ in vals]
        upd = [[_gla_update(u, vs, h, head_lane) for h in range(GLA_HEADS)] for u, vs in vals]
        intra = []
        if want_out:
            intra = [[_gla_intra(scores[n][p], vals[n][1], p, units[n][0]["mask2"])
                      for p in range(pairs)] for n in range(len(units))]
        states = {}
        for n, (dirn, _, t0) in enumerate(units):
            key = id(dirn["st_ref"])
            if key not in states:
                states[key] = (dirn["st_ref"], [dirn["st_ref"][h] for h in range(GLA_HEADS)])
            st = states[key][1]
            u = vals[n][0]
            inter = [[None] * n_chunks for _ in range(pairs)]
            for c in dirn["order"]:
                if want_out:
                    for p in range(pairs):
                        st2 = jnp.concatenate([st[2 * p].astype(BF16),
                                               st[2 * p + 1].astype(BF16)], axis=0)
                        inter[p][c] = lax.dot_general(u["qg"][c][:, _pair_lanes(p)], st2, NT,
                                                      preferred_element_type=F32)
                for h in range(GLA_HEADS):
                    e = u["e"][c][:, _pair_lanes(h // 2)]
                    st[h] = st[h] * e + upd[n][h][:, LANES * c:LANES * (c + 1)]
            if not want_out:
                continue
            fwd_unit = dirn is fwd
            for p in range(pairs):
                o2 = intra[n][p] + jnp.concatenate(inter[p], axis=0)
                for j in range(2):
                    cols = slice(GLA_DV * (2 * p + j), GLA_DV * (2 * p + j + 1))
                    o = o2[:, GLA_DV * j:GLA_DV * (j + 1)]
                    if finalize:
                        other = (ob_ref[pl.ds(t0 - half, SUPER), cols] if fwd_unit
                                 else of_ref[pl.ds(t0, SUPER), cols])
                        rs = rs_ref[pl.ds(t0, SUPER), cols].astype(F32)
                        o_ref[pl.ds(t0, SUPER), cols] = (_rms(o + other, gn) * rs).astype(BF16)
                    elif fwd_unit:
                        of_ref[pl.ds(t0, SUPER), cols] = o
                    else:
                        ob_ref[pl.ds(t0 - half, SUPER), cols] = o
        for st_ref, st in states.values():
            for h in range(GLA_HEADS):
                st_ref[h] = st[h]

    ctx_units = []
    for s in range(t_ctx // SUPER):
        ctx_units.append((fwd, (None, kc_ref, vc_ref, lafc_ref), s * SUPER))
        ctx_units.append((bwd, (None, kc_ref, vc_ref, labc_ref), t_ctx - (s + 1) * SUPER))
    run(ctx_units, want_out=False, finalize=False)

    def step(i, finalize):
        units = []
        for s in range(supers):
            t0 = pl.multiple_of(i * STEP + s * SUPER, SUPER)
            t1 = pl.multiple_of(t - (i + 1) * STEP + (supers - 1 - s) * SUPER, SUPER)
            units.append((fwd, (q_ref, k_ref, v_ref, laf_ref), t0))
            units.append((bwd, (q_ref, k_ref, v_ref, lab_ref), t1))
        run(units, want_out=True, finalize=finalize)

    def park_body(i, carry):
        step(i, False)
        return carry

    def finish_body(i, carry):
        step(i, True)
        return carry

    lax.fori_loop(0, n_steps // 2, park_body, 0)
    lax.fori_loop(n_steps // 2, n_steps, finish_body, 0)


def _gla(q, k, v, laf, lab, kc, vc, lafc, labc, rs, gnorm):
    bsz, t, _ = q.shape
    assert t % (2 * STEP) == 0 and kc.shape[1] % SUPER == 0

    def bspec(arr):
        return pl.BlockSpec((None,) + arr.shape[1:], lambda b: (b, 0, 0))

    args = (q, k, v, laf, lab, kc, vc, lafc, labc, rs)
    return pl.pallas_call(
        _gla_kernel,
        grid=(bsz,),
        in_specs=[bspec(a) for a in args] + [_const_spec(gnorm)],
        out_specs=pl.BlockSpec((None, t, HV), lambda b: (b, 0, 0)),
        out_shape=jax.ShapeDtypeStruct((bsz, t, HV), BF16),
        scratch_shapes=[
            pltpu.VMEM((t // 2, HV), F32), pltpu.VMEM((t // 2, HV), F32),
            pltpu.VMEM((GLA_HEADS, GLA_DV, LANES), F32),
            pltpu.VMEM((GLA_HEADS, GLA_DV, LANES), F32),
        ],
        compiler_params=pltpu.CompilerParams(
            dimension_semantics=("parallel",), vmem_limit_bytes=VMEM_LIMIT),
        name="gla",
    )(*args, gnorm)


def _tap_segments(n_tiles, lines_per_tile, n_lines, group):
    segs = []
    for t0 in range(0, n_tiles, group):
        t1 = min(t0 + group, n_tiles)
        lo_line = int(t0 * lines_per_tile)
        hi_line = int(-(-t1 * lines_per_tile // 1)) - 1
        k_lo = max(0, CONV_PAD - hi_line)
        k_hi = min(CONV_K, n_lines + CONV_PAD - lo_line)
        if segs and segs[-1][2:] == (k_lo, k_hi):
            segs[-1] = (segs[-1][0], t1, k_lo, k_hi)
        else:
            segs.append((t0, t1, k_lo, k_hi))
    return segs


def _conv_kernel(ah_ref, av_ref, w_ref, b_ref, yh_ref, yv_ref, ph_ref, pv_ref):
    t = ah_ref.shape[0]
    rows = t // GRID_W
    tile = 64

    def conv(src_ref, pad_ref, w_cols, line, n_lines, group, dst_ref):
        pad = CONV_PAD * line
        pad_ref[pl.ds(0, pad), :] = jnp.zeros((pad, CONV_HALF), F32)
        pad_ref[pl.ds(pad + t, pad), :] = jnp.zeros((pad, CONV_HALF), F32)

        def fill(i, carry):
            t0 = pl.multiple_of(i * tile, tile)
            pad_ref[pl.ds(pl.multiple_of(pad + t0, 8), tile), :] = (
                src_ref[pl.ds(t0, tile), :].astype(F32))
            return carry

        lax.fori_loop(0, t // tile, fill, 0)
        bias = b_ref[:, w_cols]

        for first, last, k_lo, k_hi in _tap_segments(t // tile, tile / line, n_lines, group):
            def body(i, carry, k_lo=k_lo, k_hi=k_hi):
                t0 = pl.multiple_of(i * tile, tile)
                acc = jnp.broadcast_to(bias, (tile, CONV_HALF))
                for kk in range(k_lo, k_hi):
                    win = pad_ref[pl.ds(pl.multiple_of(t0 + kk * line, 8), tile), :]
                    acc = acc + w_ref[kk:kk + 1, w_cols] * win
                dst_ref[pl.ds(t0, tile), :] = acc
                return carry

            lax.fori_loop(first, last, body, 0)

    conv(ah_ref, ph_ref, slice(0, CONV_HALF), rows, GRID_W, 2, yh_ref)
    conv(av_ref, pv_ref, slice(CONV_HALF, CONV_DIM), GRID_W, rows, 4, yv_ref)


def _conv(ah, av, w, b):
    bsz, t, c = ah.shape
    rows = t // GRID_W
    tok = pl.BlockSpec((None, t, c), lambda i: (i, 0, 0))
    return pl.pallas_call(
        _conv_kernel,
        grid=(bsz,),
        in_specs=[tok, tok, _const_spec(w), _const_spec(b)],
        out_specs=[tok, tok],
        out_shape=[jax.ShapeDtypeStruct((bsz, t, c), F32)] * 2,
        scratch_shapes=[
            pltpu.VMEM((t + 2 * CONV_PAD * rows, c), F32),
            pltpu.VMEM((t + 2 * CONV_PAD * GRID_W, c), F32),
        ],
        compiler_params=pltpu.CompilerParams(
            dimension_semantics=("parallel",), vmem_limit_bytes=VMEM_LIMIT),
        name="conv",
    )(ah, av, w, b)


def _post_kernel(x_ref, yh_ref, yv_ref, og_ref, gt_ref, mod_ref, perm_ref, lng_ref, lnb_ref,
                 gpost1_ref, gpre2_ref, gpost2_ref, wco_ref, wgo_ref, wout_ref, wff1_ref,
                 wff2_ref, o_ref):
    d = x_ref.shape[-1]
    d_ff = wff1_ref.shape[1]
    ff_tile = 1024
    sub = 256
    subs = [slice(s, s + sub) for s in range(0, x_ref.shape[0], sub)]
    gt1 = mod_ref[2:3, :]
    sh2 = mod_ref[3:4, :]
    sc2 = mod_ref[4:5, :]
    gt2 = mod_ref[5:6, :]

    yh = []
    for n in range(x_ref.shape[0] // PERM_TOKENS):
        blk = yh_ref[:, PERM_ROWS * n:PERM_ROWS * (n + 1), :].reshape(PERM_TOKENS, CONV_HALF)
        hi, lo = _split_bf16(blk)
        yh.append(_dot(perm_ref[...], hi) + _dot(perm_ref[...], lo))
    yh = jnp.concatenate(yh, axis=0)

    yn = []
    for s in subs:
        yc = jnp.concatenate([yh[s], yv_ref[s, :]], axis=-1)
        yc = yc - jnp.mean(yc, axis=-1, keepdims=True)
        var = jnp.mean(yc * yc, axis=-1, keepdims=True)
        z = yc * lax.rsqrt(var + EPS) * lng_ref[...] + lnb_ref[...]
        yn.append((z * _sigmoid(z)).astype(BF16))
    y_gla = [_dot(og_ref[s, :], wgo_ref[...]) for s in subs]
    y_conv = [_dot(v, wco_ref[...]) for v in yn]
    merged = []
    for n, s in enumerate(subs):
        gates = gt_ref[s, :].astype(F32)
        merged.append((gates[:, :d] * y_conv[n] + gates[:, d:] * y_gla[n]).astype(BF16))
    y = [_dot(v, wout_ref[...]) for v in merged]
    h_mid = [x_ref[s, :] + gt1 * _rms(y[n], gpost1_ref[...]) for n, s in enumerate(subs)]
    u2 = [(_rms(h, gpre2_ref[...]) * (1.0 + sc2) + sh2).astype(BF16) for h in h_mid]
    acc = [jnp.zeros(h.shape, F32) for h in h_mid]
    for j in range(d_ff // ff_tile):
        cols = slice(j * ff_tile, (j + 1) * ff_tile)
        f = [_dot(v, wff1_ref[:, cols]) for v in u2]
        f = [jnp.square(jnp.maximum(v, 0.0)).astype(BF16) for v in f]
        acc = [acc[n] + _dot(f[n], wff2_ref[cols, :]) for n in range(len(subs))]
    for n, s in enumerate(subs):
        o_ref[s, :] = h_mid[n] + gt2 * _rms(acc[n], gpost2_ref[...])


def _post(x, yh, yv, og, gates, mod, perm, ln_g, ln_b, gpost1, gpre2, gpost2, wco, wgo, wout,
          wff1, wff2, tm):
    bsz, t, d = x.shape
    assert tm % PERM_TOKENS == 0
    return pl.pallas_call(
        _post_kernel,
        grid=(bsz, t // tm),
        in_specs=[
            _tok_spec(tm, d), _colmajor_spec(tm), _tok_spec(tm, CONV_HALF), _tok_spec(tm, HV),
            _tok_spec(tm, 2 * d),
            pl.BlockSpec((None,) + mod.shape[1:], lambda b, i: (b, 0, 0)),
            _single_spec(perm), _const_spec(ln_g), _const_spec(ln_b),
            _const_spec(gpost1), _const_spec(gpre2), _const_spec(gpost2),
            _single_spec(wco), _single_spec(wgo), _single_spec(wout), _single_spec(wff1),
            _single_spec(wff2),
        ],
        out_specs=_tok_spec(tm, d),
        out_shape=jax.ShapeDtypeStruct((bsz, t, d), F32),
        compiler_params=pltpu.CompilerParams(
            dimension_semantics=("parallel", "parallel"), vmem_limit_bytes=VMEM_LIMIT),
        name="post",
    )(x, yh, yv, og, gates, mod, perm, ln_g, ln_b, gpost1, gpre2, gpost2, wco, wgo, wout, wff1,
      wff2)


def kernel(x, c, ctx, c_ctx, w_mod, b_mod, g_pre1, g_post1, g_pre2, g_post2, w_in, conv_w,
           conv_b, conv_ln_g, conv_ln_b, w_conv_out, w_decay, b_decay, gla_norm_g, w_gla_out,
           w_out, w_ff1, w_ff2):
    bsz, t, d = x.shape
    depth = w_in.shape[0]
    assert depth == 1 and d == D_MODEL and t % (GRID_W * 8) == 0

    n_rows = -(-(bsz + 1) // 8) * 8
    c_all = jnp.zeros((n_rows, d), F32).at[:bsz].set(c).at[bsz].set(c_ctx)
    m = _adaln(c_all, w_mod, b_mod)
    mod = m[:bsz].reshape(bsz, N_MOD, d)
    mod_ctx = m[bsz, :2 * d].reshape(1, 2, d)

    wm, wdin, wg = _split_w_in(jnp.swapaxes(w_in[0], 0, 1))
    wdec = jnp.zeros((LANES, 2 * HK), F32)
    wdec = wdec.at[:DECAY_RANK, :HK].set(w_decay[0, 0])
    wdec = wdec.at[DECAY_RANK:2 * DECAY_RANK, HK:].set(w_decay[0, 1]).astype(BF16)
    bdec = b_decay[0].reshape(1, 2 * HK)
    row = lambda v: v.reshape(1, -1)

    perm = _to_colmajor_perm()
    to_colmajor = jnp.asarray(perm, BF16)
    to_rowmajor = jnp.asarray(perm.T, BF16)

    ah, av, q, k, v, rs, laf, lab, gates, wco, wgo, wout, wff1, wff2 = _proj(
        x, mod, row(g_pre1[0]), to_colmajor, wm, wdin, wdec, bdec, wg,
        (w_conv_out, w_gla_out, w_out, w_ff1, w_ff2), tm=1024)
    kc, vc, lafc, labc = _ctx_proj(
        ctx, mod_ctx, row(g_pre1[0]), wm, wdin, wdec, bdec, nb=4)

    og = _gla(q, k, v, laf, lab, kc, vc, lafc, labc, rs, row(gla_norm_g[0]))
    yh, yv = _conv(ah.reshape(bsz, t, CONV_HALF), av, conv_w[0], row(conv_b[0]))
    yh = yh.reshape(bsz, GRID_W, t // GRID_W, CONV_HALF)

    return _post(x, yh, yv, og, gates, mod, to_rowmajor, row(conv_ln_g[0]), row(conv_ln_b[0]),
                 row(g_post1[0]), row(g_pre2[0]), row(g_post2[0]),
                 wco, wgo, wout, wff1, wff2, tm=512)
```

```python
import functools

import jax
import jax.numpy as jnp
import numpy as np
from jax import lax
from jax.experimental import pallas as pl
from jax.experimental.pallas import tpu as pltpu

D_MODEL = 1024
GRID_W = 64
CONV_DIM = 512
CONV_K = 31
GLA_HEADS = 4
GLA_DK = 64
GLA_DV = 128
DECAY_RANK = 16
GATE_NORM = 16.0
CHUNK = 64
N_MOD = 6
EPS = 1e-6

COL_Q = 2 * CONV_DIM
COL_K = COL_Q + GLA_HEADS * GLA_DK
COL_V = COL_K + GLA_HEADS * GLA_DK
COL_R = COL_V + GLA_HEADS * GLA_DV
COL_DEC = COL_R + GLA_HEADS * GLA_DV
COL_GATE = COL_DEC + 2 * DECAY_RANK
COL_END = COL_GATE + 2 * D_MODEL

LANES = 128
HK = GLA_HEADS * GLA_DK
HV = GLA_HEADS * GLA_DV
CONV_HALF = CONV_DIM // 2
CONV_PAD = CONV_K // 2
PERM_ROWS = 8
PERM_TOKENS = PERM_ROWS * GRID_W
PROJ_SUB = 256
SUPER = 2 * CHUNK
STEP = 2 * SUPER

VMEM_LIMIT = 56 * 1024 * 1024

F32 = jnp.float32
BF16 = jnp.bfloat16

NT = (((1,), (1,)), ((), ()))
TN = (((0,), (0,)), ((), ()))


def _dot(a, b):
    return jnp.dot(a, b, preferred_element_type=F32)


def _rms(x, g):
    ms = jnp.mean(x * x, axis=-1, keepdims=True)
    return x * lax.rsqrt(ms + EPS) * g


def _sigmoid(x):
    return jax.nn.sigmoid(x)


def _split_bf16(x):
    hi = x.astype(BF16)
    lo = (x - hi.astype(F32)).astype(BF16)
    return hi, lo


def _to_colmajor_perm():
    out = np.arange(PERM_TOKENS)
    src = (out % PERM_ROWS) * GRID_W + out // PERM_ROWS
    p = np.zeros((PERM_TOKENS, PERM_TOKENS), np.float32)
    p[out, src] = 1.0
    return p


def _adaln_kernel(c_ref, w_ref, b_ref, o_ref):
    @pl.when(pl.program_id(0) == 0)
    def _():
        o_ref[...] = jnp.broadcast_to(b_ref[...], o_ref.shape)

    c = c_ref[...]
    s_hi, s_lo = _split_bf16(c * _sigmoid(c))
    w_hi, w_lo = _split_bf16(w_ref[...])
    o_ref[...] += _dot(s_hi, w_hi) + _dot(s_lo, w_hi) + _dot(s_hi, w_lo)


def _adaln(c_all, w_mod, b_mod):
    n_rows, d = c_all.shape
    n_out = w_mod.shape[-1]
    tk = 256
    return pl.pallas_call(
        _adaln_kernel,
        grid=(d // tk,),
        in_specs=[
            pl.BlockSpec((n_rows, tk), lambda k: (0, k)),
            pl.BlockSpec((None, tk, n_out), lambda k: (0, k, 0)),
            pl.BlockSpec((1, n_out), lambda k: (0, 0)),
        ],
        out_specs=pl.BlockSpec((n_rows, n_out), lambda k: (0, 0)),
        out_shape=jax.ShapeDtypeStruct((n_rows, n_out), F32),
        compiler_params=pltpu.CompilerParams(
            dimension_semantics=("arbitrary",), vmem_limit_bytes=VMEM_LIMIT),
        name="adaln",
    )(c_all, w_mod, b_mod)


def _split_w_in_kernel(wm_t_ref, wd_t_ref, wg_t_ref, wm_ref, wdin_ref, wg_ref, *, n_gate):
    j = pl.program_id(0)
    wm_ref[...] = wm_t_ref[...].astype(BF16).T

    @pl.when(j == 0)
    def _():
        wdin_ref[...] = wd_t_ref[...].astype(BF16).T

    @pl.when(j < n_gate)
    def _():
        wg_ref[...] = wg_t_ref[...].astype(BF16).T


def _split_w_in(w_in_t):
    n, d = w_in_t.shape
    tr = 512
    n_main = COL_DEC // tr
    n_gate = (n - COL_GATE) // tr
    assert COL_DEC % tr == 0 and (n - COL_GATE) % tr == 0 and n_gate <= n_main
    gate_blk = lambda j: jnp.minimum(j, n_gate - 1)
    return pl.pallas_call(
        functools.partial(_split_w_in_kernel, n_gate=n_gate),
        grid=(n_main,),
        in_specs=[
            pl.BlockSpec((tr, d), lambda j: (j, 0)),
            pl.BlockSpec((LANES, d), lambda j: (COL_DEC // LANES, 0)),
            pl.BlockSpec((pl.Element(tr), pl.Element(d)),
                         lambda j: (pl.multiple_of(COL_GATE + tr * gate_blk(j), 32), 0)),
        ],
        out_specs=[
            pl.BlockSpec((d, tr), lambda j: (0, j)),
            pl.BlockSpec((d, LANES), lambda j: (0, 0)),
            pl.BlockSpec((d, tr), lambda j: (0, gate_blk(j))),
        ],
        out_shape=[jax.ShapeDtypeStruct((d, w), BF16) for w in (COL_DEC, LANES, n - COL_GATE)],
        compiler_params=pltpu.CompilerParams(
            dimension_semantics=("arbitrary",), vmem_limit_bytes=VMEM_LIMIT),
        name="split_w_in",
    )(w_in_t, w_in_t, w_in_t)


def _prenorm_mod(x, g, mod_ref):
    sh = mod_ref[0:1, :]
    sc = mod_ref[1:2, :]
    return (_rms(x, g) * (1.0 + sc) + sh).astype(BF16)


def _decay_logs(z, wdec_ref, bdec_ref):
    z_hi, z_lo = z
    logits = _dot(z_hi, wdec_ref[...]) + _dot(z_lo, wdec_ref[...]) + bdec_ref[...]
    ls = jnp.minimum(logits, 0.0) - jnp.log(1.0 + jnp.exp(-jnp.abs(logits)))
    return ls * (1.0 / GATE_NORM)


N_LATE_W = 5


def _proj_kernel(x_ref, mod_ref, g_ref, perm_ref, wm_ref, wdin_ref, wdec_ref, bdec_ref, wg_ref,
                 *refs):
    late_f32 = refs[:N_LATE_W]
    (ah_ref, av_ref, q_ref, k_ref, v_ref, rs_ref, laf_ref, lab_ref,
     gt_ref) = refs[N_LATE_W:-N_LATE_W]
    late_bf16 = refs[-N_LATE_W:]
    for src, dst in zip(late_f32, late_bf16):
        dst[...] = src[...].astype(BF16)
    subs = [slice(s, s + PROJ_SUB) for s in range(0, x_ref.shape[0], PROJ_SUB)]
    g = g_ref[...]
    ub = [_prenorm_mod(x_ref[s, :], g, mod_ref) for s in subs]
    z = [_split_bf16(_dot(u, wdin_ref[...])) for u in ub]
    a_h = []
    for n, s in enumerate(subs):
        glu = _dot(ub[n], wm_ref[:, 0:COL_Q])
        a = (glu[:, :CONV_DIM] * _sigmoid(glu[:, CONV_DIM:])).astype(BF16)
        av_ref[s, :] = a[:, CONV_HALF:]
        a_h.append(a[:, :CONV_HALF])
    for n, s in enumerate(subs):
        q_ref[s, :] = (_dot(ub[n], wm_ref[:, COL_Q:COL_K]) * (GLA_DK ** -0.5)).astype(BF16)
    for n, s in enumerate(subs):
        k_ref[s, :] = _dot(ub[n], wm_ref[:, COL_K:COL_V]).astype(BF16)
    for n, s in enumerate(subs):
        v_ref[s, :] = _dot(ub[n], wm_ref[:, COL_V:COL_R]).astype(BF16)
    for n, s in enumerate(subs):
        la = _decay_logs(z[n], wdec_ref, bdec_ref)
        laf_ref[s, :] = la[:, :HK]
        lab_ref[s, :] = la[:, HK:]
    for n, s in enumerate(subs):
        r = _dot(ub[n], wm_ref[:, COL_R:COL_DEC])
        rs_ref[s, :] = (r * _sigmoid(r)).astype(BF16)
    per_group = PERM_TOKENS // PROJ_SUB
    for n in range(x_ref.shape[0] // PERM_TOKENS):
        a_rm = jnp.concatenate(a_h[per_group * n:per_group * (n + 1)], axis=0)
        a_cm = _dot(perm_ref[...], a_rm)
        ah_ref[:, PERM_ROWS * n:PERM_ROWS * (n + 1), :] = a_cm.reshape(
            GRID_W, PERM_ROWS, CONV_HALF)
    for n, s in enumerate(subs):
        gt_ref[s, :] = _sigmoid(_dot(ub[n], wg_ref[...])).astype(BF16)


def _ctx_proj_kernel(x_ref, mod_ref, g_ref, wm_ref, wdin_ref, wdec_ref, bdec_ref,
                     k_ref, v_ref, laf_ref, lab_ref):
    nb, t, d = x_ref.shape
    ub = _prenorm_mod(x_ref[...].reshape(nb * t, d), g_ref[...], mod_ref)
    z = _split_bf16(_dot(ub, wdin_ref[...]))
    k_ref[...] = _dot(ub, wm_ref[:, COL_K:COL_V]).astype(BF16).reshape(k_ref.shape)
    v_ref[...] = _dot(ub, wm_ref[:, COL_V:COL_R]).astype(BF16).reshape(v_ref.shape)
    la = _decay_logs(z, wdec_ref, bdec_ref)
    laf_ref[...] = la[:, :HK].reshape(laf_ref.shape)
    lab_ref[...] = la[:, HK:].reshape(lab_ref.shape)


def _const_spec(arr):
    nd = arr.ndim
    return pl.BlockSpec(arr.shape, lambda *_: (0,) * nd)


def _tok_spec(tm, width):
    return pl.BlockSpec((None, tm, width), lambda b, i: (b, i, 0))


def _colmajor_spec(tm):
    return pl.BlockSpec((None, GRID_W, tm // GRID_W, CONV_HALF), lambda b, i: (b, 0, i, 0))


def _single_spec(arr):
    nd = arr.ndim
    return pl.BlockSpec(arr.shape, lambda *_: (0,) * nd, pipeline_mode=pl.Buffered(1))


def _proj(x, mod, g, perm, wm, wdin, wdec, bdec, wg, late_weights, tm):
    bsz, t, d = x.shape
    widths = (CONV_HALF, HK, HK, HV, HV, HK, HK, 2 * D_MODEL)
    dtypes = (BF16, BF16, BF16, BF16, BF16, F32, F32, BF16)
    assert tm % PERM_TOKENS == 0 and len(late_weights) == N_LATE_W
    ah_shape = jax.ShapeDtypeStruct((bsz, GRID_W, t // GRID_W, CONV_HALF), F32)
    n_i = t // tm
    n_steps = bsz * n_i
    late_in, late_out, late_shape = [], [], []
    for w in late_weights:
        _, rows, cols = w.shape
        assert rows % (16 * n_steps) == 0
        blk = rows // n_steps
        late_in.append(pl.BlockSpec((None, blk, cols), lambda b, i: (0, b * n_i + i, 0)))
        late_out.append(pl.BlockSpec((blk, cols), lambda b, i: (b * n_i + i, 0)))
        late_shape.append(jax.ShapeDtypeStruct((rows, cols), BF16))
    return pl.pallas_call(
        _proj_kernel,
        grid=(bsz, n_i),
        in_specs=[
            _tok_spec(tm, d),
            pl.BlockSpec((None,) + mod.shape[1:], lambda b, i: (b, 0, 0)),
            _const_spec(g), _single_spec(perm), _single_spec(wm), _single_spec(wdin),
            _single_spec(wdec), _const_spec(bdec), _single_spec(wg),
        ] + late_in,
        out_specs=[_colmajor_spec(tm)] + [_tok_spec(tm, w) for w in widths] + late_out,
        out_shape=[ah_shape] + [jax.ShapeDtypeStruct((bsz, t, w), dt)
                                for w, dt in zip(widths, dtypes)] + late_shape,
        compiler_params=pltpu.CompilerParams(
            dimension_semantics=("parallel", "parallel"), vmem_limit_bytes=VMEM_LIMIT),
        name="proj",
    )(x, mod, g, perm, wm, wdin, wdec, bdec, wg, *late_weights)


def _ctx_proj(x, mod, g, wm, wdin, wdec, bdec, nb):
    bsz, t, d = x.shape
    widths = (HK, HV, HK, HK)
    dtypes = (BF16, BF16, F32, F32)
    assert bsz % nb == 0
    return pl.pallas_call(
        _ctx_proj_kernel,
        grid=(bsz // nb,),
        in_specs=[
            pl.BlockSpec((nb, t, d), lambda i: (i, 0, 0)),
            pl.BlockSpec((None,) + mod.shape[1:], lambda i: (0, 0, 0)),
            _const_spec(g), _single_spec(wm), _single_spec(wdin), _single_spec(wdec),
            _const_spec(bdec),
        ],
        out_specs=[pl.BlockSpec((nb, t, w), lambda i: (i, 0, 0)) for w in widths],
        out_shape=[jax.ShapeDtypeStruct((bsz, t, w), dt) for w, dt in zip(widths, dtypes)],
        compiler_params=pltpu.CompilerParams(
            dimension_semantics=("parallel",), vmem_limit_bytes=VMEM_LIMIT),
        name="ctx_proj",
    )(x, mod, g, wm, wdin, wdec, bdec)


def _gla_scale(qs, ks, las, dirn, want_out):
    tri_b, mid, last = dirn["tri_b"], dirn["mid"], dirn["last"]
    la_hi, la_lo = _split_bf16(las)
    g = _dot(tri_b, la_hi) + _dot(tri_b, la_lo)
    u = dict(kt=[], kend=[], qt=[], qg=[], e=[])
    for c in range(SUPER // CHUNK):
        rows = slice(c * CHUNK, (c + 1) * CHUNK)
        gc = g[rows]
        g_mid = gc[mid:mid + 1]
        g_last = gc[last:last + 1]
        u["e"].append(jnp.exp(g_last))
        kt = ks[rows].astype(F32) * jnp.exp(g_mid - gc)
        u["kt"].append(kt)
        u["kend"].append(kt * jnp.exp(g_last - g_mid))
        if want_out:
            qt = qs[rows].astype(F32) * jnp.exp(gc - g_mid)
            u["qt"].append(qt.astype(BF16))
            u["qg"].append((qt * jnp.exp(g_mid)).astype(BF16))
    return u


def _pair_lanes(pair):
    return slice(LANES * pair, LANES * (pair + 1))


def _gla_scores(u, pair, head_lane):
    lanes = _pair_lanes(pair)
    kt = jnp.concatenate([c[:, lanes] for c in u["kt"]], axis=0)
    qt = jnp.concatenate([c[:, lanes] for c in u["qt"]], axis=0)
    kt2 = jnp.concatenate([(kt * hm).astype(BF16) for hm in head_lane], axis=0)
    return lax.dot_general(qt, kt2, NT, preferred_element_type=F32)


def _gla_update(u, vs, h, head_lane):
    lanes = _pair_lanes(h // 2)
    hm = head_lane[h % 2]
    k0 = (u["kend"][0][:, lanes] * hm).astype(BF16)
    k1 = (u["kend"][1][:, lanes] * hm).astype(BF16)
    z = jnp.zeros_like(k0)
    kbd = jnp.concatenate([jnp.concatenate([k0, z], axis=1),
                           jnp.concatenate([z, k1], axis=1)], axis=0)
    return lax.dot_general(vs[:, GLA_DV * h:GLA_DV * (h + 1)], kbd, TN,
                           preferred_element_type=F32)


def _gla_intra(scores, vs, pair, mask2):
    a = jnp.where(mask2, scores, 0.0).astype(BF16)
    v0 = vs[:, GLA_DV * 2 * pair:GLA_DV * (2 * pair + 1)]
    v1 = vs[:, GLA_DV * (2 * pair + 1):GLA_DV * (2 * pair + 2)]
    z = jnp.zeros_like(v0)
    vbd = jnp.concatenate([jnp.concatenate([v0, z], axis=1),
                           jnp.concatenate([z, v1], axis=1)], axis=0)
    return _dot(a, vbd)


def _gla_kernel(q_ref, k_ref, v_ref, laf_ref, lab_ref, kc_ref, vc_ref, lafc_ref, labc_ref,
                rs_ref, g_ref, o_ref, of_ref, ob_ref, sf_ref, sb_ref):
    t = q_ref.shape[0]
    t_ctx = kc_ref.shape[0]
    half = t // 2
    n_steps = t // STEP
    supers = STEP // SUPER
    n_chunks = SUPER // CHUNK
    pairs = GLA_HEADS // 2

    row = lax.broadcasted_iota(jnp.int32, (SUPER, SUPER), 0)
    col = lax.broadcasted_iota(jnp.int32, (SUPER, SUPER), 1)
    same_chunk = (row // CHUNK) == (col // CHUNK)
    low_mask = jnp.logical_and(same_chunk, col <= row)
    up_mask = jnp.logical_and(same_chunk, col >= row)
    lane = lax.broadcasted_iota(jnp.int32, (1, LANES), 1)
    head_lane = (jnp.where(lane < GLA_DK, 1.0, 0.0), jnp.where(lane >= GLA_DK, 1.0, 0.0))

    fwd = dict(tri_b=jnp.where(low_mask, 1.0, 0.0).astype(BF16),
               mask2=jnp.concatenate([low_mask, low_mask], axis=1),
               mid=CHUNK // 2 - 1, last=CHUNK - 1, order=(0, 1), st_ref=sf_ref)
    bwd = dict(tri_b=jnp.where(up_mask, 1.0, 0.0).astype(BF16),
               mask2=jnp.concatenate([up_mask, up_mask], axis=1),
               mid=CHUNK // 2, last=0, order=(1, 0), st_ref=sb_ref)

    sf_ref[...] = jnp.zeros_like(sf_ref)
    sb_ref[...] = jnp.zeros_like(sb_ref)
    gn = g_ref[...]

    def run(units, want_out, finalize):
        vals = []
        for dirn, refs, t0 in units:
            q_r, k_r, v_r, la_r = refs
            rows = pl.ds(t0, SUPER)
            qs = q_r[rows, :] if want_out else None
            vals.append((_gla_scale(qs, k_r[rows, :], la_r[rows, :], dirn, want_out),
                         v_r[rows, :]))
        scores = []
        if want_out:
            scores = [[_gla_scores(u, p, head_lane) for p in range(pairs)] for u, _ in vals]
        upd = [[_gla_update(u, vs, h, head_lane) for h in range(GLA_HEADS)] for u, vs in vals]
        intra = []
        if want_out:
            intra = [[_gla_intra(scores[n][p], vals[n][1], p, units[n][0]["mask2"])
                      for p in range(pairs)] for n in range(len(units))]
        states = {}
        for n, (dirn, _, t0) in enumerate(units):
            key = id(dirn["st_ref"])
            if key not in states:
                states[key] = (dirn["st_ref"], [dirn["st_ref"][h] for h in range(GLA_HEADS)])
            st = states[key][1]
            u = vals[n][0]
            inter = [[None] * n_chunks for _ in range(pairs)]
            for c in dirn["order"]:
                if want_out:
                    for p in range(pairs):
                        st2 = jnp.concatenate([st[2 * p].astype(BF16),
                                               st[2 * p + 1].astype(BF16)], axis=0)
                        inter[p][c] = lax.dot_general(u["qg"][c][:, _pair_lanes(p)], st2, NT,
                                                      preferred_element_type=F32)
                for h in range(GLA_HEADS):
                    e = u["e"][c][:, _pair_lanes(h // 2)]
                    st[h] = st[h] * e + upd[n][h][:, LANES * c:LANES * (c + 1)]
            if not want_out:
                continue
            fwd_unit = dirn is fwd
            for p in range(pairs):
                o2 = intra[n][p] + jnp.concatenate(inter[p], axis=0)
                for j in range(2):
                    cols = slice(GLA_DV * (2 * p + j), GLA_DV * (2 * p + j + 1))
                    o = o2[:, GLA_DV * j:GLA_DV * (j + 1)]
                    if finalize:
                        other = (ob_ref[pl.ds(t0 - half, SUPER), cols] if fwd_unit
                                 else of_ref[pl.ds(t0, SUPER), cols])
                        rs = rs_ref[pl.ds(t0, SUPER), cols].astype(F32)
                        o_ref[pl.ds(t0, SUPER), cols] = (_rms(o + other, gn) * rs).astype(BF16)
                    elif fwd_unit:
                        of_ref[pl.ds(t0, SUPER), cols] = o
                    else:
                        ob_ref[pl.ds(t0 - half, SUPER), cols] = o
        for st_ref, st in states.values():
            for h in range(GLA_HEADS):
                st_ref[h] = st[h]

    ctx_units = []
    for s in range(t_ctx // SUPER):
        ctx_units.append((fwd, (None, kc_ref, vc_ref, lafc_ref), s * SUPER))
        ctx_units.append((bwd, (None, kc_ref, vc_ref, labc_ref), t_ctx - (s + 1) * SUPER))
    run(ctx_units, want_out=False, finalize=False)

    def step(i, finalize):
        units = []
        for s in range(supers):
            t0 = pl.multiple_of(i * STEP + s * SUPER, SUPER)
            t1 = pl.multiple_of(t - (i + 1) * STEP + (supers - 1 - s) * SUPER, SUPER)
            units.append((fwd, (q_ref, k_ref, v_ref, laf_ref), t0))
            units.append((bwd, (q_ref, k_ref, v_ref, lab_ref), t1))
        run(units, want_out=True, finalize=finalize)

    def park_body(i, carry):
        step(i, False)
        return carry

    def finish_body(i, carry):
        step(i, True)
        return carry

    lax.fori_loop(0, n_steps // 2, park_body, 0)
    lax.fori_loop(n_steps // 2, n_steps, finish_body, 0)


def _gla(q, k, v, laf, lab, kc, vc, lafc, labc, rs, gnorm):
    bsz, t, _ = q.shape
    assert t % (2 * STEP) == 0 and kc.shape[1] % SUPER == 0

    def bspec(arr):
        return pl.BlockSpec((None,) + arr.shape[1:], lambda b: (b, 0, 0))

    args = (q, k, v, laf, lab, kc, vc, lafc, labc, rs)
    return pl.pallas_call(
        _gla_kernel,
        grid=(bsz,),
        in_specs=[bspec(a) for a in args] + [_const_spec(gnorm)],
        out_specs=pl.BlockSpec((None, t, HV), lambda b: (b, 0, 0)),
        out_shape=jax.ShapeDtypeStruct((bsz, t, HV), BF16),
        scratch_shapes=[
            pltpu.VMEM((t // 2, HV), F32), pltpu.VMEM((t // 2, HV), F32),
            pltpu.VMEM((GLA_HEADS, GLA_DV, LANES), F32),
            pltpu.VMEM((GLA_HEADS, GLA_DV, LANES), F32),
        ],
        compiler_params=pltpu.CompilerParams(
            dimension_semantics=("parallel",), vmem_limit_bytes=VMEM_LIMIT),
        name="gla",
    )(*args, gnorm)


def _tap_segments(n_tiles, lines_per_tile, n_lines, group):
    segs = []
    for t0 in range(0, n_tiles, group):
        t1 = min(t0 + group, n_tiles)
        lo_line = int(t0 * lines_per_tile)
        hi_line = int(-(-t1 * lines_per_tile // 1)) - 1
        k_lo = max(0, CONV_PAD - hi_line)
        k_hi = min(CONV_K, n_lines + CONV_PAD - lo_line)
        if segs and segs[-1][2:] == (k_lo, k_hi):
            segs[-1] = (segs[-1][0], t1, k_lo, k_hi)
        else:
            segs.append((t0, t1, k_lo, k_hi))
    return segs


def _conv_kernel(ah_ref, av_ref, w_ref, b_ref, yh_ref, yv_ref, ph_ref, pv_ref):
    t = ah_ref.shape[0]
    rows = t // GRID_W
    tile = 64

    def conv(src_ref, pad_ref, w_cols, line, n_lines, group, dst_ref):
        pad = CONV_PAD * line
        pad_ref[pl.ds(0, pad), :] = jnp.zeros((pad, CONV_HALF), F32)
        pad_ref[pl.ds(pad + t, pad), :] = jnp.zeros((pad, CONV_HALF), F32)

        def fill(i, carry):
            t0 = pl.multiple_of(i * tile, tile)
            pad_ref[pl.ds(pl.multiple_of(pad + t0, 8), tile), :] = (
                src_ref[pl.ds(t0, tile), :].astype(F32))
            return carry

        lax.fori_loop(0, t // tile, fill, 0)
        bias = b_ref[:, w_cols]

        for first, last, k_lo, k_hi in _tap_segments(t // tile, tile / line, n_lines, group):
            def body(i, carry, k_lo=k_lo, k_hi=k_hi):
                t0 = pl.multiple_of(i * tile, tile)
                acc = jnp.broadcast_to(bias, (tile, CONV_HALF))
                for kk in range(k_lo, k_hi):
                    win = pad_ref[pl.ds(pl.multiple_of(t0 + kk * line, 8), tile), :]
                    acc = acc + w_ref[kk:kk + 1, w_cols] * win
                dst_ref[pl.ds(t0, tile), :] = acc
                return carry

            lax.fori_loop(first, last, body, 0)

    conv(ah_ref, ph_ref, slice(0, CONV_HALF), rows, GRID_W, 2, yh_ref)
    conv(av_ref, pv_ref, slice(CONV_HALF, CONV_DIM), GRID_W, rows, 4, yv_ref)


def _conv(ah, av, w, b):
    bsz, t, c = ah.shape
    rows = t // GRID_W
    tok = pl.BlockSpec((None, t, c), lambda i: (i, 0, 0))
    return pl.pallas_call(
        _conv_kernel,
        grid=(bsz,),
        in_specs=[tok, tok, _const_spec(w), _const_spec(b)],
        out_specs=[tok, tok],
        out_shape=[jax.ShapeDtypeStruct((bsz, t, c), F32)] * 2,
        scratch_shapes=[
            pltpu.VMEM((t + 2 * CONV_PAD * rows, c), F32),
            pltpu.VMEM((t + 2 * CONV_PAD * GRID_W, c), F32),
        ],
        compiler_params=pltpu.CompilerParams(
            dimension_semantics=("parallel",), vmem_limit_bytes=VMEM_LIMIT),
        name="conv",
    )(ah, av, w, b)


def _post_kernel(x_ref, yh_ref, yv_ref, og_ref, gt_ref, mod_ref, perm_ref, lng_ref, lnb_ref,
                 gpost1_ref, gpre2_ref, gpost2_ref, wco_ref, wgo_ref, wout_ref, wff1_ref,
                 wff2_ref, o_ref):
    d = x_ref.shape[-1]
    d_ff = wff1_ref.shape[1]
    ff_tile = 1024
    sub = 256
    subs = [slice(s, s + sub) for s in range(0, x_ref.shape[0], sub)]
    gt1 = mod_ref[2:3, :]
    sh2 = mod_ref[3:4, :]
    sc2 = mod_ref[4:5, :]
    gt2 = mod_ref[5:6, :]

    yh = []
    for n in range(x_ref.shape[0] // PERM_TOKENS):
        blk = yh_ref[:, PERM_ROWS * n:PERM_ROWS * (n + 1), :].reshape(PERM_TOKENS, CONV_HALF)
        hi, lo = _split_bf16(blk)
        yh.append(_dot(perm_ref[...], hi) + _dot(perm_ref[...], lo))
    yh = jnp.concatenate(yh, axis=0)

    yn = []
    for s in subs:
        yc = jnp.concatenate([yh[s], yv_ref[s, :]], axis=-1)
        yc = yc - jnp.mean(yc, axis=-1, keepdims=True)
        var = jnp.mean(yc * yc, axis=-1, keepdims=True)
        z = yc * lax.rsqrt(var + EPS) * lng_ref[...] + lnb_ref[...]
        yn.append((z * _sigmoid(z)).astype(BF16))
    y_gla = [_dot(og_ref[s, :], wgo_ref[...]) for s in subs]
    y_conv = [_dot(v, wco_ref[...]) for v in yn]
    merged = []
    for n, s in enumerate(subs):
        gates = gt_ref[s, :].astype(F32)
        merged.append((gates[:, :d] * y_conv[n] + gates[:, d:] * y_gla[n]).astype(BF16))
    y = [_dot(v, wout_ref[...]) for v in merged]
    h_mid = [x_ref[s, :] + gt1 * _rms(y[n], gpost1_ref[...]) for n, s in enumerate(subs)]
    u2 = [(_rms(h, gpre2_ref[...]) * (1.0 + sc2) + sh2).astype(BF16) for h in h_mid]
    acc = [jnp.zeros(h.shape, F32) for h in h_mid]
    for j in range(d_ff // ff_tile):
        cols = slice(j * ff_tile, (j + 1) * ff_tile)
        f = [_dot(v, wff1_ref[:, cols]) for v in u2]
        f = [jnp.square(jnp.maximum(v, 0.0)).astype(BF16) for v in f]
        acc = [acc[n] + _dot(f[n], wff2_ref[cols, :]) for n in range(len(subs))]
    for n, s in enumerate(subs):
        o_ref[s, :] = h_mid[n] + gt2 * _rms(acc[n], gpost2_ref[...])


def _post(x, yh, yv, og, gates, mod, perm, ln_g, ln_b, gpost1, gpre2, gpost2, wco, wgo, wout,
          wff1, wff2, tm):
    bsz, t, d = x.shape
    assert tm % PERM_TOKENS == 0
    return pl.pallas_call(
        _post_kernel,
        grid=(bsz, t // tm),
        in_specs=[
            _tok_spec(tm, d), _colmajor_spec(tm), _tok_spec(tm, CONV_HALF), _tok_spec(tm, HV),
            _tok_spec(tm, 2 * d),
            pl.BlockSpec((None,) + mod.shape[1:], lambda b, i: (b, 0, 0)),
            _single_spec(perm), _const_spec(ln_g), _const_spec(ln_b),
            _const_spec(gpost1), _const_spec(gpre2), _const_spec(gpost2),
            _single_spec(wco), _single_spec(wgo), _single_spec(wout), _single_spec(wff1),
            _single_spec(wff2),
        ],
        out_specs=_tok_spec(tm, d),
        out_shape=jax.ShapeDtypeStruct((bsz, t, d), F32),
        compiler_params=pltpu.CompilerParams(
            dimension_semantics=("parallel", "parallel"), vmem_limit_bytes=VMEM_LIMIT),
        name="post",
    )(x, yh, yv, og, gates, mod, perm, ln_g, ln_b, gpost1, gpre2, gpost2, wco, wgo, wout, wff1,
      wff2)


def kernel(x, c, ctx, c_ctx, w_mod, b_mod, g_pre1, g_post1, g_pre2, g_post2, w_in, conv_w,
           conv_b, conv_ln_g, conv_ln_b, w_conv_out, w_decay, b_decay, gla_norm_g, w_gla_out,
           w_out, w_ff1, w_ff2):
    bsz, t, d = x.shape
    depth = w_in.shape[0]
    assert depth == 1 and d == D_MODEL and t % (GRID_W * 8) == 0

    n_rows = -(-(bsz + 1) // 8) * 8
    c_all = jnp.zeros((n_rows, d), F32).at[:bsz].set(c).at[bsz].set(c_ctx)
    m = _adaln(c_all, w_mod, b_mod)
    mod = m[:bsz].reshape(bsz, N_MOD, d)
    mod_ctx = m[bsz, :2 * d].reshape(1, 2, d)

    wm, wdin, wg = _split_w_in(jnp.swapaxes(w_in[0], 0, 1))
    wdec = jnp.zeros((LANES, 2 * HK), F32)
    wdec = wdec.at[:DECAY_RANK, :HK].set(w_decay[0, 0])
    wdec = wdec.at[DECAY_RANK:2 * DECAY_RANK, HK:].set(w_decay[0, 1]).astype(BF16)
    bdec = b_decay[0].reshape(1, 2 * HK)
    row = lambda v: v.reshape(1, -1)

    perm = _to_colmajor_perm()
    to_colmajor = jnp.asarray(perm, BF16)
    to_rowmajor = jnp.asarray(perm.T, BF16)

    ah, av, q, k, v, rs, laf, lab, gates, wco, wgo, wout, wff1, wff2 = _proj(
        x, mod, row(g_pre1[0]), to_colmajor, wm, wdin, wdec, bdec, wg,
        (w_conv_out, w_gla_out, w_out, w_ff1, w_ff2), tm=1024)
    kc, vc, lafc, labc = _ctx_proj(
        ctx, mod_ctx, row(g_pre1[0]), wm, wdin, wdec, bdec, nb=4)

    og = _gla(q, k, v, laf, lab, kc, vc, lafc, labc, rs, row(gla_norm_g[0]))
    yh, yv = _conv(ah.reshape(bsz, t, CONV_HALF), av, conv_w[0], row(conv_b[0]))
    yh = yh.reshape(bsz, GRID_W, t // GRID_W, CONV_HALF)

    return _post(x, yh, yv, og, gates, mod, to_rowmajor, row(conv_ln_g[0]), row(conv_ln_b[0]),
                 row(g_post1[0]), row(g_pre2[0]), row(g_post2[0]),
                 wco, wgo, wout, wff1, wff2, tm=512)
```

```python
import functools

import jax
import jax.numpy as jnp
import numpy as np
from jax import lax
from jax.experimental import pallas as pl
from jax.experimental.pallas import tpu as pltpu

D_MODEL = 1024
GRID_W = 64
CONV_DIM = 512
CONV_K = 31
GLA_HEADS = 4
GLA_DK = 64
GLA_DV = 128
DECAY_RANK = 16
GATE_NORM = 16.0
CHUNK = 64
N_MOD = 6
EPS = 1e-6

COL_Q = 2 * CONV_DIM
COL_K = COL_Q + GLA_HEADS * GLA_DK
COL_V = COL_K + GLA_HEADS * GLA_DK
COL_R = COL_V + GLA_HEADS * GLA_DV
COL_DEC = COL_R + GLA_HEADS * GLA_DV
COL_GATE = COL_DEC + 2 * DECAY_RANK
COL_END = COL_GATE + 2 * D_MODEL

LANES = 128
HK = GLA_HEADS * GLA_DK
HV = GLA_HEADS * GLA_DV
CONV_HALF = CONV_DIM // 2
CONV_PAD = CONV_K // 2
PERM_ROWS = 8
PERM_TOKENS = PERM_ROWS * GRID_W
PROJ_SUB = 256
SUPER = 2 * CHUNK
STEP = 4 * SUPER

VMEM_LIMIT = 56 * 1024 * 1024

F32 = jnp.float32
BF16 = jnp.bfloat16

NT = (((1,), (1,)), ((), ()))
TN = (((0,), (0,)), ((), ()))


def _dot(a, b):
    return jnp.dot(a, b, preferred_element_type=F32)


def _rms(x, g):
    ms = jnp.mean(x * x, axis=-1, keepdims=True)
    return x * lax.rsqrt(ms + EPS) * g


def _sigmoid(x):
    return jax.nn.sigmoid(x)


def _split_bf16(x):
    hi = x.astype(BF16)
    lo = (x - hi.astype(F32)).astype(BF16)
    return hi, lo


def _to_colmajor_perm():
    out = np.arange(PERM_TOKENS)
    src = (out % PERM_ROWS) * GRID_W + out // PERM_ROWS
    p = np.zeros((PERM_TOKENS, PERM_TOKENS), np.float32)
    p[out, src] = 1.0
    return p


def _adaln_kernel(c_ref, w_ref, b_ref, o_ref):
    @pl.when(pl.program_id(0) == 0)
    def _():
        o_ref[...] = jnp.broadcast_to(b_ref[...], o_ref.shape)

    c = c_ref[...]
    s_hi, s_lo = _split_bf16(c * _sigmoid(c))
    w_hi, w_lo = _split_bf16(w_ref[...])
    o_ref[...] += _dot(s_hi, w_hi) + _dot(s_lo, w_hi) + _dot(s_hi, w_lo)


def _adaln(c_all, w_mod, b_mod):
    n_rows, d = c_all.shape
    n_out = w_mod.shape[-1]
    tk = 256
    return pl.pallas_call(
        _adaln_kernel,
        grid=(d // tk,),
        in_specs=[
            pl.BlockSpec((n_rows, tk), lambda k: (0, k)),
            pl.BlockSpec((None, tk, n_out), lambda k: (0, k, 0)),
            pl.BlockSpec((1, n_out), lambda k: (0, 0)),
        ],
        out_specs=pl.BlockSpec((n_rows, n_out), lambda k: (0, 0)),
        out_shape=jax.ShapeDtypeStruct((n_rows, n_out), F32),
        compiler_params=pltpu.CompilerParams(
            dimension_semantics=("arbitrary",), vmem_limit_bytes=VMEM_LIMIT),
        name="adaln",
    )(c_all, w_mod, b_mod)


def _split_w_in_kernel(wm_t_ref, wd_t_ref, wg_t_ref, wm_ref, wdin_ref, wg_ref, *, n_gate):
    j = pl.program_id(0)
    wm_ref[...] = wm_t_ref[...].astype(BF16).T

    @pl.when(j == 0)
    def _():
        wdin_ref[...] = wd_t_ref[...].astype(BF16).T

    @pl.when(j < n_gate)
    def _():
        wg_ref[...] = wg_t_ref[...].astype(BF16).T


def _split_w_in(w_in_t):
    n, d = w_in_t.shape
    tr = 512
    n_main = COL_DEC // tr
    n_gate = (n - COL_GATE) // tr
    assert COL_DEC % tr == 0 and (n - COL_GATE) % tr == 0 and n_gate <= n_main
    gate_blk = lambda j: jnp.minimum(j, n_gate - 1)
    return pl.pallas_call(
        functools.partial(_split_w_in_kernel, n_gate=n_gate),
        grid=(n_main,),
        in_specs=[
            pl.BlockSpec((tr, d), lambda j: (j, 0)),
            pl.BlockSpec((LANES, d), lambda j: (COL_DEC // LANES, 0)),
            pl.BlockSpec((pl.Element(tr), pl.Element(d)),
                         lambda j: (pl.multiple_of(COL_GATE + tr * gate_blk(j), 32), 0)),
        ],
        out_specs=[
            pl.BlockSpec((d, tr), lambda j: (0, j)),
            pl.BlockSpec((d, LANES), lambda j: (0, 0)),
            pl.BlockSpec((d, tr), lambda j: (0, gate_blk(j))),
        ],
        out_shape=[jax.ShapeDtypeStruct((d, w), BF16) for w in (COL_DEC, LANES, n - COL_GATE)],
        compiler_params=pltpu.CompilerParams(
            dimension_semantics=("arbitrary",), vmem_limit_bytes=VMEM_LIMIT),
        name="split_w_in",
    )(w_in_t, w_in_t, w_in_t)


def _prenorm_mod(x, g, mod_ref):
    sh = mod_ref[0:1, :]
    sc = mod_ref[1:2, :]
    return (_rms(x, g) * (1.0 + sc) + sh).astype(BF16)


def _decay_logs(z, wdec_ref, bdec_ref):
    z_hi, z_lo = z
    logits = _dot(z_hi, wdec_ref[...]) + _dot(z_lo, wdec_ref[...]) + bdec_ref[...]
    ls = jnp.minimum(logits, 0.0) - jnp.log(1.0 + jnp.exp(-jnp.abs(logits)))
    return ls * (1.0 / GATE_NORM)


N_LATE_W = 5


def _proj_kernel(x_ref, mod_ref, g_ref, perm_ref, wm_ref, wdin_ref, wdec_ref, bdec_ref, wg_ref,
                 *refs):
    late_f32 = refs[:N_LATE_W]
    (ah_ref, av_ref, q_ref, k_ref, v_ref, rs_ref, laf_ref, lab_ref,
     gt_ref) = refs[N_LATE_W:-N_LATE_W]
    late_bf16 = refs[-N_LATE_W:]
    for src, dst in zip(late_f32, late_bf16):
        dst[...] = src[...].astype(BF16)
    subs = [slice(s, s + PROJ_SUB) for s in range(0, x_ref.shape[0], PROJ_SUB)]
    g = g_ref[...]
    ub = [_prenorm_mod(x_ref[s, :], g, mod_ref) for s in subs]
    z = [_split_bf16(_dot(u, wdin_ref[...])) for u in ub]
    a_h = []
    for n, s in enumerate(subs):
        glu = _dot(ub[n], wm_ref[:, 0:COL_Q])
        a = (glu[:, :CONV_DIM] * _sigmoid(glu[:, CONV_DIM:])).astype(BF16)
        av_ref[s, :] = a[:, CONV_HALF:]
        a_h.append(a[:, :CONV_HALF])
    for n, s in enumerate(subs):
        q_ref[s, :] = (_dot(ub[n], wm_ref[:, COL_Q:COL_K]) * (GLA_DK ** -0.5)).astype(BF16)
    for n, s in enumerate(subs):
        k_ref[s, :] = _dot(ub[n], wm_ref[:, COL_K:COL_V]).astype(BF16)
    for n, s in enumerate(subs):
        v_ref[s, :] = _dot(ub[n], wm_ref[:, COL_V:COL_R]).astype(BF16)
    for n, s in enumerate(subs):
        la = _decay_logs(z[n], wdec_ref, bdec_ref)
        laf_ref[s, :] = la[:, :HK]
        lab_ref[s, :] = la[:, HK:]
    for n, s in enumerate(subs):
        r = _dot(ub[n], wm_ref[:, COL_R:COL_DEC])
        rs_ref[s, :] = (r * _sigmoid(r)).astype(BF16)
    per_group = PERM_TOKENS // PROJ_SUB
    for n in range(x_ref.shape[0] // PERM_TOKENS):
        a_rm = jnp.concatenate(a_h[per_group * n:per_group * (n + 1)], axis=0)
        a_cm = _dot(perm_ref[...], a_rm)
        ah_ref[:, PERM_ROWS * n:PERM_ROWS * (n + 1), :] = a_cm.reshape(
            GRID_W, PERM_ROWS, CONV_HALF)
    for n, s in enumerate(subs):
        gt_ref[s, :] = _sigmoid(_dot(ub[n], wg_ref[...])).astype(BF16)


def _ctx_proj_kernel(x_ref, mod_ref, g_ref, wm_ref, wdin_ref, wdec_ref, bdec_ref,
                     k_ref, v_ref, laf_ref, lab_ref):
    nb, t, d = x_ref.shape
    ub = _prenorm_mod(x_ref[...].reshape(nb * t, d), g_ref[...], mod_ref)
    z = _split_bf16(_dot(ub, wdin_ref[...]))
    k_ref[...] = _dot(ub, wm_ref[:, COL_K:COL_V]).astype(BF16).reshape(k_ref.shape)
    v_ref[...] = _dot(ub, wm_ref[:, COL_V:COL_R]).astype(BF16).reshape(v_ref.shape)
    la = _decay_logs(z, wdec_ref, bdec_ref)
    laf_ref[...] = la[:, :HK].reshape(laf_ref.shape)
    lab_ref[...] = la[:, HK:].reshape(lab_ref.shape)


def _const_spec(arr):
    nd = arr.ndim
    return pl.BlockSpec(arr.shape, lambda *_: (0,) * nd)


def _tok_spec(tm, width):
    return pl.BlockSpec((None, tm, width), lambda b, i: (b, i, 0))


def _colmajor_spec(tm):
    return pl.BlockSpec((None, GRID_W, tm // GRID_W, CONV_HALF), lambda b, i: (b, 0, i, 0))


def _single_spec(arr):
    nd = arr.ndim
    return pl.BlockSpec(arr.shape, lambda *_: (0,) * nd, pipeline_mode=pl.Buffered(1))


def _proj(x, mod, g, perm, wm, wdin, wdec, bdec, wg, late_weights, tm):
    bsz, t, d = x.shape
    widths = (CONV_HALF, HK, HK, HV, HV, HK, HK, 2 * D_MODEL)
    dtypes = (BF16, BF16, BF16, BF16, BF16, F32, F32, BF16)
    assert tm % PERM_TOKENS == 0 and len(late_weights) == N_LATE_W
    ah_shape = jax.ShapeDtypeStruct((bsz, GRID_W, t // GRID_W, CONV_HALF), F32)
    n_i = t // tm
    n_steps = bsz * n_i
    late_in, late_out, late_shape = [], [], []
    for w in late_weights:
        _, rows, cols = w.shape
        assert rows % (16 * n_steps) == 0
        blk = rows // n_steps
        late_in.append(pl.BlockSpec((None, blk, cols), lambda b, i: (0, b * n_i + i, 0)))
        late_out.append(pl.BlockSpec((blk, cols), lambda b, i: (b * n_i + i, 0)))
        late_shape.append(jax.ShapeDtypeStruct((rows, cols), BF16))
    return pl.pallas_call(
        _proj_kernel,
        grid=(bsz, n_i),
        in_specs=[
            _tok_spec(tm, d),
            pl.BlockSpec((None,) + mod.shape[1:], lambda b, i: (b, 0, 0)),
            _const_spec(g), _single_spec(perm), _single_spec(wm), _single_spec(wdin),
            _single_spec(wdec), _const_spec(bdec), _single_spec(wg),
        ] + late_in,
        out_specs=[_colmajor_spec(tm)] + [_tok_spec(tm, w) for w in widths] + late_out,
        out_shape=[ah_shape] + [jax.ShapeDtypeStruct((bsz, t, w), dt)
                                for w, dt in zip(widths, dtypes)] + late_shape,
        compiler_params=pltpu.CompilerParams(
            dimension_semantics=("parallel", "parallel"), vmem_limit_bytes=VMEM_LIMIT),
        name="proj",
    )(x, mod, g, perm, wm, wdin, wdec, bdec, wg, *late_weights)


def _ctx_proj(x, mod, g, wm, wdin, wdec, bdec, nb):
    bsz, t, d = x.shape
    widths = (HK, HV, HK, HK)
    dtypes = (BF16, BF16, F32, F32)
    assert bsz % nb == 0
    return pl.pallas_call(
        _ctx_proj_kernel,
        grid=(bsz // nb,),
        in_specs=[
            pl.BlockSpec((nb, t, d), lambda i: (i, 0, 0)),
            pl.BlockSpec((None,) + mod.shape[1:], lambda i: (0, 0, 0)),
            _const_spec(g), _single_spec(wm), _single_spec(wdin), _single_spec(wdec),
            _const_spec(bdec),
        ],
        out_specs=[pl.BlockSpec((nb, t, w), lambda i: (i, 0, 0)) for w in widths],
        out_shape=[jax.ShapeDtypeStruct((bsz, t, w), dt) for w, dt in zip(widths, dtypes)],
        compiler_params=pltpu.CompilerParams(
            dimension_semantics=("parallel",), vmem_limit_bytes=VMEM_LIMIT),
        name="ctx_proj",
    )(x, mod, g, wm, wdin, wdec, bdec)


def _gla_scale(qs, ks, las, dirn, want_out):
    tri_b, mid, last = dirn["tri_b"], dirn["mid"], dirn["last"]
    la_hi, la_lo = _split_bf16(las)
    g = _dot(tri_b, la_hi) + _dot(tri_b, la_lo)
    u = dict(kt=[], kend=[], qt=[], qg=[], e=[])
    for c in range(SUPER // CHUNK):
        rows = slice(c * CHUNK, (c + 1) * CHUNK)
        gc = g[rows]
        g_mid = gc[mid:mid + 1]
        g_last = gc[last:last + 1]
        u["e"].append(jnp.exp(g_last))
        kt = ks[rows].astype(F32) * jnp.exp(g_mid - gc)
        u["kt"].append(kt)
        u["kend"].append(kt * jnp.exp(g_last - g_mid))
        if want_out:
            qt = qs[rows].astype(F32) * jnp.exp(gc - g_mid)
            u["qt"].append(qt.astype(BF16))
            u["qg"].append((qt * jnp.exp(g_mid)).astype(BF16))
    return u


def _pair_lanes(pair):
    return slice(LANES * pair, LANES * (pair + 1))


def _gla_scores(u, pair, head_lane):
    lanes = _pair_lanes(pair)
    kt = jnp.concatenate([c[:, lanes] for c in u["kt"]], axis=0)
    qt = jnp.concatenate([c[:, lanes] for c in u["qt"]], axis=0)
    kt2 = jnp.concatenate([(kt * hm).astype(BF16) for hm in head_lane], axis=0)
    return lax.dot_general(qt, kt2, NT, preferred_element_type=F32)


def _gla_update(u, vs, h, head_lane):
    lanes = _pair_lanes(h // 2)
    hm = head_lane[h % 2]
    k0 = (u["kend"][0][:, lanes] * hm).astype(BF16)
    k1 = (u["kend"][1][:, lanes] * hm).astype(BF16)
    z = jnp.zeros_like(k0)
    kbd = jnp.concatenate([jnp.concatenate([k0, z], axis=1),
                           jnp.concatenate([z, k1], axis=1)], axis=0)
    return lax.dot_general(vs[:, GLA_DV * h:GLA_DV * (h + 1)], kbd, TN,
                           preferred_element_type=F32)


def _gla_intra(scores, vs, pair, mask2):
    a = jnp.where(mask2, scores, 0.0).astype(BF16)
    v0 = vs[:, GLA_DV * 2 * pair:GLA_DV * (2 * pair + 1)]
    v1 = vs[:, GLA_DV * (2 * pair + 1):GLA_DV * (2 * pair + 2)]
    z = jnp.zeros_like(v0)
    vbd = jnp.concatenate([jnp.concatenate([v0, z], axis=1),
                           jnp.concatenate([z, v1], axis=1)], axis=0)
    return _dot(a, vbd)


def _gla_kernel(q_ref, k_ref, v_ref, laf_ref, lab_ref, kc_ref, vc_ref, lafc_ref, labc_ref,
                rs_ref, g_ref, o_ref, of_ref, ob_ref, sf_ref, sb_ref):
    t = q_ref.shape[0]
    t_ctx = kc_ref.shape[0]
    half = t // 2
    n_steps = t // STEP
    supers = STEP // SUPER
    n_chunks = SUPER // CHUNK
    pairs = GLA_HEADS // 2

    row = lax.broadcasted_iota(jnp.int32, (SUPER, SUPER), 0)
    col = lax.broadcasted_iota(jnp.int32, (SUPER, SUPER), 1)
    same_chunk = (row // CHUNK) == (col // CHUNK)
    low_mask = jnp.logical_and(same_chunk, col <= row)
    up_mask = jnp.logical_and(same_chunk, col >= row)
    lane = lax.broadcasted_iota(jnp.int32, (1, LANES), 1)
    head_lane = (jnp.where(lane < GLA_DK, 1.0, 0.0), jnp.where(lane >= GLA_DK, 1.0, 0.0))

    fwd = dict(tri_b=jnp.where(low_mask, 1.0, 0.0).astype(BF16),
               mask2=jnp.concatenate([low_mask, low_mask], axis=1),
               mid=CHUNK // 2 - 1, last=CHUNK - 1, order=(0, 1), st_ref=sf_ref)
    bwd = dict(tri_b=jnp.where(up_mask, 1.0, 0.0).astype(BF16),
               mask2=jnp.concatenate([up_mask, up_mask], axis=1),
               mid=CHUNK // 2, last=0, order=(1, 0), st_ref=sb_ref)

    sf_ref[...] = jnp.zeros_like(sf_ref)
    sb_ref[...] = jnp.zeros_like(sb_ref)
    gn = g_ref[...]

    def run(units, want_out, finalize):
        vals = []
        for dirn, refs, t0 in units:
            q_r, k_r, v_r, la_r = refs
            rows = pl.ds(t0, SUPER)
            qs = q_r[rows, :] if want_out else None
            vals.append((_gla_scale(qs, k_r[rows, :], la_r[rows, :], dirn, want_out),
                         v_r[rows, :]))
        scores = []
        if want_out:
            scores = [[_gla_scores(u, p, head_lane) for p in range(pairs)] for u, _ in vals]
        upd = [[_gla_update(u, vs, h, head_lane) for h in range(GLA_HEADS)] for u, vs in vals]
        intra = []
        if want_out:
            intra = [[_gla_intra(scores[n][p], vals[n][1], p, units[n][0]["mask2"])
                      for p in range(pairs)] for n in range(len(units))]
        states = {}
        for n, (dirn, _, t0) in enumerate(units):
            key = id(dirn["st_ref"])
            if key not in states:
                states[key] = (dirn["st_ref"], [dirn["st_ref"][h] for h in range(GLA_HEADS)])
            st = states[key][1]
            u = vals[n][0]
            inter = [[None] * n_chunks for _ in range(pairs)]
            for c in dirn["order"]:
                if want_out:
                    for p in range(pairs):
                        st2 = jnp.concatenate([st[2 * p].astype(BF16),
                                               st[2 * p + 1].astype(BF16)], axis=0)
                        inter[p][c] = lax.dot_general(u["qg"][c][:, _pair_lanes(p)], st2, NT,
                                                      preferred_element_type=F32)
                for h in range(GLA_HEADS):
                    e = u["e"][c][:, _pair_lanes(h // 2)]
                    st[h] = st[h] * e + upd[n][h][:, LANES * c:LANES * (c + 1)]
            if not want_out:
                continue
            fwd_unit = dirn is fwd
            for p in range(pairs):
                o2 = intra[n][p] + jnp.concatenate(inter[p], axis=0)
                for j in range(2):
                    cols = slice(GLA_DV * (2 * p + j), GLA_DV * (2 * p + j + 1))
                    o = o2[:, GLA_DV * j:GLA_DV * (j + 1)]
                    if finalize:
                        other = (ob_ref[pl.ds(t0 - half, SUPER), cols] if fwd_unit
                                 else of_ref[pl.ds(t0, SUPER), cols])
                        rs = rs_ref[pl.ds(t0, SUPER), cols].astype(F32)
                        o_ref[pl.ds(t0, SUPER), cols] = (_rms(o + other, gn) * rs).astype(BF16)
                    elif fwd_unit:
                        of_ref[pl.ds(t0, SUPER), cols] = o
                    else:
                        ob_ref[pl.ds(t0 - half, SUPER), cols] = o
        for st_ref, st in states.values():
            for h in range(GLA_HEADS):
                st_ref[h] = st[h]

    ctx_units = []
    for s in range(t_ctx // SUPER):
        ctx_units.append((fwd, (None, kc_ref, vc_ref, lafc_ref), s * SUPER))
        ctx_units.append((bwd, (None, kc_ref, vc_ref, labc_ref), t_ctx - (s + 1) * SUPER))
    run(ctx_units, want_out=False, finalize=False)

    def step(i, finalize):
        units = []
        for s in range(supers):
            t0 = pl.multiple_of(i * STEP + s * SUPER, SUPER)
            t1 = pl.multiple_of(t - (i + 1) * STEP + (supers - 1 - s) * SUPER, SUPER)
            units.append((fwd, (q_ref, k_ref, v_ref, laf_ref), t0))
            units.append((bwd, (q_ref, k_ref, v_ref, lab_ref), t1))
        run(units, want_out=True, finalize=finalize)

    def park_body(i, carry):
        step(i, False)
        return carry

    def finish_body(i, carry):
        step(i, True)
        return carry

    lax.fori_loop(0, n_steps // 2, park_body, 0)
    lax.fori_loop(n_steps // 2, n_steps, finish_body, 0)


def _gla(q, k, v, laf, lab, kc, vc, lafc, labc, rs, gnorm):
    bsz, t, _ = q.shape
    assert t % (2 * STEP) == 0 and kc.shape[1] % SUPER == 0

    def bspec(arr):
        return pl.BlockSpec((None,) + arr.shape[1:], lambda b: (b, 0, 0))

    args = (q, k, v, laf, lab, kc, vc, lafc, labc, rs)
    return pl.pallas_call(
        _gla_kernel,
        grid=(bsz,),
        in_specs=[bspec(a) for a in args] + [_const_spec(gnorm)],
        out_specs=pl.BlockSpec((None, t, HV), lambda b: (b, 0, 0)),
        out_shape=jax.ShapeDtypeStruct((bsz, t, HV), BF16),
        scratch_shapes=[
            pltpu.VMEM((t // 2, HV), F32), pltpu.VMEM((t // 2, HV), F32),
            pltpu.VMEM((GLA_HEADS, GLA_DV, LANES), F32),
            pltpu.VMEM((GLA_HEADS, GLA_DV, LANES), F32),
        ],
        compiler_params=pltpu.CompilerParams(
            dimension_semantics=("parallel",), vmem_limit_bytes=VMEM_LIMIT),
        name="gla",
    )(*args, gnorm)


def _tap_segments(n_tiles, lines_per_tile, n_lines, group):
    segs = []
    for t0 in range(0, n_tiles, group):
        t1 = min(t0 + group, n_tiles)
        lo_line = int(t0 * lines_per_tile)
        hi_line = int(-(-t1 * lines_per_tile // 1)) - 1
        k_lo = max(0, CONV_PAD - hi_line)
        k_hi = min(CONV_K, n_lines + CONV_PAD - lo_line)
        if segs and segs[-1][2:] == (k_lo, k_hi):
            segs[-1] = (segs[-1][0], t1, k_lo, k_hi)
        else:
            segs.append((t0, t1, k_lo, k_hi))
    return segs


def _conv_kernel(ah_ref, av_ref, w_ref, b_ref, yh_ref, yv_ref, ph_ref, pv_ref):
    t = ah_ref.shape[0]
    rows = t // GRID_W
    tile = 64

    def conv(src_ref, pad_ref, w_cols, line, n_lines, group, dst_ref):
        pad = CONV_PAD * line
        pad_ref[pl.ds(0, pad), :] = jnp.zeros((pad, CONV_HALF), F32)
        pad_ref[pl.ds(pad + t, pad), :] = jnp.zeros((pad, CONV_HALF), F32)

        def fill(i, carry):
            t0 = pl.multiple_of(i * tile, tile)
            pad_ref[pl.ds(pl.multiple_of(pad + t0, 8), tile), :] = (
                src_ref[pl.ds(t0, tile), :].astype(F32))
            return carry

        lax.fori_loop(0, t // tile, fill, 0)
        bias = b_ref[:, w_cols]

        for first, last, k_lo, k_hi in _tap_segments(t // tile, tile / line, n_lines, group):
            def body(i, carry, k_lo=k_lo, k_hi=k_hi):
                t0 = pl.multiple_of(i * tile, tile)
                acc = jnp.broadcast_to(bias, (tile, CONV_HALF))
                for kk in range(k_lo, k_hi):
                    win = pad_ref[pl.ds(pl.multiple_of(t0 + kk * line, 8), tile), :]
                    acc = acc + w_ref[kk:kk + 1, w_cols] * win
                dst_ref[pl.ds(t0, tile), :] = acc
                return carry

            lax.fori_loop(first, last, body, 0)

    conv(ah_ref, ph_ref, slice(0, CONV_HALF), rows, GRID_W, 2, yh_ref)
    conv(av_ref, pv_ref, slice(CONV_HALF, CONV_DIM), GRID_W, rows, 4, yv_ref)


def _conv(ah, av, w, b):
    bsz, t, c = ah.shape
    rows = t // GRID_W
    tok = pl.BlockSpec((None, t, c), lambda i: (i, 0, 0))
    return pl.pallas_call(
        _conv_kernel,
        grid=(bsz,),
        in_specs=[tok, tok, _const_spec(w), _const_spec(b)],
        out_specs=[tok, tok],
        out_shape=[jax.ShapeDtypeStruct((bsz, t, c), F32)] * 2,
        scratch_shapes=[
            pltpu.VMEM((t + 2 * CONV_PAD * rows, c), F32),
            pltpu.VMEM((t + 2 * CONV_PAD * GRID_W, c), F32),
        ],
        compiler_params=pltpu.CompilerParams(
            dimension_semantics=("parallel",), vmem_limit_bytes=VMEM_LIMIT),
        name="conv",
    )(ah, av, w, b)


def _post_kernel(x_ref, yh_ref, yv_ref, og_ref, gt_ref, mod_ref, perm_ref, lng_ref, lnb_ref,
                 gpost1_ref, gpre2_ref, gpost2_ref, wco_ref, wgo_ref, wout_ref, wff1_ref,
                 wff2_ref, o_ref):
    d = x_ref.shape[-1]
    d_ff = wff1_ref.shape[1]
    ff_tile = 1024
    sub = 256
    subs = [slice(s, s + sub) for s in range(0, x_ref.shape[0], sub)]
    gt1 = mod_ref[2:3, :]
    sh2 = mod_ref[3:4, :]
    sc2 = mod_ref[4:5, :]
    gt2 = mod_ref[5:6, :]

    yh = []
    for n in range(x_ref.shape[0] // PERM_TOKENS):
        blk = yh_ref[:, PERM_ROWS * n:PERM_ROWS * (n + 1), :].reshape(PERM_TOKENS, CONV_HALF)
        hi, lo = _split_bf16(blk)
        yh.append(_dot(perm_ref[...], hi) + _dot(perm_ref[...], lo))
    yh = jnp.concatenate(yh, axis=0)

    yn = []
    for s in subs:
        yc = jnp.concatenate([yh[s], yv_ref[s, :]], axis=-1)
        yc = yc - jnp.mean(yc, axis=-1, keepdims=True)
        var = jnp.mean(yc * yc, axis=-1, keepdims=True)
        z = yc * lax.rsqrt(var + EPS) * lng_ref[...] + lnb_ref[...]
        yn.append((z * _sigmoid(z)).astype(BF16))
    y_gla = [_dot(og_ref[s, :], wgo_ref[...]) for s in subs]
    y_conv = [_dot(v, wco_ref[...]) for v in yn]
    merged = []
    for n, s in enumerate(subs):
        gates = gt_ref[s, :].astype(F32)
        merged.append((gates[:, :d] * y_conv[n] + gates[:, d:] * y_gla[n]).astype(BF16))
    y = [_dot(v, wout_ref[...]) for v in merged]
    h_mid = [x_ref[s, :] + gt1 * _rms(y[n], gpost1_ref[...]) for n, s in enumerate(subs)]
    u2 = [(_rms(h, gpre2_ref[...]) * (1.0 + sc2) + sh2).astype(BF16) for h in h_mid]
    acc = [jnp.zeros(h.shape, F32) for h in h_mid]
    for j in range(d_ff // ff_tile):
        cols = slice(j * ff_tile, (j + 1) * ff_tile)
        f = [_dot(v, wff1_ref[:, cols]) for v in u2]
        f = [jnp.square(jnp.maximum(v, 0.0)).astype(BF16) for v in f]
        acc = [acc[n] + _dot(f[n], wff2_ref[cols, :]) for n in range(len(subs))]
    for n, s in enumerate(subs):
        o_ref[s, :] = h_mid[n] + gt2 * _rms(acc[n], gpost2_ref[...])


def _post(x, yh, yv, og, gates, mod, perm, ln_g, ln_b, gpost1, gpre2, gpost2, wco, wgo, wout,
          wff1, wff2, tm):
    bsz, t, d = x.shape
    assert tm % PERM_TOKENS == 0
    return pl.pallas_call(
        _post_kernel,
        grid=(bsz, t // tm),
        in_specs=[
            _tok_spec(tm, d), _colmajor_spec(tm), _tok_spec(tm, CONV_HALF), _tok_spec(tm, HV),
            _tok_spec(tm, 2 * d),
            pl.BlockSpec((None,) + mod.shape[1:], lambda b, i: (b, 0, 0)),
            _single_spec(perm), _const_spec(ln_g), _const_spec(ln_b),
            _const_spec(gpost1), _const_spec(gpre2), _const_spec(gpost2),
            _single_spec(wco), _single_spec(wgo), _single_spec(wout), _single_spec(wff1),
            _single_spec(wff2),
        ],
        out_specs=_tok_spec(tm, d),
        out_shape=jax.ShapeDtypeStruct((bsz, t, d), F32),
        compiler_params=pltpu.CompilerParams(
            dimension_semantics=("parallel", "parallel"), vmem_limit_bytes=VMEM_LIMIT),
        name="post",
    )(x, yh, yv, og, gates, mod, perm, ln_g, ln_b, gpost1, gpre2, gpost2, wco, wgo, wout, wff1,
      wff2)


def kernel(x, c, ctx, c_ctx, w_mod, b_mod, g_pre1, g_post1, g_pre2, g_post2, w_in, conv_w,
           conv_b, conv_ln_g, conv_ln_b, w_conv_out, w_decay, b_decay, gla_norm_g, w_gla_out,
           w_out, w_ff1, w_ff2):
    bsz, t, d = x.shape
    depth = w_in.shape[0]
    assert depth == 1 and d == D_MODEL and t % (GRID_W * 8) == 0

    n_rows = -(-(bsz + 1) // 8) * 8
    c_all = jnp.zeros((n_rows, d), F32).at[:bsz].set(c).at[bsz].set(c_ctx)
    m = _adaln(c_all, w_mod, b_mod)
    mod = m[:bsz].reshape(bsz, N_MOD, d)
    mod_ctx = m[bsz, :2 * d].reshape(1, 2, d)

    wm, wdin, wg = _split_w_in(jnp.swapaxes(w_in[0], 0, 1))
    wdec = jnp.zeros((LANES, 2 * HK), F32)
    wdec = wdec.at[:DECAY_RANK, :HK].set(w_decay[0, 0])
    wdec = wdec.at[DECAY_RANK:2 * DECAY_RANK, HK:].set(w_decay[0, 1]).astype(BF16)
    bdec = b_decay[0].reshape(1, 2 * HK)
    row = lambda v: v.reshape(1, -1)

    perm = _to_colmajor_perm()
    to_colmajor = jnp.asarray(perm, BF16)
    to_rowmajor = jnp.asarray(perm.T, BF16)

    ah, av, q, k, v, rs, laf, lab, gates, wco, wgo, wout, wff1, wff2 = _proj(
        x, mod, row(g_pre1[0]), to_colmajor, wm, wdin, wdec, bdec, wg,
        (w_conv_out, w_gla_out, w_out, w_ff1, w_ff2), tm=1024)
    kc, vc, lafc, labc = _ctx_proj(
        ctx, mod_ctx, row(g_pre1[0]), wm, wdin, wdec, bdec, nb=4)

    og = _gla(q, k, v, laf, lab, kc, vc, lafc, labc, rs, row(gla_norm_g[0]))
    yh, yv = _conv(ah.reshape(bsz, t, CONV_HALF), av, conv_w[0], row(conv_b[0]))
    yh = yh.reshape(bsz, GRID_W, t // GRID_W, CONV_HALF)

    return _post(x, yh, yv, og, gates, mod, to_rowmajor, row(conv_ln_g[0]), row(conv_ln_b[0]),
                 row(g_post1[0]), row(g_pre2[0]), row(g_post2[0]),
                 wco, wgo, wout, wff1, wff2, tm=512)
```

```python
import functools

import jax
import jax.numpy as jnp
import numpy as np
from jax import lax
from jax.experimental import pallas as pl
from jax.experimental.pallas import tpu as pltpu

D_MODEL = 1024
GRID_W = 64
CONV_DIM = 512
CONV_K = 31
GLA_HEADS = 4
GLA_DK = 64
GLA_DV = 128
DECAY_RANK = 16
GATE_NORM = 16.0
CHUNK = 64
N_MOD = 6
EPS = 1e-6

COL_Q = 2 * CONV_DIM
COL_K = COL_Q + GLA_HEADS * GLA_DK
COL_V = COL_K + GLA_HEADS * GLA_DK
COL_R = COL_V + GLA_HEADS * GLA_DV
COL_DEC = COL_R + GLA_HEADS * GLA_DV
COL_GATE = COL_DEC + 2 * DECAY_RANK
COL_END = COL_GATE + 2 * D_MODEL

LANES = 128
HK = GLA_HEADS * GLA_DK
HV = GLA_HEADS * GLA_DV
CONV_HALF = CONV_DIM // 2
CONV_PAD = CONV_K // 2
PERM_ROWS = 8
PERM_TOKENS = PERM_ROWS * GRID_W
PROJ_SUB = 256
SUPER = 2 * CHUNK
STEP = 8 * SUPER

VMEM_LIMIT = 56 * 1024 * 1024

F32 = jnp.float32
BF16 = jnp.bfloat16

NT = (((1,), (1,)), ((), ()))
TN = (((0,), (0,)), ((), ()))


def _dot(a, b):
    return jnp.dot(a, b, preferred_element_type=F32)


def _rms(x, g):
    ms = jnp.mean(x * x, axis=-1, keepdims=True)
    return x * lax.rsqrt(ms + EPS) * g


def _sigmoid(x):
    return jax.nn.sigmoid(x)


def _split_bf16(x):
    hi = x.astype(BF16)
    lo = (x - hi.astype(F32)).astype(BF16)
    return hi, lo


def _to_colmajor_perm():
    out = np.arange(PERM_TOKENS)
    src = (out % PERM_ROWS) * GRID_W + out // PERM_ROWS
    p = np.zeros((PERM_TOKENS, PERM_TOKENS), np.float32)
    p[out, src] = 1.0
    return p


def _adaln_kernel(c_ref, w_ref, b_ref, o_ref):
    @pl.when(pl.program_id(0) == 0)
    def _():
        o_ref[...] = jnp.broadcast_to(b_ref[...], o_ref.shape)

    c = c_ref[...]
    s_hi, s_lo = _split_bf16(c * _sigmoid(c))
    w_hi, w_lo = _split_bf16(w_ref[...])
    o_ref[...] += _dot(s_hi, w_hi) + _dot(s_lo, w_hi) + _dot(s_hi, w_lo)


def _adaln(c_all, w_mod, b_mod):
    n_rows, d = c_all.shape
    n_out = w_mod.shape[-1]
    tk = 256
    return pl.pallas_call(
        _adaln_kernel,
        grid=(d // tk,),
        in_specs=[
            pl.BlockSpec((n_rows, tk), lambda k: (0, k)),
            pl.BlockSpec((None, tk, n_out), lambda k: (0, k, 0)),
            pl.BlockSpec((1, n_out), lambda k: (0, 0)),
        ],
        out_specs=pl.BlockSpec((n_rows, n_out), lambda k: (0, 0)),
        out_shape=jax.ShapeDtypeStruct((n_rows, n_out), F32),
        compiler_params=pltpu.CompilerParams(
            dimension_semantics=("arbitrary",), vmem_limit_bytes=VMEM_LIMIT),
        name="adaln",
    )(c_all, w_mod, b_mod)


def _split_w_in_kernel(wm_t_ref, wd_t_ref, wg_t_ref, wm_ref, wdin_ref, wg_ref, *, n_gate):
    j = pl.program_id(0)
    wm_ref[...] = wm_t_ref[...].astype(BF16).T

    @pl.when(j == 0)
    def _():
        wdin_ref[...] = wd_t_ref[...].astype(BF16).T

    @pl.when(j < n_gate)
    def _():
        wg_ref[...] = wg_t_ref[...].astype(BF16).T


def _split_w_in(w_in_t):
    n, d = w_in_t.shape
    tr = 512
    n_main = COL_DEC // tr
    n_gate = (n - COL_GATE) // tr
    assert COL_DEC % tr == 0 and (n - COL_GATE) % tr == 0 and n_gate <= n_main
    gate_blk = lambda j: jnp.minimum(j, n_gate - 1)
    return pl.pallas_call(
        functools.partial(_split_w_in_kernel, n_gate=n_gate),
        grid=(n_main,),
        in_specs=[
            pl.BlockSpec((tr, d), lambda j: (j, 0)),
            pl.BlockSpec((LANES, d), lambda j: (COL_DEC // LANES, 0)),
            pl.BlockSpec((pl.Element(tr), pl.Element(d)),
                         lambda j: (pl.multiple_of(COL_GATE + tr * gate_blk(j), 32), 0)),
        ],
        out_specs=[
            pl.BlockSpec((d, tr), lambda j: (0, j)),
            pl.BlockSpec((d, LANES), lambda j: (0, 0)),
            pl.BlockSpec((d, tr), lambda j: (0, gate_blk(j))),
        ],
        out_shape=[jax.ShapeDtypeStruct((d, w), BF16) for w in (COL_DEC, LANES, n - COL_GATE)],
        compiler_params=pltpu.CompilerParams(
            dimension_semantics=("arbitrary",), vmem_limit_bytes=VMEM_LIMIT),
        name="split_w_in",
    )(w_in_t, w_in_t, w_in_t)


def _prenorm_mod(x, g, mod_ref):
    sh = mod_ref[0:1, :]
    sc = mod_ref[1:2, :]
    return (_rms(x, g) * (1.0 + sc) + sh).astype(BF16)


def _decay_logs(z, wdec_ref, bdec_ref):
    z_hi, z_lo = z
    logits = _dot(z_hi, wdec_ref[...]) + _dot(z_lo, wdec_ref[...]) + bdec_ref[...]
    ls = jnp.minimum(logits, 0.0) - jnp.log(1.0 + jnp.exp(-jnp.abs(logits)))
    return ls * (1.0 / GATE_NORM)


N_LATE_W = 5


def _proj_kernel(x_ref, mod_ref, g_ref, perm_ref, wm_ref, wdin_ref, wdec_ref, bdec_ref, wg_ref,
                 *refs):
    late_f32 = refs[:N_LATE_W]
    (ah_ref, av_ref, q_ref, k_ref, v_ref, rs_ref, laf_ref, lab_ref,
     gt_ref) = refs[N_LATE_W:-N_LATE_W]
    late_bf16 = refs[-N_LATE_W:]
    for src, dst in zip(late_f32, late_bf16):
        dst[...] = src[...].astype(BF16)
    subs = [slice(s, s + PROJ_SUB) for s in range(0, x_ref.shape[0], PROJ_SUB)]
    g = g_ref[...]
    ub = [_prenorm_mod(x_ref[s, :], g, mod_ref) for s in subs]
    z = [_split_bf16(_dot(u, wdin_ref[...])) for u in ub]
    a_h = []
    for n, s in enumerate(subs):
        glu = _dot(ub[n], wm_ref[:, 0:COL_Q])
        a = (glu[:, :CONV_DIM] * _sigmoid(glu[:, CONV_DIM:])).astype(BF16)
        av_ref[s, :] = a[:, CONV_HALF:]
        a_h.append(a[:, :CONV_HALF])
    for n, s in enumerate(subs):
        q_ref[s, :] = (_dot(ub[n], wm_ref[:, COL_Q:COL_K]) * (GLA_DK ** -0.5)).astype(BF16)
    for n, s in enumerate(subs):
        k_ref[s, :] = _dot(ub[n], wm_ref[:, COL_K:COL_V]).astype(BF16)
    for n, s in enumerate(subs):
        v_ref[s, :] = _dot(ub[n], wm_ref[:, COL_V:COL_R]).astype(BF16)
    for n, s in enumerate(subs):
        la = _decay_logs(z[n], wdec_ref, bdec_ref)
        laf_ref[s, :] = la[:, :HK]
        lab_ref[s, :] = la[:, HK:]
    for n, s in enumerate(subs):
        r = _dot(ub[n], wm_ref[:, COL_R:COL_DEC])
        rs_ref[s, :] = (r * _sigmoid(r)).astype(BF16)
    per_group = PERM_TOKENS // PROJ_SUB
    for n in range(x_ref.shape[0] // PERM_TOKENS):
        a_rm = jnp.concatenate(a_h[per_group * n:per_group * (n + 1)], axis=0)
        a_cm = _dot(perm_ref[...], a_rm)
        ah_ref[:, PERM_ROWS * n:PERM_ROWS * (n + 1), :] = a_cm.reshape(
            GRID_W, PERM_ROWS, CONV_HALF)
    for n, s in enumerate(subs):
        gt_ref[s, :] = _sigmoid(_dot(ub[n], wg_ref[...])).astype(BF16)


def _ctx_proj_kernel(x_ref, mod_ref, g_ref, wm_ref, wdin_ref, wdec_ref, bdec_ref,
                     k_ref, v_ref, laf_ref, lab_ref):
    nb, t, d = x_ref.shape
    ub = _prenorm_mod(x_ref[...].reshape(nb * t, d), g_ref[...], mod_ref)
    z = _split_bf16(_dot(ub, wdin_ref[...]))
    k_ref[...] = _dot(ub, wm_ref[:, COL_K:COL_V]).astype(BF16).reshape(k_ref.shape)
    v_ref[...] = _dot(ub, wm_ref[:, COL_V:COL_R]).astype(BF16).reshape(v_ref.shape)
    la = _decay_logs(z, wdec_ref, bdec_ref)
    laf_ref[...] = la[:, :HK].reshape(laf_ref.shape)
    lab_ref[...] = la[:, HK:].reshape(lab_ref.shape)


def _const_spec(arr):
    nd = arr.ndim
    return pl.BlockSpec(arr.shape, lambda *_: (0,) * nd)


def _tok_spec(tm, width):
    return pl.BlockSpec((None, tm, width), lambda b, i: (b, i, 0))


def _colmajor_spec(tm):
    return pl.BlockSpec((None, GRID_W, tm // GRID_W, CONV_HALF), lambda b, i: (b, 0, i, 0))


def _single_spec(arr):
    nd = arr.ndim
    return pl.BlockSpec(arr.shape, lambda *_: (0,) * nd, pipeline_mode=pl.Buffered(1))


def _proj(x, mod, g, perm, wm, wdin, wdec, bdec, wg, late_weights, tm):
    bsz, t, d = x.shape
    widths = (CONV_HALF, HK, HK, HV, HV, HK, HK, 2 * D_MODEL)
    dtypes = (BF16, BF16, BF16, BF16, BF16, F32, F32, BF16)
    assert tm % PERM_TOKENS == 0 and len(late_weights) == N_LATE_W
    ah_shape = jax.ShapeDtypeStruct((bsz, GRID_W, t // GRID_W, CONV_HALF), F32)
    n_i = t // tm
    n_steps = bsz * n_i
    late_in, late_out, late_shape = [], [], []
    for w in late_weights:
        _, rows, cols = w.shape
        assert rows % (16 * n_steps) == 0
        blk = rows // n_steps
        late_in.append(pl.BlockSpec((None, blk, cols), lambda b, i: (0, b * n_i + i, 0)))
        late_out.append(pl.BlockSpec((blk, cols), lambda b, i: (b * n_i + i, 0)))
        late_shape.append(jax.ShapeDtypeStruct((rows, cols), BF16))
    return pl.pallas_call(
        _proj_kernel,
        grid=(bsz, n_i),
        in_specs=[
            _tok_spec(tm, d),
            pl.BlockSpec((None,) + mod.shape[1:], lambda b, i: (b, 0, 0)),
            _const_spec(g), _single_spec(perm), _single_spec(wm), _single_spec(wdin),
            _single_spec(wdec), _const_spec(bdec), _single_spec(wg),
        ] + late_in,
        out_specs=[_colmajor_spec(tm)] + [_tok_spec(tm, w) for w in widths] + late_out,
        out_shape=[ah_shape] + [jax.ShapeDtypeStruct((bsz, t, w), dt)
                                for w, dt in zip(widths, dtypes)] + late_shape,
        compiler_params=pltpu.CompilerParams(
            dimension_semantics=("parallel", "parallel"), vmem_limit_bytes=VMEM_LIMIT),
        name="proj",
    )(x, mod, g, perm, wm, wdin, wdec, bdec, wg, *late_weights)


def _ctx_proj(x, mod, g, wm, wdin, wdec, bdec, nb):
    bsz, t, d = x.shape
    widths = (HK, HV, HK, HK)
    dtypes = (BF16, BF16, F32, F32)
    assert bsz % nb == 0
    return pl.pallas_call(
        _ctx_proj_kernel,
        grid=(bsz // nb,),
        in_specs=[
            pl.BlockSpec((nb, t, d), lambda i: (i, 0, 0)),
            pl.BlockSpec((None,) + mod.shape[1:], lambda i: (0, 0, 0)),
            _const_spec(g), _single_spec(wm), _single_spec(wdin), _single_spec(wdec),
            _const_spec(bdec),
        ],
        out_specs=[pl.BlockSpec((nb, t, w), lambda i: (i, 0, 0)) for w in widths],
        out_shape=[jax.ShapeDtypeStruct((bsz, t, w), dt) for w, dt in zip(widths, dtypes)],
        compiler_params=pltpu.CompilerParams(
            dimension_semantics=("parallel",), vmem_limit_bytes=VMEM_LIMIT),
        name="ctx_proj",
    )(x, mod, g, wm, wdin, wdec, bdec)


def _gla_scale(qs, ks, las, dirn, want_out):
    tri_b, mid, last = dirn["tri_b"], dirn["mid"], dirn["last"]
    la_hi, la_lo = _split_bf16(las)
    g = _dot(tri_b, la_hi) + _dot(tri_b, la_lo)
    u = dict(kt=[], kend=[], qt=[], qg=[], e=[])
    for c in range(SUPER // CHUNK):
        rows = slice(c * CHUNK, (c + 1) * CHUNK)
        gc = g[rows]
        g_mid = gc[mid:mid + 1]
        g_last = gc[last:last + 1]
        u["e"].append(jnp.exp(g_last))
        kt = ks[rows].astype(F32) * jnp.exp(g_mid - gc)
        u["kt"].append(kt)
        u["kend"].append(kt * jnp.exp(g_last - g_mid))
        if want_out:
            qt = qs[rows].astype(F32) * jnp.exp(gc - g_mid)
            u["qt"].append(qt.astype(BF16))
            u["qg"].append((qt * jnp.exp(g_mid)).astype(BF16))
    return u


def _pair_lanes(pair):
    return slice(LANES * pair, LANES * (pair + 1))


def _gla_scores(u, pair, head_lane):
    lanes = _pair_lanes(pair)
    kt = jnp.concatenate([c[:, lanes] for c in u["kt"]], axis=0)
    qt = jnp.concatenate([c[:, lanes] for c in u["qt"]], axis=0)
    kt2 = jnp.concatenate([(kt * hm).astype(BF16) for hm in head_lane], axis=0)
    return lax.dot_general(qt, kt2, NT, preferred_element_type=F32)


def _gla_update(u, vs, h, head_lane):
    lanes = _pair_lanes(h // 2)
    hm = head_lane[h % 2]
    k0 = (u["kend"][0][:, lanes] * hm).astype(BF16)
    k1 = (u["kend"][1][:, lanes] * hm).astype(BF16)
    z = jnp.zeros_like(k0)
    kbd = jnp.concatenate([jnp.concatenate([k0, z], axis=1),
                           jnp.concatenate([z, k1], axis=1)], axis=0)
    return lax.dot_general(vs[:, GLA_DV * h:GLA_DV * (h + 1)], kbd, TN,
                           preferred_element_type=F32)


def _gla_intra(scores, vs, pair, mask2):
    a = jnp.where(mask2, scores, 0.0).astype(BF16)
    v0 = vs[:, GLA_DV * 2 * pair:GLA_DV * (2 * pair + 1)]
    v1 = vs[:, GLA_DV * (2 * pair + 1):GLA_DV * (2 * pair + 2)]
    z = jnp.zeros_like(v0)
    vbd = jnp.concatenate([jnp.concatenate([v0, z], axis=1),
                           jnp.concatenate([z, v1], axis=1)], axis=0)
    return _dot(a, vbd)


def _gla_kernel(q_ref, k_ref, v_ref, laf_ref, lab_ref, kc_ref, vc_ref, lafc_ref, labc_ref,
                rs_ref, g_ref, o_ref, of_ref, ob_ref, sf_ref, sb_ref):
    t = q_ref.shape[0]
    t_ctx = kc_ref.shape[0]
    half = t // 2
    n_steps = t // STEP
    supers = STEP // SUPER
    n_chunks = SUPER // CHUNK
    pairs = GLA_HEADS // 2

    row = lax.broadcasted_iota(jnp.int32, (SUPER, SUPER), 0)
    col = lax.broadcasted_iota(jnp.int32, (SUPER, SUPER), 1)
    same_chunk = (row // CHUNK) == (col // CHUNK)
    low_mask = jnp.logical_and(same_chunk, col <= row)
    up_mask = jnp.logical_and(same_chunk, col >= row)
    lane = lax.broadcasted_iota(jnp.int32, (1, LANES), 1)
    head_lane = (jnp.where(lane < GLA_DK, 1.0, 0.0), jnp.where(lane >= GLA_DK, 1.0, 0.0))

    fwd = dict(tri_b=jnp.where(low_mask, 1.0, 0.0).astype(BF16),
               mask2=jnp.concatenate([low_mask, low_mask], axis=1),
               mid=CHUNK // 2 - 1, last=CHUNK - 1, order=(0, 1), st_ref=sf_ref)
    bwd = dict(tri_b=jnp.where(up_mask, 1.0, 0.0).astype(BF16),
               mask2=jnp.concatenate([up_mask, up_mask], axis=1),
               mid=CHUNK // 2, last=0, order=(1, 0), st_ref=sb_ref)

    sf_ref[...] = jnp.zeros_like(sf_ref)
    sb_ref[...] = jnp.zeros_like(sb_ref)
    gn = g_ref[...]

    def run(units, want_out, finalize):
        vals = []
        for dirn, refs, t0 in units:
            q_r, k_r, v_r, la_r = refs
            rows = pl.ds(t0, SUPER)
            qs = q_r[rows, :] if want_out else None
            vals.append((_gla_scale(qs, k_r[rows, :], la_r[rows, :], dirn, want_out),
                         v_r[rows, :]))
        scores = []
        if want_out:
            scores = [[_gla_scores(u, p, head_lane) for p in range(pairs)] for u, _ in vals]
        upd = [[_gla_update(u, vs, h, head_lane) for h in range(GLA_HEADS)] for u, vs in vals]
        intra = []
        if want_out:
            intra = [[_gla_intra(scores[n][p], vals[n][1], p, units[n][0]["mask2"])
                      for p in range(pairs)] for n in range(len(units))]
        states = {}
        for n, (dirn, _, t0) in enumerate(units):
            key = id(dirn["st_ref"])
            if key not in states:
                states[key] = (dirn["st_ref"], [dirn["st_ref"][h] for h in range(GLA_HEADS)])
            st = states[key][1]
            u = vals[n][0]
            inter = [[None] * n_chunks for _ in range(pairs)]
            for c in dirn["order"]:
                if want_out:
                    for p in range(pairs):
                        st2 = jnp.concatenate([st[2 * p].astype(BF16),
                                               st[2 * p + 1].astype(BF16)], axis=0)
                        inter[p][c] = lax.dot_general(u["qg"][c][:, _pair_lanes(p)], st2, NT,
                                                      preferred_element_type=F32)
                for h in range(GLA_HEADS):
                    e = u["e"][c][:, _pair_lanes(h // 2)]
                    st[h] = st[h] * e + upd[n][h][:, LANES * c:LANES * (c + 1)]
            if not want_out:
                continue
            fwd_unit = dirn is fwd
            for p in range(pairs):
                o2 = intra[n][p] + jnp.concatenate(inter[p], axis=0)
                for j in range(2):
                    cols = slice(GLA_DV * (2 * p + j), GLA_DV * (2 * p + j + 1))
                    o = o2[:, GLA_DV * j:GLA_DV * (j + 1)]
                    if finalize:
                        other = (ob_ref[pl.ds(t0 - half, SUPER), cols] if fwd_unit
                                 else of_ref[pl.ds(t0, SUPER), cols])
                        rs = rs_ref[pl.ds(t0, SUPER), cols].astype(F32)
                        o_ref[pl.ds(t0, SUPER), cols] = (_rms(o + other, gn) * rs).astype(BF16)
                    elif fwd_unit:
                        of_ref[pl.ds(t0, SUPER), cols] = o
                    else:
                        ob_ref[pl.ds(t0 - half, SUPER), cols] = o
        for st_ref, st in states.values():
            for h in range(GLA_HEADS):
                st_ref[h] = st[h]

    ctx_units = []
    for s in range(t_ctx // SUPER):
        ctx_units.append((fwd, (None, kc_ref, vc_ref, lafc_ref), s * SUPER))
        ctx_units.append((bwd, (None, kc_ref, vc_ref, labc_ref), t_ctx - (s + 1) * SUPER))
    run(ctx_units, want_out=False, finalize=False)

    def step(i, finalize):
        units = []
        for s in range(supers):
            t0 = pl.multiple_of(i * STEP + s * SUPER, SUPER)
            t1 = pl.multiple_of(t - (i + 1) * STEP + (supers - 1 - s) * SUPER, SUPER)
            units.append((fwd, (q_ref, k_ref, v_ref, laf_ref), t0))
            units.append((bwd, (q_ref, k_ref, v_ref, lab_ref), t1))
        run(units, want_out=True, finalize=finalize)

    def park_body(i, carry):
        step(i, False)
        return carry

    def finish_body(i, carry):
        step(i, True)
        return carry

    lax.fori_loop(0, n_steps // 2, park_body, 0)
    lax.fori_loop(n_steps // 2, n_steps, finish_body, 0)


def _gla(q, k, v, laf, lab, kc, vc, lafc, labc, rs, gnorm):
    bsz, t, _ = q.shape
    assert t % (2 * STEP) == 0 and kc.shape[1] % SUPER == 0

    def bspec(arr):
        return pl.BlockSpec((None,) + arr.shape[1:], lambda b: (b, 0, 0))

    args = (q, k, v, laf, lab, kc, vc, lafc, labc, rs)
    return pl.pallas_call(
        _gla_kernel,
        grid=(bsz,),
        in_specs=[bspec(a) for a in args] + [_const_spec(gnorm)],
        out_specs=pl.BlockSpec((None, t, HV), lambda b: (b, 0, 0)),
        out_shape=jax.ShapeDtypeStruct((bsz, t, HV), BF16),
        scratch_shapes=[
            pltpu.VMEM((t // 2, HV), F32), pltpu.VMEM((t // 2, HV), F32),
            pltpu.VMEM((GLA_HEADS, GLA_DV, LANES), F32),
            pltpu.VMEM((GLA_HEADS, GLA_DV, LANES), F32),
        ],
        compiler_params=pltpu.CompilerParams(
            dimension_semantics=("parallel",), vmem_limit_bytes=VMEM_LIMIT),
        name="gla",
    )(*args, gnorm)


def _tap_segments(n_tiles, lines_per_tile, n_lines, group):
    segs = []
    for t0 in range(0, n_tiles, group):
        t1 = min(t0 + group, n_tiles)
        lo_line = int(t0 * lines_per_tile)
        hi_line = int(-(-t1 * lines_per_tile // 1)) - 1
        k_lo = max(0, CONV_PAD - hi_line)
        k_hi = min(CONV_K, n_lines + CONV_PAD - lo_line)
        if segs and segs[-1][2:] == (k_lo, k_hi):
            segs[-1] = (segs[-1][0], t1, k_lo, k_hi)
        else:
            segs.append((t0, t1, k_lo, k_hi))
    return segs


def _conv_kernel(ah_ref, av_ref, w_ref, b_ref, yh_ref, yv_ref, ph_ref, pv_ref):
    t = ah_ref.shape[0]
    rows = t // GRID_W
    tile = 64

    def conv(src_ref, pad_ref, w_cols, line, n_lines, group, dst_ref):
        pad = CONV_PAD * line
        pad_ref[pl.ds(0, pad), :] = jnp.zeros((pad, CONV_HALF), F32)
        pad_ref[pl.ds(pad + t, pad), :] = jnp.zeros((pad, CONV_HALF), F32)

        def fill(i, carry):
            t0 = pl.multiple_of(i * tile, tile)
            pad_ref[pl.ds(pl.multiple_of(pad + t0, 8), tile), :] = (
                src_ref[pl.ds(t0, tile), :].astype(F32))
            return carry

        lax.fori_loop(0, t // tile, fill, 0)
        bias = b_ref[:, w_cols]

        for first, last, k_lo, k_hi in _tap_segments(t // tile, tile / line, n_lines, group):
            def body(i, carry, k_lo=k_lo, k_hi=k_hi):
                t0 = pl.multiple_of(i * tile, tile)
                acc = jnp.broadcast_to(bias, (tile, CONV_HALF))
                for kk in range(k_lo, k_hi):
                    win = pad_ref[pl.ds(pl.multiple_of(t0 + kk * line, 8), tile), :]
                    acc = acc + w_ref[kk:kk + 1, w_cols] * win
                dst_ref[pl.ds(t0, tile), :] = acc
                return carry

            lax.fori_loop(first, last, body, 0)

    conv(ah_ref, ph_ref, slice(0, CONV_HALF), rows, GRID_W, 2, yh_ref)
    conv(av_ref, pv_ref, slice(CONV_HALF, CONV_DIM), GRID_W, rows, 4, yv_ref)


def _conv(ah, av, w, b):
    bsz, t, c = ah.shape
    rows = t // GRID_W
    tok = pl.BlockSpec((None, t, c), lambda i: (i, 0, 0))
    return pl.pallas_call(
        _conv_kernel,
        grid=(bsz,),
        in_specs=[tok, tok, _const_spec(w), _const_spec(b)],
        out_specs=[tok, tok],
        out_shape=[jax.ShapeDtypeStruct((bsz, t, c), F32)] * 2,
        scratch_shapes=[
            pltpu.VMEM((t + 2 * CONV_PAD * rows, c), F32),
            pltpu.VMEM((t + 2 * CONV_PAD * GRID_W, c), F32),
        ],
        compiler_params=pltpu.CompilerParams(
            dimension_semantics=("parallel",), vmem_limit_bytes=VMEM_LIMIT),
        name="conv",
    )(ah, av, w, b)


def _post_kernel(x_ref, yh_ref, yv_ref, og_ref, gt_ref, mod_ref, perm_ref, lng_ref, lnb_ref,
                 gpost1_ref, gpre2_ref, gpost2_ref, wco_ref, wgo_ref, wout_ref, wff1_ref,
                 wff2_ref, o_ref):
    d = x_ref.shape[-1]
    d_ff = wff1_ref.shape[1]
    ff_tile = 1024
    sub = 256
    subs = [slice(s, s + sub) for s in range(0, x_ref.shape[0], sub)]
    gt1 = mod_ref[2:3, :]
    sh2 = mod_ref[3:4, :]
    sc2 = mod_ref[4:5, :]
    gt2 = mod_ref[5:6, :]

    yh = []
    for n in range(x_ref.shape[0] // PERM_TOKENS):
        blk = yh_ref[:, PERM_ROWS * n:PERM_ROWS * (n + 1), :].reshape(PERM_TOKENS, CONV_HALF)
        hi, lo = _split_bf16(blk)
        yh.append(_dot(perm_ref[...], hi) + _dot(perm_ref[...], lo))
    yh = jnp.concatenate(yh, axis=0)

    yn = []
    for s in subs:
        yc = jnp.concatenate([yh[s], yv_ref[s, :]], axis=-1)
        yc = yc - jnp.mean(yc, axis=-1, keepdims=True)
        var = jnp.mean(yc * yc, axis=-1, keepdims=True)
        z = yc * lax.rsqrt(var + EPS) * lng_ref[...] + lnb_ref[...]
        yn.append((z * _sigmoid(z)).astype(BF16))
    y_gla = [_dot(og_ref[s, :], wgo_ref[...]) for s in subs]
    y_conv = [_dot(v, wco_ref[...]) for v in yn]
    merged = []
    for n, s in enumerate(subs):
        gates = gt_ref[s, :].astype(F32)
        merged.append((gates[:, :d] * y_conv[n] + gates[:, d:] * y_gla[n]).astype(BF16))
    y = [_dot(v, wout_ref[...]) for v in merged]
    h_mid = [x_ref[s, :] + gt1 * _rms(y[n], gpost1_ref[...]) for n, s in enumerate(subs)]
    u2 = [(_rms(h, gpre2_ref[...]) * (1.0 + sc2) + sh2).astype(BF16) for h in h_mid]
    acc = [jnp.zeros(h.shape, F32) for h in h_mid]
    for j in range(d_ff // ff_tile):
        cols = slice(j * ff_tile, (j + 1) * ff_tile)
        f = [_dot(v, wff1_ref[:, cols]) for v in u2]
        f = [jnp.square(jnp.maximum(v, 0.0)).astype(BF16) for v in f]
        acc = [acc[n] + _dot(f[n], wff2_ref[cols, :]) for n in range(len(subs))]
    for n, s in enumerate(subs):
        o_ref[s, :] = h_mid[n] + gt2 * _rms(acc[n], gpost2_ref[...])


def _post(x, yh, yv, og, gates, mod, perm, ln_g, ln_b, gpost1, gpre2, gpost2, wco, wgo, wout,
          wff1, wff2, tm):
    bsz, t, d = x.shape
    assert tm % PERM_TOKENS == 0
    return pl.pallas_call(
        _post_kernel,
        grid=(bsz, t // tm),
        in_specs=[
            _tok_spec(tm, d), _colmajor_spec(tm), _tok_spec(tm, CONV_HALF), _tok_spec(tm, HV),
            _tok_spec(tm, 2 * d),
            pl.BlockSpec((None,) + mod.shape[1:], lambda b, i: (b, 0, 0)),
            _single_spec(perm), _const_spec(ln_g), _const_spec(ln_b),
            _const_spec(gpost1), _const_spec(gpre2), _const_spec(gpost2),
            _single_spec(wco), _single_spec(wgo), _single_spec(wout), _single_spec(wff1),
            _single_spec(wff2),
        ],
        out_specs=_tok_spec(tm, d),
        out_shape=jax.ShapeDtypeStruct((bsz, t, d), F32),
        compiler_params=pltpu.CompilerParams(
            dimension_semantics=("parallel", "parallel"), vmem_limit_bytes=VMEM_LIMIT),
        name="post",
    )(x, yh, yv, og, gates, mod, perm, ln_g, ln_b, gpost1, gpre2, gpost2, wco, wgo, wout, wff1,
      wff2)


def kernel(x, c, ctx, c_ctx, w_mod, b_mod, g_pre1, g_post1, g_pre2, g_post2, w_in, conv_w,
           conv_b, conv_ln_g, conv_ln_b, w_conv_out, w_decay, b_decay, gla_norm_g, w_gla_out,
           w_out, w_ff1, w_ff2):
    bsz, t, d = x.shape
    depth = w_in.shape[0]
    assert depth == 1 and d == D_MODEL and t % (GRID_W * 8) == 0

    n_rows = -(-(bsz + 1) // 8) * 8
    c_all = jnp.zeros((n_rows, d), F32).at[:bsz].set(c).at[bsz].set(c_ctx)
    m = _adaln(c_all, w_mod, b_mod)
    mod = m[:bsz].reshape(bsz, N_MOD, d)
    mod_ctx = m[bsz, :2 * d].reshape(1, 2, d)

    wm, wdin, wg = _split_w_in(jnp.swapaxes(w_in[0], 0, 1))
    wdec = jnp.zeros((LANES, 2 * HK), F32)
    wdec = wdec.at[:DECAY_RANK, :HK].set(w_decay[0, 0])
    wdec = wdec.at[DECAY_RANK:2 * DECAY_RANK, HK:].set(w_decay[0, 1]).astype(BF16)
    bdec = b_decay[0].reshape(1, 2 * HK)
    row = lambda v: v.reshape(1, -1)

    perm = _to_colmajor_perm()
    to_colmajor = jnp.asarray(perm, BF16)
    to_rowmajor = jnp.asarray(perm.T, BF16)

    ah, av, q, k, v, rs, laf, lab, gates, wco, wgo, wout, wff1, wff2 = _proj(
        x, mod, row(g_pre1[0]), to_colmajor, wm, wdin, wdec, bdec, wg,
        (w_conv_out, w_gla_out, w_out, w_ff1, w_ff2), tm=1024)
    kc, vc, lafc, labc = _ctx_proj(
        ctx, mod_ctx, row(g_pre1[0]), wm, wdin, wdec, bdec, nb=4)

    og = _gla(q, k, v, laf, lab, kc, vc, lafc, labc, rs, row(gla_norm_g[0]))
    yh, yv = _conv(ah.reshape(bsz, t, CONV_HALF), av, conv_w[0], row(conv_b[0]))
    yh = yh.reshape(bsz, GRID_W, t // GRID_W, CONV_HALF)

    return _post(x, yh, yv, og, gates, mod, to_rowmajor, row(conv_ln_g[0]), row(conv_ln_b[0]),
                 row(g_post1[0]), row(g_pre2[0]), row(g_post2[0]),
                 wco, wgo, wout, wff1, wff2, tm=512)
```

```python
import functools

import jax
import jax.numpy as jnp
import numpy as np
from jax import lax
from jax.experimental import pallas as pl
from jax.experimental.pallas import tpu as pltpu

D_MODEL = 1024
GRID_W = 64
CONV_DIM = 512
CONV_K = 31
GLA_HEADS = 4
GLA_DK = 64
GLA_DV = 128
DECAY_RANK = 16
GATE_NORM = 16.0
CHUNK = 64
N_MOD = 6
EPS = 1e-6

COL_Q = 2 * CONV_DIM
COL_K = COL_Q + GLA_HEADS * GLA_DK
COL_V = COL_K + GLA_HEADS * GLA_DK
COL_R = COL_V + GLA_HEADS * GLA_DV
COL_DEC = COL_R + GLA_HEADS * GLA_DV
COL_GATE = COL_DEC + 2 * DECAY_RANK
COL_END = COL_GATE + 2 * D_MODEL

LANES = 128
HK = GLA_HEADS * GLA_DK
HV = GLA_HEADS * GLA_DV
CONV_HALF = CONV_DIM // 2
CONV_PAD = CONV_K // 2
PERM_ROWS = 8
PERM_TOKENS = PERM_ROWS * GRID_W
PROJ_SUB = 256
SUPER = 2 * CHUNK
STEP = 8 * SUPER

VMEM_LIMIT = 56 * 1024 * 1024

F32 = jnp.float32
BF16 = jnp.bfloat16

NT = (((1,), (1,)), ((), ()))
TN = (((0,), (0,)), ((), ()))


def _dot(a, b):
    return jnp.dot(a, b, preferred_element_type=F32)


def _rms(x, g):
    ms = jnp.mean(x * x, axis=-1, keepdims=True)
    return x * lax.rsqrt(ms + EPS) * g


def _sigmoid(x):
    return jax.nn.sigmoid(x)


def _split_bf16(x):
    hi = x.astype(BF16)
    lo = (x - hi.astype(F32)).astype(BF16)
    return hi, lo


def _to_colmajor_perm():
    out = np.arange(PERM_TOKENS)
    src = (out % PERM_ROWS) * GRID_W + out // PERM_ROWS
    p = np.zeros((PERM_TOKENS, PERM_TOKENS), np.float32)
    p[out, src] = 1.0
    return p


def _adaln_kernel(c_ref, w_ref, b_ref, o_ref):
    @pl.when(pl.program_id(0) == 0)
    def _():
        o_ref[...] = jnp.broadcast_to(b_ref[...], o_ref.shape)

    c = c_ref[...]
    s_hi, s_lo = _split_bf16(c * _sigmoid(c))
    w_hi, w_lo = _split_bf16(w_ref[...])
    o_ref[...] += _dot(s_hi, w_hi) + _dot(s_lo, w_hi) + _dot(s_hi, w_lo)


def _adaln(c_all, w_mod, b_mod):
    n_rows, d = c_all.shape
    n_out = w_mod.shape[-1]
    tk = 256
    return pl.pallas_call(
        _adaln_kernel,
        grid=(d // tk,),
        in_specs=[
            pl.BlockSpec((n_rows, tk), lambda k: (0, k)),
            pl.BlockSpec((None, tk, n_out), lambda k: (0, k, 0)),
            pl.BlockSpec((1, n_out), lambda k: (0, 0)),
        ],
        out_specs=pl.BlockSpec((n_rows, n_out), lambda k: (0, 0)),
        out_shape=jax.ShapeDtypeStruct((n_rows, n_out), F32),
        compiler_params=pltpu.CompilerParams(
            dimension_semantics=("arbitrary",), vmem_limit_bytes=VMEM_LIMIT),
        name="adaln",
    )(c_all, w_mod, b_mod)


def _split_w_in_kernel(wm_t_ref, wd_t_ref, wg_t_ref, wm_ref, wdin_ref, wg_ref, *, n_gate):
    j = pl.program_id(0)
    wm_ref[...] = wm_t_ref[...].astype(BF16).T

    @pl.when(j == 0)
    def _():
        wdin_ref[...] = wd_t_ref[...].astype(BF16).T

    @pl.when(j < n_gate)
    def _():
        wg_ref[...] = wg_t_ref[...].astype(BF16).T


def _split_w_in(w_in_t):
    n, d = w_in_t.shape
    tr = 512
    n_main = COL_DEC // tr
    n_gate = (n - COL_GATE) // tr
    assert COL_DEC % tr == 0 and (n - COL_GATE) % tr == 0 and n_gate <= n_main
    gate_blk = lambda j: jnp.minimum(j, n_gate - 1)
    return pl.pallas_call(
        functools.partial(_split_w_in_kernel, n_gate=n_gate),
        grid=(n_main,),
        in_specs=[
            pl.BlockSpec((tr, d), lambda j: (j, 0)),
            pl.BlockSpec((LANES, d), lambda j: (COL_DEC // LANES, 0)),
            pl.BlockSpec((pl.Element(tr), pl.Element(d)),
                         lambda j: (pl.multiple_of(COL_GATE + tr * gate_blk(j), 32), 0)),
        ],
        out_specs=[
            pl.BlockSpec((d, tr), lambda j: (0, j)),
            pl.BlockSpec((d, LANES), lambda j: (0, 0)),
            pl.BlockSpec((d, tr), lambda j: (0, gate_blk(j))),
        ],
        out_shape=[jax.ShapeDtypeStruct((d, w), BF16) for w in (COL_DEC, LANES, n - COL_GATE)],
        compiler_params=pltpu.CompilerParams(
            dimension_semantics=("arbitrary",), vmem_limit_bytes=VMEM_LIMIT),
        name="split_w_in",
    )(w_in_t, w_in_t, w_in_t)


def _prenorm_mod(x, g, mod_ref):
    sh = mod_ref[0:1, :]
    sc = mod_ref[1:2, :]
    return (_rms(x, g) * (1.0 + sc) + sh).astype(BF16)


def _decay_logs(z, wdec_ref, bdec_ref):
    z_hi, z_lo = z
    logits = _dot(z_hi, wdec_ref[...]) + _dot(z_lo, wdec_ref[...]) + bdec_ref[...]
    ls = jnp.minimum(logits, 0.0) - jnp.log(1.0 + jnp.exp(-jnp.abs(logits)))
    return ls * (1.0 / GATE_NORM)


N_LATE_W = 5


def _proj_kernel(x_ref, mod_ref, g_ref, perm_ref, wm_ref, wdin_ref, wdec_ref, bdec_ref, wg_ref,
                 *refs):
    late_f32 = refs[:N_LATE_W]
    (ah_ref, av_ref, q_ref, k_ref, v_ref, rs_ref, laf_ref, lab_ref,
     gt_ref) = refs[N_LATE_W:-N_LATE_W]
    late_bf16 = refs[-N_LATE_W:]
    for src, dst in zip(late_f32, late_bf16):
        dst[...] = src[...].astype(BF16)
    subs = [slice(s, s + PROJ_SUB) for s in range(0, x_ref.shape[0], PROJ_SUB)]
    g = g_ref[...]
    ub = [_prenorm_mod(x_ref[s, :], g, mod_ref) for s in subs]
    z = [_split_bf16(_dot(u, wdin_ref[...])) for u in ub]
    a_h = []
    for n, s in enumerate(subs):
        glu = _dot(ub[n], wm_ref[:, 0:COL_Q])
        a = (glu[:, :CONV_DIM] * _sigmoid(glu[:, CONV_DIM:])).astype(BF16)
        av_ref[s, :] = a[:, CONV_HALF:]
        a_h.append(a[:, :CONV_HALF])
    for n, s in enumerate(subs):
        q_ref[s, :] = (_dot(ub[n], wm_ref[:, COL_Q:COL_K]) * (GLA_DK ** -0.5)).astype(BF16)
    for n, s in enumerate(subs):
        k_ref[s, :] = _dot(ub[n], wm_ref[:, COL_K:COL_V]).astype(BF16)
    for n, s in enumerate(subs):
        v_ref[s, :] = _dot(ub[n], wm_ref[:, COL_V:COL_R]).astype(BF16)
    for n, s in enumerate(subs):
        la = _decay_logs(z[n], wdec_ref, bdec_ref)
        laf_ref[s, :] = la[:, :HK]
        lab_ref[s, :] = la[:, HK:]
    for n, s in enumerate(subs):
        r = _dot(ub[n], wm_ref[:, COL_R:COL_DEC])
        rs_ref[s, :] = (r * _sigmoid(r)).astype(BF16)
    per_group = PERM_TOKENS // PROJ_SUB
    for n in range(x_ref.shape[0] // PERM_TOKENS):
        a_rm = jnp.concatenate(a_h[per_group * n:per_group * (n + 1)], axis=0)
        a_cm = _dot(perm_ref[...], a_rm)
        ah_ref[:, PERM_ROWS * n:PERM_ROWS * (n + 1), :] = a_cm.reshape(
            GRID_W, PERM_ROWS, CONV_HALF)
    for n, s in enumerate(subs):
        gt_ref[s, :] = _sigmoid(_dot(ub[n], wg_ref[...])).astype(BF16)


def _const_spec(arr):
    nd = arr.ndim
    return pl.BlockSpec(arr.shape, lambda *_: (0,) * nd)


def _tok_spec(tm, width):
    return pl.BlockSpec((None, tm, width), lambda b, i: (b, i, 0))


def _colmajor_spec(tm):
    return pl.BlockSpec((None, GRID_W, tm // GRID_W, CONV_HALF), lambda b, i: (b, 0, i, 0))


def _single_spec(arr):
    nd = arr.ndim
    return pl.BlockSpec(arr.shape, lambda *_: (0,) * nd, pipeline_mode=pl.Buffered(1))


def _proj(x, mod, g, perm, wm, wdin, wdec, bdec, wg, late_weights, tm):
    bsz, t, d = x.shape
    widths = (CONV_HALF, HK, HK, HV, HV, HK, HK, 2 * D_MODEL)
    dtypes = (BF16, BF16, BF16, BF16, BF16, F32, F32, BF16)
    assert tm % PERM_TOKENS == 0 and len(late_weights) == N_LATE_W
    ah_shape = jax.ShapeDtypeStruct((bsz, GRID_W, t // GRID_W, CONV_HALF), F32)
    n_i = t // tm
    n_steps = bsz * n_i
    late_in, late_out, late_shape = [], [], []
    for w in late_weights:
        _, rows, cols = w.shape
        assert rows % (16 * n_steps) == 0
        blk = rows // n_steps
        late_in.append(pl.BlockSpec((None, blk, cols), lambda b, i: (0, b * n_i + i, 0)))
        late_out.append(pl.BlockSpec((blk, cols), lambda b, i: (b * n_i + i, 0)))
        late_shape.append(jax.ShapeDtypeStruct((rows, cols), BF16))
    return pl.pallas_call(
        _proj_kernel,
        grid=(bsz, n_i),
        in_specs=[
            _tok_spec(tm, d),
            pl.BlockSpec((None,) + mod.shape[1:], lambda b, i: (b, 0, 0)),
            _const_spec(g), _single_spec(perm), _single_spec(wm), _single_spec(wdin),
            _single_spec(wdec), _const_spec(bdec), _single_spec(wg),
        ] + late_in,
        out_specs=[_colmajor_spec(tm)] + [_tok_spec(tm, w) for w in widths] + late_out,
        out_shape=[ah_shape] + [jax.ShapeDtypeStruct((bsz, t, w), dt)
                                for w, dt in zip(widths, dtypes)] + late_shape,
        compiler_params=pltpu.CompilerParams(
            dimension_semantics=("parallel", "parallel"), vmem_limit_bytes=VMEM_LIMIT),
        name="proj",
    )(x, mod, g, perm, wm, wdin, wdec, bdec, wg, *late_weights)


def _gla_scale(qs, ks, las, dirn, want_out):
    tri_b, mid, last = dirn["tri_b"], dirn["mid"], dirn["last"]
    la_hi, la_lo = _split_bf16(las)
    g = _dot(tri_b, la_hi) + _dot(tri_b, la_lo)
    u = dict(kt=[], kend=[], qt=[], qg=[], e=[])
    for c in range(SUPER // CHUNK):
        rows = slice(c * CHUNK, (c + 1) * CHUNK)
        gc = g[rows]
        g_mid = gc[mid:mid + 1]
        g_last = gc[last:last + 1]
        u["e"].append(jnp.exp(g_last))
        kt = ks[rows].astype(F32) * jnp.exp(g_mid - gc)
        u["kt"].append(kt)
        u["kend"].append(kt * jnp.exp(g_last - g_mid))
        if want_out:
            qt = qs[rows].astype(F32) * jnp.exp(gc - g_mid)
            u["qt"].append(qt.astype(BF16))
            u["qg"].append((qt * jnp.exp(g_mid)).astype(BF16))
    return u


def _pair_lanes(pair):
    return slice(LANES * pair, LANES * (pair + 1))


def _gla_scores(u, pair, head_lane):
    lanes = _pair_lanes(pair)
    kt = jnp.concatenate([c[:, lanes] for c in u["kt"]], axis=0)
    qt = jnp.concatenate([c[:, lanes] for c in u["qt"]], axis=0)
    kt2 = jnp.concatenate([(kt * hm).astype(BF16) for hm in head_lane], axis=0)
    return lax.dot_general(qt, kt2, NT, preferred_element_type=F32)


def _gla_update(u, vs, h, head_lane):
    lanes = _pair_lanes(h // 2)
    hm = head_lane[h % 2]
    k0 = (u["kend"][0][:, lanes] * hm).astype(BF16)
    k1 = (u["kend"][1][:, lanes] * hm).astype(BF16)
    z = jnp.zeros_like(k0)
    kbd = jnp.concatenate([jnp.concatenate([k0, z], axis=1),
                           jnp.concatenate([z, k1], axis=1)], axis=0)
    return lax.dot_general(vs[:, GLA_DV * h:GLA_DV * (h + 1)], kbd, TN,
                           preferred_element_type=F32)


def _gla_intra(scores, vs, pair, mask2):
    a = jnp.where(mask2, scores, 0.0).astype(BF16)
    v0 = vs[:, GLA_DV * 2 * pair:GLA_DV * (2 * pair + 1)]
    v1 = vs[:, GLA_DV * (2 * pair + 1):GLA_DV * (2 * pair + 2)]
    z = jnp.zeros_like(v0)
    vbd = jnp.concatenate([jnp.concatenate([v0, z], axis=1),
                           jnp.concatenate([z, v1], axis=1)], axis=0)
    return _dot(a, vbd)


def _gla_kernel(q_ref, k_ref, v_ref, laf_ref, lab_ref, rs_ref, g_ref,
                ctx_ref, modc_ref, gpre_ref, wm_ref, wdin_ref, wdec_ref, bdec_ref,
                o_ref, of_ref, ob_ref, sf_ref, sb_ref, kc_ref, vc_ref, lafc_ref, labc_ref):
    t = q_ref.shape[0]
    t_ctx = kc_ref.shape[0]

    ubc = _prenorm_mod(ctx_ref[...], gpre_ref[...], modc_ref)
    zc = _split_bf16(_dot(ubc, wdin_ref[...]))
    kc_ref[...] = _dot(ubc, wm_ref[:, COL_K:COL_V]).astype(BF16)
    vc_ref[...] = _dot(ubc, wm_ref[:, COL_V:COL_R]).astype(BF16)
    lac = _decay_logs(zc, wdec_ref, bdec_ref)
    lafc_ref[...] = lac[:, :HK]
    labc_ref[...] = lac[:, HK:]

    half = t // 2
    n_steps = t // STEP
    supers = STEP // SUPER
    n_chunks = SUPER // CHUNK
    pairs = GLA_HEADS // 2

    row = lax.broadcasted_iota(jnp.int32, (SUPER, SUPER), 0)
    col = lax.broadcasted_iota(jnp.int32, (SUPER, SUPER), 1)
    same_chunk = (row // CHUNK) == (col // CHUNK)
    low_mask = jnp.logical_and(same_chunk, col <= row)
    up_mask = jnp.logical_and(same_chunk, col >= row)
    lane = lax.broadcasted_iota(jnp.int32, (1, LANES), 1)
    head_lane = (jnp.where(lane < GLA_DK, 1.0, 0.0), jnp.where(lane >= GLA_DK, 1.0, 0.0))

    fwd = dict(tri_b=jnp.where(low_mask, 1.0, 0.0).astype(BF16),
               mask2=jnp.concatenate([low_mask, low_mask], axis=1),
               mid=CHUNK // 2 - 1, last=CHUNK - 1, order=(0, 1), st_ref=sf_ref)
    bwd = dict(tri_b=jnp.where(up_mask, 1.0, 0.0).astype(BF16),
               mask2=jnp.concatenate([up_mask, up_mask], axis=1),
               mid=CHUNK // 2, last=0, order=(1, 0), st_ref=sb_ref)

    sf_ref[...] = jnp.zeros_like(sf_ref)
    sb_ref[...] = jnp.zeros_like(sb_ref)
    gn = g_ref[...]

    def run(units, want_out, finalize):
        vals = []
        for dirn, refs, t0 in units:
            q_r, k_r, v_r, la_r = refs
            rows = pl.ds(t0, SUPER)
            qs = q_r[rows, :] if want_out else None
            vals.append((_gla_scale(qs, k_r[rows, :], la_r[rows, :], dirn, want_out),
                         v_r[rows, :]))
        scores = []
        if want_out:
            scores = [[_gla_scores(u, p, head_lane) for p in range(pairs)] for u, _ in vals]
        upd = [[_gla_update(u, vs, h, head_lane) for h in range(GLA_HEADS)] for u, vs in vals]
        intra = []
        if want_out:
            intra = [[_gla_intra(scores[n][p], vals[n][1], p, units[n][0]["mask2"])
                      for p in range(pairs)] for n in range(len(units))]
        states = {}
        for n, (dirn, _, t0) in enumerate(units):
            key = id(dirn["st_ref"])
            if key not in states:
                states[key] = (dirn["st_ref"], [dirn["st_ref"][h] for h in range(GLA_HEADS)])
            st = states[key][1]
            u = vals[n][0]
            inter = [[None] * n_chunks for _ in range(pairs)]
            for c in dirn["order"]:
                if want_out:
                    for p in range(pairs):
                        st2 = jnp.concatenate([st[2 * p].astype(BF16),
                                               st[2 * p + 1].astype(BF16)], axis=0)
                        inter[p][c] = lax.dot_general(u["qg"][c][:, _pair_lanes(p)], st2, NT,
                                                      preferred_element_type=F32)
                for h in range(GLA_HEADS):
                    e = u["e"][c][:, _pair_lanes(h // 2)]
                    st[h] = st[h] * e + upd[n][h][:, LANES * c:LANES * (c + 1)]
            if not want_out:
                continue
            fwd_unit = dirn is fwd
            for p in range(pairs):
                o2 = intra[n][p] + jnp.concatenate(inter[p], axis=0)
                for j in range(2):
                    cols = slice(GLA_DV * (2 * p + j), GLA_DV * (2 * p + j + 1))
                    o = o2[:, GLA_DV * j:GLA_DV * (j + 1)]
                    if finalize:
                        other = (ob_ref[pl.ds(t0 - half, SUPER), cols] if fwd_unit
                                 else of_ref[pl.ds(t0, SUPER), cols])
                        rs = rs_ref[pl.ds(t0, SUPER), cols].astype(F32)
                        o_ref[pl.ds(t0, SUPER), cols] = (_rms(o + other, gn) * rs).astype(BF16)
                    elif fwd_unit:
                        of_ref[pl.ds(t0, SUPER), cols] = o
                    else:
                        ob_ref[pl.ds(t0 - half, SUPER), cols] = o
        for st_ref, st in states.values():
            for h in range(GLA_HEADS):
                st_ref[h] = st[h]

    ctx_units = []
    for s in range(t_ctx // SUPER):
        ctx_units.append((fwd, (None, kc_ref, vc_ref, lafc_ref), s * SUPER))
        ctx_units.append((bwd, (None, kc_ref, vc_ref, labc_ref), t_ctx - (s + 1) * SUPER))
    run(ctx_units, want_out=False, finalize=False)

    def step(i, finalize):
        units = []
        for s in range(supers):
            t0 = pl.multiple_of(i * STEP + s * SUPER, SUPER)
            t1 = pl.multiple_of(t - (i + 1) * STEP + (supers - 1 - s) * SUPER, SUPER)
            units.append((fwd, (q_ref, k_ref, v_ref, laf_ref), t0))
            units.append((bwd, (q_ref, k_ref, v_ref, lab_ref), t1))
        run(units, want_out=True, finalize=finalize)

    def park_body(i, carry):
        step(i, False)
        return carry

    def finish_body(i, carry):
        step(i, True)
        return carry

    lax.fori_loop(0, n_steps // 2, park_body, 0)
    lax.fori_loop(n_steps // 2, n_steps, finish_body, 0)


def _gla(q, k, v, laf, lab, rs, gnorm, ctx, mod_ctx, gpre, wm, wdin, wdec, bdec):
    bsz, t, _ = q.shape
    t_ctx = ctx.shape[1]
    assert t % (2 * STEP) == 0 and t_ctx % SUPER == 0

    def bspec(arr):
        return pl.BlockSpec((None,) + arr.shape[1:], lambda b: (b, 0, 0))

    args = (q, k, v, laf, lab, rs)
    return pl.pallas_call(
        _gla_kernel,
        grid=(bsz,),
        in_specs=[bspec(a) for a in args] + [_const_spec(gnorm), bspec(ctx),
                  pl.BlockSpec((None,) + mod_ctx.shape[1:], lambda b: (0, 0, 0)),
                  _const_spec(gpre), _single_spec(wm), _single_spec(wdin), _single_spec(wdec),
                  _const_spec(bdec)],
        out_specs=pl.BlockSpec((None, t, HV), lambda b: (b, 0, 0)),
        out_shape=jax.ShapeDtypeStruct((bsz, t, HV), BF16),
        scratch_shapes=[
            pltpu.VMEM((t // 2, HV), F32), pltpu.VMEM((t // 2, HV), F32),
            pltpu.VMEM((GLA_HEADS, GLA_DV, LANES), F32),
            pltpu.VMEM((GLA_HEADS, GLA_DV, LANES), F32),
            pltpu.VMEM((t_ctx, HK), BF16), pltpu.VMEM((t_ctx, HV), BF16),
            pltpu.VMEM((t_ctx, HK), F32), pltpu.VMEM((t_ctx, HK), F32),
        ],
        compiler_params=pltpu.CompilerParams(
            dimension_semantics=("parallel",), vmem_limit_bytes=VMEM_LIMIT),
        name="gla",
    )(*args, gnorm, ctx, mod_ctx, gpre, wm, wdin, wdec, bdec)


def _tap_segments(n_tiles, lines_per_tile, n_lines, group):
    segs = []
    for t0 in range(0, n_tiles, group):
        t1 = min(t0 + group, n_tiles)
        lo_line = int(t0 * lines_per_tile)
        hi_line = int(-(-t1 * lines_per_tile // 1)) - 1
        k_lo = max(0, CONV_PAD - hi_line)
        k_hi = min(CONV_K, n_lines + CONV_PAD - lo_line)
        if segs and segs[-1][2:] == (k_lo, k_hi):
            segs[-1] = (segs[-1][0], t1, k_lo, k_hi)
        else:
            segs.append((t0, t1, k_lo, k_hi))
    return segs


def _conv_kernel(ah_ref, av_ref, w_ref, b_ref, yh_ref, yv_ref, ph_ref, pv_ref):
    t = ah_ref.shape[0]
    rows = t // GRID_W
    tile = 64

    def conv(src_ref, pad_ref, w_cols, line, n_lines, group, dst_ref):
        pad = CONV_PAD * line
        pad_ref[pl.ds(0, pad), :] = jnp.zeros((pad, CONV_HALF), F32)
        pad_ref[pl.ds(pad + t, pad), :] = jnp.zeros((pad, CONV_HALF), F32)

        def fill(i, carry):
            t0 = pl.multiple_of(i * tile, tile)
            pad_ref[pl.ds(pl.multiple_of(pad + t0, 8), tile), :] = (
                src_ref[pl.ds(t0, tile), :].astype(F32))
            return carry

        lax.fori_loop(0, t // tile, fill, 0)
        bias = b_ref[:, w_cols]

        for first, last, k_lo, k_hi in _tap_segments(t // tile, tile / line, n_lines, group):
            def body(i, carry, k_lo=k_lo, k_hi=k_hi):
                t0 = pl.multiple_of(i * tile, tile)
                acc = jnp.broadcast_to(bias, (tile, CONV_HALF))
                for kk in range(k_lo, k_hi):
                    win = pad_ref[pl.ds(pl.multiple_of(t0 + kk * line, 8), tile), :]
                    acc = acc + w_ref[kk:kk + 1, w_cols] * win
                dst_ref[pl.ds(t0, tile), :] = acc
                return carry

            lax.fori_loop(first, last, body, 0)

    conv(ah_ref, ph_ref, slice(0, CONV_HALF), rows, GRID_W, 2, yh_ref)
    conv(av_ref, pv_ref, slice(CONV_HALF, CONV_DIM), GRID_W, rows, 4, yv_ref)


def _conv(ah, av, w, b):
    bsz, t, c = ah.shape
    rows = t // GRID_W
    tok = pl.BlockSpec((None, t, c), lambda i: (i, 0, 0))
    return pl.pallas_call(
        _conv_kernel,
        grid=(bsz,),
        in_specs=[tok, tok, _const_spec(w), _const_spec(b)],
        out_specs=[tok, tok],
        out_shape=[jax.ShapeDtypeStruct((bsz, t, c), F32)] * 2,
        scratch_shapes=[
            pltpu.VMEM((t + 2 * CONV_PAD * rows, c), F32),
            pltpu.VMEM((t + 2 * CONV_PAD * GRID_W, c), F32),
        ],
        compiler_params=pltpu.CompilerParams(
            dimension_semantics=("parallel",), vmem_limit_bytes=VMEM_LIMIT),
        name="conv",
    )(ah, av, w, b)


def _post_kernel(x_ref, yh_ref, yv_ref, og_ref, gt_ref, mod_ref, perm_ref, lng_ref, lnb_ref,
                 gpost1_ref, gpre2_ref, gpost2_ref, wco_ref, wgo_ref, wout_ref, wff1_ref,
                 wff2_ref, o_ref):
    d = x_ref.shape[-1]
    d_ff = wff1_ref.shape[1]
    ff_tile = 1024
    sub = 256
    subs = [slice(s, s + sub) for s in range(0, x_ref.shape[0], sub)]
    gt1 = mod_ref[2:3, :]
    sh2 = mod_ref[3:4, :]
    sc2 = mod_ref[4:5, :]
    gt2 = mod_ref[5:6, :]

    yh = []
    for n in range(x_ref.shape[0] // PERM_TOKENS):
        blk = yh_ref[:, PERM_ROWS * n:PERM_ROWS * (n + 1), :].reshape(PERM_TOKENS, CONV_HALF)
        hi, lo = _split_bf16(blk)
        yh.append(_dot(perm_ref[...], hi) + _dot(perm_ref[...], lo))
    yh = jnp.concatenate(yh, axis=0)

    yn = []
    for s in subs:
        yc = jnp.concatenate([yh[s], yv_ref[s, :]], axis=-1)
        yc = yc - jnp.mean(yc, axis=-1, keepdims=True)
        var = jnp.mean(yc * yc, axis=-1, keepdims=True)
        z = yc * lax.rsqrt(var + EPS) * lng_ref[...] + lnb_ref[...]
        yn.append((z * _sigmoid(z)).astype(BF16))
    y_gla = [_dot(og_ref[s, :], wgo_ref[...]) for s in subs]
    y_conv = [_dot(v, wco_ref[...]) for v in yn]
    merged = []
    for n, s in enumerate(subs):
        gates = gt_ref[s, :].astype(F32)
        merged.append((gates[:, :d] * y_conv[n] + gates[:, d:] * y_gla[n]).astype(BF16))
    y = [_dot(v, wout_ref[...]) for v in merged]
    h_mid = [x_ref[s, :] + gt1 * _rms(y[n], gpost1_ref[...]) for n, s in enumerate(subs)]
    u2 = [(_rms(h, gpre2_ref[...]) * (1.0 + sc2) + sh2).astype(BF16) for h in h_mid]
    acc = [jnp.zeros(h.shape, F32) for h in h_mid]
    for j in range(d_ff // ff_tile):
        cols = slice(j * ff_tile, (j + 1) * ff_tile)
        f = [_dot(v, wff1_ref[:, cols]) for v in u2]
        f = [jnp.square(jnp.maximum(v, 0.0)).astype(BF16) for v in f]
        acc = [acc[n] + _dot(f[n], wff2_ref[cols, :]) for n in range(len(subs))]
    for n, s in enumerate(subs):
        o_ref[s, :] = h_mid[n] + gt2 * _rms(acc[n], gpost2_ref[...])


def _post(x, yh, yv, og, gates, mod, perm, ln_g, ln_b, gpost1, gpre2, gpost2, wco, wgo, wout,
          wff1, wff2, tm):
    bsz, t, d = x.shape
    assert tm % PERM_TOKENS == 0
    return pl.pallas_call(
        _post_kernel,
        grid=(bsz, t // tm),
        in_specs=[
            _tok_spec(tm, d), _colmajor_spec(tm), _tok_spec(tm, CONV_HALF), _tok_spec(tm, HV),
            _tok_spec(tm, 2 * d),
            pl.BlockSpec((None,) + mod.shape[1:], lambda b, i: (b, 0, 0)),
            _single_spec(perm), _const_spec(ln_g), _const_spec(ln_b),
            _const_spec(gpost1), _const_spec(gpre2), _const_spec(gpost2),
            _single_spec(wco), _single_spec(wgo), _single_spec(wout), _single_spec(wff1),
            _single_spec(wff2),
        ],
        out_specs=_tok_spec(tm, d),
        out_shape=jax.ShapeDtypeStruct((bsz, t, d), F32),
        compiler_params=pltpu.CompilerParams(
            dimension_semantics=("parallel", "parallel"), vmem_limit_bytes=VMEM_LIMIT),
        name="post",
    )(x, yh, yv, og, gates, mod, perm, ln_g, ln_b, gpost1, gpre2, gpost2, wco, wgo, wout, wff1,
      wff2)


def kernel(x, c, ctx, c_ctx, w_mod, b_mod, g_pre1, g_post1, g_pre2, g_post2, w_in, conv_w,
           conv_b, conv_ln_g, conv_ln_b, w_conv_out, w_decay, b_decay, gla_norm_g, w_gla_out,
           w_out, w_ff1, w_ff2):
    bsz, t, d = x.shape
    depth = w_in.shape[0]
    assert depth == 1 and d == D_MODEL and t % (GRID_W * 8) == 0

    n_rows = -(-(bsz + 1) // 8) * 8
    c_all = jnp.zeros((n_rows, d), F32).at[:bsz].set(c).at[bsz].set(c_ctx)
    m = _adaln(c_all, w_mod, b_mod)
    mod = m[:bsz].reshape(bsz, N_MOD, d)
    mod_ctx = m[bsz, :2 * d].reshape(1, 2, d)

    wm, wdin, wg = _split_w_in(jnp.swapaxes(w_in[0], 0, 1))
    wdec = jnp.zeros((LANES, 2 * HK), F32)
    wdec = wdec.at[:DECAY_RANK, :HK].set(w_decay[0, 0])
    wdec = wdec.at[DECAY_RANK:2 * DECAY_RANK, HK:].set(w_decay[0, 1]).astype(BF16)
    bdec = b_decay[0].reshape(1, 2 * HK)
    row = lambda v: v.reshape(1, -1)

    perm = _to_colmajor_perm()
    to_colmajor = jnp.asarray(perm, BF16)
    to_rowmajor = jnp.asarray(perm.T, BF16)

    ah, av, q, k, v, rs, laf, lab, gates, wco, wgo, wout, wff1, wff2 = _proj(
        x, mod, row(g_pre1[0]), to_colmajor, wm, wdin, wdec, bdec, wg,
        (w_conv_out, w_gla_out, w_out, w_ff1, w_ff2), tm=1024)
    og = _gla(q, k, v, laf, lab, rs, row(gla_norm_g[0]),
              ctx, mod_ctx, row(g_pre1[0]), wm, wdin, wdec, bdec)
    yh, yv = _conv(ah.reshape(bsz, t, CONV_HALF), av, conv_w[0], row(conv_b[0]))
    yh = yh.reshape(bsz, GRID_W, t // GRID_W, CONV_HALF)

    return _post(x, yh, yv, og, gates, mod, to_rowmajor, row(conv_ln_g[0]), row(conv_ln_b[0]),
                 row(g_post1[0]), row(g_pre2[0]), row(g_post2[0]),
                 wco, wgo, wout, wff1, wff2, tm=512)
```

```python
import functools

import jax
import jax.numpy as jnp
import numpy as np
from jax import lax
from jax.experimental import pallas as pl
from jax.experimental.pallas import tpu as pltpu

D_MODEL = 1024
GRID_W = 64
CONV_DIM = 512
CONV_K = 31
GLA_HEADS = 4
GLA_DK = 64
GLA_DV = 128
DECAY_RANK = 16
GATE_NORM = 16.0
CHUNK = 64
N_MOD = 6
EPS = 1e-6

COL_Q = 2 * CONV_DIM
COL_K = COL_Q + GLA_HEADS * GLA_DK
COL_V = COL_K + GLA_HEADS * GLA_DK
COL_R = COL_V + GLA_HEADS * GLA_DV
COL_DEC = COL_R + GLA_HEADS * GLA_DV
COL_GATE = COL_DEC + 2 * DECAY_RANK
COL_END = COL_GATE + 2 * D_MODEL

LANES = 128
HK = GLA_HEADS * GLA_DK
HV = GLA_HEADS * GLA_DV
CONV_HALF = CONV_DIM // 2
CONV_PAD = CONV_K // 2
PERM_ROWS = 8
PERM_TOKENS = PERM_ROWS * GRID_W
PROJ_SUB = 256
SUPER = 2 * CHUNK
STEP = 8 * SUPER

VMEM_LIMIT = 56 * 1024 * 1024

F32 = jnp.float32
BF16 = jnp.bfloat16

NT = (((1,), (1,)), ((), ()))
TN = (((0,), (0,)), ((), ()))


def _dot(a, b):
    return jnp.dot(a, b, preferred_element_type=F32)


def _rms(x, g):
    ms = jnp.mean(x * x, axis=-1, keepdims=True)
    return x * lax.rsqrt(ms + EPS) * g


def _sigmoid(x):
    return jax.nn.sigmoid(x)


def _split_bf16(x):
    hi = x.astype(BF16)
    lo = (x - hi.astype(F32)).astype(BF16)
    return hi, lo


def _to_colmajor_perm():
    out = np.arange(PERM_TOKENS)
    src = (out % PERM_ROWS) * GRID_W + out // PERM_ROWS
    p = np.zeros((PERM_TOKENS, PERM_TOKENS), np.float32)
    p[out, src] = 1.0
    return p


def _adaln_kernel(c_ref, w_ref, b_ref, o_ref):
    @pl.when(pl.program_id(0) == 0)
    def _():
        o_ref[...] = jnp.broadcast_to(b_ref[...], o_ref.shape)

    c = c_ref[...]
    s_hi, s_lo = _split_bf16(c * _sigmoid(c))
    w_hi, w_lo = _split_bf16(w_ref[...])
    o_ref[...] += _dot(s_hi, w_hi) + _dot(s_lo, w_hi) + _dot(s_hi, w_lo)


def _adaln(c_all, w_mod, b_mod):
    n_rows, d = c_all.shape
    n_out = w_mod.shape[-1]
    tk = 256
    return pl.pallas_call(
        _adaln_kernel,
        grid=(d // tk,),
        in_specs=[
            pl.BlockSpec((n_rows, tk), lambda k: (0, k)),
            pl.BlockSpec((None, tk, n_out), lambda k: (0, k, 0)),
            pl.BlockSpec((1, n_out), lambda k: (0, 0)),
        ],
        out_specs=pl.BlockSpec((n_rows, n_out), lambda k: (0, 0)),
        out_shape=jax.ShapeDtypeStruct((n_rows, n_out), F32),
        compiler_params=pltpu.CompilerParams(
            dimension_semantics=("arbitrary",), vmem_limit_bytes=VMEM_LIMIT),
        name="adaln",
    )(c_all, w_mod, b_mod)


def _split_w_in_kernel(wm_t_ref, wd_t_ref, wg_t_ref, wm_ref, wdin_ref, wg_ref, *, n_gate):
    j = pl.program_id(0)
    wm_ref[...] = wm_t_ref[...].astype(BF16).T

    @pl.when(j == 0)
    def _():
        wdin_ref[...] = wd_t_ref[...].astype(BF16).T

    @pl.when(j < n_gate)
    def _():
        wg_ref[...] = wg_t_ref[...].astype(BF16).T


def _split_w_in(w_in_t):
    n, d = w_in_t.shape
    tr = 512
    n_main = COL_DEC // tr
    n_gate = (n - COL_GATE) // tr
    assert COL_DEC % tr == 0 and (n - COL_GATE) % tr == 0 and n_gate <= n_main
    gate_blk = lambda j: jnp.minimum(j, n_gate - 1)
    return pl.pallas_call(
        functools.partial(_split_w_in_kernel, n_gate=n_gate),
        grid=(n_main,),
        in_specs=[
            pl.BlockSpec((tr, d), lambda j: (j, 0)),
            pl.BlockSpec((LANES, d), lambda j: (COL_DEC // LANES, 0)),
            pl.BlockSpec((pl.Element(tr), pl.Element(d)),
                         lambda j: (pl.multiple_of(COL_GATE + tr * gate_blk(j), 32), 0)),
        ],
        out_specs=[
            pl.BlockSpec((d, tr), lambda j: (0, j)),
            pl.BlockSpec((d, LANES), lambda j: (0, 0)),
            pl.BlockSpec((d, tr), lambda j: (0, gate_blk(j))),
        ],
        out_shape=[jax.ShapeDtypeStruct((d, w), BF16) for w in (COL_DEC, LANES, n - COL_GATE)],
        compiler_params=pltpu.CompilerParams(
            dimension_semantics=("arbitrary",), vmem_limit_bytes=VMEM_LIMIT),
        name="split_w_in",
    )(w_in_t, w_in_t, w_in_t)


def _prenorm_mod(x, g, mod_ref):
    sh = mod_ref[0:1, :]
    sc = mod_ref[1:2, :]
    return (_rms(x, g) * (1.0 + sc) + sh).astype(BF16)


def _decay_logs(z, wdec_ref, bdec_ref):
    z_hi, z_lo = z
    logits = _dot(z_hi, wdec_ref[...]) + _dot(z_lo, wdec_ref[...]) + bdec_ref[...]
    ls = jnp.minimum(logits, 0.0) - jnp.log(1.0 + jnp.exp(-jnp.abs(logits)))
    return ls * (1.0 / GATE_NORM)


N_LATE_W = 5


def _proj_kernel(x_ref, mod_ref, g_ref, perm_ref, wm_ref, wdin_ref, wdec_ref, bdec_ref, wg_ref,
                 *refs):
    late_f32 = refs[:N_LATE_W]
    (ah_ref, av_ref, q_ref, k_ref, v_ref, rs_ref, laf_ref, lab_ref,
     gt_ref) = refs[N_LATE_W:-N_LATE_W]
    late_bf16 = refs[-N_LATE_W:]
    for src, dst in zip(late_f32, late_bf16):
        dst[...] = src[...].astype(BF16)
    subs = [slice(s, s + PROJ_SUB) for s in range(0, x_ref.shape[0], PROJ_SUB)]
    g = g_ref[...]
    ub = [_prenorm_mod(x_ref[s, :], g, mod_ref) for s in subs]
    z = [_split_bf16(_dot(u, wdin_ref[...])) for u in ub]
    a_h = []
    for n, s in enumerate(subs):
        glu = _dot(ub[n], wm_ref[:, 0:COL_Q])
        a = (glu[:, :CONV_DIM] * _sigmoid(glu[:, CONV_DIM:])).astype(BF16)
        av_ref[s, :] = a[:, CONV_HALF:]
        a_h.append(a[:, :CONV_HALF])
    for n, s in enumerate(subs):
        q_ref[s, :] = (_dot(ub[n], wm_ref[:, COL_Q:COL_K]) * (GLA_DK ** -0.5)).astype(BF16)
    for n, s in enumerate(subs):
        k_ref[s, :] = _dot(ub[n], wm_ref[:, COL_K:COL_V]).astype(BF16)
    for n, s in enumerate(subs):
        v_ref[s, :] = _dot(ub[n], wm_ref[:, COL_V:COL_R]).astype(BF16)
    for n, s in enumerate(subs):
        la = _decay_logs(z[n], wdec_ref, bdec_ref)
        laf_ref[s, :] = la[:, :HK]
        lab_ref[s, :] = la[:, HK:]
    for n, s in enumerate(subs):
        r = _dot(ub[n], wm_ref[:, COL_R:COL_DEC])
        rs_ref[s, :] = (r * _sigmoid(r)).astype(BF16)
    per_group = PERM_TOKENS // PROJ_SUB
    for n in range(x_ref.shape[0] // PERM_TOKENS):
        a_rm = jnp.concatenate(a_h[per_group * n:per_group * (n + 1)], axis=0)
        a_cm = _dot(perm_ref[...], a_rm)
        ah_ref[:, PERM_ROWS * n:PERM_ROWS * (n + 1), :] = a_cm.reshape(
            GRID_W, PERM_ROWS, CONV_HALF)
    for n, s in enumerate(subs):
        gt_ref[s, :] = _sigmoid(_dot(ub[n], wg_ref[...])).astype(BF16)


def _const_spec(arr):
    nd = arr.ndim
    return pl.BlockSpec(arr.shape, lambda *_: (0,) * nd)


def _tok_spec(tm, width):
    return pl.BlockSpec((None, tm, width), lambda b, i: (b, i, 0))


def _colmajor_spec(tm):
    return pl.BlockSpec((None, GRID_W, tm // GRID_W, CONV_HALF), lambda b, i: (b, 0, i, 0))


def _single_spec(arr):
    nd = arr.ndim
    return pl.BlockSpec(arr.shape, lambda *_: (0,) * nd, pipeline_mode=pl.Buffered(1))


def _proj(x, mod, g, perm, wm, wdin, wdec, bdec, wg, late_weights, tm):
    bsz, t, d = x.shape
    widths = (CONV_HALF, HK, HK, HV, HV, HK, HK, 2 * D_MODEL)
    dtypes = (BF16, BF16, BF16, BF16, BF16, F32, F32, BF16)
    assert tm % PERM_TOKENS == 0 and len(late_weights) == N_LATE_W
    ah_shape = jax.ShapeDtypeStruct((bsz, GRID_W, t // GRID_W, CONV_HALF), F32)
    n_i = t // tm
    n_steps = bsz * n_i
    late_in, late_out, late_shape = [], [], []
    for w in late_weights:
        _, rows, cols = w.shape
        assert rows % (16 * n_steps) == 0
        blk = rows // n_steps
        late_in.append(pl.BlockSpec((None, blk, cols), lambda b, i: (0, b * n_i + i, 0)))
        late_out.append(pl.BlockSpec((blk, cols), lambda b, i: (b * n_i + i, 0)))
        late_shape.append(jax.ShapeDtypeStruct((rows, cols), BF16))
    return pl.pallas_call(
        _proj_kernel,
        grid=(bsz, n_i),
        in_specs=[
            _tok_spec(tm, d),
            pl.BlockSpec((None,) + mod.shape[1:], lambda b, i: (b, 0, 0)),
            _const_spec(g), _single_spec(perm), _single_spec(wm), _single_spec(wdin),
            _single_spec(wdec), _const_spec(bdec), _single_spec(wg),
        ] + late_in,
        out_specs=[_colmajor_spec(tm)] + [_tok_spec(tm, w) for w in widths] + late_out,
        out_shape=[ah_shape] + [jax.ShapeDtypeStruct((bsz, t, w), dt)
                                for w, dt in zip(widths, dtypes)] + late_shape,
        compiler_params=pltpu.CompilerParams(
            dimension_semantics=("parallel", "parallel"), vmem_limit_bytes=VMEM_LIMIT),
        name="proj",
    )(x, mod, g, perm, wm, wdin, wdec, bdec, wg, *late_weights)


def _gla_scale(qs, ks, las, dirn, want_out):
    tri_b, mid, last = dirn["tri_b"], dirn["mid"], dirn["last"]
    la_hi, la_lo = _split_bf16(las)
    g = _dot(tri_b, la_hi) + _dot(tri_b, la_lo)
    u = dict(kt=[], kend=[], qt=[], qg=[], e=[])
    for c in range(SUPER // CHUNK):
        rows = slice(c * CHUNK, (c + 1) * CHUNK)
        gc = g[rows]
        g_mid = gc[mid:mid + 1]
        g_last = gc[last:last + 1]
        u["e"].append(jnp.exp(g_last))
        kt = ks[rows].astype(F32) * jnp.exp(g_mid - gc)
        u["kt"].append(kt)
        u["kend"].append(kt * jnp.exp(g_last - g_mid))
        if want_out:
            qt = qs[rows].astype(F32) * jnp.exp(gc - g_mid)
            u["qt"].append(qt.astype(BF16))
            u["qg"].append((qt * jnp.exp(g_mid)).astype(BF16))
    return u


def _pair_lanes(pair):
    return slice(LANES * pair, LANES * (pair + 1))


def _gla_scores(u, pair, head_lane):
    lanes = _pair_lanes(pair)
    kt = jnp.concatenate([c[:, lanes] for c in u["kt"]], axis=0)
    qt = jnp.concatenate([c[:, lanes] for c in u["qt"]], axis=0)
    kt2 = jnp.concatenate([(kt * hm).astype(BF16) for hm in head_lane], axis=0)
    return lax.dot_general(qt, kt2, NT, preferred_element_type=F32)


def _gla_update(u, vs, h, head_lane):
    lanes = _pair_lanes(h // 2)
    hm = head_lane[h % 2]
    k0 = (u["kend"][0][:, lanes] * hm).astype(BF16)
    k1 = (u["kend"][1][:, lanes] * hm).astype(BF16)
    z = jnp.zeros_like(k0)
    kbd = jnp.concatenate([jnp.concatenate([k0, z], axis=1),
                           jnp.concatenate([z, k1], axis=1)], axis=0)
    return lax.dot_general(vs[:, GLA_DV * h:GLA_DV * (h + 1)], kbd, TN,
                           preferred_element_type=F32)


def _gla_intra(scores, vs, pair, mask2):
    a = jnp.where(mask2, scores, 0.0).astype(BF16)
    v0 = vs[:, GLA_DV * 2 * pair:GLA_DV * (2 * pair + 1)]
    v1 = vs[:, GLA_DV * (2 * pair + 1):GLA_DV * (2 * pair + 2)]
    z = jnp.zeros_like(v0)
    vbd = jnp.concatenate([jnp.concatenate([v0, z], axis=1),
                           jnp.concatenate([z, v1], axis=1)], axis=0)
    return _dot(a, vbd)


def _gla_kernel(q_ref, k_ref, v_ref, laf_ref, lab_ref, rs_ref, g_ref,
                ctx_ref, modc_ref, gpre_ref, wm_ref, wdin_ref, wdec_ref, bdec_ref,
                o_ref, of_ref, ob_ref, sf_ref, sb_ref, kc_ref, vc_ref, lafc_ref, labc_ref):
    t = q_ref.shape[0]
    t_ctx = kc_ref.shape[0]

    ubc = _prenorm_mod(ctx_ref[...], gpre_ref[...], modc_ref)
    zc = _split_bf16(_dot(ubc, wdin_ref[...]))
    kc_ref[...] = _dot(ubc, wm_ref[:, COL_K:COL_V]).astype(BF16)
    vc_ref[...] = _dot(ubc, wm_ref[:, COL_V:COL_R]).astype(BF16)
    lac = _decay_logs(zc, wdec_ref, bdec_ref)
    lafc_ref[...] = lac[:, :HK]
    labc_ref[...] = lac[:, HK:]

    half = t // 2
    n_steps = t // STEP
    supers = STEP // SUPER
    n_chunks = SUPER // CHUNK
    pairs = GLA_HEADS // 2

    row = lax.broadcasted_iota(jnp.int32, (SUPER, SUPER), 0)
    col = lax.broadcasted_iota(jnp.int32, (SUPER, SUPER), 1)
    same_chunk = (row // CHUNK) == (col // CHUNK)
    low_mask = jnp.logical_and(same_chunk, col <= row)
    up_mask = jnp.logical_and(same_chunk, col >= row)
    lane = lax.broadcasted_iota(jnp.int32, (1, LANES), 1)
    head_lane = (jnp.where(lane < GLA_DK, 1.0, 0.0), jnp.where(lane >= GLA_DK, 1.0, 0.0))

    fwd = dict(tri_b=jnp.where(low_mask, 1.0, 0.0).astype(BF16),
               mask2=jnp.concatenate([low_mask, low_mask], axis=1),
               mid=CHUNK // 2 - 1, last=CHUNK - 1, order=(0, 1), st_ref=sf_ref)
    bwd = dict(tri_b=jnp.where(up_mask, 1.0, 0.0).astype(BF16),
               mask2=jnp.concatenate([up_mask, up_mask], axis=1),
               mid=CHUNK // 2, last=0, order=(1, 0), st_ref=sb_ref)

    sf_ref[...] = jnp.zeros_like(sf_ref)
    sb_ref[...] = jnp.zeros_like(sb_ref)
    gn = g_ref[...]

    def run(units, want_out, finalize):
        vals = []
        for dirn, refs, t0 in units:
            q_r, k_r, v_r, la_r = refs
            rows = pl.ds(t0, SUPER)
            qs = q_r[rows, :] if want_out else None
            vals.append((_gla_scale(qs, k_r[rows, :], la_r[rows, :], dirn, want_out),
                         v_r[rows, :]))
        scores = []
        if want_out:
            scores = [[_gla_scores(u, p, head_lane) for p in range(pairs)] for u, _ in vals]
        upd = [[_gla_update(u, vs, h, head_lane) for h in range(GLA_HEADS)] for u, vs in vals]
        intra = []
        if want_out:
            intra = [[_gla_intra(scores[n][p], vals[n][1], p, units[n][0]["mask2"])
                      for p in range(pairs)] for n in range(len(units))]
        states = {}
        for n, (dirn, _, t0) in enumerate(units):
            key = id(dirn["st_ref"])
            if key not in states:
                states[key] = (dirn["st_ref"], [dirn["st_ref"][h] for h in range(GLA_HEADS)])
            st = states[key][1]
            u = vals[n][0]
            inter = [[None] * n_chunks for _ in range(pairs)]
            for c in dirn["order"]:
                if want_out:
                    for p in range(pairs):
                        st2 = jnp.concatenate([st[2 * p].astype(BF16),
                                               st[2 * p + 1].astype(BF16)], axis=0)
                        inter[p][c] = lax.dot_general(u["qg"][c][:, _pair_lanes(p)], st2, NT,
                                                      preferred_element_type=F32)
                for h in range(GLA_HEADS):
                    e = u["e"][c][:, _pair_lanes(h // 2)]
                    st[h] = st[h] * e + upd[n][h][:, LANES * c:LANES * (c + 1)]
            if not want_out:
                continue
            fwd_unit = dirn is fwd
            for p in range(pairs):
                o2 = intra[n][p] + jnp.concatenate(inter[p], axis=0)
                for j in range(2):
                    cols = slice(GLA_DV * (2 * p + j), GLA_DV * (2 * p + j + 1))
                    o = o2[:, GLA_DV * j:GLA_DV * (j + 1)]
                    if finalize:
                        other = (ob_ref[pl.ds(t0 - half, SUPER), cols] if fwd_unit
                                 else of_ref[pl.ds(t0, SUPER), cols])
                        rs = rs_ref[pl.ds(t0, SUPER), cols].astype(F32)
                        o_ref[pl.ds(t0, SUPER), cols] = (_rms(o + other, gn) * rs).astype(BF16)
                    elif fwd_unit:
                        of_ref[pl.ds(t0, SUPER), cols] = o
                    else:
                        ob_ref[pl.ds(t0 - half, SUPER), cols] = o
        for st_ref, st in states.values():
            for h in range(GLA_HEADS):
                st_ref[h] = st[h]

    ctx_units = []
    for s in range(t_ctx // SUPER):
        ctx_units.append((fwd, (None, kc_ref, vc_ref, lafc_ref), s * SUPER))
        ctx_units.append((bwd, (None, kc_ref, vc_ref, labc_ref), t_ctx - (s + 1) * SUPER))
    run(ctx_units, want_out=False, finalize=False)

    def step(i, finalize):
        units = []
        for s in range(supers):
            t0 = pl.multiple_of(i * STEP + s * SUPER, SUPER)
            t1 = pl.multiple_of(t - (i + 1) * STEP + (supers - 1 - s) * SUPER, SUPER)
            units.append((fwd, (q_ref, k_ref, v_ref, laf_ref), t0))
            units.append((bwd, (q_ref, k_ref, v_ref, lab_ref), t1))
        run(units, want_out=True, finalize=finalize)

    def park_body(i, carry):
        step(i, False)
        return carry

    def finish_body(i, carry):
        step(i, True)
        return carry

    lax.fori_loop(0, n_steps // 2, park_body, 0)
    lax.fori_loop(n_steps // 2, n_steps, finish_body, 0)


def _gla(q, k, v, laf, lab, rs, gnorm, ctx, mod_ctx, gpre, wm, wdin, wdec, bdec):
    bsz, t, _ = q.shape
    t_ctx = ctx.shape[1]
    assert t % (2 * STEP) == 0 and t_ctx % SUPER == 0

    def bspec(arr):
        return pl.BlockSpec((None,) + arr.shape[1:], lambda b: (b, 0, 0))

    args = (q, k, v, laf, lab, rs)
    return pl.pallas_call(
        _gla_kernel,
        grid=(bsz,),
        in_specs=[bspec(a) for a in args] + [_const_spec(gnorm), bspec(ctx),
                  pl.BlockSpec((None,) + mod_ctx.shape[1:], lambda b: (0, 0, 0)),
                  _const_spec(gpre), _single_spec(wm), _single_spec(wdin), _single_spec(wdec),
                  _const_spec(bdec)],
        out_specs=pl.BlockSpec((None, t, HV), lambda b: (b, 0, 0)),
        out_shape=jax.ShapeDtypeStruct((bsz, t, HV), BF16),
        scratch_shapes=[
            pltpu.VMEM((t // 2, HV), F32), pltpu.VMEM((t // 2, HV), F32),
            pltpu.VMEM((GLA_HEADS, GLA_DV, LANES), F32),
            pltpu.VMEM((GLA_HEADS, GLA_DV, LANES), F32),
            pltpu.VMEM((t_ctx, HK), BF16), pltpu.VMEM((t_ctx, HV), BF16),
            pltpu.VMEM((t_ctx, HK), F32), pltpu.VMEM((t_ctx, HK), F32),
        ],
        compiler_params=pltpu.CompilerParams(
            dimension_semantics=("parallel",), vmem_limit_bytes=VMEM_LIMIT),
        name="gla",
    )(*args, gnorm, ctx, mod_ctx, gpre, wm, wdin, wdec, bdec)


def _tap_segments(n_tiles, lines_per_tile, n_lines, group):
    segs = []
    for t0 in range(0, n_tiles, group):
        t1 = min(t0 + group, n_tiles)
        lo_line = int(t0 * lines_per_tile)
        hi_line = int(-(-t1 * lines_per_tile // 1)) - 1
        k_lo = max(0, CONV_PAD - hi_line)
        k_hi = min(CONV_K, n_lines + CONV_PAD - lo_line)
        if segs and segs[-1][2:] == (k_lo, k_hi):
            segs[-1] = (segs[-1][0], t1, k_lo, k_hi)
        else:
            segs.append((t0, t1, k_lo, k_hi))
    return segs


def _conv_kernel(ah_ref, av_ref, w_ref, b_ref, yh_ref, yv_ref, ph_ref, pv_ref, wb_ref):
    t = ah_ref.shape[0]
    rows = t // GRID_W
    tile = 64
    for kk in range(CONV_K):
        wb_ref[kk] = jnp.broadcast_to(w_ref[kk:kk + 1, :], wb_ref.shape[1:])

    def conv(src_ref, pad_ref, w_cols, line, n_lines, group, dst_ref):
        pad = CONV_PAD * line
        pad_ref[pl.ds(0, pad), :] = jnp.zeros((pad, CONV_HALF), F32)
        pad_ref[pl.ds(pad + t, pad), :] = jnp.zeros((pad, CONV_HALF), F32)

        def fill(i, carry):
            t0 = pl.multiple_of(i * tile, tile)
            pad_ref[pl.ds(pl.multiple_of(pad + t0, 8), tile), :] = (
                src_ref[pl.ds(t0, tile), :].astype(F32))
            return carry

        lax.fori_loop(0, t // tile, fill, 0)
        bias = b_ref[:, w_cols]

        for first, last, k_lo, k_hi in _tap_segments(t // tile, tile / line, n_lines, group):
            def body(i, carry, k_lo=k_lo, k_hi=k_hi):
                t0 = pl.multiple_of(i * tile, tile)
                acc = jnp.broadcast_to(bias, (tile, CONV_HALF))
                for kk in range(k_lo, k_hi):
                    win = pad_ref[pl.ds(pl.multiple_of(t0 + kk * line, 8), tile), :]
                    w8 = wb_ref[kk, :, w_cols]
                    acc = acc + jnp.concatenate([w8] * (tile // 8), axis=0) * win
                dst_ref[pl.ds(t0, tile), :] = acc
                return carry

            lax.fori_loop(first, last, body, 0)

    conv(ah_ref, ph_ref, slice(0, CONV_HALF), rows, GRID_W, 2, yh_ref)
    conv(av_ref, pv_ref, slice(CONV_HALF, CONV_DIM), GRID_W, rows, 4, yv_ref)


def _conv(ah, av, w, b):
    bsz, t, c = ah.shape
    rows = t // GRID_W
    tok = pl.BlockSpec((None, t, c), lambda i: (i, 0, 0))
    return pl.pallas_call(
        _conv_kernel,
        grid=(bsz,),
        in_specs=[tok, tok, _const_spec(w), _const_spec(b)],
        out_specs=[tok, tok],
        out_shape=[jax.ShapeDtypeStruct((bsz, t, c), F32)] * 2,
        scratch_shapes=[
            pltpu.VMEM((t + 2 * CONV_PAD * rows, c), F32),
            pltpu.VMEM((t + 2 * CONV_PAD * GRID_W, c), F32),
            pltpu.VMEM((CONV_K, 8, 2 * c), F32),
        ],
        compiler_params=pltpu.CompilerParams(
            dimension_semantics=("parallel",), vmem_limit_bytes=VMEM_LIMIT),
        name="conv",
    )(ah, av, w, b)


def _post_kernel(x_ref, yh_ref, yv_ref, og_ref, gt_ref, mod_ref, perm_ref, lng_ref, lnb_ref,
                 gpost1_ref, gpre2_ref, gpost2_ref, wco_ref, wgo_ref, wout_ref, wff1_ref,
                 wff2_ref, o_ref):
    d = x_ref.shape[-1]
    d_ff = wff1_ref.shape[1]
    ff_tile = 1024
    sub = 256
    subs = [slice(s, s + sub) for s in range(0, x_ref.shape[0], sub)]
    gt1 = mod_ref[2:3, :]
    sh2 = mod_ref[3:4, :]
    sc2 = mod_ref[4:5, :]
    gt2 = mod_ref[5:6, :]

    yh = []
    for n in range(x_ref.shape[0] // PERM_TOKENS):
        blk = yh_ref[:, PERM_ROWS * n:PERM_ROWS * (n + 1), :].reshape(PERM_TOKENS, CONV_HALF)
        hi, lo = _split_bf16(blk)
        yh.append(_dot(perm_ref[...], hi) + _dot(perm_ref[...], lo))
    yh = jnp.concatenate(yh, axis=0)

    yn = []
    for s in subs:
        yc = jnp.concatenate([yh[s], yv_ref[s, :]], axis=-1)
        yc = yc - jnp.mean(yc, axis=-1, keepdims=True)
        var = jnp.mean(yc * yc, axis=-1, keepdims=True)
        z = yc * lax.rsqrt(var + EPS) * lng_ref[...] + lnb_ref[...]
        yn.append((z * _sigmoid(z)).astype(BF16))
    y_gla = [_dot(og_ref[s, :], wgo_ref[...]) for s in subs]
    y_conv = [_dot(v, wco_ref[...]) for v in yn]
    merged = []
    for n, s in enumerate(subs):
        gates = gt_ref[s, :].astype(F32)
        merged.append((gates[:, :d] * y_conv[n] + gates[:, d:] * y_gla[n]).astype(BF16))
    y = [_dot(v, wout_ref[...]) for v in merged]
    h_mid = [x_ref[s, :] + gt1 * _rms(y[n], gpost1_ref[...]) for n, s in enumerate(subs)]
    u2 = [(_rms(h, gpre2_ref[...]) * (1.0 + sc2) + sh2).astype(BF16) for h in h_mid]
    acc = [jnp.zeros(h.shape, F32) for h in h_mid]
    for j in range(d_ff // ff_tile):
        cols = slice(j * ff_tile, (j + 1) * ff_tile)
        f = [_dot(v, wff1_ref[:, cols]) for v in u2]
        f = [jnp.square(jnp.maximum(v, 0.0)).astype(BF16) for v in f]
        acc = [acc[n] + _dot(f[n], wff2_ref[cols, :]) for n in range(len(subs))]
    for n, s in enumerate(subs):
        o_ref[s, :] = h_mid[n] + gt2 * _rms(acc[n], gpost2_ref[...])


def _post(x, yh, yv, og, gates, mod, perm, ln_g, ln_b, gpost1, gpre2, gpost2, wco, wgo, wout,
          wff1, wff2, tm):
    bsz, t, d = x.shape
    assert tm % PERM_TOKENS == 0
    return pl.pallas_call(
        _post_kernel,
        grid=(bsz, t // tm),
        in_specs=[
            _tok_spec(tm, d), _colmajor_spec(tm), _tok_spec(tm, CONV_HALF), _tok_spec(tm, HV),
            _tok_spec(tm, 2 * d),
            pl.BlockSpec((None,) + mod.shape[1:], lambda b, i: (b, 0, 0)),
            _single_spec(perm), _const_spec(ln_g), _const_spec(ln_b),
            _const_spec(gpost1), _const_spec(gpre2), _const_spec(gpost2),
            _single_spec(wco), _single_spec(wgo), _single_spec(wout), _single_spec(wff1),
            _single_spec(wff2),
        ],
        out_specs=_tok_spec(tm, d),
        out_shape=jax.ShapeDtypeStruct((bsz, t, d), F32),
        compiler_params=pltpu.CompilerParams(
            dimension_semantics=("parallel", "parallel"), vmem_limit_bytes=VMEM_LIMIT),
        name="post",
    )(x, yh, yv, og, gates, mod, perm, ln_g, ln_b, gpost1, gpre2, gpost2, wco, wgo, wout, wff1,
      wff2)


def kernel(x, c, ctx, c_ctx, w_mod, b_mod, g_pre1, g_post1, g_pre2, g_post2, w_in, conv_w,
           conv_b, conv_ln_g, conv_ln_b, w_conv_out, w_decay, b_decay, gla_norm_g, w_gla_out,
           w_out, w_ff1, w_ff2):
    bsz, t, d = x.shape
    depth = w_in.shape[0]
    assert depth == 1 and d == D_MODEL and t % (GRID_W * 8) == 0

    n_rows = -(-(bsz + 1) // 8) * 8
    c_all = jnp.zeros((n_rows, d), F32).at[:bsz].set(c).at[bsz].set(c_ctx)
    m = _adaln(c_all, w_mod, b_mod)
    mod = m[:bsz].reshape(bsz, N_MOD, d)
    mod_ctx = m[bsz, :2 * d].reshape(1, 2, d)

    wm, wdin, wg = _split_w_in(jnp.swapaxes(w_in[0], 0, 1))
    wdec = jnp.zeros((LANES, 2 * HK), F32)
    wdec = wdec.at[:DECAY_RANK, :HK].set(w_decay[0, 0])
    wdec = wdec.at[DECAY_RANK:2 * DECAY_RANK, HK:].set(w_decay[0, 1]).astype(BF16)
    bdec = b_decay[0].reshape(1, 2 * HK)
    row = lambda v: v.reshape(1, -1)

    perm = _to_colmajor_perm()
    to_colmajor = jnp.asarray(perm, BF16)
    to_rowmajor = jnp.asarray(perm.T, BF16)

    ah, av, q, k, v, rs, laf, lab, gates, wco, wgo, wout, wff1, wff2 = _proj(
        x, mod, row(g_pre1[0]), to_colmajor, wm, wdin, wdec, bdec, wg,
        (w_conv_out, w_gla_out, w_out, w_ff1, w_ff2), tm=1024)
    og = _gla(q, k, v, laf, lab, rs, row(gla_norm_g[0]),
              ctx, mod_ctx, row(g_pre1[0]), wm, wdin, wdec, bdec)
    yh, yv = _conv(ah.reshape(bsz, t, CONV_HALF), av, conv_w[0], row(conv_b[0]))
    yh = yh.reshape(bsz, GRID_W, t // GRID_W, CONV_HALF)

    return _post(x, yh, yv, og, gates, mod, to_rowmajor, row(conv_ln_g[0]), row(conv_ln_b[0]),
                 row(g_post1[0]), row(g_pre2[0]), row(g_post2[0]),
                 wco, wgo, wout, wff1, wff2, tm=512)
```

```python
import functools

import jax
import jax.numpy as jnp
import numpy as np
from jax import lax
from jax.experimental import pallas as pl
from jax.experimental.pallas import tpu as pltpu

D_MODEL = 1024
GRID_W = 64
CONV_DIM = 512
CONV_K = 31
GLA_HEADS = 4
GLA_DK = 64
GLA_DV = 128
DECAY_RANK = 16
GATE_NORM = 16.0
CHUNK = 64
N_MOD = 6
EPS = 1e-6

COL_Q = 2 * CONV_DIM
COL_K = COL_Q + GLA_HEADS * GLA_DK
COL_V = COL_K + GLA_HEADS * GLA_DK
COL_R = COL_V + GLA_HEADS * GLA_DV
COL_DEC = COL_R + GLA_HEADS * GLA_DV
COL_GATE = COL_DEC + 2 * DECAY_RANK
COL_END = COL_GATE + 2 * D_MODEL

LANES = 128
HK = GLA_HEADS * GLA_DK
HV = GLA_HEADS * GLA_DV
CONV_HALF = CONV_DIM // 2
CONV_PAD = CONV_K // 2
PERM_ROWS = 8
PERM_TOKENS = PERM_ROWS * GRID_W
PROJ_SUB = 256
SUPER = 2 * CHUNK
STEP = 8 * SUPER

VMEM_LIMIT = 56 * 1024 * 1024

F32 = jnp.float32
BF16 = jnp.bfloat16

NT = (((1,), (1,)), ((), ()))
TN = (((0,), (0,)), ((), ()))


def _dot(a, b):
    return jnp.dot(a, b, preferred_element_type=F32)


def _rms(x, g):
    ms = jnp.mean(x * x, axis=-1, keepdims=True)
    return x * lax.rsqrt(ms + EPS) * g


def _sigmoid(x):
    return jax.nn.sigmoid(x)


def _split_bf16(x):
    hi = x.astype(BF16)
    lo = (x - hi.astype(F32)).astype(BF16)
    return hi, lo


def _to_colmajor_perm():
    out = np.arange(PERM_TOKENS)
    src = (out % PERM_ROWS) * GRID_W + out // PERM_ROWS
    p = np.zeros((PERM_TOKENS, PERM_TOKENS), np.float32)
    p[out, src] = 1.0
    return p


def _adaln_kernel(c_ref, w_ref, b_ref, o_ref):
    @pl.when(pl.program_id(0) == 0)
    def _():
        o_ref[...] = jnp.broadcast_to(b_ref[...], o_ref.shape)

    c = c_ref[...]
    s_hi, s_lo = _split_bf16(c * _sigmoid(c))
    w_hi, w_lo = _split_bf16(w_ref[...])
    o_ref[...] += _dot(s_hi, w_hi) + _dot(s_lo, w_hi) + _dot(s_hi, w_lo)


def _adaln(c_all, w_mod, b_mod):
    n_rows, d = c_all.shape
    n_out = w_mod.shape[-1]
    tk = 128
    return pl.pallas_call(
        _adaln_kernel,
        grid=(d // tk,),
        in_specs=[
            pl.BlockSpec((n_rows, tk), lambda k: (0, k)),
            pl.BlockSpec((None, tk, n_out), lambda k: (0, k, 0)),
            pl.BlockSpec((1, n_out), lambda k: (0, 0)),
        ],
        out_specs=pl.BlockSpec((n_rows, n_out), lambda k: (0, 0)),
        out_shape=jax.ShapeDtypeStruct((n_rows, n_out), F32),
        compiler_params=pltpu.CompilerParams(
            dimension_semantics=("arbitrary",), vmem_limit_bytes=VMEM_LIMIT),
        name="adaln",
    )(c_all, w_mod, b_mod)


def _split_w_in_kernel(wm_t_ref, wd_t_ref, wg_t_ref, wm_ref, wdin_ref, wg_ref, *, n_gate):
    j = pl.program_id(0)
    wm_ref[...] = wm_t_ref[...].astype(BF16).T

    @pl.when(j == 0)
    def _():
        wdin_ref[...] = wd_t_ref[...].astype(BF16).T

    @pl.when(j < n_gate)
    def _():
        wg_ref[...] = wg_t_ref[...].astype(BF16).T


def _split_w_in(w_in_t):
    n, d = w_in_t.shape
    tr = 512
    n_main = COL_DEC // tr
    n_gate = (n - COL_GATE) // tr
    assert COL_DEC % tr == 0 and (n - COL_GATE) % tr == 0 and n_gate <= n_main
    gate_blk = lambda j: jnp.minimum(j, n_gate - 1)
    return pl.pallas_call(
        functools.partial(_split_w_in_kernel, n_gate=n_gate),
        grid=(n_main,),
        in_specs=[
            pl.BlockSpec((tr, d), lambda j: (j, 0)),
            pl.BlockSpec((LANES, d), lambda j: (COL_DEC // LANES, 0)),
            pl.BlockSpec((pl.Element(tr), pl.Element(d)),
                         lambda j: (pl.multiple_of(COL_GATE + tr * gate_blk(j), 32), 0)),
        ],
        out_specs=[
            pl.BlockSpec((d, tr), lambda j: (0, j)),
            pl.BlockSpec((d, LANES), lambda j: (0, 0)),
            pl.BlockSpec((d, tr), lambda j: (0, gate_blk(j))),
        ],
        out_shape=[jax.ShapeDtypeStruct((d, w), BF16) for w in (COL_DEC, LANES, n - COL_GATE)],
        compiler_params=pltpu.CompilerParams(
            dimension_semantics=("arbitrary",), vmem_limit_bytes=VMEM_LIMIT),
        name="split_w_in",
    )(w_in_t, w_in_t, w_in_t)


def _prenorm_mod(x, g, mod_ref):
    sh = mod_ref[0:1, :]
    sc = mod_ref[1:2, :]
    return (_rms(x, g) * (1.0 + sc) + sh).astype(BF16)


def _decay_logs(z, wdec_ref, bdec_ref):
    z_hi, z_lo = z
    logits = _dot(z_hi, wdec_ref[...]) + _dot(z_lo, wdec_ref[...]) + bdec_ref[...]
    ls = jnp.minimum(logits, 0.0) - jnp.log(1.0 + jnp.exp(-jnp.abs(logits)))
    return ls * (1.0 / GATE_NORM)


N_LATE_W = 5


def _proj_kernel(x_ref, mod_ref, g_ref, perm_ref, wm_ref, wdin_ref, wdec_ref, bdec_ref, wg_ref,
                 *refs):
    late_f32 = refs[:N_LATE_W]
    (ah_ref, av_ref, q_ref, k_ref, v_ref, rs_ref, laf_ref, lab_ref,
     gt_ref) = refs[N_LATE_W:-N_LATE_W]
    late_bf16 = refs[-N_LATE_W:]
    for src, dst in zip(late_f32, late_bf16):
        dst[...] = src[...].astype(BF16)
    subs = [slice(s, s + PROJ_SUB) for s in range(0, x_ref.shape[0], PROJ_SUB)]
    g = g_ref[...]
    ub = [_prenorm_mod(x_ref[s, :], g, mod_ref) for s in subs]
    z = [_split_bf16(_dot(u, wdin_ref[...])) for u in ub]
    a_h = []
    for n, s in enumerate(subs):
        glu = _dot(ub[n], wm_ref[:, 0:COL_Q])
        a = (glu[:, :CONV_DIM] * _sigmoid(glu[:, CONV_DIM:])).astype(BF16)
        av_ref[s, :] = a[:, CONV_HALF:]
        a_h.append(a[:, :CONV_HALF])
    for n, s in enumerate(subs):
        q_ref[s, :] = (_dot(ub[n], wm_ref[:, COL_Q:COL_K]) * (GLA_DK ** -0.5)).astype(BF16)
    for n, s in enumerate(subs):
        k_ref[s, :] = _dot(ub[n], wm_ref[:, COL_K:COL_V]).astype(BF16)
    for n, s in enumerate(subs):
        v_ref[s, :] = _dot(ub[n], wm_ref[:, COL_V:COL_R]).astype(BF16)
    for n, s in enumerate(subs):
        la = _decay_logs(z[n], wdec_ref, bdec_ref)
        laf_ref[s, :] = la[:, :HK]
        lab_ref[s, :] = la[:, HK:]
    for n, s in enumerate(subs):
        r = _dot(ub[n], wm_ref[:, COL_R:COL_DEC])
        rs_ref[s, :] = (r * _sigmoid(r)).astype(BF16)
    per_group = PERM_TOKENS // PROJ_SUB
    for n in range(x_ref.shape[0] // PERM_TOKENS):
        a_rm = jnp.concatenate(a_h[per_group * n:per_group * (n + 1)], axis=0)
        a_cm = _dot(perm_ref[...], a_rm)
        ah_ref[:, PERM_ROWS * n:PERM_ROWS * (n + 1), :] = a_cm.reshape(
            GRID_W, PERM_ROWS, CONV_HALF)
    for n, s in enumerate(subs):
        gt_ref[s, :] = _sigmoid(_dot(ub[n], wg_ref[...])).astype(BF16)


def _const_spec(arr):
    nd = arr.ndim
    return pl.BlockSpec(arr.shape, lambda *_: (0,) * nd)


def _tok_spec(tm, width):
    return pl.BlockSpec((None, tm, width), lambda b, i: (b, i, 0))


def _colmajor_spec(tm):
    return pl.BlockSpec((None, GRID_W, tm // GRID_W, CONV_HALF), lambda b, i: (b, 0, i, 0))


def _single_spec(arr):
    nd = arr.ndim
    return pl.BlockSpec(arr.shape, lambda *_: (0,) * nd, pipeline_mode=pl.Buffered(1))


def _proj(x, mod, g, perm, wm, wdin, wdec, bdec, wg, late_weights, tm):
    bsz, t, d = x.shape
    widths = (CONV_HALF, HK, HK, HV, HV, HK, HK, 2 * D_MODEL)
    dtypes = (BF16, BF16, BF16, BF16, BF16, F32, F32, BF16)
    assert tm % PERM_TOKENS == 0 and len(late_weights) == N_LATE_W
    ah_shape = jax.ShapeDtypeStruct((bsz, GRID_W, t // GRID_W, CONV_HALF), F32)
    n_i = t // tm
    n_steps = bsz * n_i
    late_in, late_out, late_shape = [], [], []
    for w in late_weights:
        _, rows, cols = w.shape
        assert rows % (16 * n_steps) == 0
        blk = rows // n_steps
        late_in.append(pl.BlockSpec((None, blk, cols), lambda b, i: (0, b * n_i + i, 0)))
        late_out.append(pl.BlockSpec((blk, cols), lambda b, i: (b * n_i + i, 0)))
        late_shape.append(jax.ShapeDtypeStruct((rows, cols), BF16))
    return pl.pallas_call(
        _proj_kernel,
        grid=(bsz, n_i),
        in_specs=[
            _tok_spec(tm, d),
            pl.BlockSpec((None,) + mod.shape[1:], lambda b, i: (b, 0, 0)),
            _const_spec(g), _single_spec(perm), _single_spec(wm), _single_spec(wdin),
            _single_spec(wdec), _const_spec(bdec), _single_spec(wg),
        ] + late_in,
        out_specs=[_colmajor_spec(tm)] + [_tok_spec(tm, w) for w in widths] + late_out,
        out_shape=[ah_shape] + [jax.ShapeDtypeStruct((bsz, t, w), dt)
                                for w, dt in zip(widths, dtypes)] + late_shape,
        compiler_params=pltpu.CompilerParams(
            dimension_semantics=("parallel", "parallel"), vmem_limit_bytes=VMEM_LIMIT),
        name="proj",
    )(x, mod, g, perm, wm, wdin, wdec, bdec, wg, *late_weights)


def _gla_scale(qs, ks, las, dirn, want_out):
    tri_b, mid, last = dirn["tri_b"], dirn["mid"], dirn["last"]
    la_hi, la_lo = _split_bf16(las)
    g = _dot(tri_b, la_hi) + _dot(tri_b, la_lo)
    u = dict(kt=[], kend=[], qt=[], qg=[], e=[])
    for c in range(SUPER // CHUNK):
        rows = slice(c * CHUNK, (c + 1) * CHUNK)
        gc = g[rows]
        g_mid = gc[mid:mid + 1]
        g_last = gc[last:last + 1]
        u["e"].append(jnp.exp(g_last))
        kt = ks[rows].astype(F32) * jnp.exp(g_mid - gc)
        u["kt"].append(kt)
        u["kend"].append(kt * jnp.exp(g_last - g_mid))
        if want_out:
            qt = qs[rows].astype(F32) * jnp.exp(gc - g_mid)
            u["qt"].append(qt.astype(BF16))
            u["qg"].append((qt * jnp.exp(g_mid)).astype(BF16))
    return u


def _pair_lanes(pair):
    return slice(LANES * pair, LANES * (pair + 1))


def _gla_scores(u, pair, head_lane):
    lanes = _pair_lanes(pair)
    kt = jnp.concatenate([c[:, lanes] for c in u["kt"]], axis=0)
    qt = jnp.concatenate([c[:, lanes] for c in u["qt"]], axis=0)
    kt2 = jnp.concatenate([(kt * hm).astype(BF16) for hm in head_lane], axis=0)
    return lax.dot_general(qt, kt2, NT, preferred_element_type=F32)


def _gla_update(u, vs, h, head_lane):
    lanes = _pair_lanes(h // 2)
    hm = head_lane[h % 2]
    k0 = (u["kend"][0][:, lanes] * hm).astype(BF16)
    k1 = (u["kend"][1][:, lanes] * hm).astype(BF16)
    z = jnp.zeros_like(k0)
    kbd = jnp.concatenate([jnp.concatenate([k0, z], axis=1),
                           jnp.concatenate([z, k1], axis=1)], axis=0)
    return lax.dot_general(vs[:, GLA_DV * h:GLA_DV * (h + 1)], kbd, TN,
                           preferred_element_type=F32)


def _gla_intra(scores, vs, pair, mask2):
    a = jnp.where(mask2, scores, 0.0).astype(BF16)
    v0 = vs[:, GLA_DV * 2 * pair:GLA_DV * (2 * pair + 1)]
    v1 = vs[:, GLA_DV * (2 * pair + 1):GLA_DV * (2 * pair + 2)]
    z = jnp.zeros_like(v0)
    vbd = jnp.concatenate([jnp.concatenate([v0, z], axis=1),
                           jnp.concatenate([z, v1], axis=1)], axis=0)
    return _dot(a, vbd)


def _gla_kernel(q_ref, k_ref, v_ref, laf_ref, lab_ref, rs_ref, g_ref,
                ctx_ref, modc_ref, gpre_ref, wm_ref, wdin_ref, wdec_ref, bdec_ref,
                o_ref, of_ref, ob_ref, sf_ref, sb_ref, kc_ref, vc_ref, lafc_ref, labc_ref):
    t = q_ref.shape[0]
    t_ctx = kc_ref.shape[0]

    ubc = _prenorm_mod(ctx_ref[...], gpre_ref[...], modc_ref)
    zc = _split_bf16(_dot(ubc, wdin_ref[...]))
    kc_ref[...] = _dot(ubc, wm_ref[:, COL_K:COL_V]).astype(BF16)
    vc_ref[...] = _dot(ubc, wm_ref[:, COL_V:COL_R]).astype(BF16)
    lac = _decay_logs(zc, wdec_ref, bdec_ref)
    lafc_ref[...] = lac[:, :HK]
    labc_ref[...] = lac[:, HK:]

    half = t // 2
    n_steps = t // STEP
    supers = STEP // SUPER
    n_chunks = SUPER // CHUNK
    pairs = GLA_HEADS // 2

    row = lax.broadcasted_iota(jnp.int32, (SUPER, SUPER), 0)
    col = lax.broadcasted_iota(jnp.int32, (SUPER, SUPER), 1)
    same_chunk = (row // CHUNK) == (col // CHUNK)
    low_mask = jnp.logical_and(same_chunk, col <= row)
    up_mask = jnp.logical_and(same_chunk, col >= row)
    lane = lax.broadcasted_iota(jnp.int32, (1, LANES), 1)
    head_lane = (jnp.where(lane < GLA_DK, 1.0, 0.0), jnp.where(lane >= GLA_DK, 1.0, 0.0))

    fwd = dict(tri_b=jnp.where(low_mask, 1.0, 0.0).astype(BF16),
               mask2=jnp.concatenate([low_mask, low_mask], axis=1),
               mid=CHUNK // 2 - 1, last=CHUNK - 1, order=(0, 1), st_ref=sf_ref)
    bwd = dict(tri_b=jnp.where(up_mask, 1.0, 0.0).astype(BF16),
               mask2=jnp.concatenate([up_mask, up_mask], axis=1),
               mid=CHUNK // 2, last=0, order=(1, 0), st_ref=sb_ref)

    sf_ref[...] = jnp.zeros_like(sf_ref)
    sb_ref[...] = jnp.zeros_like(sb_ref)
    gn = g_ref[...]

    def run(units, want_out, finalize):
        vals = []
        for dirn, refs, t0 in units:
            q_r, k_r, v_r, la_r = refs
            rows = pl.ds(t0, SUPER)
            qs = q_r[rows, :] if want_out else None
            vals.append((_gla_scale(qs, k_r[rows, :], la_r[rows, :], dirn, want_out),
                         v_r[rows, :]))
        scores = []
        if want_out:
            scores = [[_gla_scores(u, p, head_lane) for p in range(pairs)] for u, _ in vals]
        upd = [[_gla_update(u, vs, h, head_lane) for h in range(GLA_HEADS)] for u, vs in vals]
        intra = []
        if want_out:
            intra = [[_gla_intra(scores[n][p], vals[n][1], p, units[n][0]["mask2"])
                      for p in range(pairs)] for n in range(len(units))]
        states = {}
        for n, (dirn, _, t0) in enumerate(units):
            key = id(dirn["st_ref"])
            if key not in states:
                states[key] = (dirn["st_ref"], [dirn["st_ref"][h] for h in range(GLA_HEADS)])
            st = states[key][1]
            u = vals[n][0]
            inter = [[None] * n_chunks for _ in range(pairs)]
            for c in dirn["order"]:
                if want_out:
                    for p in range(pairs):
                        st2 = jnp.concatenate([st[2 * p].astype(BF16),
                                               st[2 * p + 1].astype(BF16)], axis=0)
                        inter[p][c] = lax.dot_general(u["qg"][c][:, _pair_lanes(p)], st2, NT,
                                                      preferred_element_type=F32)
                for h in range(GLA_HEADS):
                    e = u["e"][c][:, _pair_lanes(h // 2)]
                    st[h] = st[h] * e + upd[n][h][:, LANES * c:LANES * (c + 1)]
            if not want_out:
                continue
            fwd_unit = dirn is fwd
            for p in range(pairs):
                o2 = intra[n][p] + jnp.concatenate(inter[p], axis=0)
                for j in range(2):
                    cols = slice(GLA_DV * (2 * p + j), GLA_DV * (2 * p + j + 1))
                    o = o2[:, GLA_DV * j:GLA_DV * (j + 1)]
                    if finalize:
                        other = (ob_ref[pl.ds(t0 - half, SUPER), cols] if fwd_unit
                                 else of_ref[pl.ds(t0, SUPER), cols])
                        rs = rs_ref[pl.ds(t0, SUPER), cols].astype(F32)
                        o_ref[pl.ds(t0, SUPER), cols] = (_rms(o + other, gn) * rs).astype(BF16)
                    elif fwd_unit:
                        of_ref[pl.ds(t0, SUPER), cols] = o
                    else:
                        ob_ref[pl.ds(t0 - half, SUPER), cols] = o
        for st_ref, st in states.values():
            for h in range(GLA_HEADS):
                st_ref[h] = st[h]

    ctx_units = []
    for s in range(t_ctx // SUPER):
        ctx_units.append((fwd, (None, kc_ref, vc_ref, lafc_ref), s * SUPER))
        ctx_units.append((bwd, (None, kc_ref, vc_ref, labc_ref), t_ctx - (s + 1) * SUPER))
    run(ctx_units, want_out=False, finalize=False)

    def step(i, finalize):
        units = []
        for s in range(supers):
            t0 = pl.multiple_of(i * STEP + s * SUPER, SUPER)
            t1 = pl.multiple_of(t - (i + 1) * STEP + (supers - 1 - s) * SUPER, SUPER)
            units.append((fwd, (q_ref, k_ref, v_ref, laf_ref), t0))
            units.append((bwd, (q_ref, k_ref, v_ref, lab_ref), t1))
        run(units, want_out=True, finalize=finalize)

    def park_body(i, carry):
        step(i, False)
        return carry

    def finish_body(i, carry):
        step(i, True)
        return carry

    lax.fori_loop(0, n_steps // 2, park_body, 0)
    lax.fori_loop(n_steps // 2, n_steps, finish_body, 0)


def _gla(q, k, v, laf, lab, rs, gnorm, ctx, mod_ctx, gpre, wm, wdin, wdec, bdec):
    bsz, t, _ = q.shape
    t_ctx = ctx.shape[1]
    assert t % (2 * STEP) == 0 and t_ctx % SUPER == 0

    def bspec(arr):
        return pl.BlockSpec((None,) + arr.shape[1:], lambda b: (b, 0, 0))

    args = (q, k, v, laf, lab, rs)
    return pl.pallas_call(
        _gla_kernel,
        grid=(bsz,),
        in_specs=[bspec(a) for a in args] + [_const_spec(gnorm), bspec(ctx),
                  pl.BlockSpec((None,) + mod_ctx.shape[1:], lambda b: (0, 0, 0)),
                  _const_spec(gpre), _single_spec(wm), _single_spec(wdin), _single_spec(wdec),
                  _const_spec(bdec)],
        out_specs=pl.BlockSpec((None, t, HV), lambda b: (b, 0, 0)),
        out_shape=jax.ShapeDtypeStruct((bsz, t, HV), BF16),
        scratch_shapes=[
            pltpu.VMEM((t // 2, HV), F32), pltpu.VMEM((t // 2, HV), F32),
            pltpu.VMEM((GLA_HEADS, GLA_DV, LANES), F32),
            pltpu.VMEM((GLA_HEADS, GLA_DV, LANES), F32),
            pltpu.VMEM((t_ctx, HK), BF16), pltpu.VMEM((t_ctx, HV), BF16),
            pltpu.VMEM((t_ctx, HK), F32), pltpu.VMEM((t_ctx, HK), F32),
        ],
        compiler_params=pltpu.CompilerParams(
            dimension_semantics=("parallel",), vmem_limit_bytes=VMEM_LIMIT),
        name="gla",
    )(*args, gnorm, ctx, mod_ctx, gpre, wm, wdin, wdec, bdec)


def _tap_segments(n_tiles, lines_per_tile, n_lines, group):
    segs = []
    for t0 in range(0, n_tiles, group):
        t1 = min(t0 + group, n_tiles)
        lo_line = int(t0 * lines_per_tile)
        hi_line = int(-(-t1 * lines_per_tile // 1)) - 1
        k_lo = max(0, CONV_PAD - hi_line)
        k_hi = min(CONV_K, n_lines + CONV_PAD - lo_line)
        if segs and segs[-1][2:] == (k_lo, k_hi):
            segs[-1] = (segs[-1][0], t1, k_lo, k_hi)
        else:
            segs.append((t0, t1, k_lo, k_hi))
    return segs


def _conv_kernel(ah_ref, av_ref, w_ref, b_ref, yh_ref, yv_ref, ph_ref, pv_ref, wb_ref):
    t = ah_ref.shape[0]
    rows = t // GRID_W
    tile = 64
    for kk in range(CONV_K):
        wb_ref[kk] = jnp.broadcast_to(w_ref[kk:kk + 1, :], wb_ref.shape[1:])

    def conv(src_ref, pad_ref, w_cols, line, n_lines, group, dst_ref):
        pad = CONV_PAD * line
        pad_ref[pl.ds(0, pad), :] = jnp.zeros((pad, CONV_HALF), F32)
        pad_ref[pl.ds(pad + t, pad), :] = jnp.zeros((pad, CONV_HALF), F32)

        def fill(i, carry):
            t0 = pl.multiple_of(i * tile, tile)
            pad_ref[pl.ds(pl.multiple_of(pad + t0, 8), tile), :] = (
                src_ref[pl.ds(t0, tile), :].astype(F32))
            return carry

        lax.fori_loop(0, t // tile, fill, 0)
        bias = b_ref[:, w_cols]

        for first, last, k_lo, k_hi in _tap_segments(t // tile, tile / line, n_lines, group):
            def body(i, carry, k_lo=k_lo, k_hi=k_hi):
                t0 = pl.multiple_of(i * tile, tile)
                acc = jnp.broadcast_to(bias, (tile, CONV_HALF))
                for kk in range(k_lo, k_hi):
                    win = pad_ref[pl.ds(pl.multiple_of(t0 + kk * line, 8), tile), :]
                    w8 = wb_ref[kk, :, w_cols]
                    acc = acc + jnp.concatenate([w8] * (tile // 8), axis=0) * win
                dst_ref[pl.ds(t0, tile), :] = acc
                return carry

            lax.fori_loop(first, last, body, 0)

    conv(ah_ref, ph_ref, slice(0, CONV_HALF), rows, GRID_W, 2, yh_ref)
    conv(av_ref, pv_ref, slice(CONV_HALF, CONV_DIM), GRID_W, rows, 4, yv_ref)


def _conv(ah, av, w, b):
    bsz, t, c = ah.shape
    rows = t // GRID_W
    tok = pl.BlockSpec((None, t, c), lambda i: (i, 0, 0))
    return pl.pallas_call(
        _conv_kernel,
        grid=(bsz,),
        in_specs=[tok, tok, _const_spec(w), _const_spec(b)],
        out_specs=[tok, tok],
        out_shape=[jax.ShapeDtypeStruct((bsz, t, c), F32)] * 2,
        scratch_shapes=[
            pltpu.VMEM((t + 2 * CONV_PAD * rows, c), F32),
            pltpu.VMEM((t + 2 * CONV_PAD * GRID_W, c), F32),
            pltpu.VMEM((CONV_K, 8, 2 * c), F32),
        ],
        compiler_params=pltpu.CompilerParams(
            dimension_semantics=("parallel",), vmem_limit_bytes=VMEM_LIMIT),
        name="conv",
    )(ah, av, w, b)


def _post_kernel(x_ref, yh_ref, yv_ref, og_ref, gt_ref, mod_ref, perm_ref, lng_ref, lnb_ref,
                 gpost1_ref, gpre2_ref, gpost2_ref, wco_ref, wgo_ref, wout_ref, wff1_ref,
                 wff2_ref, o_ref):
    d = x_ref.shape[-1]
    d_ff = wff1_ref.shape[1]
    ff_tile = 1024
    sub = 256
    subs = [slice(s, s + sub) for s in range(0, x_ref.shape[0], sub)]
    gt1 = mod_ref[2:3, :]
    sh2 = mod_ref[3:4, :]
    sc2 = mod_ref[4:5, :]
    gt2 = mod_ref[5:6, :]

    yh = []
    for n in range(x_ref.shape[0] // PERM_TOKENS):
        blk = yh_ref[:, PERM_ROWS * n:PERM_ROWS * (n + 1), :].reshape(PERM_TOKENS, CONV_HALF)
        hi, lo = _split_bf16(blk)
        yh.append(_dot(perm_ref[...], hi) + _dot(perm_ref[...], lo))
    yh = jnp.concatenate(yh, axis=0)

    yn = []
    for s in subs:
        yc = jnp.concatenate([yh[s], yv_ref[s, :]], axis=-1)
        yc = yc - jnp.mean(yc, axis=-1, keepdims=True)
        var = jnp.mean(yc * yc, axis=-1, keepdims=True)
        z = yc * lax.rsqrt(var + EPS) * lng_ref[...] + lnb_ref[...]
        yn.append((z * _sigmoid(z)).astype(BF16))
    y_gla = [_dot(og_ref[s, :], wgo_ref[...]) for s in subs]
    y_conv = [_dot(v, wco_ref[...]) for v in yn]
    merged = []
    for n, s in enumerate(subs):
        gates = gt_ref[s, :].astype(F32)
        merged.append((gates[:, :d] * y_conv[n] + gates[:, d:] * y_gla[n]).astype(BF16))
    y = [_dot(v, wout_ref[...]) for v in merged]
    h_mid = [x_ref[s, :] + gt1 * _rms(y[n], gpost1_ref[...]) for n, s in enumerate(subs)]
    u2 = [(_rms(h, gpre2_ref[...]) * (1.0 + sc2) + sh2).astype(BF16) for h in h_mid]
    acc = [jnp.zeros(h.shape, F32) for h in h_mid]
    for j in range(d_ff // ff_tile):
        cols = slice(j * ff_tile, (j + 1) * ff_tile)
        f = [_dot(v, wff1_ref[:, cols]) for v in u2]
        f = [jnp.square(jnp.maximum(v, 0.0)).astype(BF16) for v in f]
        acc = [acc[n] + _dot(f[n], wff2_ref[cols, :]) for n in range(len(subs))]
    for n, s in enumerate(subs):
        o_ref[s, :] = h_mid[n] + gt2 * _rms(acc[n], gpost2_ref[...])


def _post(x, yh, yv, og, gates, mod, perm, ln_g, ln_b, gpost1, gpre2, gpost2, wco, wgo, wout,
          wff1, wff2, tm):
    bsz, t, d = x.shape
    assert tm % PERM_TOKENS == 0
    return pl.pallas_call(
        _post_kernel,
        grid=(bsz, t // tm),
        in_specs=[
            _tok_spec(tm, d), _colmajor_spec(tm), _tok_spec(tm, CONV_HALF), _tok_spec(tm, HV),
            _tok_spec(tm, 2 * d),
            pl.BlockSpec((None,) + mod.shape[1:], lambda b, i: (b, 0, 0)),
            _single_spec(perm), _const_spec(ln_g), _const_spec(ln_b),
            _const_spec(gpost1), _const_spec(gpre2), _const_spec(gpost2),
            _single_spec(wco), _single_spec(wgo), _single_spec(wout), _single_spec(wff1),
            _single_spec(wff2),
        ],
        out_specs=_tok_spec(tm, d),
        out_shape=jax.ShapeDtypeStruct((bsz, t, d), F32),
        compiler_params=pltpu.CompilerParams(
            dimension_semantics=("parallel", "parallel"), vmem_limit_bytes=VMEM_LIMIT),
        name="post",
    )(x, yh, yv, og, gates, mod, perm, ln_g, ln_b, gpost1, gpre2, gpost2, wco, wgo, wout, wff1,
      wff2)


def kernel(x, c, ctx, c_ctx, w_mod, b_mod, g_pre1, g_post1, g_pre2, g_post2, w_in, conv_w,
           conv_b, conv_ln_g, conv_ln_b, w_conv_out, w_decay, b_decay, gla_norm_g, w_gla_out,
           w_out, w_ff1, w_ff2):
    bsz, t, d = x.shape
    depth = w_in.shape[0]
    assert depth == 1 and d == D_MODEL and t % (GRID_W * 8) == 0

    n_rows = -(-(bsz + 1) // 8) * 8
    c_all = jnp.zeros((n_rows, d), F32).at[:bsz].set(c).at[bsz].set(c_ctx)
    m = _adaln(c_all, w_mod, b_mod)
    mod = m[:bsz].reshape(bsz, N_MOD, d)
    mod_ctx = m[bsz, :2 * d].reshape(1, 2, d)

    wm, wdin, wg = _split_w_in(jnp.swapaxes(w_in[0], 0, 1))
    wdec = jnp.zeros((LANES, 2 * HK), F32)
    wdec = wdec.at[:DECAY_RANK, :HK].set(w_decay[0, 0])
    wdec = wdec.at[DECAY_RANK:2 * DECAY_RANK, HK:].set(w_decay[0, 1]).astype(BF16)
    bdec = b_decay[0].reshape(1, 2 * HK)
    row = lambda v: v.reshape(1, -1)

    perm = _to_colmajor_perm()
    to_colmajor = jnp.asarray(perm, BF16)
    to_rowmajor = jnp.asarray(perm.T, BF16)

    ah, av, q, k, v, rs, laf, lab, gates, wco, wgo, wout, wff1, wff2 = _proj(
        x, mod, row(g_pre1[0]), to_colmajor, wm, wdin, wdec, bdec, wg,
        (w_conv_out, w_gla_out, w_out, w_ff1, w_ff2), tm=1024)
    og = _gla(q, k, v, laf, lab, rs, row(gla_norm_g[0]),
              ctx, mod_ctx, row(g_pre1[0]), wm, wdin, wdec, bdec)
    yh, yv = _conv(ah.reshape(bsz, t, CONV_HALF), av, conv_w[0], row(conv_b[0]))
    yh = yh.reshape(bsz, GRID_W, t // GRID_W, CONV_HALF)

    return _post(x, yh, yv, og, gates, mod, to_rowmajor, row(conv_ln_g[0]), row(conv_ln_b[0]),
                 row(g_post1[0]), row(g_pre2[0]), row(g_post2[0]),
                 wco, wgo, wout, wff1, wff2, tm=512)
```

```python
import functools

import jax
import jax.numpy as jnp
import numpy as np
from jax import lax
from jax.experimental import pallas as pl
from jax.experimental.pallas import tpu as pltpu

D_MODEL = 1024
GRID_W = 64
CONV_DIM = 512
CONV_K = 31
GLA_HEADS = 4
GLA_DK = 64
GLA_DV = 128
DECAY_RANK = 16
GATE_NORM = 16.0
CHUNK = 64
N_MOD = 6
EPS = 1e-6

COL_Q = 2 * CONV_DIM
COL_K = COL_Q + GLA_HEADS * GLA_DK
COL_V = COL_K + GLA_HEADS * GLA_DK
COL_R = COL_V + GLA_HEADS * GLA_DV
COL_DEC = COL_R + GLA_HEADS * GLA_DV
COL_GATE = COL_DEC + 2 * DECAY_RANK
COL_END = COL_GATE + 2 * D_MODEL

LANES = 128
HK = GLA_HEADS * GLA_DK
HV = GLA_HEADS * GLA_DV
CONV_HALF = CONV_DIM // 2
CONV_PAD = CONV_K // 2
PERM_ROWS = 8
PERM_TOKENS = PERM_ROWS * GRID_W
PROJ_SUB = 256
SUPER = 2 * CHUNK
STEP = 8 * SUPER

VMEM_LIMIT = 56 * 1024 * 1024

F32 = jnp.float32
BF16 = jnp.bfloat16

NT = (((1,), (1,)), ((), ()))
TN = (((0,), (0,)), ((), ()))


def _dot(a, b):
    return jnp.dot(a, b, preferred_element_type=F32)


def _rms(x, g):
    ms = jnp.mean(x * x, axis=-1, keepdims=True)
    return x * lax.rsqrt(ms + EPS) * g


def _sigmoid(x):
    return jax.nn.sigmoid(x)


def _split_bf16(x):
    hi = x.astype(BF16)
    lo = (x - hi.astype(F32)).astype(BF16)
    return hi, lo


def _to_colmajor_perm():
    out = np.arange(PERM_TOKENS)
    src = (out % PERM_ROWS) * GRID_W + out // PERM_ROWS
    p = np.zeros((PERM_TOKENS, PERM_TOKENS), np.float32)
    p[out, src] = 1.0
    return p


def _adaln_kernel(c_ref, w_ref, b_ref, o_ref):
    @pl.when(pl.program_id(0) == 0)
    def _():
        o_ref[...] = jnp.broadcast_to(b_ref[...], o_ref.shape)

    c = c_ref[...]
    s_hi, s_lo = _split_bf16(c * _sigmoid(c))
    w_hi, w_lo = _split_bf16(w_ref[...])
    o_ref[...] += _dot(s_hi, w_hi) + _dot(s_lo, w_hi) + _dot(s_hi, w_lo)


def _adaln(c_all, w_mod, b_mod):
    n_rows, d = c_all.shape
    n_out = w_mod.shape[-1]
    tk = 256
    return pl.pallas_call(
        _adaln_kernel,
        grid=(d // tk,),
        in_specs=[
            pl.BlockSpec((n_rows, tk), lambda k: (0, k)),
            pl.BlockSpec((None, tk, n_out), lambda k: (0, k, 0)),
            pl.BlockSpec((1, n_out), lambda k: (0, 0)),
        ],
        out_specs=pl.BlockSpec((n_rows, n_out), lambda k: (0, 0)),
        out_shape=jax.ShapeDtypeStruct((n_rows, n_out), F32),
        compiler_params=pltpu.CompilerParams(
            dimension_semantics=("arbitrary",), vmem_limit_bytes=VMEM_LIMIT),
        name="adaln",
    )(c_all, w_mod, b_mod)


def _split_w_in_kernel(wm_t_ref, wd_t_ref, wg_t_ref, wm_ref, wdin_ref, wg_ref, *, n_gate):
    j = pl.program_id(0)
    wm_ref[...] = wm_t_ref[...].astype(BF16).T

    @pl.when(j == 0)
    def _():
        wdin_ref[...] = wd_t_ref[...].astype(BF16).T

    @pl.when(j < n_gate)
    def _():
        wg_ref[...] = wg_t_ref[...].astype(BF16).T


def _split_w_in(w_in_t):
    n, d = w_in_t.shape
    tr = 512
    n_main = COL_DEC // tr
    n_gate = (n - COL_GATE) // tr
    assert COL_DEC % tr == 0 and (n - COL_GATE) % tr == 0 and n_gate <= n_main
    gate_blk = lambda j: jnp.minimum(j, n_gate - 1)
    return pl.pallas_call(
        functools.partial(_split_w_in_kernel, n_gate=n_gate),
        grid=(n_main,),
        in_specs=[
            pl.BlockSpec((tr, d), lambda j: (j, 0)),
            pl.BlockSpec((LANES, d), lambda j: (COL_DEC // LANES, 0)),
            pl.BlockSpec((pl.Element(tr), pl.Element(d)),
                         lambda j: (pl.multiple_of(COL_GATE + tr * gate_blk(j), 32), 0)),
        ],
        out_specs=[
            pl.BlockSpec((d, tr), lambda j: (0, j)),
            pl.BlockSpec((d, LANES), lambda j: (0, 0)),
            pl.BlockSpec((d, tr), lambda j: (0, gate_blk(j))),
        ],
        out_shape=[jax.ShapeDtypeStruct((d, w), BF16) for w in (COL_DEC, LANES, n - COL_GATE)],
        compiler_params=pltpu.CompilerParams(
            dimension_semantics=("arbitrary",), vmem_limit_bytes=VMEM_LIMIT),
        name="split_w_in",
    )(w_in_t, w_in_t, w_in_t)


def _prenorm_mod(x, g, mod_ref):
    sh = mod_ref[0:1, :]
    sc = mod_ref[1:2, :]
    return (_rms(x, g) * (1.0 + sc) + sh).astype(BF16)


def _decay_logs(z, wdec_ref, bdec_ref):
    z_hi, z_lo = z
    logits = _dot(z_hi, wdec_ref[...]) + _dot(z_lo, wdec_ref[...]) + bdec_ref[...]
    ls = jnp.minimum(logits, 0.0) - jnp.log(1.0 + jnp.exp(-jnp.abs(logits)))
    return ls * (1.0 / GATE_NORM)


N_LATE_W = 5


def _proj_kernel(x_ref, mod_ref, g_ref, perm_ref, wm_ref, wdin_ref, wdec_ref, bdec_ref, wg_ref,
                 *refs):
    late_f32 = refs[:N_LATE_W]
    (ah_ref, av_ref, q_ref, k_ref, v_ref, rs_ref, laf_ref, lab_ref,
     gt_ref) = refs[N_LATE_W:-N_LATE_W]
    late_bf16 = refs[-N_LATE_W:]
    for src, dst in zip(late_f32, late_bf16):
        dst[...] = src[...].astype(BF16)
    subs = [slice(s, s + PROJ_SUB) for s in range(0, x_ref.shape[0], PROJ_SUB)]
    g = g_ref[...]
    ub = [_prenorm_mod(x_ref[s, :], g, mod_ref) for s in subs]
    z = [_split_bf16(_dot(u, wdin_ref[...])) for u in ub]
    a_h = []
    for n, s in enumerate(subs):
        glu = _dot(ub[n], wm_ref[:, 0:COL_Q])
        a = (glu[:, :CONV_DIM] * _sigmoid(glu[:, CONV_DIM:])).astype(BF16)
        av_ref[s, :] = a[:, CONV_HALF:]
        a_h.append(a[:, :CONV_HALF])
    for n, s in enumerate(subs):
        q_ref[s, :] = (_dot(ub[n], wm_ref[:, COL_Q:COL_K]) * (GLA_DK ** -0.5)).astype(BF16)
    for n, s in enumerate(subs):
        k_ref[s, :] = _dot(ub[n], wm_ref[:, COL_K:COL_V]).astype(BF16)
    for n, s in enumerate(subs):
        v_ref[s, :] = _dot(ub[n], wm_ref[:, COL_V:COL_R]).astype(BF16)
    for n, s in enumerate(subs):
        la = _decay_logs(z[n], wdec_ref, bdec_ref)
        laf_ref[s, :] = la[:, :HK]
        lab_ref[s, :] = la[:, HK:]
    for n, s in enumerate(subs):
        r = _dot(ub[n], wm_ref[:, COL_R:COL_DEC])
        rs_ref[s, :] = (r * _sigmoid(r)).astype(BF16)
    per_group = PERM_TOKENS // PROJ_SUB
    for n in range(x_ref.shape[0] // PERM_TOKENS):
        a_rm = jnp.concatenate(a_h[per_group * n:per_group * (n + 1)], axis=0)
        a_cm = _dot(perm_ref[...], a_rm)
        ah_ref[:, PERM_ROWS * n:PERM_ROWS * (n + 1), :] = a_cm.reshape(
            GRID_W, PERM_ROWS, CONV_HALF)
    for n, s in enumerate(subs):
        gt_ref[s, :] = _sigmoid(_dot(ub[n], wg_ref[...])).astype(BF16)


def _const_spec(arr):
    nd = arr.ndim
    return pl.BlockSpec(arr.shape, lambda *_: (0,) * nd)


def _tok_spec(tm, width):
    return pl.BlockSpec((None, tm, width), lambda b, i: (b, i, 0))


def _colmajor_spec(tm):
    return pl.BlockSpec((None, GRID_W, tm // GRID_W, CONV_HALF), lambda b, i: (b, 0, i, 0))


def _single_spec(arr):
    nd = arr.ndim
    return pl.BlockSpec(arr.shape, lambda *_: (0,) * nd, pipeline_mode=pl.Buffered(1))


def _proj(x, mod, g, perm, wm, wdin, wdec, bdec, wg, late_weights, tm):
    bsz, t, d = x.shape
    widths = (CONV_HALF, HK, HK, HV, HV, HK, HK, 2 * D_MODEL)
    dtypes = (BF16, BF16, BF16, BF16, BF16, F32, F32, BF16)
    assert tm % PERM_TOKENS == 0 and len(late_weights) == N_LATE_W
    ah_shape = jax.ShapeDtypeStruct((bsz, GRID_W, t // GRID_W, CONV_HALF), F32)
    n_i = t // tm
    n_steps = bsz * n_i
    late_in, late_out, late_shape = [], [], []
    for w in late_weights:
        _, rows, cols = w.shape
        assert rows % (16 * n_steps) == 0
        blk = rows // n_steps
        late_in.append(pl.BlockSpec((None, blk, cols), lambda b, i: (0, b * n_i + i, 0)))
        late_out.append(pl.BlockSpec((blk, cols), lambda b, i: (b * n_i + i, 0)))
        late_shape.append(jax.ShapeDtypeStruct((rows, cols), BF16))
    return pl.pallas_call(
        _proj_kernel,
        grid=(bsz, n_i),
        in_specs=[
            _tok_spec(tm, d),
            pl.BlockSpec((None,) + mod.shape[1:], lambda b, i: (b, 0, 0)),
            _const_spec(g), _single_spec(perm), _single_spec(wm), _single_spec(wdin),
            _single_spec(wdec), _const_spec(bdec), _single_spec(wg),
        ] + late_in,
        out_specs=[_colmajor_spec(tm)] + [_tok_spec(tm, w) for w in widths] + late_out,
        out_shape=[ah_shape] + [jax.ShapeDtypeStruct((bsz, t, w), dt)
                                for w, dt in zip(widths, dtypes)] + late_shape,
        compiler_params=pltpu.CompilerParams(
            dimension_semantics=("parallel", "parallel"), vmem_limit_bytes=VMEM_LIMIT),
        name="proj",
    )(x, mod, g, perm, wm, wdin, wdec, bdec, wg, *late_weights)


def _gla_scale(qs, ks, las, dirn, want_out):
    tri_b, mid, last = dirn["tri_b"], dirn["mid"], dirn["last"]
    la_hi, la_lo = _split_bf16(las)
    g = _dot(tri_b, la_hi) + _dot(tri_b, la_lo)
    u = dict(kt=[], kend=[], qt=[], qg=[], e=[])
    for c in range(SUPER // CHUNK):
        rows = slice(c * CHUNK, (c + 1) * CHUNK)
        gc = g[rows]
        g_mid = gc[mid:mid + 1]
        g_last = gc[last:last + 1]
        u["e"].append(jnp.exp(g_last))
        kt = ks[rows].astype(F32) * jnp.exp(g_mid - gc)
        u["kt"].append(kt)
        u["kend"].append(kt * jnp.exp(g_last - g_mid))
        if want_out:
            qt = qs[rows].astype(F32) * jnp.exp(gc - g_mid)
            u["qt"].append(qt.astype(BF16))
            u["qg"].append((qt * jnp.exp(g_mid)).astype(BF16))
    return u


def _pair_lanes(pair):
    return slice(LANES * pair, LANES * (pair + 1))


def _gla_scores(u, pair, head_lane):
    lanes = _pair_lanes(pair)
    kt = jnp.concatenate([c[:, lanes] for c in u["kt"]], axis=0)
    qt = jnp.concatenate([c[:, lanes] for c in u["qt"]], axis=0)
    kt2 = jnp.concatenate([(kt * hm).astype(BF16) for hm in head_lane], axis=0)
    return lax.dot_general(qt, kt2, NT, preferred_element_type=F32)


def _gla_update(u, vs, h, head_lane):
    lanes = _pair_lanes(h // 2)
    hm = head_lane[h % 2]
    k0 = (u["kend"][0][:, lanes] * hm).astype(BF16)
    k1 = (u["kend"][1][:, lanes] * hm).astype(BF16)
    z = jnp.zeros_like(k0)
    kbd = jnp.concatenate([jnp.concatenate([k0, z], axis=1),
                           jnp.concatenate([z, k1], axis=1)], axis=0)
    return lax.dot_general(vs[:, GLA_DV * h:GLA_DV * (h + 1)], kbd, TN,
                           preferred_element_type=F32)


def _gla_intra(scores, vs, pair, mask2):
    a = jnp.where(mask2, scores, 0.0).astype(BF16)
    v0 = vs[:, GLA_DV * 2 * pair:GLA_DV * (2 * pair + 1)]
    v1 = vs[:, GLA_DV * (2 * pair + 1):GLA_DV * (2 * pair + 2)]
    z = jnp.zeros_like(v0)
    vbd = jnp.concatenate([jnp.concatenate([v0, z], axis=1),
                           jnp.concatenate([z, v1], axis=1)], axis=0)
    return _dot(a, vbd)


def _gla_kernel(q_ref, k_ref, v_ref, laf_ref, lab_ref, rs_ref, g_ref,
                ctx_ref, modc_ref, gpre_ref, wm_ref, wdin_ref, wdec_ref, bdec_ref,
                o_ref, of_ref, ob_ref, sf_ref, sb_ref, kc_ref, vc_ref, lafc_ref, labc_ref):
    t = q_ref.shape[0]
    t_ctx = kc_ref.shape[0]

    ubc = _prenorm_mod(ctx_ref[...], gpre_ref[...], modc_ref)
    zc = _split_bf16(_dot(ubc, wdin_ref[...]))
    kc_ref[...] = _dot(ubc, wm_ref[:, COL_K:COL_V]).astype(BF16)
    vc_ref[...] = _dot(ubc, wm_ref[:, COL_V:COL_R]).astype(BF16)
    lac = _decay_logs(zc, wdec_ref, bdec_ref)
    lafc_ref[...] = lac[:, :HK]
    labc_ref[...] = lac[:, HK:]

    half = t // 2
    n_steps = t // STEP
    supers = STEP // SUPER
    n_chunks = SUPER // CHUNK
    pairs = GLA_HEADS // 2

    row = lax.broadcasted_iota(jnp.int32, (SUPER, SUPER), 0)
    col = lax.broadcasted_iota(jnp.int32, (SUPER, SUPER), 1)
    same_chunk = (row // CHUNK) == (col // CHUNK)
    low_mask = jnp.logical_and(same_chunk, col <= row)
    up_mask = jnp.logical_and(same_chunk, col >= row)
    lane = lax.broadcasted_iota(jnp.int32, (1, LANES), 1)
    head_lane = (jnp.where(lane < GLA_DK, 1.0, 0.0), jnp.where(lane >= GLA_DK, 1.0, 0.0))

    fwd = dict(tri_b=jnp.where(low_mask, 1.0, 0.0).astype(BF16),
               mask2=jnp.concatenate([low_mask, low_mask], axis=1),
               mid=CHUNK // 2 - 1, last=CHUNK - 1, order=(0, 1), st_ref=sf_ref)
    bwd = dict(tri_b=jnp.where(up_mask, 1.0, 0.0).astype(BF16),
               mask2=jnp.concatenate([up_mask, up_mask], axis=1),
               mid=CHUNK // 2, last=0, order=(1, 0), st_ref=sb_ref)

    sf_ref[...] = jnp.zeros_like(sf_ref)
    sb_ref[...] = jnp.zeros_like(sb_ref)
    gn = g_ref[...]

    def run(units, want_out, finalize):
        vals = []
        for dirn, refs, t0 in units:
            q_r, k_r, v_r, la_r = refs
            rows = pl.ds(t0, SUPER)
            qs = q_r[rows, :] if want_out else None
            vals.append((_gla_scale(qs, k_r[rows, :], la_r[rows, :], dirn, want_out),
                         v_r[rows, :]))
        scores = []
        if want_out:
            scores = [[_gla_scores(u, p, head_lane) for p in range(pairs)] for u, _ in vals]
        upd = [[_gla_update(u, vs, h, head_lane) for h in range(GLA_HEADS)] for u, vs in vals]
        intra = []
        if want_out:
            intra = [[_gla_intra(scores[n][p], vals[n][1], p, units[n][0]["mask2"])
                      for p in range(pairs)] for n in range(len(units))]
        states = {}
        for n, (dirn, _, t0) in enumerate(units):
            key = id(dirn["st_ref"])
            if key not in states:
                states[key] = (dirn["st_ref"], [dirn["st_ref"][h] for h in range(GLA_HEADS)])
            st = states[key][1]
            u = vals[n][0]
            inter = [[None] * n_chunks for _ in range(pairs)]
            for c in dirn["order"]:
                if want_out:
                    for p in range(pairs):
                        st2 = jnp.concatenate([st[2 * p].astype(BF16),
                                               st[2 * p + 1].astype(BF16)], axis=0)
                        inter[p][c] = lax.dot_general(u["qg"][c][:, _pair_lanes(p)], st2, NT,
                                                      preferred_element_type=F32)
                for h in range(GLA_HEADS):
                    e = u["e"][c][:, _pair_lanes(h // 2)]
                    st[h] = st[h] * e + upd[n][h][:, LANES * c:LANES * (c + 1)]
            if not want_out:
                continue
            fwd_unit = dirn is fwd
            for p in range(pairs):
                o2 = intra[n][p] + jnp.concatenate(inter[p], axis=0)
                for j in range(2):
                    cols = slice(GLA_DV * (2 * p + j), GLA_DV * (2 * p + j + 1))
                    o = o2[:, GLA_DV * j:GLA_DV * (j + 1)]
                    if finalize:
                        other = (ob_ref[pl.ds(t0 - half, SUPER), cols] if fwd_unit
                                 else of_ref[pl.ds(t0, SUPER), cols])
                        rs = rs_ref[pl.ds(t0, SUPER), cols].astype(F32)
                        o_ref[pl.ds(t0, SUPER), cols] = (_rms(o + other, gn) * rs).astype(BF16)
                    elif fwd_unit:
                        of_ref[pl.ds(t0, SUPER), cols] = o
                    else:
                        ob_ref[pl.ds(t0 - half, SUPER), cols] = o
        for st_ref, st in states.values():
            for h in range(GLA_HEADS):
                st_ref[h] = st[h]

    ctx_units = []
    for s in range(t_ctx // SUPER):
        ctx_units.append((fwd, (None, kc_ref, vc_ref, lafc_ref), s * SUPER))
        ctx_units.append((bwd, (None, kc_ref, vc_ref, labc_ref), t_ctx - (s + 1) * SUPER))
    run(ctx_units, want_out=False, finalize=False)

    def step(i, finalize):
        units = []
        for s in range(supers):
            t0 = pl.multiple_of(i * STEP + s * SUPER, SUPER)
            t1 = pl.multiple_of(t - (i + 1) * STEP + (supers - 1 - s) * SUPER, SUPER)
            units.append((fwd, (q_ref, k_ref, v_ref, laf_ref), t0))
            units.append((bwd, (q_ref, k_ref, v_ref, lab_ref), t1))
        run(units, want_out=True, finalize=finalize)

    def park_body(i, carry):
        step(i, False)
        return carry

    def finish_body(i, carry):
        step(i, True)
        return carry

    lax.fori_loop(0, n_steps // 2, park_body, 0)
    lax.fori_loop(n_steps // 2, n_steps, finish_body, 0)


def _gla(q, k, v, laf, lab, rs, gnorm, ctx, mod_ctx, gpre, wm, wdin, wdec, bdec):
    bsz, t, _ = q.shape
    t_ctx = ctx.shape[1]
    assert t % (2 * STEP) == 0 and t_ctx % SUPER == 0

    def bspec(arr):
        return pl.BlockSpec((None,) + arr.shape[1:], lambda b: (b, 0, 0))

    args = (q, k, v, laf, lab, rs)
    return pl.pallas_call(
        _gla_kernel,
        grid=(bsz,),
        in_specs=[bspec(a) for a in args] + [_const_spec(gnorm), bspec(ctx),
                  pl.BlockSpec((None,) + mod_ctx.shape[1:], lambda b: (0, 0, 0)),
                  _const_spec(gpre), _single_spec(wm), _single_spec(wdin), _single_spec(wdec),
                  _const_spec(bdec)],
        out_specs=pl.BlockSpec((None, t, HV), lambda b: (b, 0, 0)),
        out_shape=jax.ShapeDtypeStruct((bsz, t, HV), BF16),
        scratch_shapes=[
            pltpu.VMEM((t // 2, HV), F32), pltpu.VMEM((t // 2, HV), F32),
            pltpu.VMEM((GLA_HEADS, GLA_DV, LANES), F32),
            pltpu.VMEM((GLA_HEADS, GLA_DV, LANES), F32),
            pltpu.VMEM((t_ctx, HK), BF16), pltpu.VMEM((t_ctx, HV), BF16),
            pltpu.VMEM((t_ctx, HK), F32), pltpu.VMEM((t_ctx, HK), F32),
        ],
        compiler_params=pltpu.CompilerParams(
            dimension_semantics=("parallel",), vmem_limit_bytes=VMEM_LIMIT),
        name="gla",
    )(*args, gnorm, ctx, mod_ctx, gpre, wm, wdin, wdec, bdec)


def _tap_segments(n_tiles, lines_per_tile, n_lines, group):
    segs = []
    for t0 in range(0, n_tiles, group):
        t1 = min(t0 + group, n_tiles)
        lo_line = int(t0 * lines_per_tile)
        hi_line = int(-(-t1 * lines_per_tile // 1)) - 1
        k_lo = max(0, CONV_PAD - hi_line)
        k_hi = min(CONV_K, n_lines + CONV_PAD - lo_line)
        if segs and segs[-1][2:] == (k_lo, k_hi):
            segs[-1] = (segs[-1][0], t1, k_lo, k_hi)
        else:
            segs.append((t0, t1, k_lo, k_hi))
    return segs


def _conv_kernel(ah_ref, av_ref, w_ref, b_ref, yh_ref, yv_ref, ph_ref, pv_ref, wb_ref):
    t = ah_ref.shape[0]
    rows = t // GRID_W
    tile = 64
    for kk in range(CONV_K):
        wb_ref[kk] = jnp.broadcast_to(w_ref[kk:kk + 1, :], wb_ref.shape[1:])

    def conv(src_ref, pad_ref, w_cols, line, n_lines, group, dst_ref):
        pad = CONV_PAD * line
        pad_ref[pl.ds(0, pad), :] = jnp.zeros((pad, CONV_HALF), F32)
        pad_ref[pl.ds(pad + t, pad), :] = jnp.zeros((pad, CONV_HALF), F32)

        def fill(i, carry):
            t0 = pl.multiple_of(i * tile, tile)
            pad_ref[pl.ds(pl.multiple_of(pad + t0, 8), tile), :] = (
                src_ref[pl.ds(t0, tile), :].astype(F32))
            return carry

        lax.fori_loop(0, t // tile, fill, 0)
        bias = b_ref[:, w_cols]

        for first, last, k_lo, k_hi in _tap_segments(t // tile, tile / line, n_lines, group):
            def body(i, carry, k_lo=k_lo, k_hi=k_hi):
                t0 = pl.multiple_of(i * tile, tile)
                acc = jnp.broadcast_to(bias, (tile, CONV_HALF))
                for kk in range(k_lo, k_hi):
                    win = pad_ref[pl.ds(pl.multiple_of(t0 + kk * line, 8), tile), :]
                    w8 = wb_ref[kk, :, w_cols]
                    acc = acc + jnp.concatenate([w8] * (tile // 8), axis=0) * win
                dst_ref[pl.ds(t0, tile), :] = acc
                return carry

            lax.fori_loop(first, last, body, 0)

    conv(ah_ref, ph_ref, slice(0, CONV_HALF), rows, GRID_W, 2, yh_ref)
    conv(av_ref, pv_ref, slice(CONV_HALF, CONV_DIM), GRID_W, rows, 4, yv_ref)


def _conv(ah, av, w, b):
    bsz, t, c = ah.shape
    rows = t // GRID_W
    tok = pl.BlockSpec((None, t, c), lambda i: (i, 0, 0))
    return pl.pallas_call(
        _conv_kernel,
        grid=(bsz,),
        in_specs=[tok, tok, _const_spec(w), _const_spec(b)],
        out_specs=[tok, tok],
        out_shape=[jax.ShapeDtypeStruct((bsz, t, c), F32)] * 2,
        scratch_shapes=[
            pltpu.VMEM((t + 2 * CONV_PAD * rows, c), F32),
            pltpu.VMEM((t + 2 * CONV_PAD * GRID_W, c), F32),
            pltpu.VMEM((CONV_K, 8, 2 * c), F32),
        ],
        compiler_params=pltpu.CompilerParams(
            dimension_semantics=("parallel",), vmem_limit_bytes=VMEM_LIMIT),
        name="conv",
    )(ah, av, w, b)


def _post_kernel(x_ref, yh_ref, yv_ref, og_ref, gt_ref, mod_ref, perm_ref, lng_ref, lnb_ref,
                 gpost1_ref, gpre2_ref, gpost2_ref, wco_ref, wgo_ref, wout_ref, wff1_ref,
                 wff2_ref, o_ref):
    d = x_ref.shape[-1]
    d_ff = wff1_ref.shape[1]
    ff_tile = 1024
    sub = 256
    subs = [slice(s, s + sub) for s in range(0, x_ref.shape[0], sub)]
    gt1 = mod_ref[2:3, :]
    sh2 = mod_ref[3:4, :]
    sc2 = mod_ref[4:5, :]
    gt2 = mod_ref[5:6, :]

    yh = []
    for n in range(x_ref.shape[0] // PERM_TOKENS):
        blk = yh_ref[:, PERM_ROWS * n:PERM_ROWS * (n + 1), :].reshape(PERM_TOKENS, CONV_HALF)
        hi, lo = _split_bf16(blk)
        yh.append(_dot(perm_ref[...], hi) + _dot(perm_ref[...], lo))
    yh = jnp.concatenate(yh, axis=0)

    yn = []
    for s in subs:
        yc = jnp.concatenate([yh[s], yv_ref[s, :]], axis=-1)
        yc = yc - jnp.mean(yc, axis=-1, keepdims=True)
        var = jnp.mean(yc * yc, axis=-1, keepdims=True)
        z = yc * lax.rsqrt(var + EPS) * lng_ref[...] + lnb_ref[...]
        yn.append((z * _sigmoid(z)).astype(BF16))
    y_gla = [_dot(og_ref[s, :], wgo_ref[...]) for s in subs]
    y_conv = [_dot(v, wco_ref[...]) for v in yn]
    merged = []
    for n, s in enumerate(subs):
        gates = gt_ref[s, :].astype(F32)
        merged.append((gates[:, :d] * y_conv[n] + gates[:, d:] * y_gla[n]).astype(BF16))
    y = [_dot(v, wout_ref[...]) for v in merged]
    h_mid = [x_ref[s, :] + gt1 * _rms(y[n], gpost1_ref[...]) for n, s in enumerate(subs)]
    u2 = [(_rms(h, gpre2_ref[...]) * (1.0 + sc2) + sh2).astype(BF16) for h in h_mid]
    acc = [jnp.zeros(h.shape, F32) for h in h_mid]
    for j in range(d_ff // ff_tile):
        cols = slice(j * ff_tile, (j + 1) * ff_tile)
        f = [_dot(v, wff1_ref[:, cols]) for v in u2]
        f = [jnp.square(jnp.maximum(v, 0.0)).astype(BF16) for v in f]
        acc = [acc[n] + _dot(f[n], wff2_ref[cols, :]) for n in range(len(subs))]
    for n, s in enumerate(subs):
        o_ref[s, :] = h_mid[n] + gt2 * _rms(acc[n], gpost2_ref[...])


def _post(x, yh, yv, og, gates, mod, perm, ln_g, ln_b, gpost1, gpre2, gpost2, wco, wgo, wout,
          wff1, wff2, tm):
    bsz, t, d = x.shape
    assert tm % PERM_TOKENS == 0
    return pl.pallas_call(
        _post_kernel,
        grid=(bsz, t // tm),
        in_specs=[
            _tok_spec(tm, d), _colmajor_spec(tm), _tok_spec(tm, CONV_HALF), _tok_spec(tm, HV),
            _tok_spec(tm, 2 * d),
            pl.BlockSpec((None,) + mod.shape[1:], lambda b, i: (b, 0, 0)),
            _single_spec(perm), _const_spec(ln_g), _const_spec(ln_b),
            _const_spec(gpost1), _const_spec(gpre2), _const_spec(gpost2),
            _single_spec(wco), _single_spec(wgo), _single_spec(wout), _single_spec(wff1),
            _single_spec(wff2),
        ],
        out_specs=_tok_spec(tm, d),
        out_shape=jax.ShapeDtypeStruct((bsz, t, d), F32),
        compiler_params=pltpu.CompilerParams(
            dimension_semantics=("parallel", "parallel"), vmem_limit_bytes=VMEM_LIMIT),
        name="post",
    )(x, yh, yv, og, gates, mod, perm, ln_g, ln_b, gpost1, gpre2, gpost2, wco, wgo, wout, wff1,
      wff2)


def kernel(x, c, ctx, c_ctx, w_mod, b_mod, g_pre1, g_post1, g_pre2, g_post2, w_in, conv_w,
           conv_b, conv_ln_g, conv_ln_b, w_conv_out, w_decay, b_decay, gla_norm_g, w_gla_out,
           w_out, w_ff1, w_ff2):
    bsz, t, d = x.shape
    depth = w_in.shape[0]
    assert depth == 1 and d == D_MODEL and t % (GRID_W * 8) == 0

    n_rows = -(-(bsz + 1) // 8) * 8
    c_all = jnp.zeros((n_rows, d), F32).at[:bsz].set(c).at[bsz].set(c_ctx)
    m = _adaln(c_all, w_mod, b_mod)
    mod = m[:bsz].reshape(bsz, N_MOD, d)
    mod_ctx = m[bsz, :2 * d].reshape(1, 2, d)

    wm, wdin, wg = _split_w_in(jnp.swapaxes(w_in[0], 0, 1))
    wdec = jnp.zeros((LANES, 2 * HK), F32)
    wdec = wdec.at[:DECAY_RANK, :HK].set(w_decay[0, 0])
    wdec = wdec.at[DECAY_RANK:2 * DECAY_RANK, HK:].set(w_decay[0, 1]).astype(BF16)
    bdec = b_decay[0].reshape(1, 2 * HK)
    row = lambda v: v.reshape(1, -1)

    perm = _to_colmajor_perm()
    to_colmajor = jnp.asarray(perm, BF16)
    to_rowmajor = jnp.asarray(perm.T, BF16)

    ah, av, q, k, v, rs, laf, lab, gates, wco, wgo, wout, wff1, wff2 = _proj(
        x, mod, row(g_pre1[0]), to_colmajor, wm, wdin, wdec, bdec, wg,
        (w_conv_out, w_gla_out, w_out, w_ff1, w_ff2), tm=1024)
    og = _gla(q, k, v, laf, lab, rs, row(gla_norm_g[0]),
              ctx, mod_ctx, row(g_pre1[0]), wm, wdin, wdec, bdec)
    yh, yv = _conv(ah.reshape(bsz, t, CONV_HALF), av, conv_w[0], row(conv_b[0]))
    yh = yh.reshape(bsz, GRID_W, t // GRID_W, CONV_HALF)

    return _post(x, yh, yv, og, gates, mod, to_rowmajor, row(conv_ln_g[0]), row(conv_ln_b[0]),
                 row(g_post1[0]), row(g_pre2[0]), row(g_post2[0]),
                 wco, wgo, wout, wff1, wff2, tm=512)
```

```python
import functools

import jax
import jax.numpy as jnp
import numpy as np
from jax import lax
from jax.experimental import pallas as pl
from jax.experimental.pallas import tpu as pltpu

D_MODEL = 1024
GRID_W = 64
CONV_DIM = 512
CONV_K = 31
GLA_HEADS = 4
GLA_DK = 64
GLA_DV = 128
DECAY_RANK = 16
GATE_NORM = 16.0
CHUNK = 64
N_MOD = 6
EPS = 1e-6

COL_Q = 2 * CONV_DIM
COL_K = COL_Q + GLA_HEADS * GLA_DK
COL_V = COL_K + GLA_HEADS * GLA_DK
COL_R = COL_V + GLA_HEADS * GLA_DV
COL_DEC = COL_R + GLA_HEADS * GLA_DV
COL_GATE = COL_DEC + 2 * DECAY_RANK
COL_END = COL_GATE + 2 * D_MODEL

LANES = 128
HK = GLA_HEADS * GLA_DK
HV = GLA_HEADS * GLA_DV
CONV_HALF = CONV_DIM // 2
CONV_PAD = CONV_K // 2
PERM_ROWS = 8
PERM_TOKENS = PERM_ROWS * GRID_W
PROJ_SUB = 256
SUPER = 2 * CHUNK
STEP = 8 * SUPER

VMEM_LIMIT = 56 * 1024 * 1024

F32 = jnp.float32
BF16 = jnp.bfloat16

NT = (((1,), (1,)), ((), ()))
TN = (((0,), (0,)), ((), ()))


def _dot(a, b):
    return jnp.dot(a, b, preferred_element_type=F32)


def _rms(x, g):
    ms = jnp.mean(x * x, axis=-1, keepdims=True)
    return x * lax.rsqrt(ms + EPS) * g


def _sigmoid(x):
    return jax.nn.sigmoid(x)


def _split_bf16(x):
    hi = x.astype(BF16)
    lo = (x - hi.astype(F32)).astype(BF16)
    return hi, lo


def _to_colmajor_perm():
    out = np.arange(PERM_TOKENS)
    src = (out % PERM_ROWS) * GRID_W + out // PERM_ROWS
    p = np.zeros((PERM_TOKENS, PERM_TOKENS), np.float32)
    p[out, src] = 1.0
    return p


def _adaln_kernel(c_ref, w_ref, b_ref, o_ref):
    @pl.when(pl.program_id(0) == 0)
    def _():
        o_ref[...] = jnp.broadcast_to(b_ref[...], o_ref.shape)

    c = c_ref[...]
    s_hi, s_lo = _split_bf16(c * _sigmoid(c))
    w_hi, w_lo = _split_bf16(w_ref[...])
    o_ref[...] += _dot(s_hi, w_hi) + _dot(s_lo, w_hi) + _dot(s_hi, w_lo)


def _adaln(c_all, w_mod, b_mod):
    n_rows, d = c_all.shape
    n_out = w_mod.shape[-1]
    tk = 256
    return pl.pallas_call(
        _adaln_kernel,
        grid=(d // tk,),
        in_specs=[
            pl.BlockSpec((n_rows, tk), lambda k: (0, k)),
            pl.BlockSpec((None, tk, n_out), lambda k: (0, k, 0)),
            pl.BlockSpec((1, n_out), lambda k: (0, 0)),
        ],
        out_specs=pl.BlockSpec((n_rows, n_out), lambda k: (0, 0)),
        out_shape=jax.ShapeDtypeStruct((n_rows, n_out), F32),
        compiler_params=pltpu.CompilerParams(
            dimension_semantics=("arbitrary",), vmem_limit_bytes=VMEM_LIMIT),
        name="adaln",
    )(c_all, w_mod, b_mod)


def _split_w_in_kernel(wm_t_ref, wd_t_ref, wg_t_ref, wm_ref, wdin_ref, wg_ref, *, n_gate):
    j = pl.program_id(0)
    wm_ref[...] = wm_t_ref[...].astype(BF16).T

    @pl.when(j == 0)
    def _():
        wdin_ref[...] = wd_t_ref[...].astype(BF16).T

    @pl.when(j < n_gate)
    def _():
        wg_ref[...] = wg_t_ref[...].astype(BF16).T


def _split_w_in(w_in_t):
    n, d = w_in_t.shape
    tr = 512
    n_main = COL_DEC // tr
    n_gate = (n - COL_GATE) // tr
    assert COL_DEC % tr == 0 and (n - COL_GATE) % tr == 0 and n_gate <= n_main
    gate_blk = lambda j: jnp.minimum(j, n_gate - 1)
    return pl.pallas_call(
        functools.partial(_split_w_in_kernel, n_gate=n_gate),
        grid=(n_main,),
        in_specs=[
            pl.BlockSpec((tr, d), lambda j: (j, 0)),
            pl.BlockSpec((LANES, d), lambda j: (COL_DEC // LANES, 0)),
            pl.BlockSpec((pl.Element(tr), pl.Element(d)),
                         lambda j: (pl.multiple_of(COL_GATE + tr * gate_blk(j), 32), 0)),
        ],
        out_specs=[
            pl.BlockSpec((d, tr), lambda j: (0, j)),
            pl.BlockSpec((d, LANES), lambda j: (0, 0)),
            pl.BlockSpec((d, tr), lambda j: (0, gate_blk(j))),
        ],
        out_shape=[jax.ShapeDtypeStruct((d, w), BF16) for w in (COL_DEC, LANES, n - COL_GATE)],
        compiler_params=pltpu.CompilerParams(
            dimension_semantics=("arbitrary",), vmem_limit_bytes=VMEM_LIMIT),
        name="split_w_in",
    )(w_in_t, w_in_t, w_in_t)


def _prenorm_mod(x, g, mod_ref):
    sh = mod_ref[0:1, :]
    sc = mod_ref[1:2, :]
    return (_rms(x, g) * (1.0 + sc) + sh).astype(BF16)


def _decay_logs(z, wdec_ref, bdec_ref):
    z_hi, z_lo = z
    logits = _dot(z_hi, wdec_ref[...]) + _dot(z_lo, wdec_ref[...]) + bdec_ref[...]
    ls = jnp.minimum(logits, 0.0) - jnp.log(1.0 + jnp.exp(-jnp.abs(logits)))
    return ls * (1.0 / GATE_NORM)


N_LATE_W = 5


def _proj_kernel(x_ref, mod_ref, g_ref, perm_ref, wm_ref, wdin_ref, wdec_ref, bdec_ref, wg_ref,
                 *refs):
    late_f32 = refs[:N_LATE_W]
    (ah_ref, vt_ref, av_ref, q_ref, k_ref, v_ref, rs_ref, laf_ref, lab_ref,
     gt_ref) = refs[N_LATE_W:-N_LATE_W]
    late_bf16 = refs[-N_LATE_W:]
    for src, dst in zip(late_f32, late_bf16):
        dst[...] = src[...].astype(BF16)
    subs = [slice(s, s + PROJ_SUB) for s in range(0, x_ref.shape[0], PROJ_SUB)]
    g = g_ref[...]
    ub = [_prenorm_mod(x_ref[s, :], g, mod_ref) for s in subs]
    z = [_split_bf16(_dot(u, wdin_ref[...])) for u in ub]
    a_h = []
    for n, s in enumerate(subs):
        glu = _dot(ub[n], wm_ref[:, 0:COL_Q])
        a = (glu[:, :CONV_DIM] * _sigmoid(glu[:, CONV_DIM:])).astype(BF16)
        av_ref[s, :] = a[:, CONV_HALF:]
        a_h.append(a[:, :CONV_HALF])
    for n, s in enumerate(subs):
        q_ref[s, :] = (_dot(ub[n], wm_ref[:, COL_Q:COL_K]) * (GLA_DK ** -0.5)).astype(BF16)
    for n, s in enumerate(subs):
        k_ref[s, :] = _dot(ub[n], wm_ref[:, COL_K:COL_V]).astype(BF16)
    for n, s in enumerate(subs):
        vb = _dot(ub[n], wm_ref[:, COL_V:COL_R]).astype(BF16)
        v_ref[s, :] = vb
        vt_ref[:, s] = vb.T
    for n, s in enumerate(subs):
        la = _decay_logs(z[n], wdec_ref, bdec_ref)
        laf_ref[s, :] = la[:, :HK]
        lab_ref[s, :] = la[:, HK:]
    for n, s in enumerate(subs):
        r = _dot(ub[n], wm_ref[:, COL_R:COL_DEC])
        rs_ref[s, :] = (r * _sigmoid(r)).astype(BF16)
    per_group = PERM_TOKENS // PROJ_SUB
    for n in range(x_ref.shape[0] // PERM_TOKENS):
        a_rm = jnp.concatenate(a_h[per_group * n:per_group * (n + 1)], axis=0)
        a_cm = _dot(perm_ref[...], a_rm)
        ah_ref[:, PERM_ROWS * n:PERM_ROWS * (n + 1), :] = a_cm.reshape(
            GRID_W, PERM_ROWS, CONV_HALF)
    for n, s in enumerate(subs):
        gt_ref[s, :] = _sigmoid(_dot(ub[n], wg_ref[...])).astype(BF16)


def _const_spec(arr):
    nd = arr.ndim
    return pl.BlockSpec(arr.shape, lambda *_: (0,) * nd)


def _tok_spec(tm, width):
    return pl.BlockSpec((None, tm, width), lambda b, i: (b, i, 0))


def _colmajor_spec(tm):
    return pl.BlockSpec((None, GRID_W, tm // GRID_W, CONV_HALF), lambda b, i: (b, 0, i, 0))


def _single_spec(arr):
    nd = arr.ndim
    return pl.BlockSpec(arr.shape, lambda *_: (0,) * nd, pipeline_mode=pl.Buffered(1))


def _proj(x, mod, g, perm, wm, wdin, wdec, bdec, wg, late_weights, tm):
    bsz, t, d = x.shape
    widths = (CONV_HALF, HK, HK, HV, HV, HK, HK, 2 * D_MODEL)
    dtypes = (BF16, BF16, BF16, BF16, BF16, F32, F32, BF16)
    assert tm % PERM_TOKENS == 0 and len(late_weights) == N_LATE_W
    ah_shape = jax.ShapeDtypeStruct((bsz, GRID_W, t // GRID_W, CONV_HALF), F32)
    n_i = t // tm
    n_steps = bsz * n_i
    late_in, late_out, late_shape = [], [], []
    for w in late_weights:
        _, rows, cols = w.shape
        assert rows % (16 * n_steps) == 0
        blk = rows // n_steps
        late_in.append(pl.BlockSpec((None, blk, cols), lambda b, i: (0, b * n_i + i, 0)))
        late_out.append(pl.BlockSpec((blk, cols), lambda b, i: (b * n_i + i, 0)))
        late_shape.append(jax.ShapeDtypeStruct((rows, cols), BF16))
    return pl.pallas_call(
        _proj_kernel,
        grid=(bsz, n_i),
        in_specs=[
            _tok_spec(tm, d),
            pl.BlockSpec((None,) + mod.shape[1:], lambda b, i: (b, 0, 0)),
            _const_spec(g), _single_spec(perm), _single_spec(wm), _single_spec(wdin),
            _single_spec(wdec), _const_spec(bdec), _single_spec(wg),
        ] + late_in,
        out_specs=[_colmajor_spec(tm), pl.BlockSpec((None, HV, tm), lambda b, i: (b, 0, i))]
        + [_tok_spec(tm, w) for w in widths] + late_out,
        out_shape=[ah_shape, jax.ShapeDtypeStruct((bsz, HV, t), BF16)]
        + [jax.ShapeDtypeStruct((bsz, t, w), dt) for w, dt in zip(widths, dtypes)] + late_shape,
        compiler_params=pltpu.CompilerParams(
            dimension_semantics=("parallel", "parallel"), vmem_limit_bytes=VMEM_LIMIT),
        name="proj",
    )(x, mod, g, perm, wm, wdin, wdec, bdec, wg, *late_weights)


def _gla_scale(qs, ks, las, dirn, want_out):
    tri_b, mid, last = dirn["tri_b"], dirn["mid"], dirn["last"]
    la_hi, la_lo = _split_bf16(las)
    g = _dot(tri_b, la_hi) + _dot(tri_b, la_lo)
    u = dict(kt=[], kend=[], qt=[], qg=[], e=[])
    for c in range(SUPER // CHUNK):
        rows = slice(c * CHUNK, (c + 1) * CHUNK)
        gc = g[rows]
        g_mid = gc[mid:mid + 1]
        g_last = gc[last:last + 1]
        u["e"].append(jnp.exp(g_last))
        kt = ks[rows].astype(F32) * jnp.exp(g_mid - gc)
        u["kt"].append(kt.astype(BF16))
        u["kend"].append((kt * jnp.exp(g_last - g_mid)).astype(BF16))
        if want_out:
            qt = qs[rows].astype(F32) * jnp.exp(gc - g_mid)
            u["qt"].append(qt.astype(BF16))
            u["qg"].append((qt * jnp.exp(g_mid)).astype(BF16))
    return u


def _pair_lanes(pair):
    return slice(LANES * pair, LANES * (pair + 1))


def _gla_scores(u, pair, head_lane):
    lanes = _pair_lanes(pair)
    kt = jnp.concatenate([c[:, lanes] for c in u["kt"]], axis=0)
    qt = jnp.concatenate([c[:, lanes] for c in u["qt"]], axis=0)
    kt2 = jnp.concatenate([kt * hm.astype(BF16) for hm in head_lane], axis=0)
    return lax.dot_general(qt, kt2, NT, preferred_element_type=F32)


def _gla_update(u, vt, h, head_lane):
    lanes = _pair_lanes(h // 2)
    hm = head_lane[h % 2].astype(BF16)
    k0 = u["kend"][0][:, lanes] * hm
    k1 = u["kend"][1][:, lanes] * hm
    z = jnp.zeros_like(k0)
    kbd = jnp.concatenate([jnp.concatenate([k0, z], axis=1),
                           jnp.concatenate([z, k1], axis=1)], axis=0)
    return _dot(vt[GLA_DV * h:GLA_DV * (h + 1), :], kbd)


def _gla_intra(scores, vs, pair, mask2):
    a = jnp.where(mask2, scores, 0.0).astype(BF16)
    v0 = vs[:, GLA_DV * 2 * pair:GLA_DV * (2 * pair + 1)]
    v1 = vs[:, GLA_DV * (2 * pair + 1):GLA_DV * (2 * pair + 2)]
    z = jnp.zeros_like(v0)
    vbd = jnp.concatenate([jnp.concatenate([v0, z], axis=1),
                           jnp.concatenate([z, v1], axis=1)], axis=0)
    return _dot(a, vbd)


def _gla_kernel(q_ref, k_ref, v_ref, vt_ref, laf_ref, lab_ref, rs_ref, g_ref,
                ctx_ref, modc_ref, gpre_ref, wm_ref, wdin_ref, wdec_ref, bdec_ref,
                o_ref, of_ref, ob_ref, sf_ref, sb_ref, kc_ref, vc_ref, vct_ref, lafc_ref,
                labc_ref):
    t = q_ref.shape[0]
    t_ctx = kc_ref.shape[0]

    ubc = _prenorm_mod(ctx_ref[...], gpre_ref[...], modc_ref)
    zc = _split_bf16(_dot(ubc, wdin_ref[...]))
    kc_ref[...] = _dot(ubc, wm_ref[:, COL_K:COL_V]).astype(BF16)
    vcb = _dot(ubc, wm_ref[:, COL_V:COL_R]).astype(BF16)
    vc_ref[...] = vcb
    vct_ref[...] = vcb.T
    lac = _decay_logs(zc, wdec_ref, bdec_ref)
    lafc_ref[...] = lac[:, :HK]
    labc_ref[...] = lac[:, HK:]

    half = t // 2
    n_steps = t // STEP
    supers = STEP // SUPER
    n_chunks = SUPER // CHUNK
    pairs = GLA_HEADS // 2

    row = lax.broadcasted_iota(jnp.int32, (SUPER, SUPER), 0)
    col = lax.broadcasted_iota(jnp.int32, (SUPER, SUPER), 1)
    same_chunk = (row // CHUNK) == (col // CHUNK)
    low_mask = jnp.logical_and(same_chunk, col <= row)
    up_mask = jnp.logical_and(same_chunk, col >= row)
    lane = lax.broadcasted_iota(jnp.int32, (1, LANES), 1)
    head_lane = (jnp.where(lane < GLA_DK, 1.0, 0.0), jnp.where(lane >= GLA_DK, 1.0, 0.0))

    fwd = dict(tri_b=jnp.where(low_mask, 1.0, 0.0).astype(BF16),
               mask2=jnp.concatenate([low_mask, low_mask], axis=1),
               mid=CHUNK // 2 - 1, last=CHUNK - 1, order=(0, 1), st_ref=sf_ref)
    bwd = dict(tri_b=jnp.where(up_mask, 1.0, 0.0).astype(BF16),
               mask2=jnp.concatenate([up_mask, up_mask], axis=1),
               mid=CHUNK // 2, last=0, order=(1, 0), st_ref=sb_ref)

    sf_ref[...] = jnp.zeros_like(sf_ref)
    sb_ref[...] = jnp.zeros_like(sb_ref)
    gn = g_ref[...]

    def run(units, want_out, finalize):
        vals = []
        for dirn, refs, t0 in units:
            q_r, k_r, v_r, vt_r, la_r = refs
            rows = pl.ds(t0, SUPER)
            qs = q_r[rows, :] if want_out else None
            vals.append((_gla_scale(qs, k_r[rows, :], la_r[rows, :], dirn, want_out),
                         v_r[rows, :], vt_r[:, rows]))
        scores = []
        if want_out:
            scores = [[_gla_scores(u, p, head_lane) for p in range(pairs)] for u, _, _ in vals]
        upd = [[_gla_update(u, vt, h, head_lane) for h in range(GLA_HEADS)]
               for u, _, vt in vals]
        intra = []
        if want_out:
            intra = [[_gla_intra(scores[n][p], vals[n][1], p, units[n][0]["mask2"])
                      for p in range(pairs)] for n in range(len(units))]
        states = {}
        for n, (dirn, _, t0) in enumerate(units):
            key = id(dirn["st_ref"])
            if key not in states:
                states[key] = (dirn["st_ref"], [dirn["st_ref"][h] for h in range(GLA_HEADS)])
            st = states[key][1]
            u = vals[n][0]
            inter = [[None] * n_chunks for _ in range(pairs)]
            for c in dirn["order"]:
                if want_out:
                    for p in range(pairs):
                        st2 = jnp.concatenate([st[2 * p].astype(BF16),
                                               st[2 * p + 1].astype(BF16)], axis=0)
                        inter[p][c] = lax.dot_general(u["qg"][c][:, _pair_lanes(p)], st2, NT,
                                                      preferred_element_type=F32)
                for h in range(GLA_HEADS):
                    e = u["e"][c][:, _pair_lanes(h // 2)]
                    st[h] = st[h] * e + upd[n][h][:, LANES * c:LANES * (c + 1)]
            if not want_out:
                continue
            fwd_unit = dirn is fwd
            for p in range(pairs):
                o2 = intra[n][p] + jnp.concatenate(inter[p], axis=0)
                for j in range(2):
                    cols = slice(GLA_DV * (2 * p + j), GLA_DV * (2 * p + j + 1))
                    o = o2[:, GLA_DV * j:GLA_DV * (j + 1)]
                    if finalize:
                        other = (ob_ref[pl.ds(t0 - half, SUPER), cols] if fwd_unit
                                 else of_ref[pl.ds(t0, SUPER), cols])
                        rs = rs_ref[pl.ds(t0, SUPER), cols].astype(F32)
                        o_ref[pl.ds(t0, SUPER), cols] = (_rms(o + other, gn) * rs).astype(BF16)
                    elif fwd_unit:
                        of_ref[pl.ds(t0, SUPER), cols] = o
                    else:
                        ob_ref[pl.ds(t0 - half, SUPER), cols] = o
        for st_ref, st in states.values():
            for h in range(GLA_HEADS):
                st_ref[h] = st[h]

    ctx_units = []
    for s in range(t_ctx // SUPER):
        ctx_units.append((fwd, (None, kc_ref, vc_ref, vct_ref, lafc_ref), s * SUPER))
        ctx_units.append((bwd, (None, kc_ref, vc_ref, vct_ref, labc_ref),
                          t_ctx - (s + 1) * SUPER))
    run(ctx_units, want_out=False, finalize=False)

    def step(i, finalize):
        units = []
        for s in range(supers):
            t0 = pl.multiple_of(i * STEP + s * SUPER, SUPER)
            t1 = pl.multiple_of(t - (i + 1) * STEP + (supers - 1 - s) * SUPER, SUPER)
            units.append((fwd, (q_ref, k_ref, v_ref, vt_ref, laf_ref), t0))
            units.append((bwd, (q_ref, k_ref, v_ref, vt_ref, lab_ref), t1))
        run(units, want_out=True, finalize=finalize)

    def park_body(i, carry):
        step(i, False)
        return carry

    def finish_body(i, carry):
        step(i, True)
        return carry

    lax.fori_loop(0, n_steps // 2, park_body, 0)
    lax.fori_loop(n_steps // 2, n_steps, finish_body, 0)


def _gla(q, k, v, vt, laf, lab, rs, gnorm, ctx, mod_ctx, gpre, wm, wdin, wdec, bdec):
    bsz, t, _ = q.shape
    t_ctx = ctx.shape[1]
    assert t % (2 * STEP) == 0 and t_ctx % SUPER == 0

    def bspec(arr):
        return pl.BlockSpec((None,) + arr.shape[1:], lambda b: (b, 0, 0))

    args = (q, k, v, vt, laf, lab, rs)
    return pl.pallas_call(
        _gla_kernel,
        grid=(bsz,),
        in_specs=[bspec(a) for a in args] + [_const_spec(gnorm), bspec(ctx),
                  pl.BlockSpec((None,) + mod_ctx.shape[1:], lambda b: (0, 0, 0)),
                  _const_spec(gpre), _single_spec(wm), _single_spec(wdin), _single_spec(wdec),
                  _const_spec(bdec)],
        out_specs=pl.BlockSpec((None, t, HV), lambda b: (b, 0, 0)),
        out_shape=jax.ShapeDtypeStruct((bsz, t, HV), BF16),
        scratch_shapes=[
            pltpu.VMEM((t // 2, HV), F32), pltpu.VMEM((t // 2, HV), F32),
            pltpu.VMEM((GLA_HEADS, GLA_DV, LANES), F32),
            pltpu.VMEM((GLA_HEADS, GLA_DV, LANES), F32),
            pltpu.VMEM((t_ctx, HK), BF16), pltpu.VMEM((t_ctx, HV), BF16),
            pltpu.VMEM((HV, t_ctx), BF16),
            pltpu.VMEM((t_ctx, HK), F32), pltpu.VMEM((t_ctx, HK), F32),
        ],
        compiler_params=pltpu.CompilerParams(
            dimension_semantics=("parallel",), vmem_limit_bytes=VMEM_LIMIT),
        name="gla",
    )(*args, gnorm, ctx, mod_ctx, gpre, wm, wdin, wdec, bdec)


def _tap_segments(n_tiles, lines_per_tile, n_lines, group):
    segs = []
    for t0 in range(0, n_tiles, group):
        t1 = min(t0 + group, n_tiles)
        lo_line = int(t0 * lines_per_tile)
        hi_line = int(-(-t1 * lines_per_tile // 1)) - 1
        k_lo = max(0, CONV_PAD - hi_line)
        k_hi = min(CONV_K, n_lines + CONV_PAD - lo_line)
        if segs and segs[-1][2:] == (k_lo, k_hi):
            segs[-1] = (segs[-1][0], t1, k_lo, k_hi)
        else:
            segs.append((t0, t1, k_lo, k_hi))
    return segs


def _conv_kernel(ah_ref, av_ref, w_ref, b_ref, yh_ref, yv_ref, ph_ref, pv_ref, wb_ref):
    t = ah_ref.shape[0]
    rows = t // GRID_W
    tile = 64
    for kk in range(CONV_K):
        wb_ref[kk] = jnp.broadcast_to(w_ref[kk:kk + 1, :], wb_ref.shape[1:])

    def conv(src_ref, pad_ref, w_cols, line, n_lines, group, dst_ref):
        pad = CONV_PAD * line
        pad_ref[pl.ds(0, pad), :] = jnp.zeros((pad, CONV_HALF), F32)
        pad_ref[pl.ds(pad + t, pad), :] = jnp.zeros((pad, CONV_HALF), F32)

        def fill(i, carry):
            t0 = pl.multiple_of(i * tile, tile)
            pad_ref[pl.ds(pl.multiple_of(pad + t0, 8), tile), :] = (
                src_ref[pl.ds(t0, tile), :].astype(F32))
            return carry

        lax.fori_loop(0, t // tile, fill, 0)
        bias = b_ref[:, w_cols]

        for first, last, k_lo, k_hi in _tap_segments(t // tile, tile / line, n_lines, group):
            def body(i, carry, k_lo=k_lo, k_hi=k_hi):
                t0 = pl.multiple_of(i * tile, tile)
                acc = jnp.broadcast_to(bias, (tile, CONV_HALF))
                for kk in range(k_lo, k_hi):
                    win = pad_ref[pl.ds(pl.multiple_of(t0 + kk * line, 8), tile), :]
                    w8 = wb_ref[kk, :, w_cols]
                    acc = acc + jnp.concatenate([w8] * (tile // 8), axis=0) * win
                dst_ref[pl.ds(t0, tile), :] = acc
                return carry

            lax.fori_loop(first, last, body, 0)

    conv(ah_ref, ph_ref, slice(0, CONV_HALF), rows, GRID_W, 2, yh_ref)
    conv(av_ref, pv_ref, slice(CONV_HALF, CONV_DIM), GRID_W, rows, 4, yv_ref)


def _conv(ah, av, w, b):
    bsz, t, c = ah.shape
    rows = t // GRID_W
    tok = pl.BlockSpec((None, t, c), lambda i: (i, 0, 0))
    return pl.pallas_call(
        _conv_kernel,
        grid=(bsz,),
        in_specs=[tok, tok, _const_spec(w), _const_spec(b)],
        out_specs=[tok, tok],
        out_shape=[jax.ShapeDtypeStruct((bsz, t, c), F32)] * 2,
        scratch_shapes=[
            pltpu.VMEM((t + 2 * CONV_PAD * rows, c), F32),
            pltpu.VMEM((t + 2 * CONV_PAD * GRID_W, c), F32),
            pltpu.VMEM((CONV_K, 8, 2 * c), F32),
        ],
        compiler_params=pltpu.CompilerParams(
            dimension_semantics=("parallel",), vmem_limit_bytes=VMEM_LIMIT),
        name="conv",
    )(ah, av, w, b)


def _post_kernel(x_ref, yh_ref, yv_ref, og_ref, gt_ref, mod_ref, perm_ref, lng_ref, lnb_ref,
                 gpost1_ref, gpre2_ref, gpost2_ref, wco_ref, wgo_ref, wout_ref, wff1_ref,
                 wff2_ref, o_ref):
    d = x_ref.shape[-1]
    d_ff = wff1_ref.shape[1]
    ff_tile = 1024
    sub = 256
    subs = [slice(s, s + sub) for s in range(0, x_ref.shape[0], sub)]
    gt1 = mod_ref[2:3, :]
    sh2 = mod_ref[3:4, :]
    sc2 = mod_ref[4:5, :]
    gt2 = mod_ref[5:6, :]

    yh = []
    for n in range(x_ref.shape[0] // PERM_TOKENS):
        blk = yh_ref[:, PERM_ROWS * n:PERM_ROWS * (n + 1), :].reshape(PERM_TOKENS, CONV_HALF)
        hi, lo = _split_bf16(blk)
        yh.append(_dot(perm_ref[...], hi) + _dot(perm_ref[...], lo))
    yh = jnp.concatenate(yh, axis=0)

    yn = []
    for s in subs:
        yc = jnp.concatenate([yh[s], yv_ref[s, :]], axis=-1)
        yc = yc - jnp.mean(yc, axis=-1, keepdims=True)
        var = jnp.mean(yc * yc, axis=-1, keepdims=True)
        z = yc * lax.rsqrt(var + EPS) * lng_ref[...] + lnb_ref[...]
        yn.append((z * _sigmoid(z)).astype(BF16))
    y_gla = [_dot(og_ref[s, :], wgo_ref[...]) for s in subs]
    y_conv = [_dot(v, wco_ref[...]) for v in yn]
    merged = []
    for n, s in enumerate(subs):
        gates = gt_ref[s, :].astype(F32)
        merged.append((gates[:, :d] * y_conv[n] + gates[:, d:] * y_gla[n]).astype(BF16))
    y = [_dot(v, wout_ref[...]) for v in merged]
    h_mid = [x_ref[s, :] + gt1 * _rms(y[n], gpost1_ref[...]) for n, s in enumerate(subs)]
    u2 = [(_rms(h, gpre2_ref[...]) * (1.0 + sc2) + sh2).astype(BF16) for h in h_mid]
    acc = [jnp.zeros(h.shape, F32) for h in h_mid]
    for j in range(d_ff // ff_tile):
        cols = slice(j * ff_tile, (j + 1) * ff_tile)
        f = [_dot(v, wff1_ref[:, cols]) for v in u2]
        f = [jnp.square(jnp.maximum(v, 0.0)).astype(BF16) for v in f]
        acc = [acc[n] + _dot(f[n], wff2_ref[cols, :]) for n in range(len(subs))]
    for n, s in enumerate(subs):
        o_ref[s, :] = h_mid[n] + gt2 * _rms(acc[n], gpost2_ref[...])


def _post(x, yh, yv, og, gates, mod, perm, ln_g, ln_b, gpost1, gpre2, gpost2, wco, wgo, wout,
          wff1, wff2, tm):
    bsz, t, d = x.shape
    assert tm % PERM_TOKENS == 0
    return pl.pallas_call(
        _post_kernel,
        grid=(bsz, t // tm),
        in_specs=[
            _tok_spec(tm, d), _colmajor_spec(tm), _tok_spec(tm, CONV_HALF), _tok_spec(tm, HV),
            _tok_spec(tm, 2 * d),
            pl.BlockSpec((None,) + mod.shape[1:], lambda b, i: (b, 0, 0)),
            _single_spec(perm), _const_spec(ln_g), _const_spec(ln_b),
            _const_spec(gpost1), _const_spec(gpre2), _const_spec(gpost2),
            _single_spec(wco), _single_spec(wgo), _single_spec(wout), _single_spec(wff1),
            _single_spec(wff2),
        ],
        out_specs=_tok_spec(tm, d),
        out_shape=jax.ShapeDtypeStruct((bsz, t, d), F32),
        compiler_params=pltpu.CompilerParams(
            dimension_semantics=("parallel", "parallel"), vmem_limit_bytes=VMEM_LIMIT),
        name="post",
    )(x, yh, yv, og, gates, mod, perm, ln_g, ln_b, gpost1, gpre2, gpost2, wco, wgo, wout, wff1,
      wff2)


def kernel(x, c, ctx, c_ctx, w_mod, b_mod, g_pre1, g_post1, g_pre2, g_post2, w_in, conv_w,
           conv_b, conv_ln_g, conv_ln_b, w_conv_out, w_decay, b_decay, gla_norm_g, w_gla_out,
           w_out, w_ff1, w_ff2):
    bsz, t, d = x.shape
    depth = w_in.shape[0]
    assert depth == 1 and d == D_MODEL and t % (GRID_W * 8) == 0

    n_rows = -(-(bsz + 1) // 8) * 8
    c_all = jnp.zeros((n_rows, d), F32).at[:bsz].set(c).at[bsz].set(c_ctx)
    m = _adaln(c_all, w_mod, b_mod)
    mod = m[:bsz].reshape(bsz, N_MOD, d)
    mod_ctx = m[bsz, :2 * d].reshape(1, 2, d)

    wm, wdin, wg = _split_w_in(jnp.swapaxes(w_in[0], 0, 1))
    wdec = jnp.zeros((LANES, 2 * HK), F32)
    wdec = wdec.at[:DECAY_RANK, :HK].set(w_decay[0, 0])
    wdec = wdec.at[DECAY_RANK:2 * DECAY_RANK, HK:].set(w_decay[0, 1]).astype(BF16)
    bdec = b_decay[0].reshape(1, 2 * HK)
    row = lambda v: v.reshape(1, -1)

    perm = _to_colmajor_perm()
    to_colmajor = jnp.asarray(perm, BF16)
    to_rowmajor = jnp.asarray(perm.T, BF16)

    ah, vt, av, q, k, v, rs, laf, lab, gates, wco, wgo, wout, wff1, wff2 = _proj(
        x, mod, row(g_pre1[0]), to_colmajor, wm, wdin, wdec, bdec, wg,
        (w_conv_out, w_gla_out, w_out, w_ff1, w_ff2), tm=1024)
    og = _gla(q, k, v, vt, laf, lab, rs, row(gla_norm_g[0]),
              ctx, mod_ctx, row(g_pre1[0]), wm, wdin, wdec, bdec)
    yh, yv = _conv(ah.reshape(bsz, t, CONV_HALF), av, conv_w[0], row(conv_b[0]))
    yh = yh.reshape(bsz, GRID_W, t // GRID_W, CONV_HALF)

    return _post(x, yh, yv, og, gates, mod, to_rowmajor, row(conv_ln_g[0]), row(conv_ln_b[0]),
                 row(g_post1[0]), row(g_pre2[0]), row(g_post2[0]),
                 wco, wgo, wout, wff1, wff2, tm=512)
```

```python
import functools

import jax
import jax.numpy as jnp
import numpy as np
from jax import lax
from jax.experimental import pallas as pl
from jax.experimental.pallas import tpu as pltpu

D_MODEL = 1024
GRID_W = 64
CONV_DIM = 512
CONV_K = 31
GLA_HEADS = 4
GLA_DK = 64
GLA_DV = 128
DECAY_RANK = 16
GATE_NORM = 16.0
CHUNK = 64
N_MOD = 6
EPS = 1e-6

COL_Q = 2 * CONV_DIM
COL_K = COL_Q + GLA_HEADS * GLA_DK
COL_V = COL_K + GLA_HEADS * GLA_DK
COL_R = COL_V + GLA_HEADS * GLA_DV
COL_DEC = COL_R + GLA_HEADS * GLA_DV
COL_GATE = COL_DEC + 2 * DECAY_RANK
COL_END = COL_GATE + 2 * D_MODEL

LANES = 128
HK = GLA_HEADS * GLA_DK
HV = GLA_HEADS * GLA_DV
CONV_HALF = CONV_DIM // 2
CONV_PAD = CONV_K // 2
PERM_ROWS = 8
PERM_TOKENS = PERM_ROWS * GRID_W
PROJ_SUB = 256
SUPER = 2 * CHUNK
STEP = 8 * SUPER

VMEM_LIMIT = 56 * 1024 * 1024

F32 = jnp.float32
BF16 = jnp.bfloat16

NT = (((1,), (1,)), ((), ()))
TN = (((0,), (0,)), ((), ()))


def _dot(a, b):
    return jnp.dot(a, b, preferred_element_type=F32)


def _rms(x, g):
    ms = jnp.mean(x * x, axis=-1, keepdims=True)
    return x * lax.rsqrt(ms + EPS) * g


def _sigmoid(x):
    return jax.nn.sigmoid(x)


def _split_bf16(x):
    hi = x.astype(BF16)
    lo = (x - hi.astype(F32)).astype(BF16)
    return hi, lo


def _to_colmajor_perm():
    out = np.arange(PERM_TOKENS)
    src = (out % PERM_ROWS) * GRID_W + out // PERM_ROWS
    p = np.zeros((PERM_TOKENS, PERM_TOKENS), np.float32)
    p[out, src] = 1.0
    return p


def _adaln_kernel(c_ref, w_ref, b_ref, o_ref):
    @pl.when(pl.program_id(0) == 0)
    def _():
        o_ref[...] = jnp.broadcast_to(b_ref[...], o_ref.shape)

    c = c_ref[...]
    s_hi, s_lo = _split_bf16(c * _sigmoid(c))
    w_hi, w_lo = _split_bf16(w_ref[...])
    o_ref[...] += _dot(s_hi, w_hi) + _dot(s_lo, w_hi) + _dot(s_hi, w_lo)


def _adaln(c_all, w_mod, b_mod):
    n_rows, d = c_all.shape
    n_out = w_mod.shape[-1]
    tk = 256
    return pl.pallas_call(
        _adaln_kernel,
        grid=(d // tk,),
        in_specs=[
            pl.BlockSpec((n_rows, tk), lambda k: (0, k)),
            pl.BlockSpec((None, tk, n_out), lambda k: (0, k, 0)),
            pl.BlockSpec((1, n_out), lambda k: (0, 0)),
        ],
        out_specs=pl.BlockSpec((n_rows, n_out), lambda k: (0, 0)),
        out_shape=jax.ShapeDtypeStruct((n_rows, n_out), F32),
        compiler_params=pltpu.CompilerParams(
            dimension_semantics=("arbitrary",), vmem_limit_bytes=VMEM_LIMIT),
        name="adaln",
    )(c_all, w_mod, b_mod)


def _split_w_in_kernel(wm_t_ref, wd_t_ref, wg_t_ref, wm_ref, wdin_ref, wg_ref, *, n_gate):
    j = pl.program_id(0)
    wm_ref[...] = wm_t_ref[...].astype(BF16).T

    @pl.when(j == 0)
    def _():
        wdin_ref[...] = wd_t_ref[...].astype(BF16).T

    @pl.when(j < n_gate)
    def _():
        wg_ref[...] = wg_t_ref[...].astype(BF16).T


def _split_w_in(w_in_t):
    n, d = w_in_t.shape
    tr = 512
    n_main = COL_DEC // tr
    n_gate = (n - COL_GATE) // tr
    assert COL_DEC % tr == 0 and (n - COL_GATE) % tr == 0 and n_gate <= n_main
    gate_blk = lambda j: jnp.minimum(j, n_gate - 1)
    return pl.pallas_call(
        functools.partial(_split_w_in_kernel, n_gate=n_gate),
        grid=(n_main,),
        in_specs=[
            pl.BlockSpec((tr, d), lambda j: (j, 0)),
            pl.BlockSpec((LANES, d), lambda j: (COL_DEC // LANES, 0)),
            pl.BlockSpec((pl.Element(tr), pl.Element(d)),
                         lambda j: (pl.multiple_of(COL_GATE + tr * gate_blk(j), 32), 0)),
        ],
        out_specs=[
            pl.BlockSpec((d, tr), lambda j: (0, j)),
            pl.BlockSpec((d, LANES), lambda j: (0, 0)),
            pl.BlockSpec((d, tr), lambda j: (0, gate_blk(j))),
        ],
        out_shape=[jax.ShapeDtypeStruct((d, w), BF16) for w in (COL_DEC, LANES, n - COL_GATE)],
        compiler_params=pltpu.CompilerParams(
            dimension_semantics=("arbitrary",), vmem_limit_bytes=VMEM_LIMIT),
        name="split_w_in",
    )(w_in_t, w_in_t, w_in_t)


def _prenorm_mod(x, g, mod_ref):
    sh = mod_ref[0:1, :]
    sc = mod_ref[1:2, :]
    return (_rms(x, g) * (1.0 + sc) + sh).astype(BF16)


def _decay_logs(z, wdec_ref, bdec_ref):
    z_hi, z_lo = z
    logits = _dot(z_hi, wdec_ref[...]) + _dot(z_lo, wdec_ref[...]) + bdec_ref[...]
    ls = jnp.minimum(logits, 0.0) - jnp.log(1.0 + jnp.exp(-jnp.abs(logits)))
    return ls * (1.0 / GATE_NORM)


N_LATE_W = 5


def _proj_kernel(x_ref, mod_ref, g_ref, perm_ref, wm_ref, wdin_ref, wdec_ref, bdec_ref, wg_ref,
                 *refs):
    late_f32 = refs[:N_LATE_W]
    (ah_ref, av_ref, q_ref, k_ref, v_ref, rs_ref, laf_ref, lab_ref,
     gt_ref) = refs[N_LATE_W:-N_LATE_W]
    late_bf16 = refs[-N_LATE_W:]
    for src, dst in zip(late_f32, late_bf16):
        dst[...] = src[...].astype(BF16)
    subs = [slice(s, s + PROJ_SUB) for s in range(0, x_ref.shape[0], PROJ_SUB)]
    g = g_ref[...]
    ub = [_prenorm_mod(x_ref[s, :], g, mod_ref) for s in subs]
    z = [_split_bf16(_dot(u, wdin_ref[...])) for u in ub]
    a_h = []
    for n, s in enumerate(subs):
        glu = _dot(ub[n], wm_ref[:, 0:COL_Q])
        a = (glu[:, :CONV_DIM] * _sigmoid(glu[:, CONV_DIM:])).astype(BF16)
        av_ref[s, :] = a[:, CONV_HALF:]
        a_h.append(a[:, :CONV_HALF])
    for n, s in enumerate(subs):
        q_ref[s, :] = (_dot(ub[n], wm_ref[:, COL_Q:COL_K]) * (GLA_DK ** -0.5)).astype(BF16)
    for n, s in enumerate(subs):
        k_ref[s, :] = _dot(ub[n], wm_ref[:, COL_K:COL_V]).astype(BF16)
    for n, s in enumerate(subs):
        v_ref[s, :] = _dot(ub[n], wm_ref[:, COL_V:COL_R]).astype(BF16)
    for n, s in enumerate(subs):
        la = _decay_logs(z[n], wdec_ref, bdec_ref)
        laf_ref[s, :] = la[:, :HK]
        lab_ref[s, :] = la[:, HK:]
    for n, s in enumerate(subs):
        r = _dot(ub[n], wm_ref[:, COL_R:COL_DEC])
        rs_ref[s, :] = (r * _sigmoid(r)).astype(BF16)
    per_group = PERM_TOKENS // PROJ_SUB
    for n in range(x_ref.shape[0] // PERM_TOKENS):
        a_rm = jnp.concatenate(a_h[per_group * n:per_group * (n + 1)], axis=0)
        a_cm = _dot(perm_ref[...], a_rm)
        ah_ref[:, PERM_ROWS * n:PERM_ROWS * (n + 1), :] = a_cm.reshape(
            GRID_W, PERM_ROWS, CONV_HALF)
    for n, s in enumerate(subs):
        gt_ref[s, :] = _sigmoid(_dot(ub[n], wg_ref[...])).astype(BF16)


def _const_spec(arr):
    nd = arr.ndim
    return pl.BlockSpec(arr.shape, lambda *_: (0,) * nd)


def _tok_spec(tm, width):
    return pl.BlockSpec((None, tm, width), lambda b, i: (b, i, 0))


def _colmajor_spec(tm):
    return pl.BlockSpec((None, GRID_W, tm // GRID_W, CONV_HALF), lambda b, i: (b, 0, i, 0))


def _single_spec(arr):
    nd = arr.ndim
    return pl.BlockSpec(arr.shape, lambda *_: (0,) * nd, pipeline_mode=pl.Buffered(1))


def _proj(x, mod, g, perm, wm, wdin, wdec, bdec, wg, late_weights, tm):
    bsz, t, d = x.shape
    widths = (CONV_HALF, HK, HK, HV, HV, HK, HK, 2 * D_MODEL)
    dtypes = (BF16, BF16, BF16, BF16, BF16, F32, F32, BF16)
    assert tm % PERM_TOKENS == 0 and len(late_weights) == N_LATE_W
    ah_shape = jax.ShapeDtypeStruct((bsz, GRID_W, t // GRID_W, CONV_HALF), F32)
    n_i = t // tm
    n_steps = bsz * n_i
    late_in, late_out, late_shape = [], [], []
    for w in late_weights:
        _, rows, cols = w.shape
        assert rows % (16 * n_steps) == 0
        blk = rows // n_steps
        late_in.append(pl.BlockSpec((None, blk, cols), lambda b, i: (0, b * n_i + i, 0)))
        late_out.append(pl.BlockSpec((blk, cols), lambda b, i: (b * n_i + i, 0)))
        late_shape.append(jax.ShapeDtypeStruct((rows, cols), BF16))
    return pl.pallas_call(
        _proj_kernel,
        grid=(bsz, n_i),
        in_specs=[
            _tok_spec(tm, d),
            pl.BlockSpec((None,) + mod.shape[1:], lambda b, i: (b, 0, 0)),
            _const_spec(g), _single_spec(perm), _single_spec(wm), _single_spec(wdin),
            _single_spec(wdec), _const_spec(bdec), _single_spec(wg),
        ] + late_in,
        out_specs=[_colmajor_spec(tm)] + [_tok_spec(tm, w) for w in widths] + late_out,
        out_shape=[ah_shape] + [jax.ShapeDtypeStruct((bsz, t, w), dt)
                                for w, dt in zip(widths, dtypes)] + late_shape,
        compiler_params=pltpu.CompilerParams(
            dimension_semantics=("parallel", "parallel"), vmem_limit_bytes=VMEM_LIMIT),
        name="proj",
    )(x, mod, g, perm, wm, wdin, wdec, bdec, wg, *late_weights)


def _gla_scale(qs, ks, las, dirn, want_out):
    tri_b, mid, last = dirn["tri_b"], dirn["mid"], dirn["last"]
    la_hi, la_lo = _split_bf16(las)
    g = _dot(tri_b, la_hi) + _dot(tri_b, la_lo)
    u = dict(kt=[], kend=[], qt=[], qg=[], e=[])
    for c in range(SUPER // CHUNK):
        rows = slice(c * CHUNK, (c + 1) * CHUNK)
        gc = g[rows]
        g_mid = gc[mid:mid + 1]
        g_last = gc[last:last + 1]
        u["e"].append(jnp.exp(g_last))
        kt = ks[rows].astype(F32) * jnp.exp(g_mid - gc)
        u["kt"].append(kt.astype(BF16))
        u["kend"].append((kt * jnp.exp(g_last - g_mid)).astype(BF16))
        if want_out:
            qt = qs[rows].astype(F32) * jnp.exp(gc - g_mid)
            u["qt"].append(qt.astype(BF16))
            u["qg"].append((qt * jnp.exp(g_mid)).astype(BF16))
    return u


def _pair_lanes(pair):
    return slice(LANES * pair, LANES * (pair + 1))


def _gla_scores(u, pair, head_lane):
    lanes = _pair_lanes(pair)
    kt = jnp.concatenate([c[:, lanes] for c in u["kt"]], axis=0)
    qt = jnp.concatenate([c[:, lanes] for c in u["qt"]], axis=0)
    kt2 = jnp.concatenate([kt * hm.astype(BF16) for hm in head_lane], axis=0)
    return lax.dot_general(qt, kt2, NT, preferred_element_type=F32)


def _gla_update(u, vs, h, head_lane):
    lanes = _pair_lanes(h // 2)
    hm = head_lane[h % 2].astype(BF16)
    k0 = u["kend"][0][:, lanes] * hm
    k1 = u["kend"][1][:, lanes] * hm
    z = jnp.zeros_like(k0)
    kbd = jnp.concatenate([jnp.concatenate([k0, z], axis=1),
                           jnp.concatenate([z, k1], axis=1)], axis=0)
    return lax.dot_general(vs[:, GLA_DV * h:GLA_DV * (h + 1)], kbd, TN,
                           preferred_element_type=F32)


def _gla_intra(scores, vs, pair, mask2):
    a = jnp.where(mask2, scores, 0.0).astype(BF16)
    v0 = vs[:, GLA_DV * 2 * pair:GLA_DV * (2 * pair + 1)]
    v1 = vs[:, GLA_DV * (2 * pair + 1):GLA_DV * (2 * pair + 2)]
    z = jnp.zeros_like(v0)
    vbd = jnp.concatenate([jnp.concatenate([v0, z], axis=1),
                           jnp.concatenate([z, v1], axis=1)], axis=0)
    return _dot(a, vbd)


def _gla_kernel(q_ref, k_ref, v_ref, laf_ref, lab_ref, rs_ref, g_ref,
                ctx_ref, modc_ref, gpre_ref, wm_ref, wdin_ref, wdec_ref, bdec_ref,
                o_ref, of_ref, ob_ref, sf_ref, sb_ref, kc_ref, vc_ref, lafc_ref, labc_ref):
    t = q_ref.shape[0]
    t_ctx = kc_ref.shape[0]

    ubc = _prenorm_mod(ctx_ref[...], gpre_ref[...], modc_ref)
    zc = _split_bf16(_dot(ubc, wdin_ref[...]))
    kc_ref[...] = _dot(ubc, wm_ref[:, COL_K:COL_V]).astype(BF16)
    vc_ref[...] = _dot(ubc, wm_ref[:, COL_V:COL_R]).astype(BF16)
    lac = _decay_logs(zc, wdec_ref, bdec_ref)
    lafc_ref[...] = lac[:, :HK]
    labc_ref[...] = lac[:, HK:]

    half = t // 2
    n_steps = t // STEP
    supers = STEP // SUPER
    n_chunks = SUPER // CHUNK
    pairs = GLA_HEADS // 2

    row = lax.broadcasted_iota(jnp.int32, (SUPER, SUPER), 0)
    col = lax.broadcasted_iota(jnp.int32, (SUPER, SUPER), 1)
    same_chunk = (row // CHUNK) == (col // CHUNK)
    low_mask = jnp.logical_and(same_chunk, col <= row)
    up_mask = jnp.logical_and(same_chunk, col >= row)
    lane = lax.broadcasted_iota(jnp.int32, (1, LANES), 1)
    head_lane = (jnp.where(lane < GLA_DK, 1.0, 0.0), jnp.where(lane >= GLA_DK, 1.0, 0.0))

    fwd = dict(tri_b=jnp.where(low_mask, 1.0, 0.0).astype(BF16),
               mask2=jnp.concatenate([low_mask, low_mask], axis=1),
               mid=CHUNK // 2 - 1, last=CHUNK - 1, order=(0, 1), st_ref=sf_ref)
    bwd = dict(tri_b=jnp.where(up_mask, 1.0, 0.0).astype(BF16),
               mask2=jnp.concatenate([up_mask, up_mask], axis=1),
               mid=CHUNK // 2, last=0, order=(1, 0), st_ref=sb_ref)

    sf_ref[...] = jnp.zeros_like(sf_ref)
    sb_ref[...] = jnp.zeros_like(sb_ref)
    gn = g_ref[...]

    def run(units, want_out, finalize):
        vals = []
        for dirn, refs, t0 in units:
            q_r, k_r, v_r, la_r = refs
            rows = pl.ds(t0, SUPER)
            qs = q_r[rows, :] if want_out else None
            vals.append((_gla_scale(qs, k_r[rows, :], la_r[rows, :], dirn, want_out),
                         v_r[rows, :]))
        scores = []
        if want_out:
            scores = [[_gla_scores(u, p, head_lane) for p in range(pairs)] for u, _ in vals]
        upd = [[_gla_update(u, vs, h, head_lane) for h in range(GLA_HEADS)] for u, vs in vals]
        intra = []
        if want_out:
            intra = [[_gla_intra(scores[n][p], vals[n][1], p, units[n][0]["mask2"])
                      for p in range(pairs)] for n in range(len(units))]
        states = {}
        for n, (dirn, _, t0) in enumerate(units):
            key = id(dirn["st_ref"])
            if key not in states:
                states[key] = (dirn["st_ref"], [dirn["st_ref"][h] for h in range(GLA_HEADS)])
            st = states[key][1]
            u = vals[n][0]
            inter = [[None] * n_chunks for _ in range(pairs)]
            for c in dirn["order"]:
                if want_out:
                    for p in range(pairs):
                        st2 = jnp.concatenate([st[2 * p].astype(BF16),
                                               st[2 * p + 1].astype(BF16)], axis=0)
                        inter[p][c] = lax.dot_general(u["qg"][c][:, _pair_lanes(p)], st2, NT,
                                                      preferred_element_type=F32)
                for h in range(GLA_HEADS):
                    e = u["e"][c][:, _pair_lanes(h // 2)]
                    st[h] = st[h] * e + upd[n][h][:, LANES * c:LANES * (c + 1)]
            if not want_out:
                continue
            fwd_unit = dirn is fwd
            for p in range(pairs):
                o2 = intra[n][p] + jnp.concatenate(inter[p], axis=0)
                for j in range(2):
                    cols = slice(GLA_DV * (2 * p + j), GLA_DV * (2 * p + j + 1))
                    o = o2[:, GLA_DV * j:GLA_DV * (j + 1)]
                    if finalize:
                        other = (ob_ref[pl.ds(t0 - half, SUPER), cols] if fwd_unit
                                 else of_ref[pl.ds(t0, SUPER), cols])
                        rs = rs_ref[pl.ds(t0, SUPER), cols].astype(F32)
                        o_ref[pl.ds(t0, SUPER), cols] = (_rms(o + other, gn) * rs).astype(BF16)
                    elif fwd_unit:
                        of_ref[pl.ds(t0, SUPER), cols] = o
                    else:
                        ob_ref[pl.ds(t0 - half, SUPER), cols] = o
        for st_ref, st in states.values():
            for h in range(GLA_HEADS):
                st_ref[h] = st[h]

    ctx_units = []
    for s in range(t_ctx // SUPER):
        ctx_units.append((fwd, (None, kc_ref, vc_ref, lafc_ref), s * SUPER))
        ctx_units.append((bwd, (None, kc_ref, vc_ref, labc_ref), t_ctx - (s + 1) * SUPER))
    run(ctx_units, want_out=False, finalize=False)

    def step(i, finalize):
        units = []
        for s in range(supers):
            t0 = pl.multiple_of(i * STEP + s * SUPER, SUPER)
            t1 = pl.multiple_of(t - (i + 1) * STEP + (supers - 1 - s) * SUPER, SUPER)
            units.append((fwd, (q_ref, k_ref, v_ref, laf_ref), t0))
            units.append((bwd, (q_ref, k_ref, v_ref, lab_ref), t1))
        run(units, want_out=True, finalize=finalize)

    def park_body(i, carry):
        step(i, False)
        return carry

    def finish_body(i, carry):
        step(i, True)
        return carry

    lax.fori_loop(0, n_steps // 2, park_body, 0)
    lax.fori_loop(n_steps // 2, n_steps, finish_body, 0)


def _gla(q, k, v, laf, lab, rs, gnorm, ctx, mod_ctx, gpre, wm, wdin, wdec, bdec):
    bsz, t, _ = q.shape
    t_ctx = ctx.shape[1]
    assert t % (2 * STEP) == 0 and t_ctx % SUPER == 0

    def bspec(arr):
        return pl.BlockSpec((None,) + arr.shape[1:], lambda b: (b, 0, 0))

    args = (q, k, v, laf, lab, rs)
    return pl.pallas_call(
        _gla_kernel,
        grid=(bsz,),
        in_specs=[bspec(a) for a in args] + [_const_spec(gnorm), bspec(ctx),
                  pl.BlockSpec((None,) + mod_ctx.shape[1:], lambda b: (0, 0, 0)),
                  _const_spec(gpre), _single_spec(wm), _single_spec(wdin), _single_spec(wdec),
                  _const_spec(bdec)],
        out_specs=pl.BlockSpec((None, t, HV), lambda b: (b, 0, 0)),
        out_shape=jax.ShapeDtypeStruct((bsz, t, HV), BF16),
        scratch_shapes=[
            pltpu.VMEM((t // 2, HV), F32), pltpu.VMEM((t // 2, HV), F32),
            pltpu.VMEM((GLA_HEADS, GLA_DV, LANES), F32),
            pltpu.VMEM((GLA_HEADS, GLA_DV, LANES), F32),
            pltpu.VMEM((t_ctx, HK), BF16), pltpu.VMEM((t_ctx, HV), BF16),
            pltpu.VMEM((t_ctx, HK), F32), pltpu.VMEM((t_ctx, HK), F32),
        ],
        compiler_params=pltpu.CompilerParams(
            dimension_semantics=("parallel",), vmem_limit_bytes=VMEM_LIMIT),
        name="gla",
    )(*args, gnorm, ctx, mod_ctx, gpre, wm, wdin, wdec, bdec)


def _tap_segments(n_tiles, lines_per_tile, n_lines, group):
    segs = []
    for t0 in range(0, n_tiles, group):
        t1 = min(t0 + group, n_tiles)
        lo_line = int(t0 * lines_per_tile)
        hi_line = int(-(-t1 * lines_per_tile // 1)) - 1
        k_lo = max(0, CONV_PAD - hi_line)
        k_hi = min(CONV_K, n_lines + CONV_PAD - lo_line)
        if segs and segs[-1][2:] == (k_lo, k_hi):
            segs[-1] = (segs[-1][0], t1, k_lo, k_hi)
        else:
            segs.append((t0, t1, k_lo, k_hi))
    return segs


def _conv_kernel(ah_ref, av_ref, w_ref, b_ref, yh_ref, yv_ref, ph_ref, pv_ref, wb_ref):
    t = ah_ref.shape[0]
    rows = t // GRID_W
    tile = 64
    for kk in range(CONV_K):
        wb_ref[kk] = jnp.broadcast_to(w_ref[kk:kk + 1, :], wb_ref.shape[1:])

    def conv(src_ref, pad_ref, w_cols, line, n_lines, group, dst_ref):
        pad = CONV_PAD * line
        pad_ref[pl.ds(0, pad), :] = jnp.zeros((pad, CONV_HALF), F32)
        pad_ref[pl.ds(pad + t, pad), :] = jnp.zeros((pad, CONV_HALF), F32)

        def fill(i, carry):
            t0 = pl.multiple_of(i * tile, tile)
            pad_ref[pl.ds(pl.multiple_of(pad + t0, 8), tile), :] = (
                src_ref[pl.ds(t0, tile), :].astype(F32))
            return carry

        lax.fori_loop(0, t // tile, fill, 0)
        bias = b_ref[:, w_cols]

        for first, last, k_lo, k_hi in _tap_segments(t // tile, tile / line, n_lines, group):
            def body(i, carry, k_lo=k_lo, k_hi=k_hi):
                t0 = pl.multiple_of(i * tile, tile)
                acc = jnp.broadcast_to(bias, (tile, CONV_HALF))
                for kk in range(k_lo, k_hi):
                    win = pad_ref[pl.ds(pl.multiple_of(t0 + kk * line, 8), tile), :]
                    w8 = wb_ref[kk, :, w_cols]
                    acc = acc + jnp.concatenate([w8] * (tile // 8), axis=0) * win
                dst_ref[pl.ds(t0, tile), :] = acc
                return carry

            lax.fori_loop(first, last, body, 0)

    conv(ah_ref, ph_ref, slice(0, CONV_HALF), rows, GRID_W, 2, yh_ref)
    conv(av_ref, pv_ref, slice(CONV_HALF, CONV_DIM), GRID_W, rows, 4, yv_ref)


def _conv(ah, av, w, b):
    bsz, t, c = ah.shape
    rows = t // GRID_W
    tok = pl.BlockSpec((None, t, c), lambda i: (i, 0, 0))
    return pl.pallas_call(
        _conv_kernel,
        grid=(bsz,),
        in_specs=[tok, tok, _const_spec(w), _const_spec(b)],
        out_specs=[tok, tok],
        out_shape=[jax.ShapeDtypeStruct((bsz, t, c), F32)] * 2,
        scratch_shapes=[
            pltpu.VMEM((t + 2 * CONV_PAD * rows, c), F32),
            pltpu.VMEM((t + 2 * CONV_PAD * GRID_W, c), F32),
            pltpu.VMEM((CONV_K, 8, 2 * c), F32),
        ],
        compiler_params=pltpu.CompilerParams(
            dimension_semantics=("parallel",), vmem_limit_bytes=VMEM_LIMIT),
        name="conv",
    )(ah, av, w, b)


def _post_kernel(x_ref, yh_ref, yv_ref, og_ref, gt_ref, mod_ref, perm_ref, lng_ref, lnb_ref,
                 gpost1_ref, gpre2_ref, gpost2_ref, wco_ref, wgo_ref, wout_ref, wff1_ref,
                 wff2_ref, o_ref):
    d = x_ref.shape[-1]
    d_ff = wff1_ref.shape[1]
    ff_tile = 1024
    sub = 256
    subs = [slice(s, s + sub) for s in range(0, x_ref.shape[0], sub)]
    gt1 = mod_ref[2:3, :]
    sh2 = mod_ref[3:4, :]
    sc2 = mod_ref[4:5, :]
    gt2 = mod_ref[5:6, :]

    yh = []
    for n in range(x_ref.shape[0] // PERM_TOKENS):
        blk = yh_ref[:, PERM_ROWS * n:PERM_ROWS * (n + 1), :].reshape(PERM_TOKENS, CONV_HALF)
        hi, lo = _split_bf16(blk)
        yh.append(_dot(perm_ref[...], hi) + _dot(perm_ref[...], lo))
    yh = jnp.concatenate(yh, axis=0)

    yn = []
    for s in subs:
        yc = jnp.concatenate([yh[s], yv_ref[s, :]], axis=-1)
        yc = yc - jnp.mean(yc, axis=-1, keepdims=True)
        var = jnp.mean(yc * yc, axis=-1, keepdims=True)
        z = yc * lax.rsqrt(var + EPS) * lng_ref[...] + lnb_ref[...]
        yn.append((z * _sigmoid(z)).astype(BF16))
    y_gla = [_dot(og_ref[s, :], wgo_ref[...]) for s in subs]
    y_conv = [_dot(v, wco_ref[...]) for v in yn]
    merged = []
    for n, s in enumerate(subs):
        gates = gt_ref[s, :].astype(F32)
        merged.append((gates[:, :d] * y_conv[n] + gates[:, d:] * y_gla[n]).astype(BF16))
    y = [_dot(v, wout_ref[...]) for v in merged]
    h_mid = [x_ref[s, :] + gt1 * _rms(y[n], gpost1_ref[...]) for n, s in enumerate(subs)]
    u2 = [(_rms(h, gpre2_ref[...]) * (1.0 + sc2) + sh2).astype(BF16) for h in h_mid]
    acc = [jnp.zeros(h.shape, F32) for h in h_mid]
    for j in range(d_ff // ff_tile):
        cols = slice(j * ff_tile, (j + 1) * ff_tile)
        f = [_dot(v, wff1_ref[:, cols]) for v in u2]
        f = [jnp.square(jnp.maximum(v, 0.0)).astype(BF16) for v in f]
        acc = [acc[n] + _dot(f[n], wff2_ref[cols, :]) for n in range(len(subs))]
    for n, s in enumerate(subs):
        o_ref[s, :] = h_mid[n] + gt2 * _rms(acc[n], gpost2_ref[...])


def _post(x, yh, yv, og, gates, mod, perm, ln_g, ln_b, gpost1, gpre2, gpost2, wco, wgo, wout,
          wff1, wff2, tm):
    bsz, t, d = x.shape
    assert tm % PERM_TOKENS == 0
    return pl.pallas_call(
        _post_kernel,
        grid=(bsz, t // tm),
        in_specs=[
            _tok_spec(tm, d), _colmajor_spec(tm), _tok_spec(tm, CONV_HALF), _tok_spec(tm, HV),
            _tok_spec(tm, 2 * d),
            pl.BlockSpec((None,) + mod.shape[1:], lambda b, i: (b, 0, 0)),
            _single_spec(perm), _const_spec(ln_g), _const_spec(ln_b),
            _const_spec(gpost1), _const_spec(gpre2), _const_spec(gpost2),
            _single_spec(wco), _single_spec(wgo), _single_spec(wout), _single_spec(wff1),
            _single_spec(wff2),
        ],
        out_specs=_tok_spec(tm, d),
        out_shape=jax.ShapeDtypeStruct((bsz, t, d), F32),
        compiler_params=pltpu.CompilerParams(
            dimension_semantics=("parallel", "parallel"), vmem_limit_bytes=VMEM_LIMIT),
        name="post",
    )(x, yh, yv, og, gates, mod, perm, ln_g, ln_b, gpost1, gpre2, gpost2, wco, wgo, wout, wff1,
      wff2)


def kernel(x, c, ctx, c_ctx, w_mod, b_mod, g_pre1, g_post1, g_pre2, g_post2, w_in, conv_w,
           conv_b, conv_ln_g, conv_ln_b, w_conv_out, w_decay, b_decay, gla_norm_g, w_gla_out,
           w_out, w_ff1, w_ff2):
    bsz, t, d = x.shape
    depth = w_in.shape[0]
    assert depth == 1 and d == D_MODEL and t % (GRID_W * 8) == 0

    n_rows = -(-(bsz + 1) // 8) * 8
    c_all = jnp.zeros((n_rows, d), F32).at[:bsz].set(c).at[bsz].set(c_ctx)
    m = _adaln(c_all, w_mod, b_mod)
    mod = m[:bsz].reshape(bsz, N_MOD, d)
    mod_ctx = m[bsz, :2 * d].reshape(1, 2, d)

    wm, wdin, wg = _split_w_in(jnp.swapaxes(w_in[0], 0, 1))
    wdec = jnp.zeros((LANES, 2 * HK), F32)
    wdec = wdec.at[:DECAY_RANK, :HK].set(w_decay[0, 0])
    wdec = wdec.at[DECAY_RANK:2 * DECAY_RANK, HK:].set(w_decay[0, 1]).astype(BF16)
    bdec = b_decay[0].reshape(1, 2 * HK)
    row = lambda v: v.reshape(1, -1)

    perm = _to_colmajor_perm()
    to_colmajor = jnp.asarray(perm, BF16)
    to_rowmajor = jnp.asarray(perm.T, BF16)

    ah, av, q, k, v, rs, laf, lab, gates, wco, wgo, wout, wff1, wff2 = _proj(
        x, mod, row(g_pre1[0]), to_colmajor, wm, wdin, wdec, bdec, wg,
        (w_conv_out, w_gla_out, w_out, w_ff1, w_ff2), tm=1024)
    og = _gla(q, k, v, laf, lab, rs, row(gla_norm_g[0]),
              ctx, mod_ctx, row(g_pre1[0]), wm, wdin, wdec, bdec)
    yh, yv = _conv(ah.reshape(bsz, t, CONV_HALF), av, conv_w[0], row(conv_b[0]))
    yh = yh.reshape(bsz, GRID_W, t // GRID_W, CONV_HALF)

    return _post(x, yh, yv, og, gates, mod, to_rowmajor, row(conv_ln_g[0]), row(conv_ln_b[0]),
                 row(g_post1[0]), row(g_pre2[0]), row(g_post2[0]),
                 wco, wgo, wout, wff1, wff2, tm=512)
```

```python
import functools

import jax
import jax.numpy as jnp
import numpy as np
from jax import lax
from jax.experimental import pallas as pl
from jax.experimental.pallas import tpu as pltpu

D_MODEL = 1024
GRID_W = 64
CONV_DIM = 512
CONV_K = 31
GLA_HEADS = 4
GLA_DK = 64
GLA_DV = 128
DECAY_RANK = 16
GATE_NORM = 16.0
CHUNK = 64
N_MOD = 6
EPS = 1e-6

COL_Q = 2 * CONV_DIM
COL_K = COL_Q + GLA_HEADS * GLA_DK
COL_V = COL_K + GLA_HEADS * GLA_DK
COL_R = COL_V + GLA_HEADS * GLA_DV
COL_DEC = COL_R + GLA_HEADS * GLA_DV
COL_GATE = COL_DEC + 2 * DECAY_RANK
COL_END = COL_GATE + 2 * D_MODEL

LANES = 128
HK = GLA_HEADS * GLA_DK
HV = GLA_HEADS * GLA_DV
CONV_HALF = CONV_DIM // 2
CONV_PAD = CONV_K // 2
PERM_ROWS = 8
PERM_TOKENS = PERM_ROWS * GRID_W
PROJ_SUB = 256
SUPER = 2 * CHUNK
STEP = 8 * SUPER

VMEM_LIMIT = 56 * 1024 * 1024

F32 = jnp.float32
BF16 = jnp.bfloat16

NT = (((1,), (1,)), ((), ()))
TN = (((0,), (0,)), ((), ()))


def _dot(a, b):
    return jnp.dot(a, b, preferred_element_type=F32)


def _rms(x, g):
    ms = jnp.mean(x * x, axis=-1, keepdims=True)
    return x * lax.rsqrt(ms + EPS) * g


def _sigmoid(x):
    return jax.nn.sigmoid(x)


def _split_bf16(x):
    hi = x.astype(BF16)
    lo = (x - hi.astype(F32)).astype(BF16)
    return hi, lo


def _to_colmajor_perm():
    out = np.arange(PERM_TOKENS)
    src = (out % PERM_ROWS) * GRID_W + out // PERM_ROWS
    p = np.zeros((PERM_TOKENS, PERM_TOKENS), np.float32)
    p[out, src] = 1.0
    return p


def _adaln_kernel(c_ref, w_ref, b_ref, o_ref):
    @pl.when(pl.program_id(0) == 0)
    def _():
        o_ref[...] = jnp.broadcast_to(b_ref[...], o_ref.shape)

    c = c_ref[...]
    s_hi, s_lo = _split_bf16(c * _sigmoid(c))
    w_hi, w_lo = _split_bf16(w_ref[...])
    o_ref[...] += _dot(s_hi, w_hi) + _dot(s_lo, w_hi) + _dot(s_hi, w_lo)


def _adaln(c_all, w_mod, b_mod):
    n_rows, d = c_all.shape
    n_out = w_mod.shape[-1]
    tk = 256
    return pl.pallas_call(
        _adaln_kernel,
        grid=(d // tk,),
        in_specs=[
            pl.BlockSpec((n_rows, tk), lambda k: (0, k)),
            pl.BlockSpec((None, tk, n_out), lambda k: (0, k, 0)),
            pl.BlockSpec((1, n_out), lambda k: (0, 0)),
        ],
        out_specs=pl.BlockSpec((n_rows, n_out), lambda k: (0, 0)),
        out_shape=jax.ShapeDtypeStruct((n_rows, n_out), F32),
        compiler_params=pltpu.CompilerParams(
            dimension_semantics=("arbitrary",), vmem_limit_bytes=VMEM_LIMIT),
        name="adaln",
    )(c_all, w_mod, b_mod)


def _split_w_in_kernel(wm_t_ref, wd_t_ref, wg_t_ref, wm_ref, wdin_ref, wg_ref, *, n_gate):
    j = pl.program_id(0)
    wm_ref[...] = wm_t_ref[...].astype(BF16).T

    @pl.when(j == 0)
    def _():
        wdin_ref[...] = wd_t_ref[...].astype(BF16).T

    @pl.when(j < n_gate)
    def _():
        wg_ref[...] = wg_t_ref[...].astype(BF16).T


def _split_w_in(w_in_t):
    n, d = w_in_t.shape
    tr = 512
    n_main = COL_DEC // tr
    n_gate = (n - COL_GATE) // tr
    assert COL_DEC % tr == 0 and (n - COL_GATE) % tr == 0 and n_gate <= n_main
    gate_blk = lambda j: jnp.minimum(j, n_gate - 1)
    return pl.pallas_call(
        functools.partial(_split_w_in_kernel, n_gate=n_gate),
        grid=(n_main,),
        in_specs=[
            pl.BlockSpec((tr, d), lambda j: (j, 0)),
            pl.BlockSpec((LANES, d), lambda j: (COL_DEC // LANES, 0)),
            pl.BlockSpec((pl.Element(tr), pl.Element(d)),
                         lambda j: (pl.multiple_of(COL_GATE + tr * gate_blk(j), 32), 0)),
        ],
        out_specs=[
            pl.BlockSpec((d, tr), lambda j: (0, j)),
            pl.BlockSpec((d, LANES), lambda j: (0, 0)),
            pl.BlockSpec((d, tr), lambda j: (0, gate_blk(j))),
        ],
        out_shape=[jax.ShapeDtypeStruct((d, w), BF16) for w in (COL_DEC, LANES, n - COL_GATE)],
        compiler_params=pltpu.CompilerParams(
            dimension_semantics=("arbitrary",), vmem_limit_bytes=VMEM_LIMIT),
        name="split_w_in",
    )(w_in_t, w_in_t, w_in_t)


def _prenorm_mod(x, g, mod_ref):
    sh = mod_ref[0:1, :]
    sc = mod_ref[1:2, :]
    return (_rms(x, g) * (1.0 + sc) + sh).astype(BF16)


def _decay_logs(z, wdec_ref, bdec_ref):
    z_hi, z_lo = z
    logits = _dot(z_hi, wdec_ref[...]) + _dot(z_lo, wdec_ref[...]) + bdec_ref[...]
    ls = jnp.minimum(logits, 0.0) - jnp.log(1.0 + jnp.exp(-jnp.abs(logits)))
    return ls * (1.0 / GATE_NORM)


N_LATE_W = 5


def _proj_kernel(x_ref, mod_ref, g_ref, perm_ref, wm_ref, wdin_ref, wdec_ref, bdec_ref, wg_ref,
                 *refs):
    late_f32 = refs[:N_LATE_W]
    (ah_ref, av_ref, q_ref, k_ref, v_ref, rs_ref, laf_ref, lab_ref,
     gt_ref) = refs[N_LATE_W:-N_LATE_W]
    late_bf16 = refs[-N_LATE_W:]
    for src, dst in zip(late_f32, late_bf16):
        dst[...] = src[...].astype(BF16)
    subs = [slice(s, s + PROJ_SUB) for s in range(0, x_ref.shape[0], PROJ_SUB)]
    g = g_ref[...]
    ub = [_prenorm_mod(x_ref[s, :], g, mod_ref) for s in subs]
    z = [_split_bf16(_dot(u, wdin_ref[...])) for u in ub]
    a_h = []
    for n, s in enumerate(subs):
        glu = _dot(ub[n], wm_ref[:, 0:COL_Q])
        a = (glu[:, :CONV_DIM] * _sigmoid(glu[:, CONV_DIM:])).astype(BF16)
        av_ref[s, :] = a[:, CONV_HALF:]
        a_h.append(a[:, :CONV_HALF])
    for n, s in enumerate(subs):
        q_ref[s, :] = (_dot(ub[n], wm_ref[:, COL_Q:COL_K]) * (GLA_DK ** -0.5)).astype(BF16)
    for n, s in enumerate(subs):
        k_ref[s, :] = _dot(ub[n], wm_ref[:, COL_K:COL_V]).astype(BF16)
    for n, s in enumerate(subs):
        v_ref[s, :] = _dot(ub[n], wm_ref[:, COL_V:COL_R]).astype(BF16)
    for n, s in enumerate(subs):
        la = _decay_logs(z[n], wdec_ref, bdec_ref)
        laf_ref[s, :] = la[:, :HK]
        lab_ref[s, :] = la[:, HK:]
    for n, s in enumerate(subs):
        r = _dot(ub[n], wm_ref[:, COL_R:COL_DEC])
        rs_ref[s, :] = (r * _sigmoid(r)).astype(BF16)
    per_group = PERM_TOKENS // PROJ_SUB
    for n in range(x_ref.shape[0] // PERM_TOKENS):
        a_rm = jnp.concatenate(a_h[per_group * n:per_group * (n + 1)], axis=0)
        a_cm = _dot(perm_ref[...], a_rm)
        ah_ref[:, PERM_ROWS * n:PERM_ROWS * (n + 1), :] = a_cm.reshape(
            GRID_W, PERM_ROWS, CONV_HALF)
    for n, s in enumerate(subs):
        gt_ref[s, :] = _sigmoid(_dot(ub[n], wg_ref[...])).astype(BF16)


def _const_spec(arr):
    nd = arr.ndim
    return pl.BlockSpec(arr.shape, lambda *_: (0,) * nd)


def _tok_spec(tm, width):
    return pl.BlockSpec((None, tm, width), lambda b, i: (b, i, 0))


def _colmajor_spec(tm):
    return pl.BlockSpec((None, GRID_W, tm // GRID_W, CONV_HALF), lambda b, i: (b, 0, i, 0))


def _single_spec(arr):
    nd = arr.ndim
    return pl.BlockSpec(arr.shape, lambda *_: (0,) * nd, pipeline_mode=pl.Buffered(1))


def _proj(x, mod, g, perm, wm, wdin, wdec, bdec, wg, late_weights, tm):
    bsz, t, d = x.shape
    widths = (CONV_HALF, HK, HK, HV, HV, HK, HK, 2 * D_MODEL)
    dtypes = (BF16, BF16, BF16, BF16, BF16, F32, F32, BF16)
    assert tm % PERM_TOKENS == 0 and len(late_weights) == N_LATE_W
    ah_shape = jax.ShapeDtypeStruct((bsz, GRID_W, t // GRID_W, CONV_HALF), F32)
    n_i = t // tm
    n_steps = bsz * n_i
    late_in, late_out, late_shape = [], [], []
    for w in late_weights:
        _, rows, cols = w.shape
        assert rows % (16 * n_steps) == 0
        blk = rows // n_steps
        late_in.append(pl.BlockSpec((None, blk, cols), lambda b, i: (0, b * n_i + i, 0)))
        late_out.append(pl.BlockSpec((blk, cols), lambda b, i: (b * n_i + i, 0)))
        late_shape.append(jax.ShapeDtypeStruct((rows, cols), BF16))
    return pl.pallas_call(
        _proj_kernel,
        grid=(bsz, n_i),
        in_specs=[
            _tok_spec(tm, d),
            pl.BlockSpec((None,) + mod.shape[1:], lambda b, i: (b, 0, 0)),
            _const_spec(g), _single_spec(perm), _single_spec(wm), _single_spec(wdin),
            _single_spec(wdec), _const_spec(bdec), _single_spec(wg),
        ] + late_in,
        out_specs=[_colmajor_spec(tm)] + [_tok_spec(tm, w) for w in widths] + late_out,
        out_shape=[ah_shape] + [jax.ShapeDtypeStruct((bsz, t, w), dt)
                                for w, dt in zip(widths, dtypes)] + late_shape,
        compiler_params=pltpu.CompilerParams(
            dimension_semantics=("parallel", "parallel"), vmem_limit_bytes=VMEM_LIMIT),
        name="proj",
    )(x, mod, g, perm, wm, wdin, wdec, bdec, wg, *late_weights)


def _gla_scale(qs, ks, las, dirn, want_out):
    tri_b, mid, last = dirn["tri_b"], dirn["mid"], dirn["last"]
    la_hi, la_lo = _split_bf16(las)
    g = _dot(tri_b, la_hi) + _dot(tri_b, la_lo)
    u = dict(kt=[], kend=[], qt=[], qg=[], e=[])
    for c in range(SUPER // CHUNK):
        rows = slice(c * CHUNK, (c + 1) * CHUNK)
        gc = g[rows]
        g_mid = gc[mid:mid + 1]
        g_last = gc[last:last + 1]
        u["e"].append(jnp.exp(g_last))
        kt = ks[rows].astype(F32) * jnp.exp(g_mid - gc)
        u["kt"].append(kt.astype(BF16))
        u["kend"].append((kt * jnp.exp(g_last - g_mid)).astype(BF16))
        if want_out:
            qt = qs[rows].astype(F32) * jnp.exp(gc - g_mid)
            u["qt"].append(qt.astype(BF16))
            u["qg"].append((qt * jnp.exp(g_mid)).astype(BF16))
    return u


def _pair_lanes(pair):
    return slice(LANES * pair, LANES * (pair + 1))


def _gla_scores(u, pair, head_lane):
    lanes = _pair_lanes(pair)
    kt = jnp.concatenate([c[:, lanes] for c in u["kt"]], axis=0)
    qt = jnp.concatenate([c[:, lanes] for c in u["qt"]], axis=0)
    kt2 = jnp.concatenate([kt * hm.astype(BF16) for hm in head_lane], axis=0)
    return lax.dot_general(qt, kt2, NT, preferred_element_type=F32)


def _gla_update(u, vs, h, head_lane):
    lanes = _pair_lanes(h // 2)
    hm = head_lane[h % 2].astype(BF16)
    k0 = u["kend"][0][:, lanes] * hm
    k1 = u["kend"][1][:, lanes] * hm
    z = jnp.zeros_like(k0)
    kbd = jnp.concatenate([jnp.concatenate([k0, z], axis=1),
                           jnp.concatenate([z, k1], axis=1)], axis=0)
    return lax.dot_general(vs[:, GLA_DV * h:GLA_DV * (h + 1)], kbd, TN,
                           preferred_element_type=F32)


def _gla_intra(scores, vs, pair, mask2):
    a = jnp.where(mask2, scores, 0.0).astype(BF16)
    v0 = vs[:, GLA_DV * 2 * pair:GLA_DV * (2 * pair + 1)]
    v1 = vs[:, GLA_DV * (2 * pair + 1):GLA_DV * (2 * pair + 2)]
    z = jnp.zeros_like(v0)
    vbd = jnp.concatenate([jnp.concatenate([v0, z], axis=1),
                           jnp.concatenate([z, v1], axis=1)], axis=0)
    return _dot(a, vbd)


def _gla_kernel(q_ref, k_ref, v_ref, laf_ref, lab_ref, rs_ref, g_ref,
                ctx_ref, modc_ref, gpre_ref, wm_ref, wdin_ref, wdec_ref, bdec_ref,
                o_ref, of_ref, ob_ref, sf_ref, sb_ref, kc_ref, vc_ref, lafc_ref, labc_ref):
    t = q_ref.shape[0]
    t_ctx = kc_ref.shape[0]

    ubc = _prenorm_mod(ctx_ref[...], gpre_ref[...], modc_ref)
    zc = _split_bf16(_dot(ubc, wdin_ref[...]))
    kc_ref[...] = _dot(ubc, wm_ref[:, COL_K:COL_V]).astype(BF16)
    vc_ref[...] = _dot(ubc, wm_ref[:, COL_V:COL_R]).astype(BF16)
    lac = _decay_logs(zc, wdec_ref, bdec_ref)
    lafc_ref[...] = lac[:, :HK]
    labc_ref[...] = lac[:, HK:]

    half = t // 2
    n_steps = t // STEP
    supers = STEP // SUPER
    n_chunks = SUPER // CHUNK
    pairs = GLA_HEADS // 2

    row = lax.broadcasted_iota(jnp.int32, (SUPER, SUPER), 0)
    col = lax.broadcasted_iota(jnp.int32, (SUPER, SUPER), 1)
    same_chunk = (row // CHUNK) == (col // CHUNK)
    low_mask = jnp.logical_and(same_chunk, col <= row)
    up_mask = jnp.logical_and(same_chunk, col >= row)
    lane = lax.broadcasted_iota(jnp.int32, (1, LANES), 1)
    head_lane = (jnp.where(lane < GLA_DK, 1.0, 0.0), jnp.where(lane >= GLA_DK, 1.0, 0.0))

    fwd = dict(tri_b=jnp.where(low_mask, 1.0, 0.0).astype(BF16),
               mask2=jnp.concatenate([low_mask, low_mask], axis=1),
               mid=CHUNK // 2 - 1, last=CHUNK - 1, order=(0, 1), st_ref=sf_ref)
    bwd = dict(tri_b=jnp.where(up_mask, 1.0, 0.0).astype(BF16),
               mask2=jnp.concatenate([up_mask, up_mask], axis=1),
               mid=CHUNK // 2, last=0, order=(1, 0), st_ref=sb_ref)

    sf_ref[...] = jnp.zeros_like(sf_ref)
    sb_ref[...] = jnp.zeros_like(sb_ref)
    gn = g_ref[...]

    def run(units, want_out, finalize):
        vals = []
        for dirn, refs, t0 in units:
            q_r, k_r, v_r, la_r = refs
            rows = pl.ds(t0, SUPER)
            qs = q_r[rows, :] if want_out else None
            vals.append((_gla_scale(qs, k_r[rows, :], la_r[rows, :], dirn, want_out),
                         v_r[rows, :]))
        scores = []
        if want_out:
            scores = [[_gla_scores(u, p, head_lane) for p in range(pairs)] for u, _ in vals]
        upd = [[_gla_update(u, vs, h, head_lane) for h in range(GLA_HEADS)] for u, vs in vals]
        intra = []
        if want_out:
            intra = [[_gla_intra(scores[n][p], vals[n][1], p, units[n][0]["mask2"])
                      for p in range(pairs)] for n in range(len(units))]
        states = {}
        for n, (dirn, _, t0) in enumerate(units):
            key = id(dirn["st_ref"])
            if key not in states:
                states[key] = (dirn["st_ref"], [dirn["st_ref"][h] for h in range(GLA_HEADS)])
            st = states[key][1]
            u = vals[n][0]
            inter = [[None] * n_chunks for _ in range(pairs)]
            for c in dirn["order"]:
                if want_out:
                    for p in range(pairs):
                        st2 = jnp.concatenate([st[2 * p].astype(BF16),
                                               st[2 * p + 1].astype(BF16)], axis=0)
                        inter[p][c] = lax.dot_general(u["qg"][c][:, _pair_lanes(p)], st2, NT,
                                                      preferred_element_type=F32)
                for h in range(GLA_HEADS):
                    e = u["e"][c][:, _pair_lanes(h // 2)]
                    st[h] = st[h] * e + upd[n][h][:, LANES * c:LANES * (c + 1)]
            if not want_out:
                continue
            fwd_unit = dirn is fwd
            for p in range(pairs):
                o2 = intra[n][p] + jnp.concatenate(inter[p], axis=0)
                for j in range(2):
                    cols = slice(GLA_DV * (2 * p + j), GLA_DV * (2 * p + j + 1))
                    o = o2[:, GLA_DV * j:GLA_DV * (j + 1)]
                    if finalize:
                        other = (ob_ref[pl.ds(t0 - half, SUPER), cols] if fwd_unit
                                 else of_ref[pl.ds(t0, SUPER), cols])
                        rs = rs_ref[pl.ds(t0, SUPER), cols].astype(F32)
                        o_ref[pl.ds(t0, SUPER), cols] = (_rms(o + other, gn) * rs).astype(BF16)
                    elif fwd_unit:
                        of_ref[pl.ds(t0, SUPER), cols] = o
                    else:
                        ob_ref[pl.ds(t0 - half, SUPER), cols] = o
        for st_ref, st in states.values():
            for h in range(GLA_HEADS):
                st_ref[h] = st[h]

    ctx_units = []
    for s in range(t_ctx // SUPER):
        ctx_units.append((fwd, (None, kc_ref, vc_ref, lafc_ref), s * SUPER))
        ctx_units.append((bwd, (None, kc_ref, vc_ref, labc_ref), t_ctx - (s + 1) * SUPER))
    run(ctx_units, want_out=False, finalize=False)

    def step(i, finalize):
        units = []
        for s in range(supers):
            t0 = pl.multiple_of(i * STEP + s * SUPER, SUPER)
            t1 = pl.multiple_of(t - (i + 1) * STEP + (supers - 1 - s) * SUPER, SUPER)
            units.append((fwd, (q_ref, k_ref, v_ref, laf_ref), t0))
            units.append((bwd, (q_ref, k_ref, v_ref, lab_ref), t1))
        run(units, want_out=True, finalize=finalize)

    def park_body(i, carry):
        step(i, False)
        return carry

    def finish_body(i, carry):
        step(i, True)
        return carry

    lax.fori_loop(0, n_steps // 2, park_body, 0)
    lax.fori_loop(n_steps // 2, n_steps, finish_body, 0)


def _gla(q, k, v, laf, lab, rs, gnorm, ctx, mod_ctx, gpre, wm, wdin, wdec, bdec):
    bsz, t, _ = q.shape
    t_ctx = ctx.shape[1]
    assert t % (2 * STEP) == 0 and t_ctx % SUPER == 0

    def bspec(arr):
        return pl.BlockSpec((None,) + arr.shape[1:], lambda b: (b, 0, 0))

    args = (q, k, v, laf, lab, rs)
    return pl.pallas_call(
        _gla_kernel,
        grid=(bsz,),
        in_specs=[bspec(a) for a in args] + [_const_spec(gnorm), bspec(ctx),
                  pl.BlockSpec((None,) + mod_ctx.shape[1:], lambda b: (0, 0, 0)),
                  _const_spec(gpre), _single_spec(wm), _single_spec(wdin), _single_spec(wdec),
                  _const_spec(bdec)],
        out_specs=pl.BlockSpec((None, t, HV), lambda b: (b, 0, 0)),
        out_shape=jax.ShapeDtypeStruct((bsz, t, HV), BF16),
        scratch_shapes=[
            pltpu.VMEM((t // 2, HV), F32), pltpu.VMEM((t // 2, HV), F32),
            pltpu.VMEM((GLA_HEADS, GLA_DV, LANES), F32),
            pltpu.VMEM((GLA_HEADS, GLA_DV, LANES), F32),
            pltpu.VMEM((t_ctx, HK), BF16), pltpu.VMEM((t_ctx, HV), BF16),
            pltpu.VMEM((t_ctx, HK), F32), pltpu.VMEM((t_ctx, HK), F32),
        ],
        compiler_params=pltpu.CompilerParams(
            dimension_semantics=("parallel",), vmem_limit_bytes=VMEM_LIMIT),
        name="gla",
    )(*args, gnorm, ctx, mod_ctx, gpre, wm, wdin, wdec, bdec)


def _tap_segments(n_tiles, lines_per_tile, n_lines, group):
    segs = []
    for t0 in range(0, n_tiles, group):
        t1 = min(t0 + group, n_tiles)
        lo_line = int(t0 * lines_per_tile)
        hi_line = int(-(-t1 * lines_per_tile // 1)) - 1
        k_lo = max(0, CONV_PAD - hi_line)
        k_hi = min(CONV_K, n_lines + CONV_PAD - lo_line)
        if segs and segs[-1][2:] == (k_lo, k_hi):
            segs[-1] = (segs[-1][0], t1, k_lo, k_hi)
        else:
            segs.append((t0, t1, k_lo, k_hi))
    return segs


def _conv_kernel(ah_ref, av_ref, w_ref, b_ref, yh_ref, yv_ref, ph_ref, pv_ref, wb_ref):
    t = ah_ref.shape[0]
    rows = t // GRID_W
    tile = 64
    for kk in range(CONV_K):
        wb_ref[kk] = jnp.broadcast_to(w_ref[kk:kk + 1, :], wb_ref.shape[1:])

    def conv(src_ref, pad_ref, w_cols, line, n_lines, group, dst_ref):
        pad = CONV_PAD * line
        pad_ref[pl.ds(0, pad), :] = jnp.zeros((pad, CONV_HALF), F32)
        pad_ref[pl.ds(pad + t, pad), :] = jnp.zeros((pad, CONV_HALF), F32)

        def fill(i, carry):
            t0 = pl.multiple_of(i * tile, tile)
            pad_ref[pl.ds(pl.multiple_of(pad + t0, 8), tile), :] = (
                src_ref[pl.ds(t0, tile), :].astype(F32))
            return carry

        lax.fori_loop(0, t // tile, fill, 0)
        bias = b_ref[:, w_cols]

        for first, last, k_lo, k_hi in _tap_segments(t // tile, tile / line, n_lines, group):
            def body(i, carry, k_lo=k_lo, k_hi=k_hi):
                t0 = pl.multiple_of(i * tile, tile)
                acc = jnp.broadcast_to(bias, (tile, CONV_HALF))
                for kk in range(k_lo, k_hi):
                    win = pad_ref[pl.ds(pl.multiple_of(t0 + kk * line, 8), tile), :]
                    w8 = wb_ref[kk, :, w_cols]
                    acc = acc + jnp.concatenate([w8] * (tile // 8), axis=0) * win
                dst_ref[pl.ds(t0, tile), :] = acc
                return carry

            lax.fori_loop(first, last, body, 0)

    conv(ah_ref, ph_ref, slice(0, CONV_HALF), rows, GRID_W, 2, yh_ref)
    conv(av_ref, pv_ref, slice(CONV_HALF, CONV_DIM), GRID_W, rows, 4, yv_ref)


def _conv(ah, av, w, b):
    bsz, t, c = ah.shape
    rows = t // GRID_W
    tok = pl.BlockSpec((None, t, c), lambda i: (i, 0, 0))
    return pl.pallas_call(
        _conv_kernel,
        grid=(bsz,),
        in_specs=[tok, tok, _const_spec(w), _const_spec(b)],
        out_specs=[tok, tok],
        out_shape=[jax.ShapeDtypeStruct((bsz, t, c), F32)] * 2,
        scratch_shapes=[
            pltpu.VMEM((t + 2 * CONV_PAD * rows, c), F32),
            pltpu.VMEM((t + 2 * CONV_PAD * GRID_W, c), F32),
            pltpu.VMEM((CONV_K, 8, 2 * c), F32),
        ],
        compiler_params=pltpu.CompilerParams(
            dimension_semantics=("parallel",), vmem_limit_bytes=VMEM_LIMIT),
        name="conv",
    )(ah, av, w, b)


def _post_kernel(x_ref, yh_ref, yv_ref, og_ref, gt_ref, mod_ref, perm_ref, lng_ref, lnb_ref,
                 gpost1_ref, gpre2_ref, gpost2_ref, wco_ref, wgo_ref, wout_ref, wff1_ref,
                 wff2_ref, o_ref):
    d = x_ref.shape[-1]
    d_ff = wff1_ref.shape[1]
    ff_tile = 1024
    sub = 256
    subs = [slice(s, s + sub) for s in range(0, x_ref.shape[0], sub)]
    gt1 = mod_ref[2:3, :]
    sh2 = mod_ref[3:4, :]
    sc2 = mod_ref[4:5, :]
    gt2 = mod_ref[5:6, :]

    yh = []
    for n in range(x_ref.shape[0] // PERM_TOKENS):
        blk = yh_ref[:, PERM_ROWS * n:PERM_ROWS * (n + 1), :].reshape(PERM_TOKENS, CONV_HALF)
        hi, lo = _split_bf16(blk)
        yh.append(_dot(perm_ref[...], hi) + _dot(perm_ref[...], lo))
    yh = jnp.concatenate(yh, axis=0)

    yn = []
    for s in subs:
        yc = jnp.concatenate([yh[s], yv_ref[s, :]], axis=-1)
        yc = yc - jnp.mean(yc, axis=-1, keepdims=True)
        var = jnp.mean(yc * yc, axis=-1, keepdims=True)
        z = yc * lax.rsqrt(var + EPS) * lng_ref[...] + lnb_ref[...]
        yn.append((z * _sigmoid(z)).astype(BF16))
    y_gla = [_dot(og_ref[s, :], wgo_ref[...]) for s in subs]
    y_conv = [_dot(v, wco_ref[...]) for v in yn]
    merged = []
    for n, s in enumerate(subs):
        gates = gt_ref[s, :].astype(F32)
        merged.append((gates[:, :d] * y_conv[n] + gates[:, d:] * y_gla[n]).astype(BF16))
    y = [_dot(v, wout_ref[...]) for v in merged]
    h_mid = [x_ref[s, :] + gt1 * _rms(y[n], gpost1_ref[...]) for n, s in enumerate(subs)]
    u2 = [(_rms(h, gpre2_ref[...]) * (1.0 + sc2) + sh2).astype(BF16) for h in h_mid]
    acc = [jnp.zeros(h.shape, F32) for h in h_mid]
    for j in range(d_ff // ff_tile):
        cols = slice(j * ff_tile, (j + 1) * ff_tile)
        f = [_dot(v, wff1_ref[:, cols]) for v in u2]
        f = [jnp.square(jnp.maximum(v, 0.0)).astype(BF16) for v in f]
        acc = [acc[n] + _dot(f[n], wff2_ref[cols, :]) for n in range(len(subs))]
    for n, s in enumerate(subs):
        o_ref[s, :] = h_mid[n] + gt2 * _rms(acc[n], gpost2_ref[...])


def _post(x, yh, yv, og, gates, mod, perm, ln_g, ln_b, gpost1, gpre2, gpost2, wco, wgo, wout,
          wff1, wff2, tm):
    bsz, t, d = x.shape
    assert tm % PERM_TOKENS == 0
    return pl.pallas_call(
        _post_kernel,
        grid=(bsz, t // tm),
        in_specs=[
            _tok_spec(tm, d), _colmajor_spec(tm), _tok_spec(tm, CONV_HALF), _tok_spec(tm, HV),
            _tok_spec(tm, 2 * d),
            pl.BlockSpec((None,) + mod.shape[1:], lambda b, i: (b, 0, 0)),
            _single_spec(perm), _const_spec(ln_g), _const_spec(ln_b),
            _const_spec(gpost1), _const_spec(gpre2), _const_spec(gpost2),
            _single_spec(wco), _single_spec(wgo), _single_spec(wout), _single_spec(wff1),
            _single_spec(wff2),
        ],
        out_specs=_tok_spec(tm, d),
        out_shape=jax.ShapeDtypeStruct((bsz, t, d), F32),
        compiler_params=pltpu.CompilerParams(
            dimension_semantics=("parallel", "parallel"), vmem_limit_bytes=VMEM_LIMIT),
        name="post",
    )(x, yh, yv, og, gates, mod, perm, ln_g, ln_b, gpost1, gpre2, gpost2, wco, wgo, wout, wff1,
      wff2)


def kernel(x, c, ctx, c_ctx, w_mod, b_mod, g_pre1, g_post1, g_pre2, g_post2, w_in, conv_w,
           conv_b, conv_ln_g, conv_ln_b, w_conv_out, w_decay, b_decay, gla_norm_g, w_gla_out,
           w_out, w_ff1, w_ff2):
    bsz, t, d = x.shape
    depth = w_in.shape[0]
    assert depth == 1 and d == D_MODEL and t % (GRID_W * 8) == 0
    assert w_in.shape[1:] == (d, COL_END) and conv_w.shape[1:] == (CONV_K, CONV_DIM)
    assert w_decay.shape[1:] == (2, DECAY_RANK, HK) and ctx.shape[0] == bsz and ctx.shape[2] == d

    n_rows = -(-(bsz + 1) // 8) * 8
    c_all = jnp.zeros((n_rows, d), F32).at[:bsz].set(c).at[bsz].set(c_ctx)
    m = _adaln(c_all, w_mod, b_mod)
    mod = m[:bsz].reshape(bsz, N_MOD, d)
    mod_ctx = m[bsz, :2 * d].reshape(1, 2, d)

    wm, wdin, wg = _split_w_in(jnp.swapaxes(w_in[0], 0, 1))
    wdec = jnp.zeros((LANES, 2 * HK), F32)
    wdec = wdec.at[:DECAY_RANK, :HK].set(w_decay[0, 0])
    wdec = wdec.at[DECAY_RANK:2 * DECAY_RANK, HK:].set(w_decay[0, 1]).astype(BF16)
    bdec = b_decay[0].reshape(1, 2 * HK)
    row = lambda v: v.reshape(1, -1)

    perm = _to_colmajor_perm()
    to_colmajor = jnp.asarray(perm, BF16)
    to_rowmajor = jnp.asarray(perm.T, BF16)

    ah, av, q, k, v, rs, laf, lab, gates, wco, wgo, wout, wff1, wff2 = _proj(
        x, mod, row(g_pre1[0]), to_colmajor, wm, wdin, wdec, bdec, wg,
        (w_conv_out, w_gla_out, w_out, w_ff1, w_ff2), tm=1024)
    og = _gla(q, k, v, laf, lab, rs, row(gla_norm_g[0]),
              ctx, mod_ctx, row(g_pre1[0]), wm, wdin, wdec, bdec)
    yh, yv = _conv(ah.reshape(bsz, t, CONV_HALF), av, conv_w[0], row(conv_b[0]))
    yh = yh.reshape(bsz, GRID_W, t // GRID_W, CONV_HALF)

    return _post(x, yh, yv, og, gates, mod, to_rowmajor, row(conv_ln_g[0]), row(conv_ln_b[0]),
                 row(g_post1[0]), row(g_pre2[0]), row(g_post2[0]),
                 wco, wgo, wout, wff1, wff2, tm=512)
```

```python
import functools

import jax
import jax.numpy as jnp
import numpy as np
from jax import lax
from jax.experimental import pallas as pl
from jax.experimental.pallas import tpu as pltpu

D_MODEL = 1024
GRID_W = 64
CONV_DIM = 512
CONV_K = 31
GLA_HEADS = 4
GLA_DK = 64
GLA_DV = 128
DECAY_RANK = 16
GATE_NORM = 16.0
CHUNK = 64
N_MOD = 6
EPS = 1e-6

COL_Q = 2 * CONV_DIM
COL_K = COL_Q + GLA_HEADS * GLA_DK
COL_V = COL_K + GLA_HEADS * GLA_DK
COL_R = COL_V + GLA_HEADS * GLA_DV
COL_DEC = COL_R + GLA_HEADS * GLA_DV
COL_GATE = COL_DEC + 2 * DECAY_RANK
COL_END = COL_GATE + 2 * D_MODEL

LANES = 128
HK = GLA_HEADS * GLA_DK
HV = GLA_HEADS * GLA_DV
CONV_HALF = CONV_DIM // 2
CONV_PAD = CONV_K // 2
PERM_ROWS = 8
PERM_TOKENS = PERM_ROWS * GRID_W
PROJ_SUB = 256
SUPER = 2 * CHUNK
STEP = 8 * SUPER

VMEM_LIMIT = 56 * 1024 * 1024

F32 = jnp.float32
BF16 = jnp.bfloat16

NT = (((1,), (1,)), ((), ()))
TN = (((0,), (0,)), ((), ()))


def _dot(a, b):
    return jnp.dot(a, b, preferred_element_type=F32)


def _rms(x, g):
    ms = jnp.mean(x * x, axis=-1, keepdims=True)
    return x * lax.rsqrt(ms + EPS) * g


def _sigmoid(x):
    return jax.nn.sigmoid(x)


def _split_bf16(x):
    hi = x.astype(BF16)
    lo = (x - hi.astype(F32)).astype(BF16)
    return hi, lo


def _to_colmajor_perm():
    out = np.arange(PERM_TOKENS)
    src = (out % PERM_ROWS) * GRID_W + out // PERM_ROWS
    p = np.zeros((PERM_TOKENS, PERM_TOKENS), np.float32)
    p[out, src] = 1.0
    return p


def _prep_kernel(c_ref, wmod_ref, bmod_ref, wm_t_ref, wd_t_ref, wg_t_ref,
                 m_ref, wm_ref, wdin_ref, wg_ref, *, n_k, n_gate):
    j = pl.program_id(0)

    @pl.when(j == 0)
    def _():
        m_ref[...] = jnp.broadcast_to(bmod_ref[...], m_ref.shape)
        wdin_ref[...] = wd_t_ref[...].astype(BF16).T

    @pl.when(j < n_k)
    def _():
        c = c_ref[...]
        s_hi, s_lo = _split_bf16(c * _sigmoid(c))
        w_hi, w_lo = _split_bf16(wmod_ref[...])
        m_ref[...] += _dot(s_hi, w_hi) + _dot(s_lo, w_hi) + _dot(s_hi, w_lo)

    wm_ref[...] = wm_t_ref[...].astype(BF16).T

    @pl.when(j < n_gate)
    def _():
        wg_ref[...] = wg_t_ref[...].astype(BF16).T


def _prep(c_all, w_mod, b_mod, w_in_t):
    n_rows, d = c_all.shape
    n_out = w_mod.shape[-1]
    n = w_in_t.shape[0]
    tk = 256
    tr = 512
    n_k = d // tk
    n_main = COL_DEC // tr
    n_gate = (n - COL_GATE) // tr
    assert COL_DEC % tr == 0 and (n - COL_GATE) % tr == 0 and max(n_gate, n_k) <= n_main
    k_blk = lambda j: jnp.minimum(j, n_k - 1)
    gate_blk = lambda j: jnp.minimum(j, n_gate - 1)
    return pl.pallas_call(
        functools.partial(_prep_kernel, n_k=n_k, n_gate=n_gate),
        grid=(n_main,),
        in_specs=[
            pl.BlockSpec((n_rows, tk), lambda j: (0, k_blk(j))),
            pl.BlockSpec((None, tk, n_out), lambda j: (0, k_blk(j), 0)),
            pl.BlockSpec((1, n_out), lambda j: (0, 0)),
            pl.BlockSpec((tr, d), lambda j: (j, 0)),
            pl.BlockSpec((LANES, d), lambda j: (COL_DEC // LANES, 0)),
            pl.BlockSpec((pl.Element(tr), pl.Element(d)),
                         lambda j: (pl.multiple_of(COL_GATE + tr * gate_blk(j), 32), 0)),
        ],
        out_specs=[
            pl.BlockSpec((n_rows, n_out), lambda j: (0, 0)),
            pl.BlockSpec((d, tr), lambda j: (0, j)),
            pl.BlockSpec((d, LANES), lambda j: (0, 0)),
            pl.BlockSpec((d, tr), lambda j: (0, gate_blk(j))),
        ],
        out_shape=[jax.ShapeDtypeStruct((n_rows, n_out), F32)]
        + [jax.ShapeDtypeStruct((d, w), BF16) for w in (COL_DEC, LANES, n - COL_GATE)],
        compiler_params=pltpu.CompilerParams(
            dimension_semantics=("arbitrary",), vmem_limit_bytes=VMEM_LIMIT),
        name="prep",
    )(c_all, w_mod, b_mod, w_in_t, w_in_t, w_in_t)


def _prenorm_mod(x, g, mod_ref):
    sh = mod_ref[0:1, :]
    sc = mod_ref[1:2, :]
    return (_rms(x, g) * (1.0 + sc) + sh).astype(BF16)


def _decay_logs(z, wdec_ref, bdec_ref):
    z_hi, z_lo = z
    logits = _dot(z_hi, wdec_ref[...]) + _dot(z_lo, wdec_ref[...]) + bdec_ref[...]
    ls = jnp.minimum(logits, 0.0) - jnp.log(1.0 + jnp.exp(-jnp.abs(logits)))
    return ls * (1.0 / GATE_NORM)


N_LATE_W = 5


def _proj_kernel(x_ref, mod_ref, g_ref, perm_ref, wm_ref, wdin_ref, wdec_ref, bdec_ref, wg_ref,
                 *refs):
    late_f32 = refs[:N_LATE_W]
    (ah_ref, av_ref, q_ref, k_ref, v_ref, rs_ref, laf_ref, lab_ref,
     gt_ref) = refs[N_LATE_W:-N_LATE_W]
    late_bf16 = refs[-N_LATE_W:]
    for src, dst in zip(late_f32, late_bf16):
        dst[...] = src[...].astype(BF16)
    subs = [slice(s, s + PROJ_SUB) for s in range(0, x_ref.shape[0], PROJ_SUB)]
    g = g_ref[...]
    ub = [_prenorm_mod(x_ref[s, :], g, mod_ref) for s in subs]
    z = [_split_bf16(_dot(u, wdin_ref[...])) for u in ub]
    a_h = []
    for n, s in enumerate(subs):
        glu = _dot(ub[n], wm_ref[:, 0:COL_Q])
        a = (glu[:, :CONV_DIM] * _sigmoid(glu[:, CONV_DIM:])).astype(BF16)
        av_ref[s, :] = a[:, CONV_HALF:]
        a_h.append(a[:, :CONV_HALF])
    for n, s in enumerate(subs):
        q_ref[s, :] = (_dot(ub[n], wm_ref[:, COL_Q:COL_K]) * (GLA_DK ** -0.5)).astype(BF16)
    for n, s in enumerate(subs):
        k_ref[s, :] = _dot(ub[n], wm_ref[:, COL_K:COL_V]).astype(BF16)
    for n, s in enumerate(subs):
        v_ref[s, :] = _dot(ub[n], wm_ref[:, COL_V:COL_R]).astype(BF16)
    for n, s in enumerate(subs):
        la = _decay_logs(z[n], wdec_ref, bdec_ref)
        laf_ref[s, :] = la[:, :HK]
        lab_ref[s, :] = la[:, HK:]
    for n, s in enumerate(subs):
        r = _dot(ub[n], wm_ref[:, COL_R:COL_DEC])
        rs_ref[s, :] = (r * _sigmoid(r)).astype(BF16)
    per_group = PERM_TOKENS // PROJ_SUB
    for n in range(x_ref.shape[0] // PERM_TOKENS):
        a_rm = jnp.concatenate(a_h[per_group * n:per_group * (n + 1)], axis=0)
        a_cm = _dot(perm_ref[...], a_rm)
        ah_ref[:, PERM_ROWS * n:PERM_ROWS * (n + 1), :] = a_cm.reshape(
            GRID_W, PERM_ROWS, CONV_HALF)
    for n, s in enumerate(subs):
        gt_ref[s, :] = _sigmoid(_dot(ub[n], wg_ref[...])).astype(BF16)


def _const_spec(arr):
    nd = arr.ndim
    return pl.BlockSpec(arr.shape, lambda *_: (0,) * nd)


def _tok_spec(tm, width):
    return pl.BlockSpec((None, tm, width), lambda b, i: (b, i, 0))


def _colmajor_spec(tm):
    return pl.BlockSpec((None, GRID_W, tm // GRID_W, CONV_HALF), lambda b, i: (b, 0, i, 0))


def _single_spec(arr):
    nd = arr.ndim
    return pl.BlockSpec(arr.shape, lambda *_: (0,) * nd, pipeline_mode=pl.Buffered(1))


def _proj(x, mod, g, perm, wm, wdin, wdec, bdec, wg, late_weights, tm):
    bsz, t, d = x.shape
    widths = (CONV_HALF, HK, HK, HV, HV, HK, HK, 2 * D_MODEL)
    dtypes = (BF16, BF16, BF16, BF16, BF16, F32, F32, BF16)
    assert tm % PERM_TOKENS == 0 and len(late_weights) == N_LATE_W
    ah_shape = jax.ShapeDtypeStruct((bsz, GRID_W, t // GRID_W, CONV_HALF), F32)
    n_i = t // tm
    n_steps = bsz * n_i
    late_in, late_out, late_shape = [], [], []
    for w in late_weights:
        _, rows, cols = w.shape
        assert rows % (16 * n_steps) == 0
        blk = rows // n_steps
        late_in.append(pl.BlockSpec((None, blk, cols), lambda b, i: (0, b * n_i + i, 0)))
        late_out.append(pl.BlockSpec((blk, cols), lambda b, i: (b * n_i + i, 0)))
        late_shape.append(jax.ShapeDtypeStruct((rows, cols), BF16))
    return pl.pallas_call(
        _proj_kernel,
        grid=(bsz, n_i),
        in_specs=[
            _tok_spec(tm, d),
            pl.BlockSpec((None,) + mod.shape[1:], lambda b, i: (b, 0, 0)),
            _const_spec(g), _single_spec(perm), _single_spec(wm), _single_spec(wdin),
            _single_spec(wdec), _const_spec(bdec), _single_spec(wg),
        ] + late_in,
        out_specs=[_colmajor_spec(tm)] + [_tok_spec(tm, w) for w in widths] + late_out,
        out_shape=[ah_shape] + [jax.ShapeDtypeStruct((bsz, t, w), dt)
                                for w, dt in zip(widths, dtypes)] + late_shape,
        compiler_params=pltpu.CompilerParams(
            dimension_semantics=("parallel", "parallel"), vmem_limit_bytes=VMEM_LIMIT),
        name="proj",
    )(x, mod, g, perm, wm, wdin, wdec, bdec, wg, *late_weights)


def _gla_scale(qs, ks, las, dirn, want_out):
    tri_b, mid, last = dirn["tri_b"], dirn["mid"], dirn["last"]
    la_hi, la_lo = _split_bf16(las)
    g = _dot(tri_b, la_hi) + _dot(tri_b, la_lo)
    u = dict(kt=[], kend=[], qt=[], qg=[], e=[])
    for c in range(SUPER // CHUNK):
        rows = slice(c * CHUNK, (c + 1) * CHUNK)
        gc = g[rows]
        g_mid = gc[mid:mid + 1]
        g_last = gc[last:last + 1]
        u["e"].append(jnp.exp(g_last))
        kt = ks[rows].astype(F32) * jnp.exp(g_mid - gc)
        u["kt"].append(kt.astype(BF16))
        u["kend"].append((kt * jnp.exp(g_last - g_mid)).astype(BF16))
        if want_out:
            qt = qs[rows].astype(F32) * jnp.exp(gc - g_mid)
            u["qt"].append(qt.astype(BF16))
            u["qg"].append((qt * jnp.exp(g_mid)).astype(BF16))
    return u


def _pair_lanes(pair):
    return slice(LANES * pair, LANES * (pair + 1))


def _gla_scores(u, pair, head_lane):
    lanes = _pair_lanes(pair)
    kt = jnp.concatenate([c[:, lanes] for c in u["kt"]], axis=0)
    qt = jnp.concatenate([c[:, lanes] for c in u["qt"]], axis=0)
    kt2 = jnp.concatenate([kt * hm.astype(BF16) for hm in head_lane], axis=0)
    return lax.dot_general(qt, kt2, NT, preferred_element_type=F32)


def _gla_update(u, vs, h, head_lane):
    lanes = _pair_lanes(h // 2)
    hm = head_lane[h % 2].astype(BF16)
    k0 = u["kend"][0][:, lanes] * hm
    k1 = u["kend"][1][:, lanes] * hm
    z = jnp.zeros_like(k0)
    kbd = jnp.concatenate([jnp.concatenate([k0, z], axis=1),
                           jnp.concatenate([z, k1], axis=1)], axis=0)
    return lax.dot_general(vs[:, GLA_DV * h:GLA_DV * (h + 1)], kbd, TN,
                           preferred_element_type=F32)


def _gla_intra(scores, vs, pair, mask2):
    a = jnp.where(mask2, scores, 0.0).astype(BF16)
    v0 = vs[:, GLA_DV * 2 * pair:GLA_DV * (2 * pair + 1)]
    v1 = vs[:, GLA_DV * (2 * pair + 1):GLA_DV * (2 * pair + 2)]
    z = jnp.zeros_like(v0)
    vbd = jnp.concatenate([jnp.concatenate([v0, z], axis=1),
                           jnp.concatenate([z, v1], axis=1)], axis=0)
    return _dot(a, vbd)


def _gla_kernel(q_ref, k_ref, v_ref, laf_ref, lab_ref, rs_ref, g_ref,
                ctx_ref, modc_ref, gpre_ref, wm_ref, wdin_ref, wdec_ref, bdec_ref,
                o_ref, of_ref, ob_ref, sf_ref, sb_ref, kc_ref, vc_ref, lafc_ref, labc_ref):
    t = q_ref.shape[0]
    t_ctx = kc_ref.shape[0]

    ubc = _prenorm_mod(ctx_ref[...], gpre_ref[...], modc_ref)
    zc = _split_bf16(_dot(ubc, wdin_ref[...]))
    kc_ref[...] = _dot(ubc, wm_ref[:, COL_K:COL_V]).astype(BF16)
    vc_ref[...] = _dot(ubc, wm_ref[:, COL_V:COL_R]).astype(BF16)
    lac = _decay_logs(zc, wdec_ref, bdec_ref)
    lafc_ref[...] = lac[:, :HK]
    labc_ref[...] = lac[:, HK:]

    half = t // 2
    n_steps = t // STEP
    supers = STEP // SUPER
    n_chunks = SUPER // CHUNK
    pairs = GLA_HEADS // 2

    row = lax.broadcasted_iota(jnp.int32, (SUPER, SUPER), 0)
    col = lax.broadcasted_iota(jnp.int32, (SUPER, SUPER), 1)
    same_chunk = (row // CHUNK) == (col // CHUNK)
    low_mask = jnp.logical_and(same_chunk, col <= row)
    up_mask = jnp.logical_and(same_chunk, col >= row)
    lane = lax.broadcasted_iota(jnp.int32, (1, LANES), 1)
    head_lane = (jnp.where(lane < GLA_DK, 1.0, 0.0), jnp.where(lane >= GLA_DK, 1.0, 0.0))

    fwd = dict(tri_b=jnp.where(low_mask, 1.0, 0.0).astype(BF16),
               mask2=jnp.concatenate([low_mask, low_mask], axis=1),
               mid=CHUNK // 2 - 1, last=CHUNK - 1, order=(0, 1), st_ref=sf_ref)
    bwd = dict(tri_b=jnp.where(up_mask, 1.0, 0.0).astype(BF16),
               mask2=jnp.concatenate([up_mask, up_mask], axis=1),
               mid=CHUNK // 2, last=0, order=(1, 0), st_ref=sb_ref)

    sf_ref[...] = jnp.zeros_like(sf_ref)
    sb_ref[...] = jnp.zeros_like(sb_ref)
    gn = g_ref[...]

    def run(units, want_out, finalize):
        vals = []
        for dirn, refs, t0 in units:
            q_r, k_r, v_r, la_r = refs
            rows = pl.ds(t0, SUPER)
            qs = q_r[rows, :] if want_out else None
            vals.append((_gla_scale(qs, k_r[rows, :], la_r[rows, :], dirn, want_out),
                         v_r[rows, :]))
        scores = []
        if want_out:
            scores = [[_gla_scores(u, p, head_lane) for p in range(pairs)] for u, _ in vals]
        upd = [[_gla_update(u, vs, h, head_lane) for h in range(GLA_HEADS)] for u, vs in vals]
        intra = []
        if want_out:
            intra = [[_gla_intra(scores[n][p], vals[n][1], p, units[n][0]["mask2"])
                      for p in range(pairs)] for n in range(len(units))]
        states = {}
        for n, (dirn, _, t0) in enumerate(units):
            key = id(dirn["st_ref"])
            if key not in states:
                states[key] = (dirn["st_ref"], [dirn["st_ref"][h] for h in range(GLA_HEADS)])
            st = states[key][1]
            u = vals[n][0]
            inter = [[None] * n_chunks for _ in range(pairs)]
            for c in dirn["order"]:
                if want_out:
                    for p in range(pairs):
                        st2 = jnp.concatenate([st[2 * p].astype(BF16),
                                               st[2 * p + 1].astype(BF16)], axis=0)
                        inter[p][c] = lax.dot_general(u["qg"][c][:, _pair_lanes(p)], st2, NT,
                                                      preferred_element_type=F32)
                for h in range(GLA_HEADS):
                    e = u["e"][c][:, _pair_lanes(h // 2)]
                    st[h] = st[h] * e + upd[n][h][:, LANES * c:LANES * (c + 1)]
            if not want_out:
                continue
            fwd_unit = dirn is fwd
            for p in range(pairs):
                o2 = intra[n][p] + jnp.concatenate(inter[p], axis=0)
                for j in range(2):
                    cols = slice(GLA_DV * (2 * p + j), GLA_DV * (2 * p + j + 1))
                    o = o2[:, GLA_DV * j:GLA_DV * (j + 1)]
                    if finalize:
                        other = (ob_ref[pl.ds(t0 - half, SUPER), cols] if fwd_unit
                                 else of_ref[pl.ds(t0, SUPER), cols])
                        rs = rs_ref[pl.ds(t0, SUPER), cols].astype(F32)
                        o_ref[pl.ds(t0, SUPER), cols] = (_rms(o + other, gn) * rs).astype(BF16)
                    elif fwd_unit:
                        of_ref[pl.ds(t0, SUPER), cols] = o
                    else:
                        ob_ref[pl.ds(t0 - half, SUPER), cols] = o
        for st_ref, st in states.values():
            for h in range(GLA_HEADS):
                st_ref[h] = st[h]

    ctx_units = []
    for s in range(t_ctx // SUPER):
        ctx_units.append((fwd, (None, kc_ref, vc_ref, lafc_ref), s * SUPER))
        ctx_units.append((bwd, (None, kc_ref, vc_ref, labc_ref), t_ctx - (s + 1) * SUPER))
    run(ctx_units, want_out=False, finalize=False)

    def step(i, finalize):
        units = []
        for s in range(supers):
            t0 = pl.multiple_of(i * STEP + s * SUPER, SUPER)
            t1 = pl.multiple_of(t - (i + 1) * STEP + (supers - 1 - s) * SUPER, SUPER)
            units.append((fwd, (q_ref, k_ref, v_ref, laf_ref), t0))
            units.append((bwd, (q_ref, k_ref, v_ref, lab_ref), t1))
        run(units, want_out=True, finalize=finalize)

    def park_body(i, carry):
        step(i, False)
        return carry

    def finish_body(i, carry):
        step(i, True)
        return carry

    lax.fori_loop(0, n_steps // 2, park_body, 0)
    lax.fori_loop(n_steps // 2, n_steps, finish_body, 0)


def _gla(q, k, v, laf, lab, rs, gnorm, ctx, mod_ctx, gpre, wm, wdin, wdec, bdec):
    bsz, t, _ = q.shape
    t_ctx = ctx.shape[1]
    assert t % (2 * STEP) == 0 and t_ctx % SUPER == 0

    def bspec(arr):
        return pl.BlockSpec((None,) + arr.shape[1:], lambda b: (b, 0, 0))

    args = (q, k, v, laf, lab, rs)
    return pl.pallas_call(
        _gla_kernel,
        grid=(bsz,),
        in_specs=[bspec(a) for a in args] + [_const_spec(gnorm), bspec(ctx),
                  pl.BlockSpec((None,) + mod_ctx.shape[1:], lambda b: (0, 0, 0)),
                  _const_spec(gpre), _single_spec(wm), _single_spec(wdin), _single_spec(wdec),
                  _const_spec(bdec)],
        out_specs=pl.BlockSpec((None, t, HV), lambda b: (b, 0, 0)),
        out_shape=jax.ShapeDtypeStruct((bsz, t, HV), BF16),
        scratch_shapes=[
            pltpu.VMEM((t // 2, HV), F32), pltpu.VMEM((t // 2, HV), F32),
            pltpu.VMEM((GLA_HEADS, GLA_DV, LANES), F32),
            pltpu.VMEM((GLA_HEADS, GLA_DV, LANES), F32),
            pltpu.VMEM((t_ctx, HK), BF16), pltpu.VMEM((t_ctx, HV), BF16),
            pltpu.VMEM((t_ctx, HK), F32), pltpu.VMEM((t_ctx, HK), F32),
        ],
        compiler_params=pltpu.CompilerParams(
            dimension_semantics=("parallel",), vmem_limit_bytes=VMEM_LIMIT),
        name="gla",
    )(*args, gnorm, ctx, mod_ctx, gpre, wm, wdin, wdec, bdec)


def _tap_segments(n_tiles, lines_per_tile, n_lines, group):
    segs = []
    for t0 in range(0, n_tiles, group):
        t1 = min(t0 + group, n_tiles)
        lo_line = int(t0 * lines_per_tile)
        hi_line = int(-(-t1 * lines_per_tile // 1)) - 1
        k_lo = max(0, CONV_PAD - hi_line)
        k_hi = min(CONV_K, n_lines + CONV_PAD - lo_line)
        if segs and segs[-1][2:] == (k_lo, k_hi):
            segs[-1] = (segs[-1][0], t1, k_lo, k_hi)
        else:
            segs.append((t0, t1, k_lo, k_hi))
    return segs


def _conv_kernel(ah_ref, av_ref, w_ref, b_ref, yh_ref, yv_ref, ph_ref, pv_ref, wb_ref):
    t = ah_ref.shape[0]
    rows = t // GRID_W
    tile = 64
    for kk in range(CONV_K):
        wb_ref[kk] = jnp.broadcast_to(w_ref[kk:kk + 1, :], wb_ref.shape[1:])

    def conv(src_ref, pad_ref, w_cols, line, n_lines, group, dst_ref):
        pad = CONV_PAD * line
        pad_ref[pl.ds(0, pad), :] = jnp.zeros((pad, CONV_HALF), F32)
        pad_ref[pl.ds(pad + t, pad), :] = jnp.zeros((pad, CONV_HALF), F32)

        def fill(i, carry):
            t0 = pl.multiple_of(i * tile, tile)
            pad_ref[pl.ds(pl.multiple_of(pad + t0, 8), tile), :] = (
                src_ref[pl.ds(t0, tile), :].astype(F32))
            return carry

        lax.fori_loop(0, t // tile, fill, 0)
        bias = b_ref[:, w_cols]

        for first, last, k_lo, k_hi in _tap_segments(t // tile, tile / line, n_lines, group):
            def body(i, carry, k_lo=k_lo, k_hi=k_hi):
                t0 = pl.multiple_of(i * tile, tile)
                acc = jnp.broadcast_to(bias, (tile, CONV_HALF))
                for kk in range(k_lo, k_hi):
                    win = pad_ref[pl.ds(pl.multiple_of(t0 + kk * line, 8), tile), :]
                    w8 = wb_ref[kk, :, w_cols]
                    acc = acc + jnp.concatenate([w8] * (tile // 8), axis=0) * win
                dst_ref[pl.ds(t0, tile), :] = acc
                return carry

            lax.fori_loop(first, last, body, 0)

    conv(ah_ref, ph_ref, slice(0, CONV_HALF), rows, GRID_W, 2, yh_ref)
    conv(av_ref, pv_ref, slice(CONV_HALF, CONV_DIM), GRID_W, rows, 4, yv_ref)


def _conv(ah, av, w, b):
    bsz, t, c = ah.shape
    rows = t // GRID_W
    tok = pl.BlockSpec((None, t, c), lambda i: (i, 0, 0))
    return pl.pallas_call(
        _conv_kernel,
        grid=(bsz,),
        in_specs=[tok, tok, _const_spec(w), _const_spec(b)],
        out_specs=[tok, tok],
        out_shape=[jax.ShapeDtypeStruct((bsz, t, c), F32)] * 2,
        scratch_shapes=[
            pltpu.VMEM((t + 2 * CONV_PAD * rows, c), F32),
            pltpu.VMEM((t + 2 * CONV_PAD * GRID_W, c), F32),
            pltpu.VMEM((CONV_K, 8, 2 * c), F32),
        ],
        compiler_params=pltpu.CompilerParams(
            dimension_semantics=("parallel",), vmem_limit_bytes=VMEM_LIMIT),
        name="conv",
    )(ah, av, w, b)


def _post_kernel(x_ref, yh_ref, yv_ref, og_ref, gt_ref, mod_ref, perm_ref, lng_ref, lnb_ref,
                 gpost1_ref, gpre2_ref, gpost2_ref, wco_ref, wgo_ref, wout_ref, wff1_ref,
                 wff2_ref, o_ref):
    d = x_ref.shape[-1]
    d_ff = wff1_ref.shape[1]
    ff_tile = 1024
    sub = 256
    subs = [slice(s, s + sub) for s in range(0, x_ref.shape[0], sub)]
    gt1 = mod_ref[2:3, :]
    sh2 = mod_ref[3:4, :]
    sc2 = mod_ref[4:5, :]
    gt2 = mod_ref[5:6, :]

    yh = []
    for n in range(x_ref.shape[0] // PERM_TOKENS):
        blk = yh_ref[:, PERM_ROWS * n:PERM_ROWS * (n + 1), :].reshape(PERM_TOKENS, CONV_HALF)
        hi, lo = _split_bf16(blk)
        yh.append(_dot(perm_ref[...], hi) + _dot(perm_ref[...], lo))
    yh = jnp.concatenate(yh, axis=0)

    yn = []
    for s in subs:
        yc = jnp.concatenate([yh[s], yv_ref[s, :]], axis=-1)
        yc = yc - jnp.mean(yc, axis=-1, keepdims=True)
        var = jnp.mean(yc * yc, axis=-1, keepdims=True)
        z = yc * lax.rsqrt(var + EPS) * lng_ref[...] + lnb_ref[...]
        yn.append((z * _sigmoid(z)).astype(BF16))
    y_gla = [_dot(og_ref[s, :], wgo_ref[...]) for s in subs]
    y_conv = [_dot(v, wco_ref[...]) for v in yn]
    merged = []
    for n, s in enumerate(subs):
        gates = gt_ref[s, :].astype(F32)
        merged.append((gates[:, :d] * y_conv[n] + gates[:, d:] * y_gla[n]).astype(BF16))
    y = [_dot(v, wout_ref[...]) for v in merged]
    h_mid = [x_ref[s, :] + gt1 * _rms(y[n], gpost1_ref[...]) for n, s in enumerate(subs)]
    u2 = [(_rms(h, gpre2_ref[...]) * (1.0 + sc2) + sh2).astype(BF16) for h in h_mid]
    acc = [jnp.zeros(h.shape, F32) for h in h_mid]
    for j in range(d_ff // ff_tile):
        cols = slice(j * ff_tile, (j + 1) * ff_tile)
        f = [_dot(v, wff1_ref[:, cols]) for v in u2]
        f = [jnp.square(jnp.maximum(v, 0.0)).astype(BF16) for v in f]
        acc = [acc[n] + _dot(f[n], wff2_ref[cols, :]) for n in range(len(subs))]
    for n, s in enumerate(subs):
        o_ref[s, :] = h_mid[n] + gt2 * _rms(acc[n], gpost2_ref[...])


def _post(x, yh, yv, og, gates, mod, perm, ln_g, ln_b, gpost1, gpre2, gpost2, wco, wgo, wout,
          wff1, wff2, tm):
    bsz, t, d = x.shape
    assert tm % PERM_TOKENS == 0
    return pl.pallas_call(
        _post_kernel,
        grid=(bsz, t // tm),
        in_specs=[
            _tok_spec(tm, d), _colmajor_spec(tm), _tok_spec(tm, CONV_HALF), _tok_spec(tm, HV),
            _tok_spec(tm, 2 * d),
            pl.BlockSpec((None,) + mod.shape[1:], lambda b, i: (b, 0, 0)),
            _single_spec(perm), _const_spec(ln_g), _const_spec(ln_b),
            _const_spec(gpost1), _const_spec(gpre2), _const_spec(gpost2),
            _single_spec(wco), _single_spec(wgo), _single_spec(wout), _single_spec(wff1),
            _single_spec(wff2),
        ],
        out_specs=_tok_spec(tm, d),
        out_shape=jax.ShapeDtypeStruct((bsz, t, d), F32),
        compiler_params=pltpu.CompilerParams(
            dimension_semantics=("parallel", "parallel"), vmem_limit_bytes=VMEM_LIMIT),
        name="post",
    )(x, yh, yv, og, gates, mod, perm, ln_g, ln_b, gpost1, gpre2, gpost2, wco, wgo, wout, wff1,
      wff2)


def kernel(x, c, ctx, c_ctx, w_mod, b_mod, g_pre1, g_post1, g_pre2, g_post2, w_in, conv_w,
           conv_b, conv_ln_g, conv_ln_b, w_conv_out, w_decay, b_decay, gla_norm_g, w_gla_out,
           w_out, w_ff1, w_ff2):
    bsz, t, d = x.shape
    depth = w_in.shape[0]
    assert depth == 1 and d == D_MODEL and t % (GRID_W * 8) == 0
    assert w_in.shape[1:] == (d, COL_END) and conv_w.shape[1:] == (CONV_K, CONV_DIM)
    assert w_decay.shape[1:] == (2, DECAY_RANK, HK) and ctx.shape[0] == bsz and ctx.shape[2] == d

    n_rows = -(-(bsz + 1) // 8) * 8
    c_all = jnp.zeros((n_rows, d), F32).at[:bsz].set(c).at[bsz].set(c_ctx)
    m, wm, wdin, wg = _prep(c_all, w_mod, b_mod, jnp.swapaxes(w_in[0], 0, 1))
    mod = m[:bsz].reshape(bsz, N_MOD, d)
    mod_ctx = m[bsz, :2 * d].reshape(1, 2, d)

    wdec = jnp.zeros((LANES, 2 * HK), F32)
    wdec = wdec.at[:DECAY_RANK, :HK].set(w_decay[0, 0])
    wdec = wdec.at[DECAY_RANK:2 * DECAY_RANK, HK:].set(w_decay[0, 1]).astype(BF16)
    bdec = b_decay[0].reshape(1, 2 * HK)
    row = lambda v: v.reshape(1, -1)

    perm = _to_colmajor_perm()
    to_colmajor = jnp.asarray(perm, BF16)
    to_rowmajor = jnp.asarray(perm.T, BF16)

    ah, av, q, k, v, rs, laf, lab, gates, wco, wgo, wout, wff1, wff2 = _proj(
        x, mod, row(g_pre1[0]), to_colmajor, wm, wdin, wdec, bdec, wg,
        (w_conv_out, w_gla_out, w_out, w_ff1, w_ff2), tm=1024)
    og = _gla(q, k, v, laf, lab, rs, row(gla_norm_g[0]),
              ctx, mod_ctx, row(g_pre1[0]), wm, wdin, wdec, bdec)
    yh, yv = _conv(ah.reshape(bsz, t, CONV_HALF), av, conv_w[0], row(conv_b[0]))
    yh = yh.reshape(bsz, GRID_W, t // GRID_W, CONV_HALF)

    return _post(x, yh, yv, og, gates, mod, to_rowmajor, row(conv_ln_g[0]), row(conv_ln_b[0]),
                 row(g_post1[0]), row(g_pre2[0]), row(g_post2[0]),
                 wco, wgo, wout, wff1, wff2, tm=512)
```

```python
import functools

import jax
import jax.numpy as jnp
import numpy as np
from jax import lax
from jax.experimental import pallas as pl
from jax.experimental.pallas import tpu as pltpu

D_MODEL = 1024
GRID_W = 64
CONV_DIM = 512
CONV_K = 31
GLA_HEADS = 4
GLA_DK = 64
GLA_DV = 128
DECAY_RANK = 16
GATE_NORM = 16.0
CHUNK = 64
N_MOD = 6
EPS = 1e-6

COL_Q = 2 * CONV_DIM
COL_K = COL_Q + GLA_HEADS * GLA_DK
COL_V = COL_K + GLA_HEADS * GLA_DK
COL_R = COL_V + GLA_HEADS * GLA_DV
COL_DEC = COL_R + GLA_HEADS * GLA_DV
COL_GATE = COL_DEC + 2 * DECAY_RANK
COL_END = COL_GATE + 2 * D_MODEL

LANES = 128
HK = GLA_HEADS * GLA_DK
HV = GLA_HEADS * GLA_DV
CONV_HALF = CONV_DIM // 2
CONV_PAD = CONV_K // 2
PERM_ROWS = 8
PERM_TOKENS = PERM_ROWS * GRID_W
PROJ_SUB = 256
SUPER = 2 * CHUNK
STEP = 8 * SUPER

VMEM_LIMIT = 56 * 1024 * 1024

F32 = jnp.float32
BF16 = jnp.bfloat16

NT = (((1,), (1,)), ((), ()))
TN = (((0,), (0,)), ((), ()))


def _dot(a, b):
    return jnp.dot(a, b, preferred_element_type=F32)


def _rms(x, g):
    ms = jnp.mean(x * x, axis=-1, keepdims=True)
    return x * lax.rsqrt(ms + EPS) * g


def _sigmoid(x):
    return jax.nn.sigmoid(x)


def _split_bf16(x):
    hi = x.astype(BF16)
    lo = (x - hi.astype(F32)).astype(BF16)
    return hi, lo


def _to_colmajor_perm():
    out = np.arange(PERM_TOKENS)
    src = (out % PERM_ROWS) * GRID_W + out // PERM_ROWS
    p = np.zeros((PERM_TOKENS, PERM_TOKENS), np.float32)
    p[out, src] = 1.0
    return p


def _prep_kernel(c_ref, wmod_ref, bmod_ref, wm_t_ref, wd_t_ref, wg_t_ref,
                 m_ref, wm_ref, wdin_ref, wg_ref, *, n_k, n_gate):
    j = pl.program_id(0)

    @pl.when(j == 0)
    def _():
        m_ref[...] = jnp.broadcast_to(bmod_ref[...], m_ref.shape)
        wdin_ref[...] = wd_t_ref[...].astype(BF16).T

    @pl.when(j < n_k)
    def _():
        c = c_ref[...]
        s_hi, s_lo = _split_bf16(c * _sigmoid(c))
        w_hi, w_lo = _split_bf16(wmod_ref[...])
        m_ref[...] += _dot(s_hi, w_hi) + _dot(s_lo, w_hi) + _dot(s_hi, w_lo)

    wm_ref[...] = wm_t_ref[...].astype(BF16).T

    @pl.when(j < n_gate)
    def _():
        wg_ref[...] = wg_t_ref[...].astype(BF16).T


def _prep(c_all, w_mod, b_mod, w_in_t):
    n_rows, d = c_all.shape
    n_out = w_mod.shape[-1]
    n = w_in_t.shape[0]
    tk = 256
    tr = 512
    n_k = d // tk
    n_main = COL_DEC // tr
    n_gate = (n - COL_GATE) // tr
    assert COL_DEC % tr == 0 and (n - COL_GATE) % tr == 0 and max(n_gate, n_k) <= n_main
    k_blk = lambda j: jnp.minimum(j, n_k - 1)
    gate_blk = lambda j: jnp.minimum(j, n_gate - 1)
    return pl.pallas_call(
        functools.partial(_prep_kernel, n_k=n_k, n_gate=n_gate),
        grid=(n_main,),
        in_specs=[
            pl.BlockSpec((n_rows, tk), lambda j: (0, k_blk(j))),
            pl.BlockSpec((None, tk, n_out), lambda j: (0, k_blk(j), 0)),
            pl.BlockSpec((1, n_out), lambda j: (0, 0)),
            pl.BlockSpec((tr, d), lambda j: (j, 0)),
            pl.BlockSpec((LANES, d), lambda j: (COL_DEC // LANES, 0)),
            pl.BlockSpec((pl.Element(tr), pl.Element(d)),
                         lambda j: (pl.multiple_of(COL_GATE + tr * gate_blk(j), 32), 0)),
        ],
        out_specs=[
            pl.BlockSpec((n_rows, n_out), lambda j: (0, 0)),
            pl.BlockSpec((d, tr), lambda j: (0, j)),
            pl.BlockSpec((d, LANES), lambda j: (0, 0)),
            pl.BlockSpec((d, tr), lambda j: (0, gate_blk(j))),
        ],
        out_shape=[jax.ShapeDtypeStruct((n_rows, n_out), F32)]
        + [jax.ShapeDtypeStruct((d, w), BF16) for w in (COL_DEC, LANES, n - COL_GATE)],
        compiler_params=pltpu.CompilerParams(
            dimension_semantics=("arbitrary",), vmem_limit_bytes=VMEM_LIMIT),
        name="prep",
    )(c_all, w_mod, b_mod, w_in_t, w_in_t, w_in_t)


def _mod_vec(m_ref, row, j):
    return m_ref[pl.ds(row, 1), D_MODEL * j:D_MODEL * (j + 1)]


def _prenorm_mod(x, g, m_ref, row):
    sh = _mod_vec(m_ref, row, 0)
    sc = _mod_vec(m_ref, row, 1)
    return (_rms(x, g) * (1.0 + sc) + sh).astype(BF16)


def _decay_logs(z, wdec_ref, bdec_ref):
    z_hi, z_lo = z
    logits = _dot(z_hi, wdec_ref[...]) + _dot(z_lo, wdec_ref[...]) + bdec_ref[...]
    ls = jnp.minimum(logits, 0.0) - jnp.log(1.0 + jnp.exp(-jnp.abs(logits)))
    return ls * (1.0 / GATE_NORM)


N_LATE_W = 5


def _proj_kernel(x_ref, mod_ref, g_ref, perm_ref, wm_ref, wdin_ref, wdec_ref, bdec_ref, wg_ref,
                 *refs):
    late_f32 = refs[:N_LATE_W]
    (ah_ref, av_ref, q_ref, k_ref, v_ref, rs_ref, laf_ref, lab_ref,
     gt_ref) = refs[N_LATE_W:-N_LATE_W]
    late_bf16 = refs[-N_LATE_W:]
    for src, dst in zip(late_f32, late_bf16):
        dst[...] = src[...].astype(BF16)
    subs = [slice(s, s + PROJ_SUB) for s in range(0, x_ref.shape[0], PROJ_SUB)]
    g = g_ref[...]
    ub = [_prenorm_mod(x_ref[s, :], g, mod_ref, pl.program_id(0)) for s in subs]
    z = [_split_bf16(_dot(u, wdin_ref[...])) for u in ub]
    a_h = []
    for n, s in enumerate(subs):
        glu = _dot(ub[n], wm_ref[:, 0:COL_Q])
        a = (glu[:, :CONV_DIM] * _sigmoid(glu[:, CONV_DIM:])).astype(BF16)
        av_ref[s, :] = a[:, CONV_HALF:]
        a_h.append(a[:, :CONV_HALF])
    for n, s in enumerate(subs):
        q_ref[s, :] = (_dot(ub[n], wm_ref[:, COL_Q:COL_K]) * (GLA_DK ** -0.5)).astype(BF16)
    for n, s in enumerate(subs):
        k_ref[s, :] = _dot(ub[n], wm_ref[:, COL_K:COL_V]).astype(BF16)
    for n, s in enumerate(subs):
        v_ref[s, :] = _dot(ub[n], wm_ref[:, COL_V:COL_R]).astype(BF16)
    for n, s in enumerate(subs):
        la = _decay_logs(z[n], wdec_ref, bdec_ref)
        laf_ref[s, :] = la[:, :HK]
        lab_ref[s, :] = la[:, HK:]
    for n, s in enumerate(subs):
        r = _dot(ub[n], wm_ref[:, COL_R:COL_DEC])
        rs_ref[s, :] = (r * _sigmoid(r)).astype(BF16)
    per_group = PERM_TOKENS // PROJ_SUB
    for n in range(x_ref.shape[0] // PERM_TOKENS):
        a_rm = jnp.concatenate(a_h[per_group * n:per_group * (n + 1)], axis=0)
        a_cm = _dot(perm_ref[...], a_rm)
        ah_ref[:, PERM_ROWS * n:PERM_ROWS * (n + 1), :] = a_cm.reshape(
            GRID_W, PERM_ROWS, CONV_HALF)
    for n, s in enumerate(subs):
        gt_ref[s, :] = _sigmoid(_dot(ub[n], wg_ref[...])).astype(BF16)


def _const_spec(arr):
    nd = arr.ndim
    return pl.BlockSpec(arr.shape, lambda *_: (0,) * nd)


def _tok_spec(tm, width):
    return pl.BlockSpec((None, tm, width), lambda b, i: (b, i, 0))


def _colmajor_spec(tm):
    return pl.BlockSpec((None, GRID_W, tm // GRID_W, CONV_HALF), lambda b, i: (b, 0, i, 0))


def _single_spec(arr):
    nd = arr.ndim
    return pl.BlockSpec(arr.shape, lambda *_: (0,) * nd, pipeline_mode=pl.Buffered(1))


def _proj(x, mod, g, perm, wm, wdin, wdec, bdec, wg, late_weights, tm):
    bsz, t, d = x.shape
    widths = (CONV_HALF, HK, HK, HV, HV, HK, HK, 2 * D_MODEL)
    dtypes = (BF16, BF16, BF16, BF16, BF16, F32, F32, BF16)
    assert tm % PERM_TOKENS == 0 and len(late_weights) == N_LATE_W
    ah_shape = jax.ShapeDtypeStruct((bsz, GRID_W, t // GRID_W, CONV_HALF), F32)
    n_i = t // tm
    n_steps = bsz * n_i
    late_in, late_out, late_shape = [], [], []
    for w in late_weights:
        _, rows, cols = w.shape
        assert rows % (16 * n_steps) == 0
        blk = rows // n_steps
        late_in.append(pl.BlockSpec((None, blk, cols), lambda b, i: (0, b * n_i + i, 0)))
        late_out.append(pl.BlockSpec((blk, cols), lambda b, i: (b * n_i + i, 0)))
        late_shape.append(jax.ShapeDtypeStruct((rows, cols), BF16))
    return pl.pallas_call(
        _proj_kernel,
        grid=(bsz, n_i),
        in_specs=[
            _tok_spec(tm, d),
            _const_spec(mod),
            _const_spec(g), _single_spec(perm), _single_spec(wm), _single_spec(wdin),
            _single_spec(wdec), _const_spec(bdec), _single_spec(wg),
        ] + late_in,
        out_specs=[_colmajor_spec(tm)] + [_tok_spec(tm, w) for w in widths] + late_out,
        out_shape=[ah_shape] + [jax.ShapeDtypeStruct((bsz, t, w), dt)
                                for w, dt in zip(widths, dtypes)] + late_shape,
        compiler_params=pltpu.CompilerParams(
            dimension_semantics=("parallel", "parallel"), vmem_limit_bytes=VMEM_LIMIT),
        name="proj",
    )(x, mod, g, perm, wm, wdin, wdec, bdec, wg, *late_weights)


def _gla_scale(qs, ks, las, dirn, want_out):
    tri_b, mid, last = dirn["tri_b"], dirn["mid"], dirn["last"]
    la_hi, la_lo = _split_bf16(las)
    g = _dot(tri_b, la_hi) + _dot(tri_b, la_lo)
    u = dict(kt=[], kend=[], qt=[], qg=[], e=[])
    for c in range(SUPER // CHUNK):
        rows = slice(c * CHUNK, (c + 1) * CHUNK)
        gc = g[rows]
        g_mid = gc[mid:mid + 1]
        g_last = gc[last:last + 1]
        u["e"].append(jnp.exp(g_last))
        kt = ks[rows].astype(F32) * jnp.exp(g_mid - gc)
        u["kt"].append(kt.astype(BF16))
        u["kend"].append((kt * jnp.exp(g_last - g_mid)).astype(BF16))
        if want_out:
            qt = qs[rows].astype(F32) * jnp.exp(gc - g_mid)
            u["qt"].append(qt.astype(BF16))
            u["qg"].append((qt * jnp.exp(g_mid)).astype(BF16))
    return u


def _pair_lanes(pair):
    return slice(LANES * pair, LANES * (pair + 1))


def _gla_scores(u, pair, head_lane):
    lanes = _pair_lanes(pair)
    kt = jnp.concatenate([c[:, lanes] for c in u["kt"]], axis=0)
    qt = jnp.concatenate([c[:, lanes] for c in u["qt"]], axis=0)
    kt2 = jnp.concatenate([kt * hm.astype(BF16) for hm in head_lane], axis=0)
    return lax.dot_general(qt, kt2, NT, preferred_element_type=F32)


def _gla_update(u, vs, h, head_lane):
    lanes = _pair_lanes(h // 2)
    hm = head_lane[h % 2].astype(BF16)
    k0 = u["kend"][0][:, lanes] * hm
    k1 = u["kend"][1][:, lanes] * hm
    z = jnp.zeros_like(k0)
    kbd = jnp.concatenate([jnp.concatenate([k0, z], axis=1),
                           jnp.concatenate([z, k1], axis=1)], axis=0)
    return lax.dot_general(vs[:, GLA_DV * h:GLA_DV * (h + 1)], kbd, TN,
                           preferred_element_type=F32)


def _gla_intra(scores, vs, pair, mask2):
    a = jnp.where(mask2, scores, 0.0).astype(BF16)
    v0 = vs[:, GLA_DV * 2 * pair:GLA_DV * (2 * pair + 1)]
    v1 = vs[:, GLA_DV * (2 * pair + 1):GLA_DV * (2 * pair + 2)]
    z = jnp.zeros_like(v0)
    vbd = jnp.concatenate([jnp.concatenate([v0, z], axis=1),
                           jnp.concatenate([z, v1], axis=1)], axis=0)
    return _dot(a, vbd)


def _gla_kernel(q_ref, k_ref, v_ref, laf_ref, lab_ref, rs_ref, g_ref,
                ctx_ref, modc_ref, gpre_ref, wm_ref, wdin_ref, wdec_ref, bdec_ref,
                o_ref, of_ref, ob_ref, sf_ref, sb_ref, kc_ref, vc_ref, lafc_ref, labc_ref):
    t = q_ref.shape[0]
    t_ctx = kc_ref.shape[0]

    ubc = _prenorm_mod(ctx_ref[...], gpre_ref[...], modc_ref, pl.num_programs(0))
    zc = _split_bf16(_dot(ubc, wdin_ref[...]))
    kc_ref[...] = _dot(ubc, wm_ref[:, COL_K:COL_V]).astype(BF16)
    vc_ref[...] = _dot(ubc, wm_ref[:, COL_V:COL_R]).astype(BF16)
    lac = _decay_logs(zc, wdec_ref, bdec_ref)
    lafc_ref[...] = lac[:, :HK]
    labc_ref[...] = lac[:, HK:]

    half = t // 2
    n_steps = t // STEP
    supers = STEP // SUPER
    n_chunks = SUPER // CHUNK
    pairs = GLA_HEADS // 2

    row = lax.broadcasted_iota(jnp.int32, (SUPER, SUPER), 0)
    col = lax.broadcasted_iota(jnp.int32, (SUPER, SUPER), 1)
    same_chunk = (row // CHUNK) == (col // CHUNK)
    low_mask = jnp.logical_and(same_chunk, col <= row)
    up_mask = jnp.logical_and(same_chunk, col >= row)
    lane = lax.broadcasted_iota(jnp.int32, (1, LANES), 1)
    head_lane = (jnp.where(lane < GLA_DK, 1.0, 0.0), jnp.where(lane >= GLA_DK, 1.0, 0.0))

    fwd = dict(tri_b=jnp.where(low_mask, 1.0, 0.0).astype(BF16),
               mask2=jnp.concatenate([low_mask, low_mask], axis=1),
               mid=CHUNK // 2 - 1, last=CHUNK - 1, order=(0, 1), st_ref=sf_ref)
    bwd = dict(tri_b=jnp.where(up_mask, 1.0, 0.0).astype(BF16),
               mask2=jnp.concatenate([up_mask, up_mask], axis=1),
               mid=CHUNK // 2, last=0, order=(1, 0), st_ref=sb_ref)

    sf_ref[...] = jnp.zeros_like(sf_ref)
    sb_ref[...] = jnp.zeros_like(sb_ref)
    gn = g_ref[...]

    def run(units, want_out, finalize):
        vals = []
        for dirn, refs, t0 in units:
            q_r, k_r, v_r, la_r = refs
            rows = pl.ds(t0, SUPER)
            qs = q_r[rows, :] if want_out else None
            vals.append((_gla_scale(qs, k_r[rows, :], la_r[rows, :], dirn, want_out),
                         v_r[rows, :]))
        scores = []
        if want_out:
            scores = [[_gla_scores(u, p, head_lane) for p in range(pairs)] for u, _ in vals]
        upd = [[_gla_update(u, vs, h, head_lane) for h in range(GLA_HEADS)] for u, vs in vals]
        intra = []
        if want_out:
            intra = [[_gla_intra(scores[n][p], vals[n][1], p, units[n][0]["mask2"])
                      for p in range(pairs)] for n in range(len(units))]
        states = {}
        for n, (dirn, _, t0) in enumerate(units):
            key = id(dirn["st_ref"])
            if key not in states:
                states[key] = (dirn["st_ref"], [dirn["st_ref"][h] for h in range(GLA_HEADS)])
            st = states[key][1]
            u = vals[n][0]
            inter = [[None] * n_chunks for _ in range(pairs)]
            for c in dirn["order"]:
                if want_out:
                    for p in range(pairs):
                        st2 = jnp.concatenate([st[2 * p].astype(BF16),
                                               st[2 * p + 1].astype(BF16)], axis=0)
                        inter[p][c] = lax.dot_general(u["qg"][c][:, _pair_lanes(p)], st2, NT,
                                                      preferred_element_type=F32)
                for h in range(GLA_HEADS):
                    e = u["e"][c][:, _pair_lanes(h // 2)]
                    st[h] = st[h] * e + upd[n][h][:, LANES * c:LANES * (c + 1)]
            if not want_out:
                continue
            fwd_unit = dirn is fwd
            for p in range(pairs):
                o2 = intra[n][p] + jnp.concatenate(inter[p], axis=0)
                for j in range(2):
                    cols = slice(GLA_DV * (2 * p + j), GLA_DV * (2 * p + j + 1))
                    o = o2[:, GLA_DV * j:GLA_DV * (j + 1)]
                    if finalize:
                        other = (ob_ref[pl.ds(t0 - half, SUPER), cols] if fwd_unit
                                 else of_ref[pl.ds(t0, SUPER), cols])
                        rs = rs_ref[pl.ds(t0, SUPER), cols].astype(F32)
                        o_ref[pl.ds(t0, SUPER), cols] = (_rms(o + other, gn) * rs).astype(BF16)
                    elif fwd_unit:
                        of_ref[pl.ds(t0, SUPER), cols] = o
                    else:
                        ob_ref[pl.ds(t0 - half, SUPER), cols] = o
        for st_ref, st in states.values():
            for h in range(GLA_HEADS):
                st_ref[h] = st[h]

    ctx_units = []
    for s in range(t_ctx // SUPER):
        ctx_units.append((fwd, (None, kc_ref, vc_ref, lafc_ref), s * SUPER))
        ctx_units.append((bwd, (None, kc_ref, vc_ref, labc_ref), t_ctx - (s + 1) * SUPER))
    run(ctx_units, want_out=False, finalize=False)

    def step(i, finalize):
        units = []
        for s in range(supers):
            t0 = pl.multiple_of(i * STEP + s * SUPER, SUPER)
            t1 = pl.multiple_of(t - (i + 1) * STEP + (supers - 1 - s) * SUPER, SUPER)
            units.append((fwd, (q_ref, k_ref, v_ref, laf_ref), t0))
            units.append((bwd, (q_ref, k_ref, v_ref, lab_ref), t1))
        run(units, want_out=True, finalize=finalize)

    def park_body(i, carry):
        step(i, False)
        return carry

    def finish_body(i, carry):
        step(i, True)
        return carry

    lax.fori_loop(0, n_steps // 2, park_body, 0)
    lax.fori_loop(n_steps // 2, n_steps, finish_body, 0)


def _gla(q, k, v, laf, lab, rs, gnorm, ctx, mod_ctx, gpre, wm, wdin, wdec, bdec):
    bsz, t, _ = q.shape
    t_ctx = ctx.shape[1]
    assert t % (2 * STEP) == 0 and t_ctx % SUPER == 0

    def bspec(arr):
        return pl.BlockSpec((None,) + arr.shape[1:], lambda b: (b, 0, 0))

    args = (q, k, v, laf, lab, rs)
    return pl.pallas_call(
        _gla_kernel,
        grid=(bsz,),
        in_specs=[bspec(a) for a in args] + [_const_spec(gnorm), bspec(ctx),
                  _const_spec(mod_ctx),
                  _const_spec(gpre), _single_spec(wm), _single_spec(wdin), _single_spec(wdec),
                  _const_spec(bdec)],
        out_specs=pl.BlockSpec((None, t, HV), lambda b: (b, 0, 0)),
        out_shape=jax.ShapeDtypeStruct((bsz, t, HV), BF16),
        scratch_shapes=[
            pltpu.VMEM((t // 2, HV), F32), pltpu.VMEM((t // 2, HV), F32),
            pltpu.VMEM((GLA_HEADS, GLA_DV, LANES), F32),
            pltpu.VMEM((GLA_HEADS, GLA_DV, LANES), F32),
            pltpu.VMEM((t_ctx, HK), BF16), pltpu.VMEM((t_ctx, HV), BF16),
            pltpu.VMEM((t_ctx, HK), F32), pltpu.VMEM((t_ctx, HK), F32),
        ],
        compiler_params=pltpu.CompilerParams(
            dimension_semantics=("parallel",), vmem_limit_bytes=VMEM_LIMIT),
        name="gla",
    )(*args, gnorm, ctx, mod_ctx, gpre, wm, wdin, wdec, bdec)


def _tap_segments(n_tiles, lines_per_tile, n_lines, group):
    segs = []
    for t0 in range(0, n_tiles, group):
        t1 = min(t0 + group, n_tiles)
        lo_line = int(t0 * lines_per_tile)
        hi_line = int(-(-t1 * lines_per_tile // 1)) - 1
        k_lo = max(0, CONV_PAD - hi_line)
        k_hi = min(CONV_K, n_lines + CONV_PAD - lo_line)
        if segs and segs[-1][2:] == (k_lo, k_hi):
            segs[-1] = (segs[-1][0], t1, k_lo, k_hi)
        else:
            segs.append((t0, t1, k_lo, k_hi))
    return segs


def _conv_kernel(ah_ref, av_ref, w_ref, b_ref, yh_ref, yv_ref, ph_ref, pv_ref, wb_ref):
    t = ah_ref.shape[0]
    rows = t // GRID_W
    tile = 64
    for kk in range(CONV_K):
        wb_ref[kk] = jnp.broadcast_to(w_ref[kk:kk + 1, :], wb_ref.shape[1:])

    def conv(src_ref, pad_ref, w_cols, line, n_lines, group, dst_ref):
        pad = CONV_PAD * line
        pad_ref[pl.ds(0, pad), :] = jnp.zeros((pad, CONV_HALF), F32)
        pad_ref[pl.ds(pad + t, pad), :] = jnp.zeros((pad, CONV_HALF), F32)

        def fill(i, carry):
            t0 = pl.multiple_of(i * tile, tile)
            pad_ref[pl.ds(pl.multiple_of(pad + t0, 8), tile), :] = (
                src_ref[pl.ds(t0, tile), :].astype(F32))
            return carry

        lax.fori_loop(0, t // tile, fill, 0)
        bias = b_ref[:, w_cols]

        for first, last, k_lo, k_hi in _tap_segments(t // tile, tile / line, n_lines, group):
            def body(i, carry, k_lo=k_lo, k_hi=k_hi):
                t0 = pl.multiple_of(i * tile, tile)
                acc = jnp.broadcast_to(bias, (tile, CONV_HALF))
                for kk in range(k_lo, k_hi):
                    win = pad_ref[pl.ds(pl.multiple_of(t0 + kk * line, 8), tile), :]
                    w8 = wb_ref[kk, :, w_cols]
                    acc = acc + jnp.concatenate([w8] * (tile // 8), axis=0) * win
                dst_ref[pl.ds(t0, tile), :] = acc
                return carry

            lax.fori_loop(first, last, body, 0)

    conv(ah_ref, ph_ref, slice(0, CONV_HALF), rows, GRID_W, 2, yh_ref)
    conv(av_ref, pv_ref, slice(CONV_HALF, CONV_DIM), GRID_W, rows, 4, yv_ref)


def _conv(ah, av, w, b):
    bsz, t, c = ah.shape
    rows = t // GRID_W
    tok = pl.BlockSpec((None, t, c), lambda i: (i, 0, 0))
    return pl.pallas_call(
        _conv_kernel,
        grid=(bsz,),
        in_specs=[tok, tok, _const_spec(w), _const_spec(b)],
        out_specs=[tok, tok],
        out_shape=[jax.ShapeDtypeStruct((bsz, t, c), F32)] * 2,
        scratch_shapes=[
            pltpu.VMEM((t + 2 * CONV_PAD * rows, c), F32),
            pltpu.VMEM((t + 2 * CONV_PAD * GRID_W, c), F32),
            pltpu.VMEM((CONV_K, 8, 2 * c), F32),
        ],
        compiler_params=pltpu.CompilerParams(
            dimension_semantics=("parallel",), vmem_limit_bytes=VMEM_LIMIT),
        name="conv",
    )(ah, av, w, b)


def _post_kernel(x_ref, yh_ref, yv_ref, og_ref, gt_ref, mod_ref, perm_ref, lng_ref, lnb_ref,
                 gpost1_ref, gpre2_ref, gpost2_ref, wco_ref, wgo_ref, wout_ref, wff1_ref,
                 wff2_ref, o_ref):
    d = x_ref.shape[-1]
    d_ff = wff1_ref.shape[1]
    ff_tile = 1024
    sub = 256
    subs = [slice(s, s + sub) for s in range(0, x_ref.shape[0], sub)]
    gt1, sh2, sc2, gt2 = (_mod_vec(mod_ref, pl.program_id(0), j) for j in (2, 3, 4, 5))

    yh = []
    for n in range(x_ref.shape[0] // PERM_TOKENS):
        blk = yh_ref[:, PERM_ROWS * n:PERM_ROWS * (n + 1), :].reshape(PERM_TOKENS, CONV_HALF)
        hi, lo = _split_bf16(blk)
        yh.append(_dot(perm_ref[...], hi) + _dot(perm_ref[...], lo))
    yh = jnp.concatenate(yh, axis=0)

    yn = []
    for s in subs:
        yc = jnp.concatenate([yh[s], yv_ref[s, :]], axis=-1)
        yc = yc - jnp.mean(yc, axis=-1, keepdims=True)
        var = jnp.mean(yc * yc, axis=-1, keepdims=True)
        z = yc * lax.rsqrt(var + EPS) * lng_ref[...] + lnb_ref[...]
        yn.append((z * _sigmoid(z)).astype(BF16))
    y_gla = [_dot(og_ref[s, :], wgo_ref[...]) for s in subs]
    y_conv = [_dot(v, wco_ref[...]) for v in yn]
    merged = []
    for n, s in enumerate(subs):
        gates = gt_ref[s, :].astype(F32)
        merged.append((gates[:, :d] * y_conv[n] + gates[:, d:] * y_gla[n]).astype(BF16))
    y = [_dot(v, wout_ref[...]) for v in merged]
    h_mid = [x_ref[s, :] + gt1 * _rms(y[n], gpost1_ref[...]) for n, s in enumerate(subs)]
    u2 = [(_rms(h, gpre2_ref[...]) * (1.0 + sc2) + sh2).astype(BF16) for h in h_mid]
    acc = [jnp.zeros(h.shape, F32) for h in h_mid]
    for j in range(d_ff // ff_tile):
        cols = slice(j * ff_tile, (j + 1) * ff_tile)
        f = [_dot(v, wff1_ref[:, cols]) for v in u2]
        f = [jnp.square(jnp.maximum(v, 0.0)).astype(BF16) for v in f]
        acc = [acc[n] + _dot(f[n], wff2_ref[cols, :]) for n in range(len(subs))]
    for n, s in enumerate(subs):
        o_ref[s, :] = h_mid[n] + gt2 * _rms(acc[n], gpost2_ref[...])


def _post(x, yh, yv, og, gates, mod, perm, ln_g, ln_b, gpost1, gpre2, gpost2, wco, wgo, wout,
          wff1, wff2, tm):
    bsz, t, d = x.shape
    assert tm % PERM_TOKENS == 0
    return pl.pallas_call(
        _post_kernel,
        grid=(bsz, t // tm),
        in_specs=[
            _tok_spec(tm, d), _colmajor_spec(tm), _tok_spec(tm, CONV_HALF), _tok_spec(tm, HV),
            _tok_spec(tm, 2 * d),
            _const_spec(mod),
            _single_spec(perm), _const_spec(ln_g), _const_spec(ln_b),
            _const_spec(gpost1), _const_spec(gpre2), _const_spec(gpost2),
            _single_spec(wco), _single_spec(wgo), _single_spec(wout), _single_spec(wff1),
            _single_spec(wff2),
        ],
        out_specs=_tok_spec(tm, d),
        out_shape=jax.ShapeDtypeStruct((bsz, t, d), F32),
        compiler_params=pltpu.CompilerParams(
            dimension_semantics=("parallel", "parallel"), vmem_limit_bytes=VMEM_LIMIT),
        name="post",
    )(x, yh, yv, og, gates, mod, perm, ln_g, ln_b, gpost1, gpre2, gpost2, wco, wgo, wout, wff1,
      wff2)


def kernel(x, c, ctx, c_ctx, w_mod, b_mod, g_pre1, g_post1, g_pre2, g_post2, w_in, conv_w,
           conv_b, conv_ln_g, conv_ln_b, w_conv_out, w_decay, b_decay, gla_norm_g, w_gla_out,
           w_out, w_ff1, w_ff2):
    bsz, t, d = x.shape
    depth = w_in.shape[0]
    assert depth == 1 and d == D_MODEL and t % (GRID_W * 8) == 0
    assert w_in.shape[1:] == (d, COL_END) and conv_w.shape[1:] == (CONV_K, CONV_DIM)
    assert w_mod.shape[1:] == (d, N_MOD * d)
    assert w_decay.shape[1:] == (2, DECAY_RANK, HK) and ctx.shape[0] == bsz and ctx.shape[2] == d

    n_rows = -(-(bsz + 1) // 8) * 8
    c_all = jnp.zeros((n_rows, d), F32).at[:bsz].set(c).at[bsz].set(c_ctx)
    m, wm, wdin, wg = _prep(c_all, w_mod, b_mod, jnp.swapaxes(w_in[0], 0, 1))

    wdec = jnp.zeros((LANES, 2 * HK), F32)
    wdec = wdec.at[:DECAY_RANK, :HK].set(w_decay[0, 0])
    wdec = wdec.at[DECAY_RANK:2 * DECAY_RANK, HK:].set(w_decay[0, 1]).astype(BF16)
    bdec = b_decay[0].reshape(1, 2 * HK)
    row = lambda v: v.reshape(1, -1)

    perm = _to_colmajor_perm()
    to_colmajor = jnp.asarray(perm, BF16)
    to_rowmajor = jnp.asarray(perm.T, BF16)

    ah, av, q, k, v, rs, laf, lab, gates, wco, wgo, wout, wff1, wff2 = _proj(
        x, m, row(g_pre1[0]), to_colmajor, wm, wdin, wdec, bdec, wg,
        (w_conv_out, w_gla_out, w_out, w_ff1, w_ff2), tm=1024)
    og = _gla(q, k, v, laf, lab, rs, row(gla_norm_g[0]),
              ctx, m, row(g_pre1[0]), wm, wdin, wdec, bdec)
    yh, yv = _conv(ah.reshape(bsz, t, CONV_HALF), av, conv_w[0], row(conv_b[0]))
    yh = yh.reshape(bsz, GRID_W, t // GRID_W, CONV_HALF)

    return _post(x, yh, yv, og, gates, m, to_rowmajor, row(conv_ln_g[0]), row(conv_ln_b[0]),
                 row(g_post1[0]), row(g_pre2[0]), row(g_post2[0]),
                 wco, wgo, wout, wff1, wff2, tm=512)
```

```python
import functools

import jax
import jax.numpy as jnp
import numpy as np
from jax import lax
from jax.experimental import pallas as pl
from jax.experimental.pallas import tpu as pltpu

D_MODEL = 1024
GRID_W = 64
CONV_DIM = 512
CONV_K = 31
GLA_HEADS = 4
GLA_DK = 64
GLA_DV = 128
DECAY_RANK = 16
GATE_NORM = 16.0
CHUNK = 64
N_MOD = 6
EPS = 1e-6

COL_Q = 2 * CONV_DIM
COL_K = COL_Q + GLA_HEADS * GLA_DK
COL_V = COL_K + GLA_HEADS * GLA_DK
COL_R = COL_V + GLA_HEADS * GLA_DV
COL_DEC = COL_R + GLA_HEADS * GLA_DV
COL_GATE = COL_DEC + 2 * DECAY_RANK
COL_END = COL_GATE + 2 * D_MODEL

LANES = 128
HK = GLA_HEADS * GLA_DK
HV = GLA_HEADS * GLA_DV
CONV_HALF = CONV_DIM // 2
CONV_PAD = CONV_K // 2
PERM_ROWS = 8
PERM_TOKENS = PERM_ROWS * GRID_W
PROJ_SUB = 256
SUPER = 2 * CHUNK
STEP = 8 * SUPER

VMEM_LIMIT = 56 * 1024 * 1024

F32 = jnp.float32
BF16 = jnp.bfloat16

NT = (((1,), (1,)), ((), ()))
TN = (((0,), (0,)), ((), ()))


def _dot(a, b):
    return jnp.dot(a, b, preferred_element_type=F32)


def _rms(x, g):
    ms = jnp.mean(x * x, axis=-1, keepdims=True)
    return x * lax.rsqrt(ms + EPS) * g


def _sigmoid(x):
    return jax.nn.sigmoid(x)


def _split_bf16(x):
    hi = x.astype(BF16)
    lo = (x - hi.astype(F32)).astype(BF16)
    return hi, lo


def _to_colmajor_perm():
    out = np.arange(PERM_TOKENS)
    src = (out % PERM_ROWS) * GRID_W + out // PERM_ROWS
    p = np.zeros((PERM_TOKENS, PERM_TOKENS), np.float32)
    p[out, src] = 1.0
    return p


def _prep_kernel(c_ref, cc_ref, wmod_ref, bmod_ref, wm_t_ref, wd_t_ref, wg_t_ref,
                 m_ref, wm_ref, wdin_ref, wg_ref, *, n_k, n_gate):
    j = pl.program_id(0)

    @pl.when(j == 0)
    def _():
        m_ref[...] = jnp.broadcast_to(bmod_ref[...], m_ref.shape)
        wdin_ref[...] = wd_t_ref[...].astype(BF16).T

    @pl.when(j < n_k)
    def _():
        pad = jnp.zeros((m_ref.shape[0] - c_ref.shape[0] - 1, c_ref.shape[1]), F32)
        c = jnp.concatenate([c_ref[...], cc_ref[...], pad], axis=0)
        s_hi, s_lo = _split_bf16(c * _sigmoid(c))
        w_hi, w_lo = _split_bf16(wmod_ref[...])
        m_ref[...] += _dot(s_hi, w_hi) + _dot(s_lo, w_hi) + _dot(s_hi, w_lo)

    wm_ref[...] = wm_t_ref[...].astype(BF16).T

    @pl.when(j < n_gate)
    def _():
        wg_ref[...] = wg_t_ref[...].astype(BF16).T


def _prep(c, c_ctx, w_mod, b_mod, w_in_t):
    bsz, d = c.shape
    n_rows = -(-(bsz + 1) // 8) * 8
    n_out = w_mod.shape[-1]
    n = w_in_t.shape[0]
    tk = 256
    tr = 512
    n_k = d // tk
    n_main = COL_DEC // tr
    n_gate = (n - COL_GATE) // tr
    assert COL_DEC % tr == 0 and (n - COL_GATE) % tr == 0 and max(n_gate, n_k) <= n_main
    k_blk = lambda j: jnp.minimum(j, n_k - 1)
    gate_blk = lambda j: jnp.minimum(j, n_gate - 1)
    return pl.pallas_call(
        functools.partial(_prep_kernel, n_k=n_k, n_gate=n_gate),
        grid=(n_main,),
        in_specs=[
            pl.BlockSpec((bsz, tk), lambda j: (0, k_blk(j))),
            pl.BlockSpec((1, tk), lambda j: (0, k_blk(j))),
            pl.BlockSpec((None, tk, n_out), lambda j: (0, k_blk(j), 0)),
            pl.BlockSpec((1, n_out), lambda j: (0, 0)),
            pl.BlockSpec((tr, d), lambda j: (j, 0)),
            pl.BlockSpec((LANES, d), lambda j: (COL_DEC // LANES, 0)),
            pl.BlockSpec((pl.Element(tr), pl.Element(d)),
                         lambda j: (pl.multiple_of(COL_GATE + tr * gate_blk(j), 32), 0)),
        ],
        out_specs=[
            pl.BlockSpec((n_rows, n_out), lambda j: (0, 0)),
            pl.BlockSpec((d, tr), lambda j: (0, j)),
            pl.BlockSpec((d, LANES), lambda j: (0, 0)),
            pl.BlockSpec((d, tr), lambda j: (0, gate_blk(j))),
        ],
        out_shape=[jax.ShapeDtypeStruct((n_rows, n_out), F32)]
        + [jax.ShapeDtypeStruct((d, w), BF16) for w in (COL_DEC, LANES, n - COL_GATE)],
        compiler_params=pltpu.CompilerParams(
            dimension_semantics=("arbitrary",), vmem_limit_bytes=VMEM_LIMIT),
        name="prep",
    )(c, c_ctx, w_mod, b_mod, w_in_t, w_in_t, w_in_t)


def _mod_vec(m_ref, row, j):
    return m_ref[pl.ds(row, 1), D_MODEL * j:D_MODEL * (j + 1)]


def _prenorm_mod(x, g, m_ref, row):
    sh = _mod_vec(m_ref, row, 0)
    sc = _mod_vec(m_ref, row, 1)
    return (_rms(x, g) * (1.0 + sc) + sh).astype(BF16)


def _decay_logs(z, wdec_ref, bdec_ref):
    z_hi, z_lo = z
    logits = _dot(z_hi, wdec_ref[...]) + _dot(z_lo, wdec_ref[...]) + bdec_ref[...]
    ls = jnp.minimum(logits, 0.0) - jnp.log(1.0 + jnp.exp(-jnp.abs(logits)))
    return ls * (1.0 / GATE_NORM)


N_LATE_W = 5


def _proj_kernel(x_ref, mod_ref, g_ref, perm_ref, wm_ref, wdin_ref, wdec_ref, bdec_ref, wg_ref,
                 *refs):
    late_f32 = refs[:N_LATE_W]
    (ah_ref, av_ref, q_ref, k_ref, v_ref, rs_ref, laf_ref, lab_ref,
     gt_ref) = refs[N_LATE_W:-N_LATE_W]
    late_bf16 = refs[-N_LATE_W:]
    for src, dst in zip(late_f32, late_bf16):
        dst[...] = src[...].astype(BF16)
    subs = [slice(s, s + PROJ_SUB) for s in range(0, x_ref.shape[0], PROJ_SUB)]
    g = g_ref[...]
    ub = [_prenorm_mod(x_ref[s, :], g, mod_ref, pl.program_id(0)) for s in subs]
    z = [_split_bf16(_dot(u, wdin_ref[...])) for u in ub]
    a_h = []
    for n, s in enumerate(subs):
        glu = _dot(ub[n], wm_ref[:, 0:COL_Q])
        a = (glu[:, :CONV_DIM] * _sigmoid(glu[:, CONV_DIM:])).astype(BF16)
        av_ref[s, :] = a[:, CONV_HALF:]
        a_h.append(a[:, :CONV_HALF])
    for n, s in enumerate(subs):
        q_ref[s, :] = (_dot(ub[n], wm_ref[:, COL_Q:COL_K]) * (GLA_DK ** -0.5)).astype(BF16)
    for n, s in enumerate(subs):
        k_ref[s, :] = _dot(ub[n], wm_ref[:, COL_K:COL_V]).astype(BF16)
    for n, s in enumerate(subs):
        v_ref[s, :] = _dot(ub[n], wm_ref[:, COL_V:COL_R]).astype(BF16)
    for n, s in enumerate(subs):
        la = _decay_logs(z[n], wdec_ref, bdec_ref)
        laf_ref[s, :] = la[:, :HK]
        lab_ref[s, :] = la[:, HK:]
    for n, s in enumerate(subs):
        r = _dot(ub[n], wm_ref[:, COL_R:COL_DEC])
        rs_ref[s, :] = (r * _sigmoid(r)).astype(BF16)
    per_group = PERM_TOKENS // PROJ_SUB
    for n in range(x_ref.shape[0] // PERM_TOKENS):
        a_rm = jnp.concatenate(a_h[per_group * n:per_group * (n + 1)], axis=0)
        a_cm = _dot(perm_ref[...], a_rm)
        ah_ref[:, PERM_ROWS * n:PERM_ROWS * (n + 1), :] = a_cm.reshape(
            GRID_W, PERM_ROWS, CONV_HALF)
    for n, s in enumerate(subs):
        gt_ref[s, :] = _sigmoid(_dot(ub[n], wg_ref[...])).astype(BF16)


def _const_spec(arr):
    nd = arr.ndim
    return pl.BlockSpec(arr.shape, lambda *_: (0,) * nd)


def _tok_spec(tm, width):
    return pl.BlockSpec((None, tm, width), lambda b, i: (b, i, 0))


def _colmajor_spec(tm):
    return pl.BlockSpec((None, GRID_W, tm // GRID_W, CONV_HALF), lambda b, i: (b, 0, i, 0))


def _single_spec(arr):
    nd = arr.ndim
    return pl.BlockSpec(arr.shape, lambda *_: (0,) * nd, pipeline_mode=pl.Buffered(1))


def _proj(x, mod, g, perm, wm, wdin, wdec, bdec, wg, late_weights, tm):
    bsz, t, d = x.shape
    widths = (CONV_HALF, HK, HK, HV, HV, HK, HK, 2 * D_MODEL)
    dtypes = (BF16, BF16, BF16, BF16, BF16, F32, F32, BF16)
    assert tm % PERM_TOKENS == 0 and len(late_weights) == N_LATE_W
    ah_shape = jax.ShapeDtypeStruct((bsz, GRID_W, t // GRID_W, CONV_HALF), F32)
    n_i = t // tm
    n_steps = bsz * n_i
    late_in, late_out, late_shape = [], [], []
    for w in late_weights:
        _, rows, cols = w.shape
        assert rows % (16 * n_steps) == 0
        blk = rows // n_steps
        late_in.append(pl.BlockSpec((None, blk, cols), lambda b, i: (0, b * n_i + i, 0)))
        late_out.append(pl.BlockSpec((blk, cols), lambda b, i: (b * n_i + i, 0)))
        late_shape.append(jax.ShapeDtypeStruct((rows, cols), BF16))
    return pl.pallas_call(
        _proj_kernel,
        grid=(bsz, n_i),
        in_specs=[
            _tok_spec(tm, d),
            _const_spec(mod),
            _const_spec(g), _single_spec(perm), _single_spec(wm), _single_spec(wdin),
            _single_spec(wdec), _const_spec(bdec), _single_spec(wg),
        ] + late_in,
        out_specs=[_colmajor_spec(tm)] + [_tok_spec(tm, w) for w in widths] + late_out,
        out_shape=[ah_shape] + [jax.ShapeDtypeStruct((bsz, t, w), dt)
                                for w, dt in zip(widths, dtypes)] + late_shape,
        compiler_params=pltpu.CompilerParams(
            dimension_semantics=("parallel", "parallel"), vmem_limit_bytes=VMEM_LIMIT),
        name="proj",
    )(x, mod, g, perm, wm, wdin, wdec, bdec, wg, *late_weights)


def _gla_scale(qs, ks, las, dirn, want_out):
    tri_b, mid, last = dirn["tri_b"], dirn["mid"], dirn["last"]
    la_hi, la_lo = _split_bf16(las)
    g = _dot(tri_b, la_hi) + _dot(tri_b, la_lo)
    u = dict(kt=[], kend=[], qt=[], qg=[], e=[])
    for c in range(SUPER // CHUNK):
        rows = slice(c * CHUNK, (c + 1) * CHUNK)
        gc = g[rows]
        g_mid = gc[mid:mid + 1]
        g_last = gc[last:last + 1]
        u["e"].append(jnp.exp(g_last))
        kt = ks[rows].astype(F32) * jnp.exp(g_mid - gc)
        u["kt"].append(kt.astype(BF16))
        u["kend"].append((kt * jnp.exp(g_last - g_mid)).astype(BF16))
        if want_out:
            qt = qs[rows].astype(F32) * jnp.exp(gc - g_mid)
            u["qt"].append(qt.astype(BF16))
            u["qg"].append((qt * jnp.exp(g_mid)).astype(BF16))
    return u


def _pair_lanes(pair):
    return slice(LANES * pair, LANES * (pair + 1))


def _gla_scores(u, pair, head_lane):
    lanes = _pair_lanes(pair)
    kt = jnp.concatenate([c[:, lanes] for c in u["kt"]], axis=0)
    qt = jnp.concatenate([c[:, lanes] for c in u["qt"]], axis=0)
    kt2 = jnp.concatenate([kt * hm.astype(BF16) for hm in head_lane], axis=0)
    return lax.dot_general(qt, kt2, NT, preferred_element_type=F32)


def _gla_update(u, vs, h, head_lane):
    lanes = _pair_lanes(h // 2)
    hm = head_lane[h % 2].astype(BF16)
    k0 = u["kend"][0][:, lanes] * hm
    k1 = u["kend"][1][:, lanes] * hm
    z = jnp.zeros_like(k0)
    kbd = jnp.concatenate([jnp.concatenate([k0, z], axis=1),
                           jnp.concatenate([z, k1], axis=1)], axis=0)
    return lax.dot_general(vs[:, GLA_DV * h:GLA_DV * (h + 1)], kbd, TN,
                           preferred_element_type=F32)


def _gla_intra(scores, vs, pair, mask2):
    a = jnp.where(mask2, scores, 0.0).astype(BF16)
    v0 = vs[:, GLA_DV * 2 * pair:GLA_DV * (2 * pair + 1)]
    v1 = vs[:, GLA_DV * (2 * pair + 1):GLA_DV * (2 * pair + 2)]
    z = jnp.zeros_like(v0)
    vbd = jnp.concatenate([jnp.concatenate([v0, z], axis=1),
                           jnp.concatenate([z, v1], axis=1)], axis=0)
    return _dot(a, vbd)


def _gla_kernel(q_ref, k_ref, v_ref, laf_ref, lab_ref, rs_ref, g_ref,
                ctx_ref, modc_ref, gpre_ref, wm_ref, wdin_ref, wdec_ref, bdec_ref,
                o_ref, of_ref, ob_ref, sf_ref, sb_ref, kc_ref, vc_ref, lafc_ref, labc_ref):
    t = q_ref.shape[0]
    t_ctx = kc_ref.shape[0]

    ubc = _prenorm_mod(ctx_ref[...], gpre_ref[...], modc_ref, pl.num_programs(0))
    zc = _split_bf16(_dot(ubc, wdin_ref[...]))
    kc_ref[...] = _dot(ubc, wm_ref[:, COL_K:COL_V]).astype(BF16)
    vc_ref[...] = _dot(ubc, wm_ref[:, COL_V:COL_R]).astype(BF16)
    lac = _decay_logs(zc, wdec_ref, bdec_ref)
    lafc_ref[...] = lac[:, :HK]
    labc_ref[...] = lac[:, HK:]

    half = t // 2
    n_steps = t // STEP
    supers = STEP // SUPER
    n_chunks = SUPER // CHUNK
    pairs = GLA_HEADS // 2

    row = lax.broadcasted_iota(jnp.int32, (SUPER, SUPER), 0)
    col = lax.broadcasted_iota(jnp.int32, (SUPER, SUPER), 1)
    same_chunk = (row // CHUNK) == (col // CHUNK)
    low_mask = jnp.logical_and(same_chunk, col <= row)
    up_mask = jnp.logical_and(same_chunk, col >= row)
    lane = lax.broadcasted_iota(jnp.int32, (1, LANES), 1)
    head_lane = (jnp.where(lane < GLA_DK, 1.0, 0.0), jnp.where(lane >= GLA_DK, 1.0, 0.0))

    fwd = dict(tri_b=jnp.where(low_mask, 1.0, 0.0).astype(BF16),
               mask2=jnp.concatenate([low_mask, low_mask], axis=1),
               mid=CHUNK // 2 - 1, last=CHUNK - 1, order=(0, 1), st_ref=sf_ref)
    bwd = dict(tri_b=jnp.where(up_mask, 1.0, 0.0).astype(BF16),
               mask2=jnp.concatenate([up_mask, up_mask], axis=1),
               mid=CHUNK // 2, last=0, order=(1, 0), st_ref=sb_ref)

    sf_ref[...] = jnp.zeros_like(sf_ref)
    sb_ref[...] = jnp.zeros_like(sb_ref)
    gn = g_ref[...]

    def run(units, want_out, finalize):
        vals = []
        for dirn, refs, t0 in units:
            q_r, k_r, v_r, la_r = refs
            rows = pl.ds(t0, SUPER)
            qs = q_r[rows, :] if want_out else None
            vals.append((_gla_scale(qs, k_r[rows, :], la_r[rows, :], dirn, want_out),
                         v_r[rows, :]))
        scores = []
        if want_out:
            scores = [[_gla_scores(u, p, head_lane) for p in range(pairs)] for u, _ in vals]
        upd = [[_gla_update(u, vs, h, head_lane) for h in range(GLA_HEADS)] for u, vs in vals]
        intra = []
        if want_out:
            intra = [[_gla_intra(scores[n][p], vals[n][1], p, units[n][0]["mask2"])
                      for p in range(pairs)] for n in range(len(units))]
        states = {}
        for n, (dirn, _, t0) in enumerate(units):
            key = id(dirn["st_ref"])
            if key not in states:
                states[key] = (dirn["st_ref"], [dirn["st_ref"][h] for h in range(GLA_HEADS)])
            st = states[key][1]
            u = vals[n][0]
            inter = [[None] * n_chunks for _ in range(pairs)]
            for c in dirn["order"]:
                if want_out:
                    for p in range(pairs):
                        st2 = jnp.concatenate([st[2 * p].astype(BF16),
                                               st[2 * p + 1].astype(BF16)], axis=0)
                        inter[p][c] = lax.dot_general(u["qg"][c][:, _pair_lanes(p)], st2, NT,
                                                      preferred_element_type=F32)
                for h in range(GLA_HEADS):
                    e = u["e"][c][:, _pair_lanes(h // 2)]
                    st[h] = st[h] * e + upd[n][h][:, LANES * c:LANES * (c + 1)]
            if not want_out:
                continue
            fwd_unit = dirn is fwd
            for p in range(pairs):
                o2 = intra[n][p] + jnp.concatenate(inter[p], axis=0)
                for j in range(2):
                    cols = slice(GLA_DV * (2 * p + j), GLA_DV * (2 * p + j + 1))
                    o = o2[:, GLA_DV * j:GLA_DV * (j + 1)]
                    if finalize:
                        other = (ob_ref[pl.ds(t0 - half, SUPER), cols] if fwd_unit
                                 else of_ref[pl.ds(t0, SUPER), cols])
                        rs = rs_ref[pl.ds(t0, SUPER), cols].astype(F32)
                        o_ref[pl.ds(t0, SUPER), cols] = (_rms(o + other, gn) * rs).astype(BF16)
                    elif fwd_unit:
                        of_ref[pl.ds(t0, SUPER), cols] = o
                    else:
                        ob_ref[pl.ds(t0 - half, SUPER), cols] = o
        for st_ref, st in states.values():
            for h in range(GLA_HEADS):
                st_ref[h] = st[h]

    ctx_units = []
    for s in range(t_ctx // SUPER):
        ctx_units.append((fwd, (None, kc_ref, vc_ref, lafc_ref), s * SUPER))
        ctx_units.append((bwd, (None, kc_ref, vc_ref, labc_ref), t_ctx - (s + 1) * SUPER))
    run(ctx_units, want_out=False, finalize=False)

    def step(i, finalize):
        units = []
        for s in range(supers):
            t0 = pl.multiple_of(i * STEP + s * SUPER, SUPER)
            t1 = pl.multiple_of(t - (i + 1) * STEP + (supers - 1 - s) * SUPER, SUPER)
            units.append((fwd, (q_ref, k_ref, v_ref, laf_ref), t0))
            units.append((bwd, (q_ref, k_ref, v_ref, lab_ref), t1))
        run(units, want_out=True, finalize=finalize)

    def park_body(i, carry):
        step(i, False)
        return carry

    def finish_body(i, carry):
        step(i, True)
        return carry

    lax.fori_loop(0, n_steps // 2, park_body, 0)
    lax.fori_loop(n_steps // 2, n_steps, finish_body, 0)


def _gla(q, k, v, laf, lab, rs, gnorm, ctx, mod_ctx, gpre, wm, wdin, wdec, bdec):
    bsz, t, _ = q.shape
    t_ctx = ctx.shape[1]
    assert t % (2 * STEP) == 0 and t_ctx % SUPER == 0

    def bspec(arr):
        return pl.BlockSpec((None,) + arr.shape[1:], lambda b: (b, 0, 0))

    args = (q, k, v, laf, lab, rs)
    return pl.pallas_call(
        _gla_kernel,
        grid=(bsz,),
        in_specs=[bspec(a) for a in args] + [_const_spec(gnorm), bspec(ctx),
                  _const_spec(mod_ctx),
                  _const_spec(gpre), _single_spec(wm), _single_spec(wdin), _single_spec(wdec),
                  _const_spec(bdec)],
        out_specs=pl.BlockSpec((None, t, HV), lambda b: (b, 0, 0)),
        out_shape=jax.ShapeDtypeStruct((bsz, t, HV), BF16),
        scratch_shapes=[
            pltpu.VMEM((t // 2, HV), F32), pltpu.VMEM((t // 2, HV), F32),
            pltpu.VMEM((GLA_HEADS, GLA_DV, LANES), F32),
            pltpu.VMEM((GLA_HEADS, GLA_DV, LANES), F32),
            pltpu.VMEM((t_ctx, HK), BF16), pltpu.VMEM((t_ctx, HV), BF16),
            pltpu.VMEM((t_ctx, HK), F32), pltpu.VMEM((t_ctx, HK), F32),
        ],
        compiler_params=pltpu.CompilerParams(
            dimension_semantics=("parallel",), vmem_limit_bytes=VMEM_LIMIT),
        name="gla",
    )(*args, gnorm, ctx, mod_ctx, gpre, wm, wdin, wdec, bdec)


def _tap_segments(n_tiles, lines_per_tile, n_lines, group):
    segs = []
    for t0 in range(0, n_tiles, group):
        t1 = min(t0 + group, n_tiles)
        lo_line = int(t0 * lines_per_tile)
        hi_line = int(-(-t1 * lines_per_tile // 1)) - 1
        k_lo = max(0, CONV_PAD - hi_line)
        k_hi = min(CONV_K, n_lines + CONV_PAD - lo_line)
        if segs and segs[-1][2:] == (k_lo, k_hi):
            segs[-1] = (segs[-1][0], t1, k_lo, k_hi)
        else:
            segs.append((t0, t1, k_lo, k_hi))
    return segs


def _conv_kernel(ah_ref, av_ref, w_ref, b_ref, yh_ref, yv_ref, ph_ref, pv_ref, wb_ref):
    t = ah_ref.shape[0]
    rows = t // GRID_W
    tile = 64
    for kk in range(CONV_K):
        wb_ref[kk] = jnp.broadcast_to(w_ref[kk:kk + 1, :], wb_ref.shape[1:])

    def conv(src_ref, pad_ref, w_cols, line, n_lines, group, dst_ref):
        pad = CONV_PAD * line
        pad_ref[pl.ds(0, pad), :] = jnp.zeros((pad, CONV_HALF), F32)
        pad_ref[pl.ds(pad + t, pad), :] = jnp.zeros((pad, CONV_HALF), F32)

        def fill(i, carry):
            t0 = pl.multiple_of(i * tile, tile)
            pad_ref[pl.ds(pl.multiple_of(pad + t0, 8), tile), :] = (
                src_ref[pl.ds(t0, tile), :].astype(F32))
            return carry

        lax.fori_loop(0, t // tile, fill, 0)
        bias = b_ref[:, w_cols]

        for first, last, k_lo, k_hi in _tap_segments(t // tile, tile / line, n_lines, group):
            def body(i, carry, k_lo=k_lo, k_hi=k_hi):
                t0 = pl.multiple_of(i * tile, tile)
                acc = jnp.broadcast_to(bias, (tile, CONV_HALF))
                for kk in range(k_lo, k_hi):
                    win = pad_ref[pl.ds(pl.multiple_of(t0 + kk * line, 8), tile), :]
                    w8 = wb_ref[kk, :, w_cols]
                    acc = acc + jnp.concatenate([w8] * (tile // 8), axis=0) * win
                dst_ref[pl.ds(t0, tile), :] = acc
                return carry

            lax.fori_loop(first, last, body, 0)

    conv(ah_ref, ph_ref, slice(0, CONV_HALF), rows, GRID_W, 2, yh_ref)
    conv(av_ref, pv_ref, slice(CONV_HALF, CONV_DIM), GRID_W, rows, 4, yv_ref)


def _conv(ah, av, w, b):
    bsz, t, c = ah.shape
    rows = t // GRID_W
    tok = pl.BlockSpec((None, t, c), lambda i: (i, 0, 0))
    return pl.pallas_call(
        _conv_kernel,
        grid=(bsz,),
        in_specs=[tok, tok, pl.BlockSpec((None,) + w.shape[1:], lambda i: (0, 0, 0)),
                  _const_spec(b)],
        out_specs=[tok, tok],
        out_shape=[jax.ShapeDtypeStruct((bsz, t, c), F32)] * 2,
        scratch_shapes=[
            pltpu.VMEM((t + 2 * CONV_PAD * rows, c), F32),
            pltpu.VMEM((t + 2 * CONV_PAD * GRID_W, c), F32),
            pltpu.VMEM((CONV_K, 8, 2 * c), F32),
        ],
        compiler_params=pltpu.CompilerParams(
            dimension_semantics=("parallel",), vmem_limit_bytes=VMEM_LIMIT),
        name="conv",
    )(ah, av, w, b)


def _post_kernel(x_ref, yh_ref, yv_ref, og_ref, gt_ref, mod_ref, perm_ref, lng_ref, lnb_ref,
                 gpost1_ref, gpre2_ref, gpost2_ref, wco_ref, wgo_ref, wout_ref, wff1_ref,
                 wff2_ref, o_ref):
    d = x_ref.shape[-1]
    d_ff = wff1_ref.shape[1]
    ff_tile = 1024
    sub = 256
    subs = [slice(s, s + sub) for s in range(0, x_ref.shape[0], sub)]
    gt1, sh2, sc2, gt2 = (_mod_vec(mod_ref, pl.program_id(0), j) for j in (2, 3, 4, 5))

    yh = []
    for n in range(x_ref.shape[0] // PERM_TOKENS):
        blk = yh_ref[:, PERM_ROWS * n:PERM_ROWS * (n + 1), :].reshape(PERM_TOKENS, CONV_HALF)
        hi, lo = _split_bf16(blk)
        yh.append(_dot(perm_ref[...], hi) + _dot(perm_ref[...], lo))
    yh = jnp.concatenate(yh, axis=0)

    yn = []
    for s in subs:
        yc = jnp.concatenate([yh[s], yv_ref[s, :]], axis=-1)
        yc = yc - jnp.mean(yc, axis=-1, keepdims=True)
        var = jnp.mean(yc * yc, axis=-1, keepdims=True)
        z = yc * lax.rsqrt(var + EPS) * lng_ref[...] + lnb_ref[...]
        yn.append((z * _sigmoid(z)).astype(BF16))
    y_gla = [_dot(og_ref[s, :], wgo_ref[...]) for s in subs]
    y_conv = [_dot(v, wco_ref[...]) for v in yn]
    merged = []
    for n, s in enumerate(subs):
        gates = gt_ref[s, :].astype(F32)
        merged.append((gates[:, :d] * y_conv[n] + gates[:, d:] * y_gla[n]).astype(BF16))
    y = [_dot(v, wout_ref[...]) for v in merged]
    h_mid = [x_ref[s, :] + gt1 * _rms(y[n], gpost1_ref[...]) for n, s in enumerate(subs)]
    u2 = [(_rms(h, gpre2_ref[...]) * (1.0 + sc2) + sh2).astype(BF16) for h in h_mid]
    acc = [jnp.zeros(h.shape, F32) for h in h_mid]
    for j in range(d_ff // ff_tile):
        cols = slice(j * ff_tile, (j + 1) * ff_tile)
        f = [_dot(v, wff1_ref[:, cols]) for v in u2]
        f = [jnp.square(jnp.maximum(v, 0.0)).astype(BF16) for v in f]
        acc = [acc[n] + _dot(f[n], wff2_ref[cols, :]) for n in range(len(subs))]
    for n, s in enumerate(subs):
        o_ref[s, :] = h_mid[n] + gt2 * _rms(acc[n], gpost2_ref[...])


def _post(x, yh, yv, og, gates, mod, perm, ln_g, ln_b, gpost1, gpre2, gpost2, wco, wgo, wout,
          wff1, wff2, tm):
    bsz, t, d = x.shape
    assert tm % PERM_TOKENS == 0
    return pl.pallas_call(
        _post_kernel,
        grid=(bsz, t // tm),
        in_specs=[
            _tok_spec(tm, d), _colmajor_spec(tm), _tok_spec(tm, CONV_HALF), _tok_spec(tm, HV),
            _tok_spec(tm, 2 * d),
            _const_spec(mod),
            _single_spec(perm), _const_spec(ln_g), _const_spec(ln_b),
            _const_spec(gpost1), _const_spec(gpre2), _const_spec(gpost2),
            _single_spec(wco), _single_spec(wgo), _single_spec(wout), _single_spec(wff1),
            _single_spec(wff2),
        ],
        out_specs=_tok_spec(tm, d),
        out_shape=jax.ShapeDtypeStruct((bsz, t, d), F32),
        compiler_params=pltpu.CompilerParams(
            dimension_semantics=("parallel", "parallel"), vmem_limit_bytes=VMEM_LIMIT),
        name="post",
    )(x, yh, yv, og, gates, mod, perm, ln_g, ln_b, gpost1, gpre2, gpost2, wco, wgo, wout, wff1,
      wff2)


def kernel(x, c, ctx, c_ctx, w_mod, b_mod, g_pre1, g_post1, g_pre2, g_post2, w_in, conv_w,
           conv_b, conv_ln_g, conv_ln_b, w_conv_out, w_decay, b_decay, gla_norm_g, w_gla_out,
           w_out, w_ff1, w_ff2):
    bsz, t, d = x.shape
    depth = w_in.shape[0]
    assert depth == 1 and d == D_MODEL and t % (GRID_W * 8) == 0
    assert w_in.shape[1:] == (d, COL_END) and conv_w.shape[1:] == (CONV_K, CONV_DIM)
    assert w_mod.shape[1:] == (d, N_MOD * d)
    assert w_decay.shape[1:] == (2, DECAY_RANK, HK) and ctx.shape[0] == bsz and ctx.shape[2] == d

    m, wm, wdin, wg = _prep(c, c_ctx.reshape(1, d), w_mod, b_mod, jnp.swapaxes(w_in[0], 0, 1))

    wdec = jnp.zeros((LANES, 2 * HK), F32)
    wdec = wdec.at[:DECAY_RANK, :HK].set(w_decay[0, 0])
    wdec = wdec.at[DECAY_RANK:2 * DECAY_RANK, HK:].set(w_decay[0, 1]).astype(BF16)
    bdec = b_decay[0].reshape(1, 2 * HK)
    row = lambda v: v.reshape(1, -1)

    perm = _to_colmajor_perm()
    to_colmajor = jnp.asarray(perm, BF16)
    to_rowmajor = jnp.asarray(perm.T, BF16)

    ah, av, q, k, v, rs, laf, lab, gates, wco, wgo, wout, wff1, wff2 = _proj(
        x, m, row(g_pre1[0]), to_colmajor, wm, wdin, wdec, bdec, wg,
        (w_conv_out, w_gla_out, w_out, w_ff1, w_ff2), tm=1024)
    og = _gla(q, k, v, laf, lab, rs, row(gla_norm_g[0]),
              ctx, m, row(g_pre1[0]), wm, wdin, wdec, bdec)
    yh, yv = _conv(ah.reshape(bsz, t, CONV_HALF), av, conv_w, row(conv_b[0]))
    yh = yh.reshape(bsz, GRID_W, t // GRID_W, CONV_HALF)

    return _post(x, yh, yv, og, gates, m, to_rowmajor, row(conv_ln_g[0]), row(conv_ln_b[0]),
                 row(g_post1[0]), row(g_pre2[0]), row(g_post2[0]),
                 wco, wgo, wout, wff1, wff2, tm=512)
```

```python
import functools

import jax
import jax.numpy as jnp
import numpy as np
from jax import lax
from jax.experimental import pallas as pl
from jax.experimental.pallas import tpu as pltpu

D_MODEL = 1024
GRID_W = 64
CONV_DIM = 512
CONV_K = 31
GLA_HEADS = 4
GLA_DK = 64
GLA_DV = 128
DECAY_RANK = 16
GATE_NORM = 16.0
CHUNK = 64
N_MOD = 6
EPS = 1e-6

COL_Q = 2 * CONV_DIM
COL_K = COL_Q + GLA_HEADS * GLA_DK
COL_V = COL_K + GLA_HEADS * GLA_DK
COL_R = COL_V + GLA_HEADS * GLA_DV
COL_DEC = COL_R + GLA_HEADS * GLA_DV
COL_GATE = COL_DEC + 2 * DECAY_RANK
COL_END = COL_GATE + 2 * D_MODEL

LANES = 128
HK = GLA_HEADS * GLA_DK
HV = GLA_HEADS * GLA_DV
CONV_HALF = CONV_DIM // 2
CONV_PAD = CONV_K // 2
PERM_ROWS = 8
PERM_TOKENS = PERM_ROWS * GRID_W
PROJ_SUB = 256
SUPER = 2 * CHUNK
STEP = 8 * SUPER

VMEM_LIMIT = 56 * 1024 * 1024

F32 = jnp.float32
BF16 = jnp.bfloat16

NT = (((1,), (1,)), ((), ()))
TN = (((0,), (0,)), ((), ()))


def _dot(a, b):
    return jnp.dot(a, b, preferred_element_type=F32)


def _rms(x, g):
    ms = jnp.mean(x * x, axis=-1, keepdims=True)
    return x * lax.rsqrt(ms + EPS) * g


def _sigmoid(x):
    return jax.nn.sigmoid(x)


def _split_bf16(x):
    hi = x.astype(BF16)
    lo = (x - hi.astype(F32)).astype(BF16)
    return hi, lo


def _to_colmajor_perm():
    out = np.arange(PERM_TOKENS)
    src = (out % PERM_ROWS) * GRID_W + out // PERM_ROWS
    p = np.zeros((PERM_TOKENS, PERM_TOKENS), np.float32)
    p[out, src] = 1.0
    return p


def _prep_kernel(c_ref, cc_ref, wmod_ref, bmod_ref, wm_t_ref, wd_t_ref, wg_t_ref,
                 m_ref, wm_ref, wdin_ref, wg_ref, *, n_k, n_gate):
    j = pl.program_id(0)

    @pl.when(j == 0)
    def _():
        m_ref[...] = jnp.broadcast_to(bmod_ref[...], m_ref.shape)
        wdin_ref[...] = wd_t_ref[...].astype(BF16).T

    @pl.when(j < n_k)
    def _():
        pad = jnp.zeros((m_ref.shape[0] - c_ref.shape[0] - 1, c_ref.shape[1]), F32)
        c = jnp.concatenate([c_ref[...], cc_ref[...], pad], axis=0)
        s_hi, s_lo = _split_bf16(c * _sigmoid(c))
        w_hi, w_lo = _split_bf16(wmod_ref[...])
        m_ref[...] += _dot(s_hi, w_hi) + _dot(s_lo, w_hi) + _dot(s_hi, w_lo)

    wm_ref[...] = wm_t_ref[...].astype(BF16).T

    @pl.when(j < n_gate)
    def _():
        wg_ref[...] = wg_t_ref[...].astype(BF16).T


def _prep(c, c_ctx, w_mod, b_mod, w_in_t):
    bsz, d = c.shape
    n_rows = -(-(bsz + 1) // 8) * 8
    n_out = w_mod.shape[-1]
    n = w_in_t.shape[0]
    tk = 256
    tr = 512
    n_k = d // tk
    n_main = COL_DEC // tr
    n_gate = (n - COL_GATE) // tr
    assert COL_DEC % tr == 0 and (n - COL_GATE) % tr == 0 and max(n_gate, n_k) <= n_main
    k_blk = lambda j: jnp.minimum(j, n_k - 1)
    gate_blk = lambda j: jnp.minimum(j, n_gate - 1)
    return pl.pallas_call(
        functools.partial(_prep_kernel, n_k=n_k, n_gate=n_gate),
        grid=(n_main,),
        in_specs=[
            pl.BlockSpec((bsz, tk), lambda j: (0, k_blk(j))),
            pl.BlockSpec((1, tk), lambda j: (0, k_blk(j))),
            pl.BlockSpec((None, tk, n_out), lambda j: (0, k_blk(j), 0)),
            pl.BlockSpec((1, n_out), lambda j: (0, 0)),
            pl.BlockSpec((tr, d), lambda j: (j, 0)),
            pl.BlockSpec((LANES, d), lambda j: (COL_DEC // LANES, 0)),
            pl.BlockSpec((pl.Element(tr), pl.Element(d)),
                         lambda j: (pl.multiple_of(COL_GATE + tr * gate_blk(j), 32), 0)),
        ],
        out_specs=[
            pl.BlockSpec((n_rows, n_out), lambda j: (0, 0)),
            pl.BlockSpec((d, tr), lambda j: (0, j)),
            pl.BlockSpec((d, LANES), lambda j: (0, 0)),
            pl.BlockSpec((d, tr), lambda j: (0, gate_blk(j))),
        ],
        out_shape=[jax.ShapeDtypeStruct((n_rows, n_out), F32)]
        + [jax.ShapeDtypeStruct((d, w), BF16) for w in (COL_DEC, LANES, n - COL_GATE)],
        compiler_params=pltpu.CompilerParams(
            dimension_semantics=("arbitrary",), vmem_limit_bytes=VMEM_LIMIT),
        name="prep",
    )(c, c_ctx, w_mod, b_mod, w_in_t, w_in_t, w_in_t)


def _mod_vec(m_ref, row, j):
    return m_ref[pl.ds(row, 1), D_MODEL * j:D_MODEL * (j + 1)]


def _prenorm_mod(x, g, m_ref, row):
    sh = _mod_vec(m_ref, row, 0)
    sc = _mod_vec(m_ref, row, 1)
    return (_rms(x, g) * (1.0 + sc) + sh).astype(BF16)


def _decay_logs(z, wdec_ref, bdec_ref):
    z_hi, z_lo = z
    logits = _dot(z_hi, wdec_ref[...]) + _dot(z_lo, wdec_ref[...]) + bdec_ref[...]
    ls = jnp.minimum(logits, 0.0) - jnp.log(1.0 + jnp.exp(-jnp.abs(logits)))
    return ls * (1.0 / GATE_NORM)


N_LATE_W = 5


def _proj_kernel(x_ref, mod_ref, g_ref, perm_ref, wm_ref, wdin_ref, wdec_ref, bdec_ref, wg_ref,
                 *refs):
    late_f32 = refs[:N_LATE_W]
    (ah_ref, av_ref, q_ref, k_ref, v_ref, rs_ref, laf_ref, lab_ref,
     gt_ref) = refs[N_LATE_W:-N_LATE_W]
    late_bf16 = refs[-N_LATE_W:]
    for src, dst in zip(late_f32, late_bf16):
        dst[...] = src[...].astype(BF16)
    subs = [slice(s, s + PROJ_SUB) for s in range(0, x_ref.shape[0], PROJ_SUB)]
    g = g_ref[...]
    ub = [_prenorm_mod(x_ref[s, :], g, mod_ref, pl.program_id(0)) for s in subs]
    z = [_split_bf16(_dot(u, wdin_ref[...])) for u in ub]
    a_h = []
    for n, s in enumerate(subs):
        glu = _dot(ub[n], wm_ref[:, 0:COL_Q])
        a = (glu[:, :CONV_DIM] * _sigmoid(glu[:, CONV_DIM:])).astype(BF16)
        av_ref[s, :] = a[:, CONV_HALF:]
        a_h.append(a[:, :CONV_HALF])
    for n, s in enumerate(subs):
        q_ref[s, :] = (_dot(ub[n], wm_ref[:, COL_Q:COL_K]) * (GLA_DK ** -0.5)).astype(BF16)
    for n, s in enumerate(subs):
        k_ref[s, :] = _dot(ub[n], wm_ref[:, COL_K:COL_V]).astype(BF16)
    for n, s in enumerate(subs):
        v_ref[s, :] = _dot(ub[n], wm_ref[:, COL_V:COL_R]).astype(BF16)
    for n, s in enumerate(subs):
        la = _decay_logs(z[n], wdec_ref, bdec_ref)
        laf_ref[s, :] = la[:, :HK]
        lab_ref[s, :] = la[:, HK:]
    for n, s in enumerate(subs):
        r = _dot(ub[n], wm_ref[:, COL_R:COL_DEC])
        rs_ref[s, :] = (r * _sigmoid(r)).astype(BF16)
    per_group = PERM_TOKENS // PROJ_SUB
    for n in range(x_ref.shape[0] // PERM_TOKENS):
        a_rm = jnp.concatenate(a_h[per_group * n:per_group * (n + 1)], axis=0)
        a_cm = _dot(perm_ref[...], a_rm)
        ah_ref[:, PERM_ROWS * n:PERM_ROWS * (n + 1), :] = a_cm.reshape(
            GRID_W, PERM_ROWS, CONV_HALF)
    for n, s in enumerate(subs):
        gt_ref[s, :] = _sigmoid(_dot(ub[n], wg_ref[...])).astype(BF16)


def _const_spec(arr):
    nd = arr.ndim
    return pl.BlockSpec(arr.shape, lambda *_: (0,) * nd)


def _tok_spec(tm, width):
    return pl.BlockSpec((None, tm, width), lambda b, i: (b, i, 0))


def _colmajor_spec(tm):
    return pl.BlockSpec((None, GRID_W, tm // GRID_W, CONV_HALF), lambda b, i: (b, 0, i, 0))


def _single_spec(arr):
    nd = arr.ndim
    return pl.BlockSpec(arr.shape, lambda *_: (0,) * nd, pipeline_mode=pl.Buffered(1))


def _proj(x, mod, g, perm, wm, wdin, wdec, bdec, wg, late_weights, tm):
    bsz, t, d = x.shape
    widths = (CONV_HALF, HK, HK, HV, HV, HK, HK, 2 * D_MODEL)
    dtypes = (BF16, BF16, BF16, BF16, BF16, F32, F32, BF16)
    assert tm % PERM_TOKENS == 0 and len(late_weights) == N_LATE_W
    ah_shape = jax.ShapeDtypeStruct((bsz, GRID_W, t // GRID_W, CONV_HALF), F32)
    n_i = t // tm
    n_steps = bsz * n_i
    late_in, late_out, late_shape = [], [], []
    for w in late_weights:
        _, rows, cols = w.shape
        assert rows % (16 * n_steps) == 0
        blk = rows // n_steps
        late_in.append(pl.BlockSpec((None, blk, cols), lambda b, i: (0, b * n_i + i, 0)))
        late_out.append(pl.BlockSpec((blk, cols), lambda b, i: (b * n_i + i, 0)))
        late_shape.append(jax.ShapeDtypeStruct((rows, cols), BF16))
    return pl.pallas_call(
        _proj_kernel,
        grid=(bsz, n_i),
        in_specs=[
            _tok_spec(tm, d),
            _const_spec(mod),
            _const_spec(g), _single_spec(perm), _single_spec(wm), _single_spec(wdin),
            _single_spec(wdec), _const_spec(bdec), _single_spec(wg),
        ] + late_in,
        out_specs=[_colmajor_spec(tm)] + [_tok_spec(tm, w) for w in widths] + late_out,
        out_shape=[ah_shape] + [jax.ShapeDtypeStruct((bsz, t, w), dt)
                                for w, dt in zip(widths, dtypes)] + late_shape,
        compiler_params=pltpu.CompilerParams(
            dimension_semantics=("parallel", "parallel"), vmem_limit_bytes=VMEM_LIMIT),
        name="proj",
    )(x, mod, g, perm, wm, wdin, wdec, bdec, wg, *late_weights)


def _gla_scale(qs, ks, las, dirn, want_out):
    tri_b, mid, last = dirn["tri_b"], dirn["mid"], dirn["last"]
    la_hi, la_lo = _split_bf16(las)
    g = _dot(tri_b, la_hi) + _dot(tri_b, la_lo)
    u = dict(kt=[], kend=[], qt=[], qg=[], e=[])
    for c in range(SUPER // CHUNK):
        rows = slice(c * CHUNK, (c + 1) * CHUNK)
        gc = g[rows]
        g_mid = gc[mid:mid + 1]
        g_last = gc[last:last + 1]
        u["e"].append(jnp.exp(g_last))
        kt = ks[rows].astype(F32) * jnp.exp(g_mid - gc)
        u["kt"].append(kt.astype(BF16))
        u["kend"].append((kt * jnp.exp(g_last - g_mid)).astype(BF16))
        if want_out:
            qt = qs[rows].astype(F32) * jnp.exp(gc - g_mid)
            u["qt"].append(qt.astype(BF16))
            u["qg"].append((qt * jnp.exp(g_mid)).astype(BF16))
    return u


def _pair_lanes(pair):
    return slice(LANES * pair, LANES * (pair + 1))


def _gla_scores(u, pair, head_lane):
    lanes = _pair_lanes(pair)
    kt = jnp.concatenate([c[:, lanes] for c in u["kt"]], axis=0)
    qt = jnp.concatenate([c[:, lanes] for c in u["qt"]], axis=0)
    kt2 = jnp.concatenate([kt * hm.astype(BF16) for hm in head_lane], axis=0)
    return lax.dot_general(qt, kt2, NT, preferred_element_type=F32)


def _gla_update(u, vs, h, head_lane):
    lanes = _pair_lanes(h // 2)
    hm = head_lane[h % 2].astype(BF16)
    k0 = u["kend"][0][:, lanes] * hm
    k1 = u["kend"][1][:, lanes] * hm
    z = jnp.zeros_like(k0)
    kbd = jnp.concatenate([jnp.concatenate([k0, z], axis=1),
                           jnp.concatenate([z, k1], axis=1)], axis=0)
    return lax.dot_general(vs[:, GLA_DV * h:GLA_DV * (h + 1)], kbd, TN,
                           preferred_element_type=F32)


def _gla_intra(scores, vs, pair, mask2):
    a = jnp.where(mask2, scores, 0.0).astype(BF16)
    v0 = vs[:, GLA_DV * 2 * pair:GLA_DV * (2 * pair + 1)]
    v1 = vs[:, GLA_DV * (2 * pair + 1):GLA_DV * (2 * pair + 2)]
    z = jnp.zeros_like(v0)
    vbd = jnp.concatenate([jnp.concatenate([v0, z], axis=1),
                           jnp.concatenate([z, v1], axis=1)], axis=0)
    return _dot(a, vbd)


def _gla_kernel(q_ref, k_ref, v_ref, laf_ref, lab_ref, rs_ref, g_ref,
                ctx_ref, modc_ref, gpre_ref, wm_ref, wdin_ref, wdec_ref, bdec_ref,
                o_ref, of_ref, ob_ref, sf_ref, sb_ref, kc_ref, vc_ref, lafc_ref, labc_ref, *,
                ctx_row):
    t = q_ref.shape[0]
    t_ctx = kc_ref.shape[0]

    ubc = _prenorm_mod(ctx_ref[...], gpre_ref[...], modc_ref, ctx_row)
    zc = _split_bf16(_dot(ubc, wdin_ref[...]))
    kc_ref[...] = _dot(ubc, wm_ref[:, COL_K:COL_V]).astype(BF16)
    vc_ref[...] = _dot(ubc, wm_ref[:, COL_V:COL_R]).astype(BF16)
    lac = _decay_logs(zc, wdec_ref, bdec_ref)
    lafc_ref[...] = lac[:, :HK]
    labc_ref[...] = lac[:, HK:]

    half = t // 2
    n_steps = t // STEP
    supers = STEP // SUPER
    n_chunks = SUPER // CHUNK
    pairs = GLA_HEADS // 2

    row = lax.broadcasted_iota(jnp.int32, (SUPER, SUPER), 0)
    col = lax.broadcasted_iota(jnp.int32, (SUPER, SUPER), 1)
    same_chunk = (row // CHUNK) == (col // CHUNK)
    low_mask = jnp.logical_and(same_chunk, col <= row)
    up_mask = jnp.logical_and(same_chunk, col >= row)
    lane = lax.broadcasted_iota(jnp.int32, (1, LANES), 1)
    head_lane = (jnp.where(lane < GLA_DK, 1.0, 0.0), jnp.where(lane >= GLA_DK, 1.0, 0.0))

    fwd = dict(tri_b=jnp.where(low_mask, 1.0, 0.0).astype(BF16),
               mask2=jnp.concatenate([low_mask, low_mask], axis=1),
               mid=CHUNK // 2 - 1, last=CHUNK - 1, order=(0, 1), st_ref=sf_ref)
    bwd = dict(tri_b=jnp.where(up_mask, 1.0, 0.0).astype(BF16),
               mask2=jnp.concatenate([up_mask, up_mask], axis=1),
               mid=CHUNK // 2, last=0, order=(1, 0), st_ref=sb_ref)

    sf_ref[...] = jnp.zeros_like(sf_ref)
    sb_ref[...] = jnp.zeros_like(sb_ref)
    gn = g_ref[...]

    def run(units, want_out, finalize):
        vals = []
        for dirn, refs, t0 in units:
            q_r, k_r, v_r, la_r = refs
            rows = pl.ds(t0, SUPER)
            qs = q_r[rows, :] if want_out else None
            vals.append((_gla_scale(qs, k_r[rows, :], la_r[rows, :], dirn, want_out),
                         v_r[rows, :]))
        scores = []
        if want_out:
            scores = [[_gla_scores(u, p, head_lane) for p in range(pairs)] for u, _ in vals]
        upd = [[_gla_update(u, vs, h, head_lane) for h in range(GLA_HEADS)] for u, vs in vals]
        intra = []
        if want_out:
            intra = [[_gla_intra(scores[n][p], vals[n][1], p, units[n][0]["mask2"])
                      for p in range(pairs)] for n in range(len(units))]
        states = {}
        for n, (dirn, _, t0) in enumerate(units):
            key = id(dirn["st_ref"])
            if key not in states:
                states[key] = (dirn["st_ref"], [dirn["st_ref"][h] for h in range(GLA_HEADS)])
            st = states[key][1]
            u = vals[n][0]
            inter = [[None] * n_chunks for _ in range(pairs)]
            for c in dirn["order"]:
                if want_out:
                    for p in range(pairs):
                        st2 = jnp.concatenate([st[2 * p].astype(BF16),
                                               st[2 * p + 1].astype(BF16)], axis=0)
                        inter[p][c] = lax.dot_general(u["qg"][c][:, _pair_lanes(p)], st2, NT,
                                                      preferred_element_type=F32)
                for h in range(GLA_HEADS):
                    e = u["e"][c][:, _pair_lanes(h // 2)]
                    st[h] = st[h] * e + upd[n][h][:, LANES * c:LANES * (c + 1)]
            if not want_out:
                continue
            fwd_unit = dirn is fwd
            for p in range(pairs):
                o2 = intra[n][p] + jnp.concatenate(inter[p], axis=0)
                for j in range(2):
                    cols = slice(GLA_DV * (2 * p + j), GLA_DV * (2 * p + j + 1))
                    o = o2[:, GLA_DV * j:GLA_DV * (j + 1)]
                    if finalize:
                        other = (ob_ref[pl.ds(t0 - half, SUPER), cols] if fwd_unit
                                 else of_ref[pl.ds(t0, SUPER), cols])
                        rs = rs_ref[pl.ds(t0, SUPER), cols].astype(F32)
                        o_ref[pl.ds(t0, SUPER), cols] = (_rms(o + other, gn) * rs).astype(BF16)
                    elif fwd_unit:
                        of_ref[pl.ds(t0, SUPER), cols] = o
                    else:
                        ob_ref[pl.ds(t0 - half, SUPER), cols] = o
        for st_ref, st in states.values():
            for h in range(GLA_HEADS):
                st_ref[h] = st[h]

    ctx_units = []
    for s in range(t_ctx // SUPER):
        ctx_units.append((fwd, (None, kc_ref, vc_ref, lafc_ref), s * SUPER))
        ctx_units.append((bwd, (None, kc_ref, vc_ref, labc_ref), t_ctx - (s + 1) * SUPER))
    run(ctx_units, want_out=False, finalize=False)

    def step(i, finalize):
        units = []
        for s in range(supers):
            t0 = pl.multiple_of(i * STEP + s * SUPER, SUPER)
            t1 = pl.multiple_of(t - (i + 1) * STEP + (supers - 1 - s) * SUPER, SUPER)
            units.append((fwd, (q_ref, k_ref, v_ref, laf_ref), t0))
            units.append((bwd, (q_ref, k_ref, v_ref, lab_ref), t1))
        run(units, want_out=True, finalize=finalize)

    def park_body(i, carry):
        step(i, False)
        return carry

    def finish_body(i, carry):
        step(i, True)
        return carry

    lax.fori_loop(0, n_steps // 2, park_body, 0)
    lax.fori_loop(n_steps // 2, n_steps, finish_body, 0)


def _gla(q, k, v, laf, lab, rs, gnorm, ctx, mod_ctx, gpre, wm, wdin, wdec, bdec):
    bsz, t, _ = q.shape
    t_ctx = ctx.shape[1]
    assert t % (2 * STEP) == 0 and t_ctx % SUPER == 0

    def bspec(arr):
        return pl.BlockSpec((None,) + arr.shape[1:], lambda b: (b, 0, 0))

    args = (q, k, v, laf, lab, rs)
    return pl.pallas_call(
        functools.partial(_gla_kernel, ctx_row=bsz % 8),
        grid=(bsz,),
        in_specs=[bspec(a) for a in args] + [_const_spec(gnorm), bspec(ctx),
                  pl.BlockSpec((8, 2 * D_MODEL), lambda b: (bsz // 8, 0)),
                  _const_spec(gpre), _single_spec(wm), _single_spec(wdin), _single_spec(wdec),
                  _const_spec(bdec)],
        out_specs=pl.BlockSpec((None, t, HV), lambda b: (b, 0, 0)),
        out_shape=jax.ShapeDtypeStruct((bsz, t, HV), BF16),
        scratch_shapes=[
            pltpu.VMEM((t // 2, HV), F32), pltpu.VMEM((t // 2, HV), F32),
            pltpu.VMEM((GLA_HEADS, GLA_DV, LANES), F32),
            pltpu.VMEM((GLA_HEADS, GLA_DV, LANES), F32),
            pltpu.VMEM((t_ctx, HK), BF16), pltpu.VMEM((t_ctx, HV), BF16),
            pltpu.VMEM((t_ctx, HK), F32), pltpu.VMEM((t_ctx, HK), F32),
        ],
        compiler_params=pltpu.CompilerParams(
            dimension_semantics=("parallel",), vmem_limit_bytes=VMEM_LIMIT),
        name="gla",
    )(*args, gnorm, ctx, mod_ctx, gpre, wm, wdin, wdec, bdec)


def _tap_segments(n_tiles, lines_per_tile, n_lines, group):
    segs = []
    for t0 in range(0, n_tiles, group):
        t1 = min(t0 + group, n_tiles)
        lo_line = int(t0 * lines_per_tile)
        hi_line = int(-(-t1 * lines_per_tile // 1)) - 1
        k_lo = max(0, CONV_PAD - hi_line)
        k_hi = min(CONV_K, n_lines + CONV_PAD - lo_line)
        if segs and segs[-1][2:] == (k_lo, k_hi):
            segs[-1] = (segs[-1][0], t1, k_lo, k_hi)
        else:
            segs.append((t0, t1, k_lo, k_hi))
    return segs


def _conv_kernel(ah_ref, av_ref, w_ref, b_ref, yh_ref, yv_ref, ph_ref, pv_ref, wb_ref):
    t = ah_ref.shape[0]
    rows = t // GRID_W
    tile = 64
    for kk in range(CONV_K):
        wb_ref[kk] = jnp.broadcast_to(w_ref[kk:kk + 1, :], wb_ref.shape[1:])

    def conv(src_ref, pad_ref, w_cols, line, n_lines, group, dst_ref):
        pad = CONV_PAD * line
        pad_ref[pl.ds(0, pad), :] = jnp.zeros((pad, CONV_HALF), F32)
        pad_ref[pl.ds(pad + t, pad), :] = jnp.zeros((pad, CONV_HALF), F32)

        def fill(i, carry):
            t0 = pl.multiple_of(i * tile, tile)
            pad_ref[pl.ds(pl.multiple_of(pad + t0, 8), tile), :] = (
                src_ref[pl.ds(t0, tile), :].astype(F32))
            return carry

        lax.fori_loop(0, t // tile, fill, 0)
        bias = b_ref[:, w_cols]

        for first, last, k_lo, k_hi in _tap_segments(t // tile, tile / line, n_lines, group):
            def body(i, carry, k_lo=k_lo, k_hi=k_hi):
                t0 = pl.multiple_of(i * tile, tile)
                acc = jnp.broadcast_to(bias, (tile, CONV_HALF))
                for kk in range(k_lo, k_hi):
                    win = pad_ref[pl.ds(pl.multiple_of(t0 + kk * line, 8), tile), :]
                    w8 = wb_ref[kk, :, w_cols]
                    acc = acc + jnp.concatenate([w8] * (tile // 8), axis=0) * win
                dst_ref[pl.ds(t0, tile), :] = acc
                return carry

            lax.fori_loop(first, last, body, 0)

    conv(ah_ref, ph_ref, slice(0, CONV_HALF), rows, GRID_W, 2, yh_ref)
    conv(av_ref, pv_ref, slice(CONV_HALF, CONV_DIM), GRID_W, rows, 4, yv_ref)


def _conv(ah, av, w, b):
    bsz, t, c = ah.shape
    rows = t // GRID_W
    tok = pl.BlockSpec((None, t, c), lambda i: (i, 0, 0))
    return pl.pallas_call(
        _conv_kernel,
        grid=(bsz,),
        in_specs=[tok, tok, pl.BlockSpec((None,) + w.shape[1:], lambda i: (0, 0, 0)),
                  _const_spec(b)],
        out_specs=[tok, tok],
        out_shape=[jax.ShapeDtypeStruct((bsz, t, c), F32)] * 2,
        scratch_shapes=[
            pltpu.VMEM((t + 2 * CONV_PAD * rows, c), F32),
            pltpu.VMEM((t + 2 * CONV_PAD * GRID_W, c), F32),
            pltpu.VMEM((CONV_K, 8, 2 * c), F32),
        ],
        compiler_params=pltpu.CompilerParams(
            dimension_semantics=("parallel",), vmem_limit_bytes=VMEM_LIMIT),
        name="conv",
    )(ah, av, w, b)


def _post_kernel(x_ref, yh_ref, yv_ref, og_ref, gt_ref, mod_ref, perm_ref, lng_ref, lnb_ref,
                 gpost1_ref, gpre2_ref, gpost2_ref, wco_ref, wgo_ref, wout_ref, wff1_ref,
                 wff2_ref, o_ref):
    d = x_ref.shape[-1]
    d_ff = wff1_ref.shape[1]
    ff_tile = 1024
    sub = 256
    subs = [slice(s, s + sub) for s in range(0, x_ref.shape[0], sub)]
    gt1, sh2, sc2, gt2 = (_mod_vec(mod_ref, pl.program_id(0), j) for j in (2, 3, 4, 5))

    yh = []
    for n in range(x_ref.shape[0] // PERM_TOKENS):
        blk = yh_ref[:, PERM_ROWS * n:PERM_ROWS * (n + 1), :].reshape(PERM_TOKENS, CONV_HALF)
        hi, lo = _split_bf16(blk)
        yh.append(_dot(perm_ref[...], hi) + _dot(perm_ref[...], lo))
    yh = jnp.concatenate(yh, axis=0)

    yn = []
    for s in subs:
        yc = jnp.concatenate([yh[s], yv_ref[s, :]], axis=-1)
        yc = yc - jnp.mean(yc, axis=-1, keepdims=True)
        var = jnp.mean(yc * yc, axis=-1, keepdims=True)
        z = yc * lax.rsqrt(var + EPS) * lng_ref[...] + lnb_ref[...]
        yn.append((z * _sigmoid(z)).astype(BF16))
    y_gla = [_dot(og_ref[s, :], wgo_ref[...]) for s in subs]
    y_conv = [_dot(v, wco_ref[...]) for v in yn]
    merged = []
    for n, s in enumerate(subs):
        gates = gt_ref[s, :].astype(F32)
        merged.append((gates[:, :d] * y_conv[n] + gates[:, d:] * y_gla[n]).astype(BF16))
    y = [_dot(v, wout_ref[...]) for v in merged]
    h_mid = [x_ref[s, :] + gt1 * _rms(y[n], gpost1_ref[...]) for n, s in enumerate(subs)]
    u2 = [(_rms(h, gpre2_ref[...]) * (1.0 + sc2) + sh2).astype(BF16) for h in h_mid]
    acc = [jnp.zeros(h.shape, F32) for h in h_mid]
    for j in range(d_ff // ff_tile):
        cols = slice(j * ff_tile, (j + 1) * ff_tile)
        f = [_dot(v, wff1_ref[:, cols]) for v in u2]
        f = [jnp.square(jnp.maximum(v, 0.0)).astype(BF16) for v in f]
        acc = [acc[n] + _dot(f[n], wff2_ref[cols, :]) for n in range(len(subs))]
    for n, s in enumerate(subs):
        o_ref[s, :] = h_mid[n] + gt2 * _rms(acc[n], gpost2_ref[...])


def _post(x, yh, yv, og, gates, mod, perm, ln_g, ln_b, gpost1, gpre2, gpost2, wco, wgo, wout,
          wff1, wff2, tm):
    bsz, t, d = x.shape
    assert tm % PERM_TOKENS == 0
    return pl.pallas_call(
        _post_kernel,
        grid=(bsz, t // tm),
        in_specs=[
            _tok_spec(tm, d), _colmajor_spec(tm), _tok_spec(tm, CONV_HALF), _tok_spec(tm, HV),
            _tok_spec(tm, 2 * d),
            _const_spec(mod),
            _single_spec(perm), _const_spec(ln_g), _const_spec(ln_b),
            _const_spec(gpost1), _const_spec(gpre2), _const_spec(gpost2),
            _single_spec(wco), _single_spec(wgo), _single_spec(wout), _single_spec(wff1),
            _single_spec(wff2),
        ],
        out_specs=_tok_spec(tm, d),
        out_shape=jax.ShapeDtypeStruct((bsz, t, d), F32),
        compiler_params=pltpu.CompilerParams(
            dimension_semantics=("parallel", "parallel"), vmem_limit_bytes=VMEM_LIMIT),
        name="post",
    )(x, yh, yv, og, gates, mod, perm, ln_g, ln_b, gpost1, gpre2, gpost2, wco, wgo, wout, wff1,
      wff2)


def kernel(x, c, ctx, c_ctx, w_mod, b_mod, g_pre1, g_post1, g_pre2, g_post2, w_in, conv_w,
           conv_b, conv_ln_g, conv_ln_b, w_conv_out, w_decay, b_decay, gla_norm_g, w_gla_out,
           w_out, w_ff1, w_ff2):
    bsz, t, d = x.shape
    depth = w_in.shape[0]
    assert depth == 1 and d == D_MODEL and t % (GRID_W * 8) == 0
    assert w_in.shape[1:] == (d, COL_END) and conv_w.shape[1:] == (CONV_K, CONV_DIM)
    assert w_mod.shape[1:] == (d, N_MOD * d)
    assert w_decay.shape[1:] == (2, DECAY_RANK, HK) and ctx.shape[0] == bsz and ctx.shape[2] == d

    m, wm, wdin, wg = _prep(c, c_ctx.reshape(1, d), w_mod, b_mod, jnp.swapaxes(w_in[0], 0, 1))

    wdec = jnp.zeros((LANES, 2 * HK), F32)
    wdec = wdec.at[:DECAY_RANK, :HK].set(w_decay[0, 0])
    wdec = wdec.at[DECAY_RANK:2 * DECAY_RANK, HK:].set(w_decay[0, 1]).astype(BF16)
    bdec = b_decay[0].reshape(1, 2 * HK)
    row = lambda v: v.reshape(1, -1)

    perm = _to_colmajor_perm()
    to_colmajor = jnp.asarray(perm, BF16)
    to_rowmajor = jnp.asarray(perm.T, BF16)

    ah, av, q, k, v, rs, laf, lab, gates, wco, wgo, wout, wff1, wff2 = _proj(
        x, m, row(g_pre1[0]), to_colmajor, wm, wdin, wdec, bdec, wg,
        (w_conv_out, w_gla_out, w_out, w_ff1, w_ff2), tm=1024)
    og = _gla(q, k, v, laf, lab, rs, row(gla_norm_g[0]),
              ctx, m, row(g_pre1[0]), wm, wdin, wdec, bdec)
    yh, yv = _conv(ah.reshape(bsz, t, CONV_HALF), av, conv_w, row(conv_b[0]))
    yh = yh.reshape(bsz, GRID_W, t // GRID_W, CONV_HALF)

    return _post(x, yh, yv, og, gates, m, to_rowmajor, row(conv_ln_g[0]), row(conv_ln_b[0]),
                 row(g_post1[0]), row(g_pre2[0]), row(g_post2[0]),
                 wco, wgo, wout, wff1, wff2, tm=512)
```

```python
import functools

import jax
import jax.numpy as jnp
import numpy as np
from jax import lax
from jax.experimental import pallas as pl
from jax.experimental.pallas import tpu as pltpu

D_MODEL = 1024
GRID_W = 64
CONV_DIM = 512
CONV_K = 31
GLA_HEADS = 4
GLA_DK = 64
GLA_DV = 128
DECAY_RANK = 16
GATE_NORM = 16.0
CHUNK = 64
N_MOD = 6
EPS = 1e-6

COL_Q = 2 * CONV_DIM
COL_K = COL_Q + GLA_HEADS * GLA_DK
COL_V = COL_K + GLA_HEADS * GLA_DK
COL_R = COL_V + GLA_HEADS * GLA_DV
COL_DEC = COL_R + GLA_HEADS * GLA_DV
COL_GATE = COL_DEC + 2 * DECAY_RANK
COL_END = COL_GATE + 2 * D_MODEL

LANES = 128
HK = GLA_HEADS * GLA_DK
HV = GLA_HEADS * GLA_DV
CONV_HALF = CONV_DIM // 2
CONV_PAD = CONV_K // 2
PERM_ROWS = 8
PERM_TOKENS = PERM_ROWS * GRID_W
PROJ_SUB = 256
SUPER = 2 * CHUNK
STEP = 8 * SUPER

VMEM_LIMIT = 56 * 1024 * 1024

F32 = jnp.float32
BF16 = jnp.bfloat16

NT = (((1,), (1,)), ((), ()))
TN = (((0,), (0,)), ((), ()))


def _dot(a, b):
    return jnp.dot(a, b, preferred_element_type=F32)


def _rms(x, g):
    ms = jnp.mean(x * x, axis=-1, keepdims=True)
    return x * lax.rsqrt(ms + EPS) * g


def _sigmoid(x):
    return jax.nn.sigmoid(x)


def _split_bf16(x):
    hi = x.astype(BF16)
    lo = (x - hi.astype(F32)).astype(BF16)
    return hi, lo


def _to_colmajor_perm():
    out = np.arange(PERM_TOKENS)
    src = (out % PERM_ROWS) * GRID_W + out // PERM_ROWS
    p = np.zeros((PERM_TOKENS, PERM_TOKENS), np.float32)
    p[out, src] = 1.0
    return p


def _prep_kernel(c_ref, cc_ref, wmod_ref, bmod_ref, wm_t_ref, wd_t_ref, wg_t_ref,
                 m_ref, wm_ref, wdin_ref, wg_ref, *, n_k, n_gate):
    j = pl.program_id(0)

    @pl.when(j == 0)
    def _():
        m_ref[...] = jnp.broadcast_to(bmod_ref[...], m_ref.shape)
        wdin_ref[...] = wd_t_ref[...].astype(BF16).T

    @pl.when(j < n_k)
    def _():
        pad = jnp.zeros((m_ref.shape[0] - c_ref.shape[0] - 1, c_ref.shape[1]), F32)
        c = jnp.concatenate([c_ref[...], cc_ref[...], pad], axis=0)
        s_hi, s_lo = _split_bf16(c * _sigmoid(c))
        w_hi, w_lo = _split_bf16(wmod_ref[...])
        m_ref[...] += _dot(s_hi, w_hi) + _dot(s_lo, w_hi) + _dot(s_hi, w_lo)

    wm_ref[...] = wm_t_ref[...].astype(BF16).T

    @pl.when(j < n_gate)
    def _():
        wg_ref[...] = wg_t_ref[...].astype(BF16).T


def _prep(c, c_ctx, w_mod, b_mod, w_in_t):
    bsz, d = c.shape
    n_rows = -(-(bsz + 1) // 8) * 8
    n_out = w_mod.shape[-1]
    n = w_in_t.shape[0]
    tk = 256
    tr = 512
    n_k = d // tk
    n_main = COL_DEC // tr
    n_gate = (n - COL_GATE) // tr
    assert COL_DEC % tr == 0 and (n - COL_GATE) % tr == 0 and max(n_gate, n_k) <= n_main
    k_blk = lambda j: jnp.minimum(j, n_k - 1)
    gate_blk = lambda j: jnp.minimum(j, n_gate - 1)
    return pl.pallas_call(
        functools.partial(_prep_kernel, n_k=n_k, n_gate=n_gate),
        grid=(n_main,),
        in_specs=[
            pl.BlockSpec((bsz, tk), lambda j: (0, k_blk(j))),
            pl.BlockSpec((1, tk), lambda j: (0, k_blk(j))),
            pl.BlockSpec((None, tk, n_out), lambda j: (0, k_blk(j), 0)),
            pl.BlockSpec((1, n_out), lambda j: (0, 0)),
            pl.BlockSpec((tr, d), lambda j: (j, 0)),
            pl.BlockSpec((LANES, d), lambda j: (COL_DEC // LANES, 0)),
            pl.BlockSpec((pl.Element(tr), pl.Element(d)),
                         lambda j: (pl.multiple_of(COL_GATE + tr * gate_blk(j), 32), 0)),
        ],
        out_specs=[
            pl.BlockSpec((n_rows, n_out), lambda j: (0, 0)),
            pl.BlockSpec((d, tr), lambda j: (0, j)),
            pl.BlockSpec((d, LANES), lambda j: (0, 0)),
            pl.BlockSpec((d, tr), lambda j: (0, gate_blk(j))),
        ],
        out_shape=[jax.ShapeDtypeStruct((n_rows, n_out), F32)]
        + [jax.ShapeDtypeStruct((d, w), BF16) for w in (COL_DEC, LANES, n - COL_GATE)],
        compiler_params=pltpu.CompilerParams(
            dimension_semantics=("arbitrary",), vmem_limit_bytes=VMEM_LIMIT),
        name="prep",
    )(c, c_ctx, w_mod, b_mod, w_in_t, w_in_t, w_in_t)


def _mod_vec(m_ref, row, j):
    return m_ref[pl.ds(row, 1), D_MODEL * j:D_MODEL * (j + 1)]


def _prenorm_mod(x, g, m_ref, row):
    sh = _mod_vec(m_ref, row, 0)
    sc = _mod_vec(m_ref, row, 1)
    return (_rms(x, g) * (1.0 + sc) + sh).astype(BF16)


def _decay_logs(z, wdec_ref, bdec_ref):
    z_hi, z_lo = z
    logits = _dot(z_hi, wdec_ref[...]) + _dot(z_lo, wdec_ref[...]) + bdec_ref[...]
    ls = jnp.minimum(logits, 0.0) - jnp.log(1.0 + jnp.exp(-jnp.abs(logits)))
    return ls * (1.0 / GATE_NORM)


N_LATE_W = 5


def _proj_kernel(x_ref, mod_ref, g_ref, perm_ref, wm_ref, wdin_ref, wdec_ref, bdec_ref, wg_ref,
                 *refs):
    late_f32 = refs[:N_LATE_W]
    (ah_ref, av_ref, q_ref, k_ref, v_ref, rs_ref, laf_ref, lab_ref,
     gt_ref) = refs[N_LATE_W:-N_LATE_W]
    late_bf16 = refs[-N_LATE_W:]
    for src, dst in zip(late_f32, late_bf16):
        dst[...] = src[...].astype(BF16)
    subs = [slice(s, s + PROJ_SUB) for s in range(0, x_ref.shape[0], PROJ_SUB)]
    g = g_ref[...]
    ub = [_prenorm_mod(x_ref[s, :], g, mod_ref, pl.program_id(0)) for s in subs]
    z = [_split_bf16(_dot(u, wdin_ref[...])) for u in ub]
    a_h = []
    for n, s in enumerate(subs):
        glu = _dot(ub[n], wm_ref[:, 0:COL_Q])
        a = (glu[:, :CONV_DIM] * _sigmoid(glu[:, CONV_DIM:])).astype(BF16)
        av_ref[s, :] = a[:, CONV_HALF:]
        a_h.append(a[:, :CONV_HALF])
    for n, s in enumerate(subs):
        q_ref[s, :] = (_dot(ub[n], wm_ref[:, COL_Q:COL_K]) * (GLA_DK ** -0.5)).astype(BF16)
    for n, s in enumerate(subs):
        k_ref[s, :] = _dot(ub[n], wm_ref[:, COL_K:COL_V]).astype(BF16)
    for n, s in enumerate(subs):
        v_ref[s, :] = _dot(ub[n], wm_ref[:, COL_V:COL_R]).astype(BF16)
    for n, s in enumerate(subs):
        la = _decay_logs(z[n], wdec_ref, bdec_ref)
        laf_ref[s, :] = la[:, :HK]
        lab_ref[s, :] = la[:, HK:]
    for n, s in enumerate(subs):
        r = _dot(ub[n], wm_ref[:, COL_R:COL_DEC])
        rs_ref[s, :] = (r * _sigmoid(r)).astype(BF16)
    per_group = PERM_TOKENS // PROJ_SUB
    for n in range(x_ref.shape[0] // PERM_TOKENS):
        a_rm = jnp.concatenate(a_h[per_group * n:per_group * (n + 1)], axis=0)
        a_cm = _dot(perm_ref[...], a_rm)
        ah_ref[:, PERM_ROWS * n:PERM_ROWS * (n + 1), :] = a_cm.reshape(
            GRID_W, PERM_ROWS, CONV_HALF)
    for n, s in enumerate(subs):
        gt_ref[s, :] = _sigmoid(_dot(ub[n], wg_ref[...])).astype(BF16)


def _const_spec(arr):
    nd = arr.ndim
    return pl.BlockSpec(arr.shape, lambda *_: (0,) * nd)


def _tok_spec(tm, width):
    return pl.BlockSpec((None, tm, width), lambda b, i: (b, i, 0))


def _colmajor_spec(tm):
    return pl.BlockSpec((None, GRID_W, tm // GRID_W, CONV_HALF), lambda b, i: (b, 0, i, 0))


def _single_spec(arr):
    nd = arr.ndim
    return pl.BlockSpec(arr.shape, lambda *_: (0,) * nd, pipeline_mode=pl.Buffered(1))


def _proj(x, mod, g, perm, wm, wdin, wdec, bdec, wg, late_weights, tm):
    bsz, t, d = x.shape
    widths = (CONV_HALF, HK, HK, HV, HV, HK, HK, 2 * D_MODEL)
    dtypes = (BF16, BF16, BF16, BF16, BF16, F32, F32, BF16)
    assert tm % PERM_TOKENS == 0 and len(late_weights) == N_LATE_W
    ah_shape = jax.ShapeDtypeStruct((bsz, GRID_W, t // GRID_W, CONV_HALF), F32)
    n_i = t // tm
    n_steps = bsz * n_i
    late_in, late_out, late_shape = [], [], []
    for w in late_weights:
        _, rows, cols = w.shape
        assert rows % (16 * n_steps) == 0
        blk = rows // n_steps
        late_in.append(pl.BlockSpec((None, blk, cols), lambda b, i: (0, b * n_i + i, 0)))
        late_out.append(pl.BlockSpec((blk, cols), lambda b, i: (b * n_i + i, 0)))
        late_shape.append(jax.ShapeDtypeStruct((rows, cols), BF16))
    return pl.pallas_call(
        _proj_kernel,
        grid=(bsz, n_i),
        in_specs=[
            _tok_spec(tm, d),
            _const_spec(mod),
            _const_spec(g), _single_spec(perm), _single_spec(wm), _single_spec(wdin),
            _single_spec(wdec), _const_spec(bdec), _single_spec(wg),
        ] + late_in,
        out_specs=[_colmajor_spec(tm)] + [_tok_spec(tm, w) for w in widths] + late_out,
        out_shape=[ah_shape] + [jax.ShapeDtypeStruct((bsz, t, w), dt)
                                for w, dt in zip(widths, dtypes)] + late_shape,
        compiler_params=pltpu.CompilerParams(
            dimension_semantics=("parallel", "parallel"), vmem_limit_bytes=VMEM_LIMIT),
        name="proj",
    )(x, mod, g, perm, wm, wdin, wdec, bdec, wg, *late_weights)


def _gla_scale(qs, ks, las, dirn, want_out):
    tri_b, mid, last = dirn["tri_b"], dirn["mid"], dirn["last"]
    la_hi, la_lo = _split_bf16(las)
    g = _dot(tri_b, la_hi) + _dot(tri_b, la_lo)
    u = dict(kt=[], kend=[], qt=[], qg=[], e=[])
    for c in range(SUPER // CHUNK):
        rows = slice(c * CHUNK, (c + 1) * CHUNK)
        gc = g[rows]
        g_mid = gc[mid:mid + 1]
        g_last = gc[last:last + 1]
        u["e"].append(jnp.exp(g_last))
        kt = ks[rows].astype(F32) * jnp.exp(g_mid - gc)
        u["kt"].append(kt.astype(BF16))
        u["kend"].append((kt * jnp.exp(g_last - g_mid)).astype(BF16))
        if want_out:
            qt = qs[rows].astype(F32) * jnp.exp(gc - g_mid)
            u["qt"].append(qt.astype(BF16))
            u["qg"].append((qt * jnp.exp(g_mid)).astype(BF16))
    return u


def _pair_lanes(pair):
    return slice(LANES * pair, LANES * (pair + 1))


def _gla_scores(u, pair, head_lane):
    lanes = _pair_lanes(pair)
    kt = jnp.concatenate([c[:, lanes] for c in u["kt"]], axis=0)
    qt = jnp.concatenate([c[:, lanes] for c in u["qt"]], axis=0)
    kt2 = jnp.concatenate([kt * hm.astype(BF16) for hm in head_lane], axis=0)
    return lax.dot_general(qt, kt2, NT, preferred_element_type=F32)


def _gla_update(u, vs, h, head_lane):
    lanes = _pair_lanes(h // 2)
    hm = head_lane[h % 2].astype(BF16)
    k0 = u["kend"][0][:, lanes] * hm
    k1 = u["kend"][1][:, lanes] * hm
    z = jnp.zeros_like(k0)
    kbd = jnp.concatenate([jnp.concatenate([k0, z], axis=1),
                           jnp.concatenate([z, k1], axis=1)], axis=0)
    return lax.dot_general(vs[:, GLA_DV * h:GLA_DV * (h + 1)], kbd, TN,
                           preferred_element_type=F32)


def _gla_intra(scores, vs, pair, mask2):
    a = jnp.where(mask2, scores, 0.0).astype(BF16)
    v0 = vs[:, GLA_DV * 2 * pair:GLA_DV * (2 * pair + 1)]
    v1 = vs[:, GLA_DV * (2 * pair + 1):GLA_DV * (2 * pair + 2)]
    z = jnp.zeros_like(v0)
    vbd = jnp.concatenate([jnp.concatenate([v0, z], axis=1),
                           jnp.concatenate([z, v1], axis=1)], axis=0)
    return _dot(a, vbd)


def _gla_kernel(q_ref, k_ref, v_ref, laf_ref, lab_ref, rs_ref, g_ref,
                ctx_ref, modc_ref, gpre_ref, wm_ref, wdin_ref, wdec_ref, bdec_ref,
                o_ref, of_ref, ob_ref, sf_ref, sb_ref, kc_ref, vc_ref, lafc_ref, labc_ref, *,
                ctx_row):
    t = q_ref.shape[0]
    t_ctx = kc_ref.shape[0]

    ubc = _prenorm_mod(ctx_ref[...], gpre_ref[...], modc_ref, ctx_row)
    zc = _split_bf16(_dot(ubc, wdin_ref[...]))
    kc_ref[...] = _dot(ubc, wm_ref[:, COL_K:COL_V]).astype(BF16)
    vc_ref[...] = _dot(ubc, wm_ref[:, COL_V:COL_R]).astype(BF16)
    lac = _decay_logs(zc, wdec_ref, bdec_ref)
    lafc_ref[...] = lac[:, :HK]
    labc_ref[...] = lac[:, HK:]

    half = t // 2
    n_steps = t // STEP
    supers = STEP // SUPER
    n_chunks = SUPER // CHUNK
    pairs = GLA_HEADS // 2

    row = lax.broadcasted_iota(jnp.int32, (SUPER, SUPER), 0)
    col = lax.broadcasted_iota(jnp.int32, (SUPER, SUPER), 1)
    same_chunk = (row // CHUNK) == (col // CHUNK)
    low_mask = jnp.logical_and(same_chunk, col <= row)
    up_mask = jnp.logical_and(same_chunk, col >= row)
    lane = lax.broadcasted_iota(jnp.int32, (1, LANES), 1)
    head_lane = (jnp.where(lane < GLA_DK, 1.0, 0.0), jnp.where(lane >= GLA_DK, 1.0, 0.0))

    fwd = dict(tri_b=jnp.where(low_mask, 1.0, 0.0).astype(BF16),
               mask2=jnp.concatenate([low_mask, low_mask], axis=1),
               mid=CHUNK // 2 - 1, last=CHUNK - 1, order=(0, 1), st_ref=sf_ref)
    bwd = dict(tri_b=jnp.where(up_mask, 1.0, 0.0).astype(BF16),
               mask2=jnp.concatenate([up_mask, up_mask], axis=1),
               mid=CHUNK // 2, last=0, order=(1, 0), st_ref=sb_ref)

    sf_ref[...] = jnp.zeros_like(sf_ref)
    sb_ref[...] = jnp.zeros_like(sb_ref)
    gn = g_ref[...]

    def run(units, want_out, finalize):
        vals = []
        for dirn, refs, t0 in units:
            q_r, k_r, v_r, la_r = refs
            rows = pl.ds(t0, SUPER)
            qs = q_r[rows, :] if want_out else None
            vals.append((_gla_scale(qs, k_r[rows, :], la_r[rows, :], dirn, want_out),
                         v_r[rows, :]))
        scores = []
        if want_out:
            scores = [[_gla_scores(u, p, head_lane) for p in range(pairs)] for u, _ in vals]
        upd = [[_gla_update(u, vs, h, head_lane) for h in range(GLA_HEADS)] for u, vs in vals]
        intra = []
        if want_out:
            intra = [[_gla_intra(scores[n][p], vals[n][1], p, units[n][0]["mask2"])
                      for p in range(pairs)] for n in range(len(units))]
        states = {}
        for n, (dirn, _, t0) in enumerate(units):
            key = id(dirn["st_ref"])
            if key not in states:
                states[key] = (dirn["st_ref"], [dirn["st_ref"][h] for h in range(GLA_HEADS)])
            st = states[key][1]
            u = vals[n][0]
            inter = [[None] * n_chunks for _ in range(pairs)]
            for c in dirn["order"]:
                if want_out:
                    for p in range(pairs):
                        st2 = jnp.concatenate([st[2 * p].astype(BF16),
                                               st[2 * p + 1].astype(BF16)], axis=0)
                        inter[p][c] = lax.dot_general(u["qg"][c][:, _pair_lanes(p)], st2, NT,
                                                      preferred_element_type=F32)
                for h in range(GLA_HEADS):
                    e = u["e"][c][:, _pair_lanes(h // 2)]
                    st[h] = st[h] * e + upd[n][h][:, LANES * c:LANES * (c + 1)]
            if not want_out:
                continue
            fwd_unit = dirn is fwd
            for p in range(pairs):
                o2 = intra[n][p] + jnp.concatenate(inter[p], axis=0)
                for j in range(2):
                    cols = slice(GLA_DV * (2 * p + j), GLA_DV * (2 * p + j + 1))
                    o = o2[:, GLA_DV * j:GLA_DV * (j + 1)]
                    if finalize:
                        other = (ob_ref[pl.ds(t0 - half, SUPER), cols] if fwd_unit
                                 else of_ref[pl.ds(t0, SUPER), cols])
                        rs = rs_ref[pl.ds(t0, SUPER), cols].astype(F32)
                        o_ref[pl.ds(t0, SUPER), cols] = (_rms(o + other, gn) * rs).astype(BF16)
                    elif fwd_unit:
                        of_ref[pl.ds(t0, SUPER), cols] = o
                    else:
                        ob_ref[pl.ds(t0 - half, SUPER), cols] = o
        for st_ref, st in states.values():
            for h in range(GLA_HEADS):
                st_ref[h] = st[h]

    ctx_units = []
    for s in range(t_ctx // SUPER):
        ctx_units.append((fwd, (None, kc_ref, vc_ref, lafc_ref), s * SUPER))
        ctx_units.append((bwd, (None, kc_ref, vc_ref, labc_ref), t_ctx - (s + 1) * SUPER))
    run(ctx_units, want_out=False, finalize=False)

    def step(i, finalize):
        units = []
        for s in range(supers):
            t0 = pl.multiple_of(i * STEP + s * SUPER, SUPER)
            t1 = pl.multiple_of(t - (i + 1) * STEP + (supers - 1 - s) * SUPER, SUPER)
            units.append((fwd, (q_ref, k_ref, v_ref, laf_ref), t0))
            units.append((bwd, (q_ref, k_ref, v_ref, lab_ref), t1))
        run(units, want_out=True, finalize=finalize)

    def park_body(i, carry):
        step(i, False)
        return carry

    def finish_body(i, carry):
        step(i, True)
        return carry

    lax.fori_loop(0, n_steps // 2, park_body, 0)
    lax.fori_loop(n_steps // 2, n_steps, finish_body, 0)


def _gla(q, k, v, laf, lab, rs, gnorm, ctx, mod_ctx, gpre, wm, wdin, wdec, bdec):
    bsz, t, _ = q.shape
    t_ctx = ctx.shape[1]
    assert t % (2 * STEP) == 0 and t_ctx % SUPER == 0

    def bspec(arr):
        return pl.BlockSpec((None,) + arr.shape[1:], lambda b: (b, 0, 0))

    args = (q, k, v, laf, lab, rs)
    return pl.pallas_call(
        functools.partial(_gla_kernel, ctx_row=0),
        grid=(bsz,),
        in_specs=[bspec(a) for a in args] + [_const_spec(gnorm), bspec(ctx),
                  _const_spec(mod_ctx),
                  _const_spec(gpre), _single_spec(wm), _single_spec(wdin), _single_spec(wdec),
                  _const_spec(bdec)],
        out_specs=pl.BlockSpec((None, t, HV), lambda b: (b, 0, 0)),
        out_shape=jax.ShapeDtypeStruct((bsz, t, HV), BF16),
        scratch_shapes=[
            pltpu.VMEM((t // 2, HV), F32), pltpu.VMEM((t // 2, HV), F32),
            pltpu.VMEM((GLA_HEADS, GLA_DV, LANES), F32),
            pltpu.VMEM((GLA_HEADS, GLA_DV, LANES), F32),
            pltpu.VMEM((t_ctx, HK), BF16), pltpu.VMEM((t_ctx, HV), BF16),
            pltpu.VMEM((t_ctx, HK), F32), pltpu.VMEM((t_ctx, HK), F32),
        ],
        compiler_params=pltpu.CompilerParams(
            dimension_semantics=("parallel",), vmem_limit_bytes=VMEM_LIMIT),
        name="gla",
    )(*args, gnorm, ctx, mod_ctx, gpre, wm, wdin, wdec, bdec)


def _tap_segments(n_tiles, lines_per_tile, n_lines, group):
    segs = []
    for t0 in range(0, n_tiles, group):
        t1 = min(t0 + group, n_tiles)
        lo_line = int(t0 * lines_per_tile)
        hi_line = int(-(-t1 * lines_per_tile // 1)) - 1
        k_lo = max(0, CONV_PAD - hi_line)
        k_hi = min(CONV_K, n_lines + CONV_PAD - lo_line)
        if segs and segs[-1][2:] == (k_lo, k_hi):
            segs[-1] = (segs[-1][0], t1, k_lo, k_hi)
        else:
            segs.append((t0, t1, k_lo, k_hi))
    return segs


def _conv_kernel(ah_ref, av_ref, w_ref, b_ref, yh_ref, yv_ref, ph_ref, pv_ref, wb_ref):
    t = ah_ref.shape[0]
    rows = t // GRID_W
    tile = 64
    for kk in range(CONV_K):
        wb_ref[kk] = jnp.broadcast_to(w_ref[kk:kk + 1, :], wb_ref.shape[1:])

    def conv(src_ref, pad_ref, w_cols, line, n_lines, group, dst_ref):
        pad = CONV_PAD * line
        pad_ref[pl.ds(0, pad), :] = jnp.zeros((pad, CONV_HALF), F32)
        pad_ref[pl.ds(pad + t, pad), :] = jnp.zeros((pad, CONV_HALF), F32)

        def fill(i, carry):
            t0 = pl.multiple_of(i * tile, tile)
            pad_ref[pl.ds(pl.multiple_of(pad + t0, 8), tile), :] = (
                src_ref[pl.ds(t0, tile), :].astype(F32))
            return carry

        lax.fori_loop(0, t // tile, fill, 0)
        bias = b_ref[:, w_cols]

        for first, last, k_lo, k_hi in _tap_segments(t // tile, tile / line, n_lines, group):
            def body(i, carry, k_lo=k_lo, k_hi=k_hi):
                t0 = pl.multiple_of(i * tile, tile)
                acc = jnp.broadcast_to(bias, (tile, CONV_HALF))
                for kk in range(k_lo, k_hi):
                    win = pad_ref[pl.ds(pl.multiple_of(t0 + kk * line, 8), tile), :]
                    w8 = wb_ref[kk, :, w_cols]
                    acc = acc + jnp.concatenate([w8] * (tile // 8), axis=0) * win
                dst_ref[pl.ds(t0, tile), :] = acc
                return carry

            lax.fori_loop(first, last, body, 0)

    conv(ah_ref, ph_ref, slice(0, CONV_HALF), rows, GRID_W, 2, yh_ref)
    conv(av_ref, pv_ref, slice(CONV_HALF, CONV_DIM), GRID_W, rows, 4, yv_ref)


def _conv(ah, av, w, b):
    bsz, t, c = ah.shape
    rows = t // GRID_W
    tok = pl.BlockSpec((None, t, c), lambda i: (i, 0, 0))
    return pl.pallas_call(
        _conv_kernel,
        grid=(bsz,),
        in_specs=[tok, tok, pl.BlockSpec((None,) + w.shape[1:], lambda i: (0, 0, 0)),
                  _const_spec(b)],
        out_specs=[tok, tok],
        out_shape=[jax.ShapeDtypeStruct((bsz, t, c), F32)] * 2,
        scratch_shapes=[
            pltpu.VMEM((t + 2 * CONV_PAD * rows, c), F32),
            pltpu.VMEM((t + 2 * CONV_PAD * GRID_W, c), F32),
            pltpu.VMEM((CONV_K, 8, 2 * c), F32),
        ],
        compiler_params=pltpu.CompilerParams(
            dimension_semantics=("parallel",), vmem_limit_bytes=VMEM_LIMIT),
        name="conv",
    )(ah, av, w, b)


def _post_kernel(x_ref, yh_ref, yv_ref, og_ref, gt_ref, mod_ref, perm_ref, lng_ref, lnb_ref,
                 gpost1_ref, gpre2_ref, gpost2_ref, wco_ref, wgo_ref, wout_ref, wff1_ref,
                 wff2_ref, o_ref):
    d = x_ref.shape[-1]
    d_ff = wff1_ref.shape[1]
    ff_tile = 1024
    sub = 256
    subs = [slice(s, s + sub) for s in range(0, x_ref.shape[0], sub)]
    gt1, sh2, sc2, gt2 = (_mod_vec(mod_ref, pl.program_id(0), j) for j in (2, 3, 4, 5))

    yh = []
    for n in range(x_ref.shape[0] // PERM_TOKENS):
        blk = yh_ref[:, PERM_ROWS * n:PERM_ROWS * (n + 1), :].reshape(PERM_TOKENS, CONV_HALF)
        hi, lo = _split_bf16(blk)
        yh.append(_dot(perm_ref[...], hi) + _dot(perm_ref[...], lo))
    yh = jnp.concatenate(yh, axis=0)

    yn = []
    for s in subs:
        yc = jnp.concatenate([yh[s], yv_ref[s, :]], axis=-1)
        yc = yc - jnp.mean(yc, axis=-1, keepdims=True)
        var = jnp.mean(yc * yc, axis=-1, keepdims=True)
        z = yc * lax.rsqrt(var + EPS) * lng_ref[...] + lnb_ref[...]
        yn.append((z * _sigmoid(z)).astype(BF16))
    y_gla = [_dot(og_ref[s, :], wgo_ref[...]) for s in subs]
    y_conv = [_dot(v, wco_ref[...]) for v in yn]
    merged = []
    for n, s in enumerate(subs):
        gates = gt_ref[s, :].astype(F32)
        merged.append((gates[:, :d] * y_conv[n] + gates[:, d:] * y_gla[n]).astype(BF16))
    y = [_dot(v, wout_ref[...]) for v in merged]
    h_mid = [x_ref[s, :] + gt1 * _rms(y[n], gpost1_ref[...]) for n, s in enumerate(subs)]
    u2 = [(_rms(h, gpre2_ref[...]) * (1.0 + sc2) + sh2).astype(BF16) for h in h_mid]
    acc = [jnp.zeros(h.shape, F32) for h in h_mid]
    for j in range(d_ff // ff_tile):
        cols = slice(j * ff_tile, (j + 1) * ff_tile)
        f = [_dot(v, wff1_ref[:, cols]) for v in u2]
        f = [jnp.square(jnp.maximum(v, 0.0)).astype(BF16) for v in f]
        acc = [acc[n] + _dot(f[n], wff2_ref[cols, :]) for n in range(len(subs))]
    for n, s in enumerate(subs):
        o_ref[s, :] = h_mid[n] + gt2 * _rms(acc[n], gpost2_ref[...])


def _post(x, yh, yv, og, gates, mod, perm, ln_g, ln_b, gpost1, gpre2, gpost2, wco, wgo, wout,
          wff1, wff2, tm):
    bsz, t, d = x.shape
    assert tm % PERM_TOKENS == 0
    return pl.pallas_call(
        _post_kernel,
        grid=(bsz, t // tm),
        in_specs=[
            _tok_spec(tm, d), _colmajor_spec(tm), _tok_spec(tm, CONV_HALF), _tok_spec(tm, HV),
            _tok_spec(tm, 2 * d),
            _const_spec(mod),
            _single_spec(perm), _const_spec(ln_g), _const_spec(ln_b),
            _const_spec(gpost1), _const_spec(gpre2), _const_spec(gpost2),
            _single_spec(wco), _single_spec(wgo), _single_spec(wout), _single_spec(wff1),
            _single_spec(wff2),
        ],
        out_specs=_tok_spec(tm, d),
        out_shape=jax.ShapeDtypeStruct((bsz, t, d), F32),
        compiler_params=pltpu.CompilerParams(
            dimension_semantics=("parallel", "parallel"), vmem_limit_bytes=VMEM_LIMIT),
        name="post",
    )(x, yh, yv, og, gates, mod, perm, ln_g, ln_b, gpost1, gpre2, gpost2, wco, wgo, wout, wff1,
      wff2)


def kernel(x, c, ctx, c_ctx, w_mod, b_mod, g_pre1, g_post1, g_pre2, g_post2, w_in, conv_w,
           conv_b, conv_ln_g, conv_ln_b, w_conv_out, w_decay, b_decay, gla_norm_g, w_gla_out,
           w_out, w_ff1, w_ff2):
    bsz, t, d = x.shape
    depth = w_in.shape[0]
    assert depth == 1 and d == D_MODEL and t % (GRID_W * 8) == 0
    assert w_in.shape[1:] == (d, COL_END) and conv_w.shape[1:] == (CONV_K, CONV_DIM)
    assert w_mod.shape[1:] == (d, N_MOD * d)
    assert w_decay.shape[1:] == (2, DECAY_RANK, HK) and ctx.shape[0] == bsz and ctx.shape[2] == d

    m, wm, wdin, wg = _prep(c, c_ctx.reshape(1, d), w_mod, b_mod, jnp.swapaxes(w_in[0], 0, 1))

    wdec = jnp.zeros((LANES, 2 * HK), F32)
    wdec = wdec.at[:DECAY_RANK, :HK].set(w_decay[0, 0])
    wdec = wdec.at[DECAY_RANK:2 * DECAY_RANK, HK:].set(w_decay[0, 1]).astype(BF16)
    bdec = b_decay[0].reshape(1, 2 * HK)
    row = lambda v: v.reshape(1, -1)

    perm = _to_colmajor_perm()
    to_colmajor = jnp.asarray(perm, BF16)
    to_rowmajor = jnp.asarray(perm.T, BF16)

    ah, av, q, k, v, rs, laf, lab, gates, wco, wgo, wout, wff1, wff2 = _proj(
        x, m, row(g_pre1[0]), to_colmajor, wm, wdin, wdec, bdec, wg,
        (w_conv_out, w_gla_out, w_out, w_ff1, w_ff2), tm=1024)
    og = _gla(q, k, v, laf, lab, rs, row(gla_norm_g[0]),
              ctx, m[bsz:bsz + 1, :2 * d], row(g_pre1[0]), wm, wdin, wdec, bdec)
    yh, yv = _conv(ah.reshape(bsz, t, CONV_HALF), av, conv_w, row(conv_b[0]))
    yh = yh.reshape(bsz, GRID_W, t // GRID_W, CONV_HALF)

    return _post(x, yh, yv, og, gates, m, to_rowmajor, row(conv_ln_g[0]), row(conv_ln_b[0]),
                 row(g_post1[0]), row(g_pre2[0]), row(g_post2[0]),
                 wco, wgo, wout, wff1, wff2, tm=512)
```

```python
import functools

import jax
import jax.numpy as jnp
import numpy as np
from jax import lax
from jax.experimental import pallas as pl
from jax.experimental.pallas import tpu as pltpu

D_MODEL = 1024
GRID_W = 64
CONV_DIM = 512
CONV_K = 31
GLA_HEADS = 4
GLA_DK = 64
GLA_DV = 128
DECAY_RANK = 16
GATE_NORM = 16.0
CHUNK = 64
N_MOD = 6
EPS = 1e-6

COL_Q = 2 * CONV_DIM
COL_K = COL_Q + GLA_HEADS * GLA_DK
COL_V = COL_K + GLA_HEADS * GLA_DK
COL_R = COL_V + GLA_HEADS * GLA_DV
COL_DEC = COL_R + GLA_HEADS * GLA_DV
COL_GATE = COL_DEC + 2 * DECAY_RANK
COL_END = COL_GATE + 2 * D_MODEL

LANES = 128
HK = GLA_HEADS * GLA_DK
HV = GLA_HEADS * GLA_DV
CONV_HALF = CONV_DIM // 2
CONV_PAD = CONV_K // 2
PERM_ROWS = 8
PERM_TOKENS = PERM_ROWS * GRID_W
PROJ_SUB = 256
SUPER = 2 * CHUNK
STEP = 8 * SUPER

VMEM_LIMIT = 56 * 1024 * 1024

F32 = jnp.float32
BF16 = jnp.bfloat16

NT = (((1,), (1,)), ((), ()))
TN = (((0,), (0,)), ((), ()))


def _dot(a, b):
    return jnp.dot(a, b, preferred_element_type=F32)


def _rms(x, g):
    ms = jnp.mean(x * x, axis=-1, keepdims=True)
    return x * lax.rsqrt(ms + EPS) * g


def _sigmoid(x):
    return jax.nn.sigmoid(x)


def _split_bf16(x):
    hi = x.astype(BF16)
    lo = (x - hi.astype(F32)).astype(BF16)
    return hi, lo


def _to_colmajor_perm():
    out = np.arange(PERM_TOKENS)
    src = (out % PERM_ROWS) * GRID_W + out // PERM_ROWS
    p = np.zeros((PERM_TOKENS, PERM_TOKENS), np.float32)
    p[out, src] = 1.0
    return p


def _prep_kernel(c_ref, cc_ref, wmod_ref, bmod_ref, wm_t_ref, wd_t_ref, wg_t_ref,
                 m_ref, wm_ref, wdin_ref, wg_ref, *, n_k, n_gate):
    j = pl.program_id(0)

    @pl.when(j == 0)
    def _():
        m_ref[...] = jnp.broadcast_to(bmod_ref[...], m_ref.shape)
        wdin_ref[...] = wd_t_ref[...].astype(BF16).T

    @pl.when(j < n_k)
    def _():
        pad = jnp.zeros((m_ref.shape[0] - c_ref.shape[0] - 1, c_ref.shape[1]), F32)
        c = jnp.concatenate([c_ref[...], cc_ref[...], pad], axis=0)
        s_hi, s_lo = _split_bf16(c * _sigmoid(c))
        w_hi, w_lo = _split_bf16(wmod_ref[...])
        m_ref[...] += _dot(s_hi, w_hi) + _dot(s_lo, w_hi) + _dot(s_hi, w_lo)

    wm_ref[...] = wm_t_ref[...].astype(BF16).T

    @pl.when(j < n_gate)
    def _():
        wg_ref[...] = wg_t_ref[...].astype(BF16).T


def _prep(c, c_ctx, w_mod, b_mod, w_in_t):
    bsz, d = c.shape
    n_rows = -(-(bsz + 1) // 8) * 8
    n_out = w_mod.shape[-1]
    n = w_in_t.shape[0]
    tk = 256
    tr = 512
    n_k = d // tk
    n_main = COL_DEC // tr
    n_gate = (n - COL_GATE) // tr
    assert COL_DEC % tr == 0 and (n - COL_GATE) % tr == 0 and max(n_gate, n_k) <= n_main
    k_blk = lambda j: jnp.minimum(j, n_k - 1)
    gate_blk = lambda j: jnp.minimum(j, n_gate - 1)
    return pl.pallas_call(
        functools.partial(_prep_kernel, n_k=n_k, n_gate=n_gate),
        grid=(n_main,),
        in_specs=[
            pl.BlockSpec((bsz, tk), lambda j: (0, k_blk(j))),
            pl.BlockSpec((1, tk), lambda j: (0, k_blk(j))),
            pl.BlockSpec((None, tk, n_out), lambda j: (0, k_blk(j), 0)),
            pl.BlockSpec((1, n_out), lambda j: (0, 0)),
            pl.BlockSpec((tr, d), lambda j: (j, 0)),
            pl.BlockSpec((LANES, d), lambda j: (COL_DEC // LANES, 0)),
            pl.BlockSpec((pl.Element(tr), pl.Element(d)),
                         lambda j: (pl.multiple_of(COL_GATE + tr * gate_blk(j), 32), 0)),
        ],
        out_specs=[
            pl.BlockSpec((n_rows, n_out), lambda j: (0, 0)),
            pl.BlockSpec((d, tr), lambda j: (0, j)),
            pl.BlockSpec((d, LANES), lambda j: (0, 0)),
            pl.BlockSpec((d, tr), lambda j: (0, gate_blk(j))),
        ],
        out_shape=[jax.ShapeDtypeStruct((n_rows, n_out), F32)]
        + [jax.ShapeDtypeStruct((d, w), BF16) for w in (COL_DEC, LANES, n - COL_GATE)],
        compiler_params=pltpu.CompilerParams(
            dimension_semantics=("arbitrary",), vmem_limit_bytes=VMEM_LIMIT),
        name="prep",
    )(c, c_ctx, w_mod, b_mod, w_in_t, w_in_t, w_in_t)


def _mod_vec(m_ref, row, j):
    return m_ref[pl.ds(row, 1), D_MODEL * j:D_MODEL * (j + 1)]


def _prenorm_mod(x, g, m_ref, row):
    sh = _mod_vec(m_ref, row, 0)
    sc = _mod_vec(m_ref, row, 1)
    return (_rms(x, g) * (1.0 + sc) + sh).astype(BF16)


def _decay_logs(z, wdec_ref, bdec_ref):
    z_hi, z_lo = z
    logits = _dot(z_hi, wdec_ref[...]) + _dot(z_lo, wdec_ref[...]) + bdec_ref[...]
    ls = jnp.minimum(logits, 0.0) - jnp.log(1.0 + jnp.exp(-jnp.abs(logits)))
    return ls * (1.0 / GATE_NORM)


N_LATE_W = 5


def _proj_kernel(x_ref, mod_ref, g_ref, perm_ref, wm_ref, wdin_ref, wdec_ref, bdec_ref, wg_ref,
                 *refs):
    late_f32 = refs[:N_LATE_W]
    (ah_ref, av_ref, q_ref, k_ref, v_ref, rs_ref, laf_ref, lab_ref,
     gt_ref) = refs[N_LATE_W:-N_LATE_W]
    late_bf16 = refs[-N_LATE_W:]
    for src, dst in zip(late_f32, late_bf16):
        dst[...] = src[...].astype(BF16)
    subs = [slice(s, s + PROJ_SUB) for s in range(0, x_ref.shape[0], PROJ_SUB)]
    g = g_ref[...]
    ub = [_prenorm_mod(x_ref[s, :], g, mod_ref, pl.program_id(0)) for s in subs]
    z = [_split_bf16(_dot(u, wdin_ref[...])) for u in ub]
    a_h = []
    for n, s in enumerate(subs):
        glu = _dot(ub[n], wm_ref[:, 0:COL_Q])
        a = (glu[:, :CONV_DIM] * _sigmoid(glu[:, CONV_DIM:])).astype(BF16)
        av_ref[s, :] = a[:, CONV_HALF:]
        a_h.append(a[:, :CONV_HALF])
    for n, s in enumerate(subs):
        q_ref[s, :] = (_dot(ub[n], wm_ref[:, COL_Q:COL_K]) * (GLA_DK ** -0.5)).astype(BF16)
    for n, s in enumerate(subs):
        k_ref[s, :] = _dot(ub[n], wm_ref[:, COL_K:COL_V]).astype(BF16)
    for n, s in enumerate(subs):
        v_ref[s, :] = _dot(ub[n], wm_ref[:, COL_V:COL_R]).astype(BF16)
    for n, s in enumerate(subs):
        la = _decay_logs(z[n], wdec_ref, bdec_ref)
        laf_ref[s, :] = la[:, :HK]
        lab_ref[s, :] = la[:, HK:]
    for n, s in enumerate(subs):
        r = _dot(ub[n], wm_ref[:, COL_R:COL_DEC])
        rs_ref[s, :] = (r * _sigmoid(r)).astype(BF16)
    per_group = PERM_TOKENS // PROJ_SUB
    for n in range(x_ref.shape[0] // PERM_TOKENS):
        a_rm = jnp.concatenate(a_h[per_group * n:per_group * (n + 1)], axis=0)
        a_cm = _dot(perm_ref[...], a_rm)
        ah_ref[:, PERM_ROWS * n:PERM_ROWS * (n + 1), :] = a_cm.reshape(
            GRID_W, PERM_ROWS, CONV_HALF)
    for n, s in enumerate(subs):
        gt_ref[s, :] = _sigmoid(_dot(ub[n], wg_ref[...])).astype(BF16)


def _const_spec(arr):
    nd = arr.ndim
    return pl.BlockSpec(arr.shape, lambda *_: (0,) * nd)


def _tok_spec(tm, width):
    return pl.BlockSpec((None, tm, width), lambda b, i: (b, i, 0))


def _colmajor_spec(tm):
    return pl.BlockSpec((None, GRID_W, tm // GRID_W, CONV_HALF), lambda b, i: (b, 0, i, 0))


def _single_spec(arr):
    nd = arr.ndim
    return pl.BlockSpec(arr.shape, lambda *_: (0,) * nd, pipeline_mode=pl.Buffered(1))


def _proj(x, mod, g, perm, wm, wdin, wdec, bdec, wg, late_weights, tm):
    bsz, t, d = x.shape
    widths = (CONV_HALF, HK, HK, HV, HV, HK, HK, 2 * D_MODEL)
    dtypes = (BF16, BF16, BF16, BF16, BF16, F32, F32, BF16)
    assert tm % PERM_TOKENS == 0 and len(late_weights) == N_LATE_W
    ah_shape = jax.ShapeDtypeStruct((bsz, GRID_W, t // GRID_W, CONV_HALF), F32)
    n_i = t // tm
    n_steps = bsz * n_i
    late_in, late_out, late_shape = [], [], []
    for w in late_weights:
        _, rows, cols = w.shape
        assert rows % (16 * n_steps) == 0
        blk = rows // n_steps
        late_in.append(pl.BlockSpec((None, blk, cols), lambda b, i: (0, b * n_i + i, 0)))
        late_out.append(pl.BlockSpec((blk, cols), lambda b, i: (b * n_i + i, 0)))
        late_shape.append(jax.ShapeDtypeStruct((rows, cols), BF16))
    return pl.pallas_call(
        _proj_kernel,
        grid=(bsz, n_i),
        in_specs=[
            _tok_spec(tm, d),
            _const_spec(mod),
            _const_spec(g), _single_spec(perm), _single_spec(wm), _single_spec(wdin),
            _single_spec(wdec), _const_spec(bdec), _single_spec(wg),
        ] + late_in,
        out_specs=[_colmajor_spec(tm)] + [_tok_spec(tm, w) for w in widths] + late_out,
        out_shape=[ah_shape] + [jax.ShapeDtypeStruct((bsz, t, w), dt)
                                for w, dt in zip(widths, dtypes)] + late_shape,
        compiler_params=pltpu.CompilerParams(
            dimension_semantics=("parallel", "parallel"), vmem_limit_bytes=VMEM_LIMIT),
        name="proj",
    )(x, mod, g, perm, wm, wdin, wdec, bdec, wg, *late_weights)


def _gla_scale(qs, ks, las, dirn, want_out):
    tri_b, mid, last = dirn["tri_b"], dirn["mid"], dirn["last"]
    la_hi, la_lo = _split_bf16(las)
    g = _dot(tri_b, la_hi) + _dot(tri_b, la_lo)
    u = dict(kt=[], kend=[], qt=[], qg=[], e=[])
    for c in range(SUPER // CHUNK):
        rows = slice(c * CHUNK, (c + 1) * CHUNK)
        gc = g[rows]
        g_mid = gc[mid:mid + 1]
        g_last = gc[last:last + 1]
        u["e"].append(jnp.exp(g_last))
        kt = ks[rows].astype(F32) * jnp.exp(g_mid - gc)
        u["kt"].append(kt.astype(BF16))
        u["kend"].append((kt * jnp.exp(g_last - g_mid)).astype(BF16))
        if want_out:
            qt = qs[rows].astype(F32) * jnp.exp(gc - g_mid)
            u["qt"].append(qt.astype(BF16))
            u["qg"].append((qt * jnp.exp(g_mid)).astype(BF16))
    return u


def _pair_lanes(pair):
    return slice(LANES * pair, LANES * (pair + 1))


def _gla_scores(u, pair, head_lane):
    lanes = _pair_lanes(pair)
    kt = jnp.concatenate([c[:, lanes] for c in u["kt"]], axis=0)
    qt = jnp.concatenate([c[:, lanes] for c in u["qt"]], axis=0)
    kt2 = jnp.concatenate([kt * hm.astype(BF16) for hm in head_lane], axis=0)
    return lax.dot_general(qt, kt2, NT, preferred_element_type=F32)


def _gla_update(u, vs, h, head_lane):
    lanes = _pair_lanes(h // 2)
    hm = head_lane[h % 2].astype(BF16)
    k0 = u["kend"][0][:, lanes] * hm
    k1 = u["kend"][1][:, lanes] * hm
    z = jnp.zeros_like(k0)
    kbd = jnp.concatenate([jnp.concatenate([k0, z], axis=1),
                           jnp.concatenate([z, k1], axis=1)], axis=0)
    return lax.dot_general(vs[:, GLA_DV * h:GLA_DV * (h + 1)], kbd, TN,
                           preferred_element_type=F32)


def _gla_intra(scores, vs, pair, mask2):
    a = jnp.where(mask2, scores, 0.0).astype(BF16)
    v0 = vs[:, GLA_DV * 2 * pair:GLA_DV * (2 * pair + 1)]
    v1 = vs[:, GLA_DV * (2 * pair + 1):GLA_DV * (2 * pair + 2)]
    z = jnp.zeros_like(v0)
    vbd = jnp.concatenate([jnp.concatenate([v0, z], axis=1),
                           jnp.concatenate([z, v1], axis=1)], axis=0)
    return _dot(a, vbd)


def _gla_kernel(q_ref, k_ref, v_ref, laf_ref, lab_ref, rs_ref, g_ref,
                ctx_ref, modc_ref, gpre_ref, wm_ref, wdin_ref, wdec_ref, bdec_ref,
                o_ref, of_ref, ob_ref, sf_ref, sb_ref, kc_ref, vc_ref, lafc_ref, labc_ref):
    t = q_ref.shape[0]
    t_ctx = kc_ref.shape[0]

    ubc = _prenorm_mod(ctx_ref[...], gpre_ref[...], modc_ref, pl.num_programs(0))
    zc = _split_bf16(_dot(ubc, wdin_ref[...]))
    kc_ref[...] = _dot(ubc, wm_ref[:, COL_K:COL_V]).astype(BF16)
    vc_ref[...] = _dot(ubc, wm_ref[:, COL_V:COL_R]).astype(BF16)
    lac = _decay_logs(zc, wdec_ref, bdec_ref)
    lafc_ref[...] = lac[:, :HK]
    labc_ref[...] = lac[:, HK:]

    half = t // 2
    n_steps = t // STEP
    supers = STEP // SUPER
    n_chunks = SUPER // CHUNK
    pairs = GLA_HEADS // 2

    row = lax.broadcasted_iota(jnp.int32, (SUPER, SUPER), 0)
    col = lax.broadcasted_iota(jnp.int32, (SUPER, SUPER), 1)
    same_chunk = (row // CHUNK) == (col // CHUNK)
    low_mask = jnp.logical_and(same_chunk, col <= row)
    up_mask = jnp.logical_and(same_chunk, col >= row)
    lane = lax.broadcasted_iota(jnp.int32, (1, LANES), 1)
    head_lane = (jnp.where(lane < GLA_DK, 1.0, 0.0), jnp.where(lane >= GLA_DK, 1.0, 0.0))

    fwd = dict(tri_b=jnp.where(low_mask, 1.0, 0.0).astype(BF16),
               mask2=jnp.concatenate([low_mask, low_mask], axis=1),
               mid=CHUNK // 2 - 1, last=CHUNK - 1, order=(0, 1), st_ref=sf_ref)
    bwd = dict(tri_b=jnp.where(up_mask, 1.0, 0.0).astype(BF16),
               mask2=jnp.concatenate([up_mask, up_mask], axis=1),
               mid=CHUNK // 2, last=0, order=(1, 0), st_ref=sb_ref)

    sf_ref[...] = jnp.zeros_like(sf_ref)
    sb_ref[...] = jnp.zeros_like(sb_ref)
    gn = g_ref[...]

    def run(units, want_out, finalize):
        vals = []
        for dirn, refs, t0 in units:
            q_r, k_r, v_r, la_r = refs
            rows = pl.ds(t0, SUPER)
            qs = q_r[rows, :] if want_out else None
            vals.append((_gla_scale(qs, k_r[rows, :], la_r[rows, :], dirn, want_out),
                         v_r[rows, :]))
        scores = []
        if want_out:
            scores = [[_gla_scores(u, p, head_lane) for p in range(pairs)] for u, _ in vals]
        upd = [[_gla_update(u, vs, h, head_lane) for h in range(GLA_HEADS)] for u, vs in vals]
        intra = []
        if want_out:
            intra = [[_gla_intra(scores[n][p], vals[n][1], p, units[n][0]["mask2"])
                      for p in range(pairs)] for n in range(len(units))]
        states = {}
        for n, (dirn, _, t0) in enumerate(units):
            key = id(dirn["st_ref"])
            if key not in states:
                states[key] = (dirn["st_ref"], [dirn["st_ref"][h] for h in range(GLA_HEADS)])
            st = states[key][1]
            u = vals[n][0]
            inter = [[None] * n_chunks for _ in range(pairs)]
            for c in dirn["order"]:
                if want_out:
                    for p in range(pairs):
                        st2 = jnp.concatenate([st[2 * p].astype(BF16),
                                               st[2 * p + 1].astype(BF16)], axis=0)
                        inter[p][c] = lax.dot_general(u["qg"][c][:, _pair_lanes(p)], st2, NT,
                                                      preferred_element_type=F32)
                for h in range(GLA_HEADS):
                    e = u["e"][c][:, _pair_lanes(h // 2)]
                    st[h] = st[h] * e + upd[n][h][:, LANES * c:LANES * (c + 1)]
            if not want_out:
                continue
            fwd_unit = dirn is fwd
            for p in range(pairs):
                o2 = intra[n][p] + jnp.concatenate(inter[p], axis=0)
                for j in range(2):
                    cols = slice(GLA_DV * (2 * p + j), GLA_DV * (2 * p + j + 1))
                    o = o2[:, GLA_DV * j:GLA_DV * (j + 1)]
                    if finalize:
                        other = (ob_ref[pl.ds(t0 - half, SUPER), cols] if fwd_unit
                                 else of_ref[pl.ds(t0, SUPER), cols])
                        rs = rs_ref[pl.ds(t0, SUPER), cols].astype(F32)
                        o_ref[pl.ds(t0, SUPER), cols] = (_rms(o + other, gn) * rs).astype(BF16)
                    elif fwd_unit:
                        of_ref[pl.ds(t0, SUPER), cols] = o
                    else:
                        ob_ref[pl.ds(t0 - half, SUPER), cols] = o
        for st_ref, st in states.values():
            for h in range(GLA_HEADS):
                st_ref[h] = st[h]

    ctx_units = []
    for s in range(t_ctx // SUPER):
        ctx_units.append((fwd, (None, kc_ref, vc_ref, lafc_ref), s * SUPER))
        ctx_units.append((bwd, (None, kc_ref, vc_ref, labc_ref), t_ctx - (s + 1) * SUPER))
    run(ctx_units, want_out=False, finalize=False)

    def step(i, finalize):
        units = []
        for s in range(supers):
            t0 = pl.multiple_of(i * STEP + s * SUPER, SUPER)
            t1 = pl.multiple_of(t - (i + 1) * STEP + (supers - 1 - s) * SUPER, SUPER)
            units.append((fwd, (q_ref, k_ref, v_ref, laf_ref), t0))
            units.append((bwd, (q_ref, k_ref, v_ref, lab_ref), t1))
        run(units, want_out=True, finalize=finalize)

    def park_body(i, carry):
        step(i, False)
        return carry

    def finish_body(i, carry):
        step(i, True)
        return carry

    lax.fori_loop(0, n_steps // 2, park_body, 0)
    lax.fori_loop(n_steps // 2, n_steps, finish_body, 0)


def _gla(q, k, v, laf, lab, rs, gnorm, ctx, mod_ctx, gpre, wm, wdin, wdec, bdec):
    bsz, t, _ = q.shape
    t_ctx = ctx.shape[1]
    assert t % (2 * STEP) == 0 and t_ctx % SUPER == 0

    def bspec(arr):
        return pl.BlockSpec((None,) + arr.shape[1:], lambda b: (b, 0, 0))

    args = (q, k, v, laf, lab, rs)
    return pl.pallas_call(
        _gla_kernel,
        grid=(bsz,),
        in_specs=[bspec(a) for a in args] + [_const_spec(gnorm), bspec(ctx),
                  _const_spec(mod_ctx),
                  _const_spec(gpre), _single_spec(wm), _single_spec(wdin), _single_spec(wdec),
                  _const_spec(bdec)],
        out_specs=pl.BlockSpec((None, t, HV), lambda b: (b, 0, 0)),
        out_shape=jax.ShapeDtypeStruct((bsz, t, HV), BF16),
        scratch_shapes=[
            pltpu.VMEM((t // 2, HV), F32), pltpu.VMEM((t // 2, HV), F32),
            pltpu.VMEM((GLA_HEADS, GLA_DV, LANES), F32),
            pltpu.VMEM((GLA_HEADS, GLA_DV, LANES), F32),
            pltpu.VMEM((t_ctx, HK), BF16), pltpu.VMEM((t_ctx, HV), BF16),
            pltpu.VMEM((t_ctx, HK), F32), pltpu.VMEM((t_ctx, HK), F32),
        ],
        compiler_params=pltpu.CompilerParams(
            dimension_semantics=("parallel",), vmem_limit_bytes=VMEM_LIMIT),
        name="gla",
    )(*args, gnorm, ctx, mod_ctx, gpre, wm, wdin, wdec, bdec)


def _tap_segments(n_tiles, lines_per_tile, n_lines, group):
    segs = []
    for t0 in range(0, n_tiles, group):
        t1 = min(t0 + group, n_tiles)
        lo_line = int(t0 * lines_per_tile)
        hi_line = int(-(-t1 * lines_per_tile // 1)) - 1
        k_lo = max(0, CONV_PAD - hi_line)
        k_hi = min(CONV_K, n_lines + CONV_PAD - lo_line)
        if segs and segs[-1][2:] == (k_lo, k_hi):
            segs[-1] = (segs[-1][0], t1, k_lo, k_hi)
        else:
            segs.append((t0, t1, k_lo, k_hi))
    return segs


def _conv_kernel(ah_ref, av_ref, w_ref, b_ref, yh_ref, yv_ref, ph_ref, pv_ref, wb_ref):
    t = ah_ref.shape[0]
    rows = t // GRID_W
    tile = 64
    for kk in range(CONV_K):
        wb_ref[kk] = jnp.broadcast_to(w_ref[kk:kk + 1, :], wb_ref.shape[1:])

    def conv(src_ref, pad_ref, w_cols, line, n_lines, group, dst_ref):
        pad = CONV_PAD * line
        pad_ref[pl.ds(0, pad), :] = jnp.zeros((pad, CONV_HALF), F32)
        pad_ref[pl.ds(pad + t, pad), :] = jnp.zeros((pad, CONV_HALF), F32)

        def fill(i, carry):
            t0 = pl.multiple_of(i * tile, tile)
            pad_ref[pl.ds(pl.multiple_of(pad + t0, 8), tile), :] = (
                src_ref[pl.ds(t0, tile), :].astype(F32))
            return carry

        lax.fori_loop(0, t // tile, fill, 0)
        bias = b_ref[:, w_cols]

        for first, last, k_lo, k_hi in _tap_segments(t // tile, tile / line, n_lines, group):
            def body(i, carry, k_lo=k_lo, k_hi=k_hi):
                t0 = pl.multiple_of(i * tile, tile)
                acc = jnp.broadcast_to(bias, (tile, CONV_HALF))
                for kk in range(k_lo, k_hi):
                    win = pad_ref[pl.ds(pl.multiple_of(t0 + kk * line, 8), tile), :]
                    w8 = wb_ref[kk, :, w_cols]
                    acc = acc + jnp.concatenate([w8] * (tile // 8), axis=0) * win
                dst_ref[pl.ds(t0, tile), :] = acc
                return carry

            lax.fori_loop(first, last, body, 0)

    conv(ah_ref, ph_ref, slice(0, CONV_HALF), rows, GRID_W, 1, yh_ref)
    conv(av_ref, pv_ref, slice(CONV_HALF, CONV_DIM), GRID_W, rows, 2, yv_ref)


def _conv(ah, av, w, b):
    bsz, t, c = ah.shape
    rows = t // GRID_W
    tok = pl.BlockSpec((None, t, c), lambda i: (i, 0, 0))
    return pl.pallas_call(
        _conv_kernel,
        grid=(bsz,),
        in_specs=[tok, tok, pl.BlockSpec((None,) + w.shape[1:], lambda i: (0, 0, 0)),
                  _const_spec(b)],
        out_specs=[tok, tok],
        out_shape=[jax.ShapeDtypeStruct((bsz, t, c), F32)] * 2,
        scratch_shapes=[
            pltpu.VMEM((t + 2 * CONV_PAD * rows, c), F32),
            pltpu.VMEM((t + 2 * CONV_PAD * GRID_W, c), F32),
            pltpu.VMEM((CONV_K, 8, 2 * c), F32),
        ],
        compiler_params=pltpu.CompilerParams(
            dimension_semantics=("parallel",), vmem_limit_bytes=VMEM_LIMIT),
        name="conv",
    )(ah, av, w, b)


def _post_kernel(x_ref, yh_ref, yv_ref, og_ref, gt_ref, mod_ref, perm_ref, lng_ref, lnb_ref,
                 gpost1_ref, gpre2_ref, gpost2_ref, wco_ref, wgo_ref, wout_ref, wff1_ref,
                 wff2_ref, o_ref):
    d = x_ref.shape[-1]
    d_ff = wff1_ref.shape[1]
    ff_tile = 1024
    sub = 256
    subs = [slice(s, s + sub) for s in range(0, x_ref.shape[0], sub)]
    gt1, sh2, sc2, gt2 = (_mod_vec(mod_ref, pl.program_id(0), j) for j in (2, 3, 4, 5))

    yh = []
    for n in range(x_ref.shape[0] // PERM_TOKENS):
        blk = yh_ref[:, PERM_ROWS * n:PERM_ROWS * (n + 1), :].reshape(PERM_TOKENS, CONV_HALF)
        hi, lo = _split_bf16(blk)
        yh.append(_dot(perm_ref[...], hi) + _dot(perm_ref[...], lo))
    yh = jnp.concatenate(yh, axis=0)

    yn = []
    for s in subs:
        yc = jnp.concatenate([yh[s], yv_ref[s, :]], axis=-1)
        yc = yc - jnp.mean(yc, axis=-1, keepdims=True)
        var = jnp.mean(yc * yc, axis=-1, keepdims=True)
        z = yc * lax.rsqrt(var + EPS) * lng_ref[...] + lnb_ref[...]
        yn.append((z * _sigmoid(z)).astype(BF16))
    y_gla = [_dot(og_ref[s, :], wgo_ref[...]) for s in subs]
    y_conv = [_dot(v, wco_ref[...]) for v in yn]
    merged = []
    for n, s in enumerate(subs):
        gates = gt_ref[s, :].astype(F32)
        merged.append((gates[:, :d] * y_conv[n] + gates[:, d:] * y_gla[n]).astype(BF16))
    y = [_dot(v, wout_ref[...]) for v in merged]
    h_mid = [x_ref[s, :] + gt1 * _rms(y[n], gpost1_ref[...]) for n, s in enumerate(subs)]
    u2 = [(_rms(h, gpre2_ref[...]) * (1.0 + sc2) + sh2).astype(BF16) for h in h_mid]
    acc = [jnp.zeros(h.shape, F32) for h in h_mid]
    for j in range(d_ff // ff_tile):
        cols = slice(j * ff_tile, (j + 1) * ff_tile)
        f = [_dot(v, wff1_ref[:, cols]) for v in u2]
        f = [jnp.square(jnp.maximum(v, 0.0)).astype(BF16) for v in f]
        acc = [acc[n] + _dot(f[n], wff2_ref[cols, :]) for n in range(len(subs))]
    for n, s in enumerate(subs):
        o_ref[s, :] = h_mid[n] + gt2 * _rms(acc[n], gpost2_ref[...])


def _post(x, yh, yv, og, gates, mod, perm, ln_g, ln_b, gpost1, gpre2, gpost2, wco, wgo, wout,
          wff1, wff2, tm):
    bsz, t, d = x.shape
    assert tm % PERM_TOKENS == 0
    return pl.pallas_call(
        _post_kernel,
        grid=(bsz, t // tm),
        in_specs=[
            _tok_spec(tm, d), _colmajor_spec(tm), _tok_spec(tm, CONV_HALF), _tok_spec(tm, HV),
            _tok_spec(tm, 2 * d),
            _const_spec(mod),
            _single_spec(perm), _const_spec(ln_g), _const_spec(ln_b),
            _const_spec(gpost1), _const_spec(gpre2), _const_spec(gpost2),
            _single_spec(wco), _single_spec(wgo), _single_spec(wout), _single_spec(wff1),
            _single_spec(wff2),
        ],
        out_specs=_tok_spec(tm, d),
        out_shape=jax.ShapeDtypeStruct((bsz, t, d), F32),
        compiler_params=pltpu.CompilerParams(
            dimension_semantics=("parallel", "parallel"), vmem_limit_bytes=VMEM_LIMIT),
        name="post",
    )(x, yh, yv, og, gates, mod, perm, ln_g, ln_b, gpost1, gpre2, gpost2, wco, wgo, wout, wff1,
      wff2)


def kernel(x, c, ctx, c_ctx, w_mod, b_mod, g_pre1, g_post1, g_pre2, g_post2, w_in, conv_w,
           conv_b, conv_ln_g, conv_ln_b, w_conv_out, w_decay, b_decay, gla_norm_g, w_gla_out,
           w_out, w_ff1, w_ff2):
    bsz, t, d = x.shape
    depth = w_in.shape[0]
    assert depth == 1 and d == D_MODEL and t % (GRID_W * 8) == 0
    assert w_in.shape[1:] == (d, COL_END) and conv_w.shape[1:] == (CONV_K, CONV_DIM)
    assert w_mod.shape[1:] == (d, N_MOD * d)
    assert w_decay.shape[1:] == (2, DECAY_RANK, HK) and ctx.shape[0] == bsz and ctx.shape[2] == d

    m, wm, wdin, wg = _prep(c, c_ctx.reshape(1, d), w_mod, b_mod, jnp.swapaxes(w_in[0], 0, 1))

    wdec = jnp.zeros((LANES, 2 * HK), F32)
    wdec = wdec.at[:DECAY_RANK, :HK].set(w_decay[0, 0])
    wdec = wdec.at[DECAY_RANK:2 * DECAY_RANK, HK:].set(w_decay[0, 1]).astype(BF16)
    bdec = b_decay[0].reshape(1, 2 * HK)
    row = lambda v: v.reshape(1, -1)

    perm = _to_colmajor_perm()
    to_colmajor = jnp.asarray(perm, BF16)
    to_rowmajor = jnp.asarray(perm.T, BF16)

    ah, av, q, k, v, rs, laf, lab, gates, wco, wgo, wout, wff1, wff2 = _proj(
        x, m, row(g_pre1[0]), to_colmajor, wm, wdin, wdec, bdec, wg,
        (w_conv_out, w_gla_out, w_out, w_ff1, w_ff2), tm=1024)
    og = _gla(q, k, v, laf, lab, rs, row(gla_norm_g[0]),
              ctx, m, row(g_pre1[0]), wm, wdin, wdec, bdec)
    yh, yv = _conv(ah.reshape(bsz, t, CONV_HALF), av, conv_w, row(conv_b[0]))
    yh = yh.reshape(bsz, GRID_W, t // GRID_W, CONV_HALF)

    return _post(x, yh, yv, og, gates, m, to_rowmajor, row(conv_ln_g[0]), row(conv_ln_b[0]),
                 row(g_post1[0]), row(g_pre2[0]), row(g_post2[0]),
                 wco, wgo, wout, wff1, wff2, tm=512)
```

```python
import functools

import jax
import jax.numpy as jnp
import numpy as np
from jax import lax
from jax.experimental import pallas as pl
from jax.experimental.pallas import tpu as pltpu

D_MODEL = 1024
GRID_W = 64
CONV_DIM = 512
CONV_K = 31
GLA_HEADS = 4
GLA_DK = 64
GLA_DV = 128
DECAY_RANK = 16
GATE_NORM = 16.0
CHUNK = 64
N_MOD = 6
EPS = 1e-6

COL_Q = 2 * CONV_DIM
COL_K = COL_Q + GLA_HEADS * GLA_DK
COL_V = COL_K + GLA_HEADS * GLA_DK
COL_R = COL_V + GLA_HEADS * GLA_DV
COL_DEC = COL_R + GLA_HEADS * GLA_DV
COL_GATE = COL_DEC + 2 * DECAY_RANK
COL_END = COL_GATE + 2 * D_MODEL

LANES = 128
HK = GLA_HEADS * GLA_DK
HV = GLA_HEADS * GLA_DV
CONV_HALF = CONV_DIM // 2
CONV_PAD = CONV_K // 2
PERM_ROWS = 8
PERM_TOKENS = PERM_ROWS * GRID_W
PROJ_SUB = 256
SUPER = 2 * CHUNK
STEP = 8 * SUPER

VMEM_LIMIT = 56 * 1024 * 1024

F32 = jnp.float32
BF16 = jnp.bfloat16

NT = (((1,), (1,)), ((), ()))
TN = (((0,), (0,)), ((), ()))


def _dot(a, b):
    return jnp.dot(a, b, preferred_element_type=F32)


def _rms(x, g):
    ms = jnp.mean(x * x, axis=-1, keepdims=True)
    return x * lax.rsqrt(ms + EPS) * g


def _sigmoid(x):
    return jax.nn.sigmoid(x)


def _split_bf16(x):
    hi = x.astype(BF16)
    lo = (x - hi.astype(F32)).astype(BF16)
    return hi, lo


def _to_colmajor_perm():
    out = np.arange(PERM_TOKENS)
    src = (out % PERM_ROWS) * GRID_W + out // PERM_ROWS
    p = np.zeros((PERM_TOKENS, PERM_TOKENS), np.float32)
    p[out, src] = 1.0
    return p


def _prep_kernel(c_ref, cc_ref, wmod_ref, bmod_ref, wm_t_ref, wd_t_ref, wg_t_ref,
                 m_ref, wm_ref, wdin_ref, wg_ref, *, n_k, n_gate):
    j = pl.program_id(0)

    @pl.when(j == 0)
    def _():
        m_ref[...] = jnp.broadcast_to(bmod_ref[...], m_ref.shape)
        wdin_ref[...] = wd_t_ref[...].astype(BF16).T

    @pl.when(j < n_k)
    def _():
        pad = jnp.zeros((m_ref.shape[0] - c_ref.shape[0] - 1, c_ref.shape[1]), F32)
        c = jnp.concatenate([c_ref[...], cc_ref[...], pad], axis=0)
        s_hi, s_lo = _split_bf16(c * _sigmoid(c))
        w_hi, w_lo = _split_bf16(wmod_ref[...])
        m_ref[...] += _dot(s_hi, w_hi) + _dot(s_lo, w_hi) + _dot(s_hi, w_lo)

    wm_ref[...] = wm_t_ref[...].astype(BF16).T

    @pl.when(j < n_gate)
    def _():
        wg_ref[...] = wg_t_ref[...].astype(BF16).T


def _prep(c, c_ctx, w_mod, b_mod, w_in_t):
    bsz, d = c.shape
    n_rows = -(-(bsz + 1) // 8) * 8
    n_out = w_mod.shape[-1]
    n = w_in_t.shape[0]
    tk = 256
    tr = 512
    n_k = d // tk
    n_main = COL_DEC // tr
    n_gate = (n - COL_GATE) // tr
    assert COL_DEC % tr == 0 and (n - COL_GATE) % tr == 0 and max(n_gate, n_k) <= n_main
    k_blk = lambda j: jnp.minimum(j, n_k - 1)
    gate_blk = lambda j: jnp.minimum(j, n_gate - 1)
    return pl.pallas_call(
        functools.partial(_prep_kernel, n_k=n_k, n_gate=n_gate),
        grid=(n_main,),
        in_specs=[
            pl.BlockSpec((bsz, tk), lambda j: (0, k_blk(j))),
            pl.BlockSpec((1, tk), lambda j: (0, k_blk(j))),
            pl.BlockSpec((None, tk, n_out), lambda j: (0, k_blk(j), 0)),
            pl.BlockSpec((1, n_out), lambda j: (0, 0)),
            pl.BlockSpec((tr, d), lambda j: (j, 0)),
            pl.BlockSpec((LANES, d), lambda j: (COL_DEC // LANES, 0)),
            pl.BlockSpec((pl.Element(tr), pl.Element(d)),
                         lambda j: (pl.multiple_of(COL_GATE + tr * gate_blk(j), 32), 0)),
        ],
        out_specs=[
            pl.BlockSpec((n_rows, n_out), lambda j: (0, 0)),
            pl.BlockSpec((d, tr), lambda j: (0, j)),
            pl.BlockSpec((d, LANES), lambda j: (0, 0)),
            pl.BlockSpec((d, tr), lambda j: (0, gate_blk(j))),
        ],
        out_shape=[jax.ShapeDtypeStruct((n_rows, n_out), F32)]
        + [jax.ShapeDtypeStruct((d, w), BF16) for w in (COL_DEC, LANES, n - COL_GATE)],
        compiler_params=pltpu.CompilerParams(
            dimension_semantics=("arbitrary",), vmem_limit_bytes=VMEM_LIMIT),
        name="prep",
    )(c, c_ctx, w_mod, b_mod, w_in_t, w_in_t, w_in_t)


def _mod_vec(m_ref, row, j):
    return m_ref[pl.ds(row, 1), D_MODEL * j:D_MODEL * (j + 1)]


def _prenorm_mod(x, g, m_ref, row):
    sh = _mod_vec(m_ref, row, 0)
    sc = _mod_vec(m_ref, row, 1)
    return (_rms(x, g) * (1.0 + sc) + sh).astype(BF16)


def _decay_logs(z, wdec_ref, bdec_ref):
    z_hi, z_lo = z
    logits = _dot(z_hi, wdec_ref[...]) + _dot(z_lo, wdec_ref[...]) + bdec_ref[...]
    ls = jnp.minimum(logits, 0.0) - jnp.log(1.0 + jnp.exp(-jnp.abs(logits)))
    return ls * (1.0 / GATE_NORM)


N_LATE_W = 5


def _proj_kernel(x_ref, mod_ref, g_ref, perm_ref, wm_ref, wdin_ref, wdec_ref, bdec_ref, wg_ref,
                 *refs):
    late_f32 = refs[:N_LATE_W]
    (ah_ref, av_ref, q_ref, k_ref, v_ref, rs_ref, laf_ref, lab_ref,
     gt_ref) = refs[N_LATE_W:-N_LATE_W]
    late_bf16 = refs[-N_LATE_W:]
    for src, dst in zip(late_f32, late_bf16):
        dst[...] = src[...].astype(BF16)
    subs = [slice(s, s + PROJ_SUB) for s in range(0, x_ref.shape[0], PROJ_SUB)]
    g = g_ref[...]
    ub = [_prenorm_mod(x_ref[s, :], g, mod_ref, pl.program_id(0)) for s in subs]
    z = [_split_bf16(_dot(u, wdin_ref[...])) for u in ub]
    a_h = []
    for n, s in enumerate(subs):
        glu = _dot(ub[n], wm_ref[:, 0:COL_Q])
        a = (glu[:, :CONV_DIM] * _sigmoid(glu[:, CONV_DIM:])).astype(BF16)
        av_ref[s, :] = a[:, CONV_HALF:]
        a_h.append(a[:, :CONV_HALF])
    for n, s in enumerate(subs):
        q_ref[s, :] = (_dot(ub[n], wm_ref[:, COL_Q:COL_K]) * (GLA_DK ** -0.5)).astype(BF16)
    for n, s in enumerate(subs):
        k_ref[s, :] = _dot(ub[n], wm_ref[:, COL_K:COL_V]).astype(BF16)
    for n, s in enumerate(subs):
        v_ref[s, :] = _dot(ub[n], wm_ref[:, COL_V:COL_R]).astype(BF16)
    for n, s in enumerate(subs):
        la = _decay_logs(z[n], wdec_ref, bdec_ref)
        laf_ref[s, :] = la[:, :HK]
        lab_ref[s, :] = la[:, HK:]
    for n, s in enumerate(subs):
        r = _dot(ub[n], wm_ref[:, COL_R:COL_DEC])
        rs_ref[s, :] = (r * _sigmoid(r)).astype(BF16)
    per_group = PERM_TOKENS // PROJ_SUB
    for n in range(x_ref.shape[0] // PERM_TOKENS):
        a_rm = jnp.concatenate(a_h[per_group * n:per_group * (n + 1)], axis=0)
        a_cm = _dot(perm_ref[...], a_rm)
        ah_ref[:, PERM_ROWS * n:PERM_ROWS * (n + 1), :] = a_cm.reshape(
            GRID_W, PERM_ROWS, CONV_HALF)
    for n, s in enumerate(subs):
        gt_ref[s, :] = _sigmoid(_dot(ub[n], wg_ref[...])).astype(BF16)


def _const_spec(arr):
    nd = arr.ndim
    return pl.BlockSpec(arr.shape, lambda *_: (0,) * nd)


def _tok_spec(tm, width):
    return pl.BlockSpec((None, tm, width), lambda b, i: (b, i, 0))


def _colmajor_spec(tm):
    return pl.BlockSpec((None, GRID_W, tm // GRID_W, CONV_HALF), lambda b, i: (b, 0, i, 0))


def _single_spec(arr):
    nd = arr.ndim
    return pl.BlockSpec(arr.shape, lambda *_: (0,) * nd, pipeline_mode=pl.Buffered(1))


def _proj(x, mod, g, perm, wm, wdin, wdec, bdec, wg, late_weights, tm):
    bsz, t, d = x.shape
    widths = (CONV_HALF, HK, HK, HV, HV, HK, HK, 2 * D_MODEL)
    dtypes = (BF16, BF16, BF16, BF16, BF16, F32, F32, BF16)
    assert tm % PERM_TOKENS == 0 and len(late_weights) == N_LATE_W
    ah_shape = jax.ShapeDtypeStruct((bsz, GRID_W, t // GRID_W, CONV_HALF), F32)
    n_i = t // tm
    n_steps = bsz * n_i
    late_in, late_out, late_shape = [], [], []
    for w in late_weights:
        _, rows, cols = w.shape
        assert rows % (16 * n_steps) == 0
        blk = rows // n_steps
        late_in.append(pl.BlockSpec((None, blk, cols), lambda b, i: (0, b * n_i + i, 0)))
        late_out.append(pl.BlockSpec((blk, cols), lambda b, i: (b * n_i + i, 0)))
        late_shape.append(jax.ShapeDtypeStruct((rows, cols), BF16))
    return pl.pallas_call(
        _proj_kernel,
        grid=(bsz, n_i),
        in_specs=[
            _tok_spec(tm, d),
            _const_spec(mod),
            _const_spec(g), _single_spec(perm), _single_spec(wm), _single_spec(wdin),
            _single_spec(wdec), _const_spec(bdec), _single_spec(wg),
        ] + late_in,
        out_specs=[_colmajor_spec(tm)] + [_tok_spec(tm, w) for w in widths] + late_out,
        out_shape=[ah_shape] + [jax.ShapeDtypeStruct((bsz, t, w), dt)
                                for w, dt in zip(widths, dtypes)] + late_shape,
        compiler_params=pltpu.CompilerParams(
            dimension_semantics=("parallel", "parallel"), vmem_limit_bytes=VMEM_LIMIT),
        name="proj",
    )(x, mod, g, perm, wm, wdin, wdec, bdec, wg, *late_weights)


def _gla_scale(qs, ks, las, dirn, want_out):
    tri_b, mid, last = dirn["tri_b"], dirn["mid"], dirn["last"]
    la_hi, la_lo = _split_bf16(las)
    g = _dot(tri_b, la_hi) + _dot(tri_b, la_lo)
    u = dict(kt=[], kend=[], qt=[], qg=[], e=[])
    for c in range(SUPER // CHUNK):
        rows = slice(c * CHUNK, (c + 1) * CHUNK)
        gc = g[rows]
        g_mid = gc[mid:mid + 1]
        g_last = gc[last:last + 1]
        u["e"].append(jnp.exp(g_last))
        kt = ks[rows].astype(F32) * jnp.exp(g_mid - gc)
        u["kt"].append(kt.astype(BF16))
        u["kend"].append((kt * jnp.exp(g_last - g_mid)).astype(BF16))
        if want_out:
            qt = qs[rows].astype(F32) * jnp.exp(gc - g_mid)
            u["qt"].append(qt.astype(BF16))
            u["qg"].append((qt * jnp.exp(g_mid)).astype(BF16))
    return u


def _pair_lanes(pair):
    return slice(LANES * pair, LANES * (pair + 1))


def _gla_scores(u, pair, head_lane):
    lanes = _pair_lanes(pair)
    kt = jnp.concatenate([c[:, lanes] for c in u["kt"]], axis=0)
    qt = jnp.concatenate([c[:, lanes] for c in u["qt"]], axis=0)
    kt2 = jnp.concatenate([kt * hm.astype(BF16) for hm in head_lane], axis=0)
    return lax.dot_general(qt, kt2, NT, preferred_element_type=F32)


def _gla_update(u, vs, h, head_lane):
    lanes = _pair_lanes(h // 2)
    hm = head_lane[h % 2].astype(BF16)
    k0 = u["kend"][0][:, lanes] * hm
    k1 = u["kend"][1][:, lanes] * hm
    z = jnp.zeros_like(k0)
    kbd = jnp.concatenate([jnp.concatenate([k0, z], axis=1),
                           jnp.concatenate([z, k1], axis=1)], axis=0)
    return lax.dot_general(vs[:, GLA_DV * h:GLA_DV * (h + 1)], kbd, TN,
                           preferred_element_type=F32)


def _gla_intra(scores, vs, pair, mask2):
    a = jnp.where(mask2, scores, 0.0).astype(BF16)
    v0 = vs[:, GLA_DV * 2 * pair:GLA_DV * (2 * pair + 1)]
    v1 = vs[:, GLA_DV * (2 * pair + 1):GLA_DV * (2 * pair + 2)]
    z = jnp.zeros_like(v0)
    vbd = jnp.concatenate([jnp.concatenate([v0, z], axis=1),
                           jnp.concatenate([z, v1], axis=1)], axis=0)
    return _dot(a, vbd)


def _gla_kernel(q_ref, k_ref, v_ref, laf_ref, lab_ref, rs_ref, g_ref,
                ctx_ref, modc_ref, gpre_ref, wm_ref, wdin_ref, wdec_ref, bdec_ref,
                o_ref, of_ref, ob_ref, sf_ref, sb_ref, kc_ref, vc_ref, lafc_ref, labc_ref):
    t = q_ref.shape[0]
    t_ctx = kc_ref.shape[0]

    ubc = _prenorm_mod(ctx_ref[...], gpre_ref[...], modc_ref, pl.num_programs(0))
    zc = _split_bf16(_dot(ubc, wdin_ref[...]))
    kc_ref[...] = _dot(ubc, wm_ref[:, COL_K:COL_V]).astype(BF16)
    vc_ref[...] = _dot(ubc, wm_ref[:, COL_V:COL_R]).astype(BF16)
    lac = _decay_logs(zc, wdec_ref, bdec_ref)
    lafc_ref[...] = lac[:, :HK]
    labc_ref[...] = lac[:, HK:]

    half = t // 2
    n_steps = t // STEP
    supers = STEP // SUPER
    n_chunks = SUPER // CHUNK
    pairs = GLA_HEADS // 2

    row = lax.broadcasted_iota(jnp.int32, (SUPER, SUPER), 0)
    col = lax.broadcasted_iota(jnp.int32, (SUPER, SUPER), 1)
    same_chunk = (row // CHUNK) == (col // CHUNK)
    low_mask = jnp.logical_and(same_chunk, col <= row)
    up_mask = jnp.logical_and(same_chunk, col >= row)
    lane = lax.broadcasted_iota(jnp.int32, (1, LANES), 1)
    head_lane = (jnp.where(lane < GLA_DK, 1.0, 0.0), jnp.where(lane >= GLA_DK, 1.0, 0.0))

    fwd = dict(tri_b=jnp.where(low_mask, 1.0, 0.0).astype(BF16),
               mask2=jnp.concatenate([low_mask, low_mask], axis=1),
               mid=CHUNK // 2 - 1, last=CHUNK - 1, order=(0, 1), st_ref=sf_ref)
    bwd = dict(tri_b=jnp.where(up_mask, 1.0, 0.0).astype(BF16),
               mask2=jnp.concatenate([up_mask, up_mask], axis=1),
               mid=CHUNK // 2, last=0, order=(1, 0), st_ref=sb_ref)

    sf_ref[...] = jnp.zeros_like(sf_ref)
    sb_ref[...] = jnp.zeros_like(sb_ref)
    gn = g_ref[...]

    def run(units, want_out, finalize):
        vals = []
        for dirn, refs, t0 in units:
            q_r, k_r, v_r, la_r = refs
            rows = pl.ds(t0, SUPER)
            qs = q_r[rows, :] if want_out else None
            vals.append((_gla_scale(qs, k_r[rows, :], la_r[rows, :], dirn, want_out),
                         v_r[rows, :]))
        scores = []
        if want_out:
            scores = [[_gla_scores(u, p, head_lane) for p in range(pairs)] for u, _ in vals]
        upd = [[_gla_update(u, vs, h, head_lane) for h in range(GLA_HEADS)] for u, vs in vals]
        intra = []
        if want_out:
            intra = [[_gla_intra(scores[n][p], vals[n][1], p, units[n][0]["mask2"])
                      for p in range(pairs)] for n in range(len(units))]
        states = {}
        for n, (dirn, _, t0) in enumerate(units):
            key = id(dirn["st_ref"])
            if key not in states:
                states[key] = (dirn["st_ref"], [dirn["st_ref"][h] for h in range(GLA_HEADS)])
            st = states[key][1]
            u = vals[n][0]
            inter = [[None] * n_chunks for _ in range(pairs)]
            for c in dirn["order"]:
                if want_out:
                    for p in range(pairs):
                        st2 = jnp.concatenate([st[2 * p].astype(BF16),
                                               st[2 * p + 1].astype(BF16)], axis=0)
                        inter[p][c] = lax.dot_general(u["qg"][c][:, _pair_lanes(p)], st2, NT,
                                                      preferred_element_type=F32)
                for h in range(GLA_HEADS):
                    e = u["e"][c][:, _pair_lanes(h // 2)]
                    st[h] = st[h] * e + upd[n][h][:, LANES * c:LANES * (c + 1)]
            if not want_out:
                continue
            fwd_unit = dirn is fwd
            for p in range(pairs):
                o2 = intra[n][p] + jnp.concatenate(inter[p], axis=0)
                for j in range(2):
                    cols = slice(GLA_DV * (2 * p + j), GLA_DV * (2 * p + j + 1))
                    o = o2[:, GLA_DV * j:GLA_DV * (j + 1)]
                    if finalize:
                        other = (ob_ref[pl.ds(t0 - half, SUPER), cols] if fwd_unit
                                 else of_ref[pl.ds(t0, SUPER), cols])
                        rs = rs_ref[pl.ds(t0, SUPER), cols].astype(F32)
                        o_ref[pl.ds(t0, SUPER), cols] = (_rms(o + other, gn) * rs).astype(BF16)
                    elif fwd_unit:
                        of_ref[pl.ds(t0, SUPER), cols] = o
                    else:
                        ob_ref[pl.ds(t0 - half, SUPER), cols] = o
        for st_ref, st in states.values():
            for h in range(GLA_HEADS):
                st_ref[h] = st[h]

    ctx_units = []
    for s in range(t_ctx // SUPER):
        ctx_units.append((fwd, (None, kc_ref, vc_ref, lafc_ref), s * SUPER))
        ctx_units.append((bwd, (None, kc_ref, vc_ref, labc_ref), t_ctx - (s + 1) * SUPER))
    run(ctx_units, want_out=False, finalize=False)

    def step(i, finalize):
        units = []
        for s in range(supers):
            t0 = pl.multiple_of(i * STEP + s * SUPER, SUPER)
            t1 = pl.multiple_of(t - (i + 1) * STEP + (supers - 1 - s) * SUPER, SUPER)
            units.append((fwd, (q_ref, k_ref, v_ref, laf_ref), t0))
            units.append((bwd, (q_ref, k_ref, v_ref, lab_ref), t1))
        run(units, want_out=True, finalize=finalize)

    def park_body(i, carry):
        step(i, False)
        return carry

    def finish_body(i, carry):
        step(i, True)
        return carry

    lax.fori_loop(0, n_steps // 2, park_body, 0)
    lax.fori_loop(n_steps // 2, n_steps, finish_body, 0)


def _gla(q, k, v, laf, lab, rs, gnorm, ctx, mod_ctx, gpre, wm, wdin, wdec, bdec):
    bsz, t, _ = q.shape
    t_ctx = ctx.shape[1]
    assert t % (2 * STEP) == 0 and t_ctx % SUPER == 0

    def bspec(arr):
        return pl.BlockSpec((None,) + arr.shape[1:], lambda b: (b, 0, 0))

    args = (q, k, v, laf, lab, rs)
    return pl.pallas_call(
        _gla_kernel,
        grid=(bsz,),
        in_specs=[bspec(a) for a in args] + [_const_spec(gnorm), bspec(ctx),
                  _const_spec(mod_ctx),
                  _const_spec(gpre), _single_spec(wm), _single_spec(wdin), _single_spec(wdec),
                  _const_spec(bdec)],
        out_specs=pl.BlockSpec((None, t, HV), lambda b: (b, 0, 0)),
        out_shape=jax.ShapeDtypeStruct((bsz, t, HV), BF16),
        scratch_shapes=[
            pltpu.VMEM((t // 2, HV), F32), pltpu.VMEM((t // 2, HV), F32),
            pltpu.VMEM((GLA_HEADS, GLA_DV, LANES), F32),
            pltpu.VMEM((GLA_HEADS, GLA_DV, LANES), F32),
            pltpu.VMEM((t_ctx, HK), BF16), pltpu.VMEM((t_ctx, HV), BF16),
            pltpu.VMEM((t_ctx, HK), F32), pltpu.VMEM((t_ctx, HK), F32),
        ],
        compiler_params=pltpu.CompilerParams(
            dimension_semantics=("parallel",), vmem_limit_bytes=VMEM_LIMIT),
        name="gla",
    )(*args, gnorm, ctx, mod_ctx, gpre, wm, wdin, wdec, bdec)


def _tap_segments(n_tiles, lines_per_tile, n_lines, group):
    segs = []
    for t0 in range(0, n_tiles, group):
        t1 = min(t0 + group, n_tiles)
        lo_line = int(t0 * lines_per_tile)
        hi_line = int(-(-t1 * lines_per_tile // 1)) - 1
        k_lo = max(0, CONV_PAD - hi_line)
        k_hi = min(CONV_K, n_lines + CONV_PAD - lo_line)
        if segs and segs[-1][2:] == (k_lo, k_hi):
            segs[-1] = (segs[-1][0], t1, k_lo, k_hi)
        else:
            segs.append((t0, t1, k_lo, k_hi))
    return segs


def _conv_kernel(ah_ref, av_ref, w_ref, b_ref, yh_ref, yv_ref, ph_ref, pv_ref, wb_ref):
    t = ah_ref.shape[0]
    rows = t // GRID_W
    tile = 64
    for kk in range(CONV_K):
        wb_ref[kk] = jnp.broadcast_to(w_ref[kk:kk + 1, :], wb_ref.shape[1:])

    def conv(src_ref, pad_ref, w_cols, line, n_lines, group, dst_ref):
        pad = CONV_PAD * line
        pad_ref[pl.ds(0, pad), :] = jnp.zeros((pad, CONV_HALF), F32)
        pad_ref[pl.ds(pad + t, pad), :] = jnp.zeros((pad, CONV_HALF), F32)

        def fill(i, carry):
            t0 = pl.multiple_of(i * tile, tile)
            pad_ref[pl.ds(pl.multiple_of(pad + t0, 8), tile), :] = (
                src_ref[pl.ds(t0, tile), :].astype(F32))
            return carry

        lax.fori_loop(0, t // tile, fill, 0)
        bias = b_ref[:, w_cols]

        for first, last, k_lo, k_hi in _tap_segments(t // tile, tile / line, n_lines, group):
            def body(i, carry, k_lo=k_lo, k_hi=k_hi):
                t0 = pl.multiple_of(i * tile, tile)
                acc = jnp.broadcast_to(bias, (tile, CONV_HALF))
                for kk in range(k_lo, k_hi):
                    win = pad_ref[pl.ds(pl.multiple_of(t0 + kk * line, 8), tile), :]
                    w8 = wb_ref[kk, :, w_cols]
                    acc = acc + jnp.concatenate([w8] * (tile // 8), axis=0) * win
                dst_ref[pl.ds(t0, tile), :] = acc
                return carry

            lax.fori_loop(first, last, body, 0)

    conv(ah_ref, ph_ref, slice(0, CONV_HALF), rows, GRID_W, 1, yh_ref)
    conv(av_ref, pv_ref, slice(CONV_HALF, CONV_DIM), GRID_W, rows, 2, yv_ref)


def _conv(ah, av, w, b):
    bsz, t, c = ah.shape
    rows = t // GRID_W
    tok = pl.BlockSpec((None, t, c), lambda i: (i, 0, 0))
    return pl.pallas_call(
        _conv_kernel,
        grid=(bsz,),
        in_specs=[tok, tok, pl.BlockSpec((None,) + w.shape[1:], lambda i: (0, 0, 0)),
                  _const_spec(b)],
        out_specs=[tok, tok],
        out_shape=[jax.ShapeDtypeStruct((bsz, t, c), F32)] * 2,
        scratch_shapes=[
            pltpu.VMEM((t + 2 * CONV_PAD * rows, c), F32),
            pltpu.VMEM((t + 2 * CONV_PAD * GRID_W, c), F32),
            pltpu.VMEM((CONV_K, 8, 2 * c), F32),
        ],
        compiler_params=pltpu.CompilerParams(
            dimension_semantics=("parallel",), vmem_limit_bytes=VMEM_LIMIT),
        name="conv",
    )(ah, av, w, b)


def _post_kernel(x_ref, yh_ref, yv_ref, og_ref, gt_ref, mod_ref, perm_ref, lng_ref, lnb_ref,
                 gpost1_ref, gpre2_ref, gpost2_ref, wco_ref, wgo_ref, wout_ref, wff1_ref,
                 wff2_ref, o_ref):
    d = x_ref.shape[-1]
    d_ff = wff1_ref.shape[1]
    ff_tile = 1024
    sub = 256
    subs = [slice(s, s + sub) for s in range(0, x_ref.shape[0], sub)]
    gt1, sh2, sc2, gt2 = (_mod_vec(mod_ref, pl.program_id(0), j) for j in (2, 3, 4, 5))

    yh = []
    for n in range(x_ref.shape[0] // PERM_TOKENS):
        blk = yh_ref[:, PERM_ROWS * n:PERM_ROWS * (n + 1), :].reshape(PERM_TOKENS, CONV_HALF)
        hi, lo = _split_bf16(blk)
        yh.append(_dot(perm_ref[...], hi) + _dot(perm_ref[...], lo))
    yh = jnp.concatenate(yh, axis=0)

    yn = []
    for s in subs:
        yc = jnp.concatenate([yh[s], yv_ref[s, :]], axis=-1)
        yc = yc - jnp.mean(yc, axis=-1, keepdims=True)
        var = jnp.mean(yc * yc, axis=-1, keepdims=True)
        z = yc * lax.rsqrt(var + EPS) * lng_ref[...] + lnb_ref[...]
        yn.append((z * _sigmoid(z)).astype(BF16))
    y_gla = [_dot(og_ref[s, :], wgo_ref[...]) for s in subs]
    y_conv = [_dot(v, wco_ref[...]) for v in yn]
    merged = []
    for n, s in enumerate(subs):
        gates = gt_ref[s, :].astype(F32)
        merged.append((gates[:, :d] * y_conv[n] + gates[:, d:] * y_gla[n]).astype(BF16))
    y = [_dot(v, wout_ref[...]) for v in merged]
    h_mid = [x_ref[s, :] + gt1 * _rms(y[n], gpost1_ref[...]) for n, s in enumerate(subs)]
    u2 = [(_rms(h, gpre2_ref[...]) * (1.0 + sc2) + sh2).astype(BF16) for h in h_mid]
    acc = [jnp.zeros(h.shape, F32) for h in h_mid]
    for j in range(d_ff // ff_tile):
        cols = slice(j * ff_tile, (j + 1) * ff_tile)
        f = [_dot(v, wff1_ref[:, cols]) for v in u2]
        f = [jnp.square(jnp.maximum(v, 0.0)).astype(BF16) for v in f]
        acc = [acc[n] + _dot(f[n], wff2_ref[cols, :]) for n in range(len(subs))]
    for n, s in enumerate(subs):
        o_ref[s, :] = h_mid[n] + gt2 * _rms(acc[n], gpost2_ref[...])


def _post(x, yh, yv, og, gates, mod, perm, ln_g, ln_b, gpost1, gpre2, gpost2, wco, wgo, wout,
          wff1, wff2, tm):
    bsz, t, d = x.shape
    assert tm % PERM_TOKENS == 0
    return pl.pallas_call(
        _post_kernel,
        grid=(bsz, t // tm),
        in_specs=[
            _tok_spec(tm, d), _colmajor_spec(tm), _tok_spec(tm, CONV_HALF), _tok_spec(tm, HV),
            _tok_spec(tm, 2 * d),
            _const_spec(mod),
            _single_spec(perm), _const_spec(ln_g), _const_spec(ln_b),
            _const_spec(gpost1), _const_spec(gpre2), _const_spec(gpost2),
            _single_spec(wco), _single_spec(wgo), _single_spec(wout), _single_spec(wff1),
            _single_spec(wff2),
        ],
        out_specs=_tok_spec(tm, d),
        out_shape=jax.ShapeDtypeStruct((bsz, t, d), F32),
        compiler_params=pltpu.CompilerParams(
            dimension_semantics=("parallel", "parallel"), vmem_limit_bytes=VMEM_LIMIT),
        name="post",
    )(x, yh, yv, og, gates, mod, perm, ln_g, ln_b, gpost1, gpre2, gpost2, wco, wgo, wout, wff1,
      wff2)


def kernel(x, c, ctx, c_ctx, w_mod, b_mod, g_pre1, g_post1, g_pre2, g_post2, w_in, conv_w,
           conv_b, conv_ln_g, conv_ln_b, w_conv_out, w_decay, b_decay, gla_norm_g, w_gla_out,
           w_out, w_ff1, w_ff2):
    bsz, t, d = x.shape
    depth = w_in.shape[0]
    assert depth == 1 and d == D_MODEL and t % (GRID_W * 8) == 0
    assert w_in.shape[1:] == (d, COL_END) and conv_w.shape[1:] == (CONV_K, CONV_DIM)
    assert w_mod.shape[1:] == (d, N_MOD * d)
    assert w_decay.shape[1:] == (2, DECAY_RANK, HK) and ctx.shape[0] == bsz and ctx.shape[2] == d

    m, wm, wdin, wg = _prep(c, c_ctx.reshape(1, d), w_mod, b_mod, jnp.swapaxes(w_in[0], 0, 1))

    wdec = jnp.zeros((LANES, 2 * HK), F32)
    wdec = wdec.at[:DECAY_RANK, :HK].set(w_decay[0, 0])
    wdec = wdec.at[DECAY_RANK:2 * DECAY_RANK, HK:].set(w_decay[0, 1]).astype(BF16)
    bdec = b_decay[0].reshape(1, 2 * HK)
    row = lambda v: v.reshape(1, -1)

    perm = _to_colmajor_perm()
    to_colmajor = jnp.asarray(perm, BF16)
    to_rowmajor = jnp.asarray(perm.T, BF16)

    ah, av, q, k, v, rs, laf, lab, gates, wco, wgo, wout, wff1, wff2 = _proj(
        x, m, row(g_pre1[0]), to_colmajor, wm, wdin, wdec, bdec, wg,
        (w_conv_out, w_gla_out, w_out, w_ff1, w_ff2), tm=1024)
    yh, yv = _conv(ah.reshape(bsz, t, CONV_HALF), av, conv_w, row(conv_b[0]))
    yh = yh.reshape(bsz, GRID_W, t // GRID_W, CONV_HALF)
    og = _gla(q, k, v, laf, lab, rs, row(gla_norm_g[0]),
              ctx, m, row(g_pre1[0]), wm, wdin, wdec, bdec)

    return _post(x, yh, yv, og, gates, m, to_rowmajor, row(conv_ln_g[0]), row(conv_ln_b[0]),
                 row(g_post1[0]), row(g_pre2[0]), row(g_post2[0]),
                 wco, wgo, wout, wff1, wff2, tm=512)
```
